```python
import math
import jax
import jax.numpy as jnp
from jax import lax
import numpy as np

D_MODEL = 1024
BATCH = 8
SEQ = 2048
DEPTH = 2

SSM_HEAD_DIM = 64
SSM_HEADS = 16
SSM_INNER = SSM_HEADS * SSM_HEAD_DIM
SSM_GROUPS = 2
SSM_STATE = 128
SSM_CHUNK = 64
GDN_HEADS = 8
GDN_HEAD_DIM = 128
GDN_INNER = GDN_HEADS * GDN_HEAD_DIM
GDN_CHUNK = 64
RWKV_HEAD_DIM = 64
RWKV_HEADS = 16
RWKV_INNER = RWKV_HEADS * RWKV_HEAD_DIM
RWKV_DECAY_LORA = 64
RWKV_ICLR_LORA = 64
RWKV_GATE_LORA = 128
RWKV_GN_EPS = 64e-5
CONV_K = 4
N_BRANCH = 3
FFN_DIM = 2816
N_EXPERTS = 8
TOP_K = 2
N_DENSE = (DEPTH + 1) // 2
N_MOE = DEPTH // 2
NORM_EPS = 1e-6

CONV_CH = SSM_INNER + 2 * SSM_GROUPS * SSM_STATE + 3 * GDN_INNER
RWKV_COLS = 3 * RWKV_INNER + RWKV_DECAY_LORA + RWKV_ICLR_LORA + RWKV_GATE_LORA
IN_SPLITS = (CONV_CH, SSM_INNER, SSM_HEADS, GDN_INNER, GDN_HEADS, GDN_HEADS, RWKV_COLS, N_BRANCH * D_MODEL)
N_IN = CONV_CH + SSM_INNER + SSM_HEADS + GDN_INNER + 2 * GDN_HEADS + RWKV_COLS + N_BRANCH * D_MODEL

kernel_name = 'hybrid_ssd_gdn_rwkv7_moe_trunk'


def split_cols(t, sizes):
    idx = np.cumsum(sizes)[:-1].tolist()
    return jnp.split(t, idx, axis=-1)


def rms_norm(x, w, eps=NORM_EPS):
    xf = x.astype(jnp.float32)
    y = xf * lax.rsqrt(jnp.mean(xf * xf, axis=-1, keepdims=True) + eps)
    return (y * w.astype(jnp.float32)).astype(x.dtype)


def l2_normalize(x, eps=1e-6):
    return x * lax.rsqrt(jnp.sum(x * x, axis=-1, keepdims=True) + eps)


def causal_depthwise_conv(u, w, b):
    ch = u.shape[-1]
    y = lax.conv_general_dilated(u, w[:, None, :].astype(u.dtype), window_strides=(1,),
                                 padding=[(CONV_K - 1, 0)],
                                 dimension_numbers=('NWC', 'WIO', 'NWC'),
                                 feature_group_count=ch)
    return y + b.astype(u.dtype)


def segsum(a):
    t = a.shape[-1]
    cs = jnp.cumsum(a, axis=-1)
    diff = cs[..., :, None] - cs[..., None, :]
    mask = jnp.tril(jnp.ones((t, t), dtype=bool))
    return jnp.where(mask, diff, -jnp.inf)


def ssd_chunked(x, a, b, c, chunk):
    bsz, seq, h, p = x.shape
    n = b.shape[-1]
    nc = seq // chunk
    x = x.reshape(bsz, nc, chunk, h, p)
    b = b.reshape(bsz, nc, chunk, h, n)
    c = c.reshape(bsz, nc, chunk, h, n)
    a = a.reshape(bsz, nc, chunk, h).transpose(0, 3, 1, 2)
    a_cum = jnp.cumsum(a, axis=-1)
    decay_in = jnp.exp(segsum(a))
    scores = jnp.einsum('bclhn,bcshn->bhcls', c, b) * decay_in
    y_diag = jnp.einsum('bhcls,bcshp->bclhp', scores, x)
    decay_states = jnp.exp(a_cum[..., -1:] - a_cum)
    states = jnp.einsum('bclhn,bhcl,bclhp->bchpn', b, decay_states, x)
    states = jnp.concatenate([jnp.zeros_like(states[:, :1]), states], axis=1)
    decay_chunk = jnp.exp(segsum(jnp.pad(a_cum[..., -1], ((0, 0), (0, 0), (1, 0)))))
    states = jnp.einsum('bhzc,bchpn->bzhpn', decay_chunk, states)[:, :-1]
    y_off = jnp.einsum('bclhn,bchpn,bhcl->bclhp', c, states, jnp.exp(a_cum))
    return (y_diag + y_off).reshape(bsz, seq, h, p)


def gated_delta_chunked(q, k, v, g, beta, chunk):
    bsz, seq, h, dk = q.shape
    dv = v.shape[-1]
    nc = seq // chunk

    def to_chunks(t):
        return t.reshape(bsz, nc, chunk, h, -1).transpose(0, 3, 1, 2, 4)

    q, k, v = to_chunks(q), to_chunks(k), to_chunks(v)
    g = g.reshape(bsz, nc, chunk, h).transpose(0, 3, 1, 2)
    beta = beta.reshape(bsz, nc, chunk, h).transpose(0, 3, 1, 2)
    g_cum = jnp.cumsum(g, axis=-1)
    tri = jnp.tril(jnp.ones((chunk, chunk), dtype=bool))
    strict = jnp.tril(jnp.ones((chunk, chunk), dtype=bool), -1)
    decay_mat = jnp.exp(jnp.where(tri, g_cum[..., :, None] - g_cum[..., None, :], -jnp.inf))
    k_beta = k * beta[..., None]
    v_beta = v * beta[..., None]
    m = jnp.where(strict, jnp.einsum('bhcld,bhcsd->bhcls', k_beta, k) * decay_mat, 0.0)
    eye = jnp.eye(chunk, dtype=m.dtype)
    t_inv = lax.linalg.triangular_solve(eye + m, jnp.broadcast_to(eye, m.shape),
                                        left_side=True, lower=True)
    u = jnp.einsum('bhcls,bhcsv->bhclv', t_inv, v_beta)
    w = jnp.einsum('bhcls,bhcsk->bhclk', t_inv, k_beta * jnp.exp(g_cum)[..., None])
    qk = jnp.where(tri, jnp.einsum('bhcld,bhcsd->bhcls', q, k) * decay_mat, 0.0)

    def step(s, inp):
        q_c, k_c, u_c, w_c, qk_c, g_c = inp
        v_new = u_c - jnp.einsum('bhlk,bhkv->bhlv', w_c, s)
        o = jnp.einsum('bhlk,bhkv->bhlv', q_c * jnp.exp(g_c)[..., None], s) + jnp.einsum('bhls,bhsv->bhlv', qk_c, v_new)
        g_last = g_c[..., -1]
        k_dec = k_c * jnp.exp(g_last[..., None] - g_c)[..., None]
        s = s * jnp.exp(g_last)[..., None, None] + jnp.einsum('bhlk,bhlv->bhkv', k_dec, v_new)
        return s, o

    xs = tuple(jnp.moveaxis(t, 2, 0) for t in (q, k, u, w, qk, g_cum))
    s0 = jnp.zeros((bsz, h, dk, dv), dtype=q.dtype)
    _, o = lax.scan(step, s0, xs)
    return o.transpose(1, 0, 3, 2, 4).reshape(bsz, seq, h, dv)


def rwkv7_scan(r, w, k, v, a, b):
    bsz, seq, h, n = r.shape

    def step(s, inp):
        r_t, w_t, k_t, v_t, a_t, b_t = inp
        sa = jnp.einsum('bhij,bhj->bhi', s, a_t)
        s = s * w_t[:, :, None, :] + sa[..., None] * b_t[:, :, None, :] + v_t[..., None] * k_t[:, :, None, :]
        return s, jnp.einsum('bhij,bhj->bhi', s, r_t)

    xs = tuple(t.transpose(1, 0, 2, 3) for t in (r, w, k, v, a, b))
    s0 = jnp.zeros((bsz, h, n, n), dtype=r.dtype)
    _, y = lax.scan(step, s0, xs)
    return y.transpose(1, 0, 2, 3)


def mamba2_branch(xs, bs, cs, z, dt_raw, a_log, dt_bias, d_skip, norm_w):
    bsz, seq, _ = xs.shape
    f32 = jnp.float32
    dt = jax.nn.softplus(dt_raw.astype(f32) + dt_bias.astype(f32))
    a = -jnp.exp(a_log.astype(f32))
    x = xs.astype(f32).reshape(bsz, seq, SSM_HEADS, SSM_HEAD_DIM)
    rep = SSM_HEADS // SSM_GROUPS
    b = jnp.repeat(bs.astype(f32).reshape(bsz, seq, SSM_GROUPS, SSM_STATE), rep, axis=2)
    c = jnp.repeat(cs.astype(f32).reshape(bsz, seq, SSM_GROUPS, SSM_STATE), rep, axis=2)
    y = ssd_chunked(x * dt[..., None], a * dt, b, c, SSM_CHUNK)
    y = y + d_skip.astype(f32)[:, None] * x
    y = y.reshape(bsz, seq, SSM_INNER) * jax.nn.silu(z.astype(f32))
    yg = y.reshape(bsz, seq, SSM_GROUPS, SSM_INNER // SSM_GROUPS)
    yg = yg * lax.rsqrt(jnp.mean(yg * yg, axis=-1, keepdims=True) + NORM_EPS)
    y = yg.reshape(bsz, seq, SSM_INNER) * norm_w.astype(f32)
    return y.astype(xs.dtype)


def gdn_branch(q, k, v, z, a_raw, b_raw, a_log, dt_bias, norm_w):
    bsz, seq, _ = q.shape
    f32 = jnp.float32
    shp = (bsz, seq, GDN_HEADS, GDN_HEAD_DIM)
    qh = l2_normalize(q.astype(f32).reshape(shp)) * (GDN_HEAD_DIM ** -0.5)
    kh = l2_normalize(k.astype(f32).reshape(shp))
    vh = v.astype(f32).reshape(shp)
    g = -jnp.exp(a_log.astype(f32)) * jax.nn.softplus(a_raw.astype(f32) + dt_bias.astype(f32))
    beta = jax.nn.sigmoid(b_raw.astype(f32))
    o = gated_delta_chunked(qh, kh, vh, g, beta, GDN_CHUNK)
    o = rms_norm(o, norm_w) * jax.nn.silu(z.astype(f32).reshape(shp))
    return o.reshape(bsz, seq, GDN_INNER).astype(q.dtype)


def rwkv7_branch(u, mu, w0, w2, a0, a2, g2, k_k, k_a, r_k, ln_w, ln_b):
    bsz, seq, _ = u.shape
    f32 = jnp.float32
    uf = u.astype(f32)
    u_prev = jnp.pad(uf, ((0, 0), (1, 0), (0, 0)))[:, :-1]
    uf = uf + (u_prev - uf) * mu.astype(f32)
    r, k, v, dw, da, dg = split_cols(uf, (RWKV_INNER, RWKV_INNER, RWKV_INNER,
                                          RWKV_DECAY_LORA, RWKV_ICLR_LORA, RWKV_GATE_LORA))
    w_log = -jax.nn.softplus(-(w0.astype(f32) + jnp.tanh(dw) @ w2.astype(f32))) - 0.5
    decay = jnp.exp(-jnp.exp(w_log))
    a = jax.nn.sigmoid(a0.astype(f32) + da @ a2.astype(f32))
    g = jax.nn.sigmoid(dg) @ g2.astype(f32)

    def heads(t):
        return t.reshape(bsz, seq, RWKV_HEADS, RWKV_HEAD_DIM)

    kk = l2_normalize(heads(k * k_k.astype(f32)))
    k = k * (1.0 + (a - 1.0) * k_a.astype(f32))
    r_h, k_h, v_h, a_h, w_h = heads(r), heads(k), heads(v), heads(a), heads(decay)
    y = rwkv7_scan(r_h, w_h, k_h, v_h, -kk, kk * a_h)
    mean = jnp.mean(y, axis=-1, keepdims=True)
    var = jnp.mean(jnp.square(y - mean), axis=-1, keepdims=True)
    y = ((y - mean) * lax.rsqrt(var + RWKV_GN_EPS)).reshape(bsz, seq, RWKV_INNER)
    y = y * ln_w.astype(f32) + ln_b.astype(f32)
    bonus = jnp.sum(r_h * k_h * r_k.astype(f32), axis=-1, keepdims=True) * v_h
    y = (y + bonus.reshape(bsz, seq, RWKV_INNER)) * g
    return y.astype(u.dtype)


def token_mixer(h, w_in, conv_w, conv_b, ssm_a_log, ssm_dt_bias, ssm_d, ssm_norm_w,
                gdn_a_log, gdn_dt_bias, gdn_norm_w, rwkv_mu, rwkv_w0, rwkv_w2, rwkv_a0, rwkv_a2,
                rwkv_g2, rwkv_k_k, rwkv_k_a, rwkv_r_k, rwkv_ln_w, rwkv_ln_b,
                proj_ssm, proj_gdn, proj_rwkv, w_out):
    proj = h @ w_in
    conv_in, z_ssm, dt_raw, z_gdn, a_gdn, b_gdn, rwkv_in, gate_in = split_cols(proj, IN_SPLITS)
    conv_out = jax.nn.silu(causal_depthwise_conv(conv_in, conv_w, conv_b))
    xs, bs, cs, q, k, v = split_cols(conv_out, (SSM_INNER, SSM_GROUPS * SSM_STATE, SSM_GROUPS * SSM_STATE,
                                                GDN_INNER, GDN_INNER, GDN_INNER))
    y_ssm = mamba2_branch(xs, bs, cs, z_ssm, dt_raw, ssm_a_log, ssm_dt_bias, ssm_d, ssm_norm_w)
    y_gdn = gdn_branch(q, k, v, z_gdn, a_gdn, b_gdn, gdn_a_log, gdn_dt_bias, gdn_norm_w)
    y_rwkv = rwkv7_branch(rwkv_in, rwkv_mu, rwkv_w0, rwkv_w2, rwkv_a0, rwkv_a2, rwkv_g2,
                          rwkv_k_k, rwkv_k_a, rwkv_r_k, rwkv_ln_w, rwkv_ln_b)
    g_ssm, g_gdn, g_rwkv = split_cols(jax.nn.sigmoid(gate_in), (D_MODEL, D_MODEL, D_MODEL))
    merged = g_ssm * (y_ssm @ proj_ssm) + g_gdn * (y_gdn @ proj_gdn) + g_rwkv * (y_rwkv @ proj_rwkv)
    return merged @ w_out


def swiglu(h, w_gate, w_up, w_down):
    return (jax.nn.silu(h @ w_gate) * (h @ w_up)) @ w_down


def moe_swiglu(h, router, w_gate, w_up, w_down):
    logits = jnp.einsum('bsd,de->bse', h, router).astype(jnp.float32)
    top_logits, top_idx = lax.top_k(logits, TOP_K)
    top_p = jax.nn.softmax(top_logits, axis=-1)
    combine = jnp.einsum('bske,bsk->bse', jax.nn.one_hot(top_idx, N_EXPERTS, dtype=jnp.float32), top_p).astype(h.dtype)
    out = jnp.zeros_like(h)
    for e in range(N_EXPERTS):
        out = out + combine[..., e:e + 1] * swiglu(h, w_gate[e], w_up[e], w_down[e])
    return out


def setup_inputs(seed: int = 0) -> dict:
    key = jax.random.key(seed)
    ks = list(jax.random.split(key, 48))
    f32 = jnp.float32
    cnt = [0]

    def nk():
        cnt[0] += 1
        return ks[cnt[0] - 1]

    def nrm(shape, scale):
        return scale * jax.random.normal(nk(), shape, f32)

    def unif(shape, lo, hi):
        return jax.random.uniform(nk(), shape, f32, lo, hi)

    def gain(shape):
        return 1.0 + nrm(shape, 0.02)

    def dt_bias(shape):
        dt = jnp.exp(unif(shape, math.log(1e-3), math.log(1e-1)))
        return dt + jnp.log(-jnp.expm1(-dt))

    L = DEPTH
    return {
        'x': jax.random.normal(nk(), (BATCH, SEQ, D_MODEL), f32),
        'attn_norm_w': gain((L, D_MODEL)),
        'w_in': nrm((L, D_MODEL, N_IN), D_MODEL ** -0.5),
        'conv_w': nrm((L, CONV_K, CONV_CH), CONV_K ** -0.5),
        'conv_b': nrm((L, CONV_CH), 0.01),
        'ssm_a_log': jnp.log(unif((L, SSM_HEADS), 1.0, 16.0)),
        'ssm_dt_bias': dt_bias((L, SSM_HEADS)),
        'ssm_d': 1.0 + nrm((L, SSM_HEADS), 0.1),
        'ssm_norm_w': gain((L, SSM_INNER)),
        'gdn_a_log': jnp.log(unif((L, GDN_HEADS), 1.0, 16.0)),
        'gdn_dt_bias': dt_bias((L, GDN_HEADS)),
        'gdn_norm_w': gain((L, GDN_HEAD_DIM)),
        'rwkv_mu': unif((L, RWKV_COLS), 0.0, 1.0),
        'rwkv_w0': unif((L, RWKV_INNER), -5.0, -1.0),
        'rwkv_w2': nrm((L, RWKV_DECAY_LORA, RWKV_INNER), 0.1 * RWKV_DECAY_LORA ** -0.5),
        'rwkv_a0': nrm((L, RWKV_INNER), 0.1),
        'rwkv_a2': nrm((L, RWKV_ICLR_LORA, RWKV_INNER), RWKV_ICLR_LORA ** -0.5),
        'rwkv_g2': nrm((L, RWKV_GATE_LORA, RWKV_INNER), RWKV_GATE_LORA ** -0.5),
        'rwkv_k_k': 0.85 + nrm((L, RWKV_INNER), 0.02),
        'rwkv_k_a': gain((L, RWKV_INNER)),
        'rwkv_r_k': nrm((L, RWKV_HEADS, RWKV_HEAD_DIM), 0.1),
        'rwkv_ln_w': gain((L, RWKV_INNER)),
        'rwkv_ln_b': nrm((L, RWKV_INNER), 0.01),
        'proj_ssm': nrm((L, SSM_INNER, D_MODEL), SSM_INNER ** -0.5),
        'proj_gdn': nrm((L, GDN_INNER, D_MODEL), GDN_INNER ** -0.5),
        'proj_rwkv': nrm((L, RWKV_INNER, D_MODEL), RWKV_INNER ** -0.5),
        'w_out': nrm((L, D_MODEL, D_MODEL), D_MODEL ** -0.5),
        'ffn_norm_w': gain((L, D_MODEL)),
        'dense_w_gate': nrm((N_DENSE, D_MODEL, FFN_DIM), D_MODEL ** -0.5),
        'dense_w_up': nrm((N_DENSE, D_MODEL, FFN_DIM), D_MODEL ** -0.5),
        'dense_w_down': nrm((N_DENSE, FFN_DIM, D_MODEL), FFN_DIM ** -0.5),
        'moe_router': nrm((N_MOE, D_MODEL, N_EXPERTS), D_MODEL ** -0.5),
        'moe_w_gate': nrm((N_MOE, N_EXPERTS, D_MODEL, FFN_DIM), D_MODEL ** -0.5),
        'moe_w_up': nrm((N_MOE, N_EXPERTS, D_MODEL, FFN_DIM), D_MODEL ** -0.5),
        'moe_w_down': nrm((N_MOE, N_EXPERTS, FFN_DIM, D_MODEL), FFN_DIM ** -0.5),
        'final_norm_w': gain((D_MODEL,)),
    }


def reference(x, attn_norm_w, w_in, conv_w, conv_b, ssm_a_log, ssm_dt_bias, ssm_d, ssm_norm_w,
              gdn_a_log, gdn_dt_bias, gdn_norm_w, rwkv_mu, rwkv_w0, rwkv_w2, rwkv_a0, rwkv_a2,
              rwkv_g2, rwkv_k_k, rwkv_k_a, rwkv_r_k, rwkv_ln_w, rwkv_ln_b, proj_ssm, proj_gdn,
              proj_rwkv, w_out, ffn_norm_w, dense_w_gate, dense_w_up, dense_w_down, moe_router,
              moe_w_gate, moe_w_up, moe_w_down, final_norm_w):
    for i in range(DEPTH):
        h = rms_norm(x, attn_norm_w[i])
        x = x + token_mixer(h, w_in[i], conv_w[i], conv_b[i], ssm_a_log[i], ssm_dt_bias[i], ssm_d[i],
                            ssm_norm_w[i], gdn_a_log[i], gdn_dt_bias[i], gdn_norm_w[i], rwkv_mu[i],
                            rwkv_w0[i], rwkv_w2[i], rwkv_a0[i], rwkv_a2[i], rwkv_g2[i], rwkv_k_k[i],
                            rwkv_k_a[i], rwkv_r_k[i], rwkv_ln_w[i], rwkv_ln_b[i], proj_ssm[i],
                            proj_gdn[i], proj_rwkv[i], w_out[i])
        h = rms_norm(x, ffn_norm_w[i])
        j = i // 2
        if i % 2 == 0:
            x = x + swiglu(h, dense_w_gate[j], dense_w_up[j], dense_w_down[j])
        else:
            x = x + moe_swiglu(h, moe_router[j], moe_w_gate[j], moe_w_up[j], moe_w_down[j])
    return rms_norm(x, final_norm_w)
```

```python
import functools
import math

import numpy as np
import jax
import jax.numpy as jnp
from jax import lax
from jax.experimental import pallas as pl
from jax.experimental.pallas import tpu as pltpu

F32 = jnp.float32
BF16 = jnp.bfloat16

LANES = 128
SUBLANES = 8
VMEM_LIMIT_BYTES = 56 * 1024 * 1024

D_MODEL = 1024
SSM_HEADS, SSM_HEAD_DIM, SSM_GROUPS, SSM_STATE = 16, 64, 2, 128
SSM_INNER = SSM_HEADS * SSM_HEAD_DIM
GDN_HEADS, GDN_HEAD_DIM = 8, 128
GDN_INNER = GDN_HEADS * GDN_HEAD_DIM
RWKV_HEADS, RWKV_HEAD_DIM = 16, 64
RWKV_INNER = RWKV_HEADS * RWKV_HEAD_DIM
RWKV_DECAY_LORA, RWKV_ICLR_LORA, RWKV_GATE_LORA = 64, 64, 128
RWKV_GN_EPS = 64e-5
CONV_K = 4
FFN_DIM = 2816
N_EXPERTS = 8
NORM_EPS = 1e-6
L2_EPS = 1e-6

COL_XS, COL_Q, COL_K, COL_V = 0, 1024, 2048, 3072
COL_ZSSM, COL_ZGDN = 4096, 5120
COL_RKV = 6144
COL_GATE = 9216
COL_BS, COL_CS = 12288, 12544
COL_LORA = 12800
COL_SMALL = 13056
N_PROJ = 13312
SMALL_DT, SMALL_A, SMALL_B = 0, 16, 24
CONV_CH = 4608
CV_XS, CV_Q, CV_K, CV_V, CV_BS, CV_CS = 0, 1024, 2048, 3072, 4096, 4352

SSD_CHUNK = 128
DPLR_CHUNK = 64


def _cparams(*sem):
    return pltpu.CompilerParams(dimension_semantics=sem, vmem_limit_bytes=VMEM_LIMIT_BYTES)


def _dot(a, b):
    return jnp.dot(a.astype(BF16), b.astype(BF16), preferred_element_type=F32)


def _dot_nt(a, b):
    return lax.dot_general(a.astype(BF16), b.astype(BF16), (((1,), (1,)), ((), ())),
                           preferred_element_type=F32)


def _dot_tn(a, b):
    return lax.dot_general(a.astype(BF16), b.astype(BF16), (((0,), (0,)), ((), ())),
                           preferred_element_type=F32)


def _split3(x):
    hi = x.astype(BF16)
    r1 = x - hi.astype(F32)
    mid = r1.astype(BF16)
    lo = (r1 - mid.astype(F32)).astype(BF16)
    return hi, mid, lo


def _dot_x3(x, w):
    hi, mid, lo = _split3(x)
    return _dot(hi, w) + _dot(mid, w) + _dot(lo, w)


def _dot_w3(w, x):
    hi, mid, lo = _split3(x)
    return _dot(w, hi) + _dot(w, mid) + _dot(w, lo)


def _dot_hl(a, b):
    ah = a.astype(BF16)
    al = (a - ah.astype(F32)).astype(BF16)
    bh = b.astype(BF16)
    bl = (b - bh.astype(F32)).astype(BF16)
    return _dot(ah, bh) + _dot(ah, bl) + _dot(al, bh)


def _sigmoid(x):
    return 1.0 / (1.0 + jnp.exp(-x))


def _silu(x):
    return x * _sigmoid(x)


def _softplus(x):
    return jnp.maximum(x, 0.0) + jnp.log(1.0 + jnp.exp(-jnp.abs(x)))


def _tri_incl(n):
    r = lax.broadcasted_iota(jnp.int32, (n, n), 0)
    c = lax.broadcasted_iota(jnp.int32, (n, n), 1)
    return jnp.where(r >= c, 1.0, 0.0).astype(BF16)


def _half_ones():
    r = lax.broadcasted_iota(jnp.int32, (LANES, LANES), 0)
    c = lax.broadcasted_iota(jnp.int32, (LANES, LANES), 1)
    return jnp.where((r // 64) == (c // 64), 1.0, 0.0).astype(BF16)


def _rmsnorm_kernel(x_ref, w_ref, o_ref):
    x = x_ref[...]
    ms = jnp.mean(x * x, axis=-1, keepdims=True)
    o_ref[...] = (x * lax.rsqrt(ms + NORM_EPS) * w_ref[...]).astype(o_ref.dtype)


def _rmsnorm(x, w, out_dtype, tm=512):
    t, d = x.shape
    return pl.pallas_call(
        _rmsnorm_kernel,
        grid=(t // tm,),
        in_specs=[pl.BlockSpec((tm, d), lambda i: (i, 0)),
                  pl.BlockSpec((1, d), lambda i: (0, 0))],
        out_specs=pl.BlockSpec((tm, d), lambda i: (i, 0)),
        out_shape=jax.ShapeDtypeStruct((t, d), out_dtype),
        compiler_params=_cparams("parallel"),
    )(x, w.reshape(1, d))


def _matmul_kernel(a_ref, w_ref, o_ref):
    o_ref[...] = jnp.dot(a_ref[...], w_ref[...], preferred_element_type=F32)


def _matmul(a, w, tm, tn):
    m, k = a.shape
    n = w.shape[1]
    return pl.pallas_call(
        _matmul_kernel,
        grid=(n // tn, m // tm),
        in_specs=[pl.BlockSpec((tm, k), lambda j, i: (i, 0)),
                  pl.BlockSpec((k, tn), lambda j, i: (0, j))],
        out_specs=pl.BlockSpec((tm, tn), lambda j, i: (i, j)),
        out_shape=jax.ShapeDtypeStruct((m, n), F32),
        compiler_params=_cparams("parallel", "parallel"),
    )(a, w)


def _conv_kernel(u_ref, w_ref, b_ref, o_ref, sh_ref, *, tt):
    @pl.when(pl.program_id(2) == 0)
    def _():
        sh_ref[0:SUBLANES, :] = jnp.zeros((SUBLANES, sh_ref.shape[1]), F32)

    u = u_ref[...]
    sh_ref[SUBLANES:SUBLANES + tt, :] = u
    acc = u * w_ref[CONV_K - 1:CONV_K, :] + b_ref[...]
    for k in range(1, CONV_K):
        acc = acc + sh_ref[pl.ds(SUBLANES - k, tt), :] * w_ref[CONV_K - 1 - k:CONV_K - k, :]
    o_ref[...] = _silu(acc)
    sh_ref[0:SUBLANES, :] = sh_ref[tt:tt + SUBLANES, :]


def _conv_silu(proj, conv_w, conv_b, bsz, seq, tt=256, tc=512):
    nblk = CONV_CH // tc
    first = (COL_V + 1024) // tc
    tail = COL_BS // tc
    tpb = seq // tt

    def in_map(b, j, t):
        return (b * tpb + t, jnp.where(j < first, j, tail + j - first))

    return pl.pallas_call(
        functools.partial(_conv_kernel, tt=tt),
        grid=(bsz, nblk, tpb),
        in_specs=[pl.BlockSpec((tt, tc), in_map),
                  pl.BlockSpec((CONV_K, tc), lambda b, j, t: (0, j)),
                  pl.BlockSpec((1, tc), lambda b, j, t: (0, j))],
        out_specs=pl.BlockSpec((tt, tc), lambda b, j, t: (b * tpb + t, j)),
        out_shape=jax.ShapeDtypeStruct((bsz * seq, CONV_CH), F32),
        scratch_shapes=[pltpu.VMEM((tt + SUBLANES, tc), F32)],
        compiler_params=_cparams("parallel", "parallel", "arbitrary"),
    )(proj, conv_w, conv_b.reshape(1, CONV_CH))


def _ssd_kernel(xs_ref, bs_ref, cs_ref, z_ref, sm_ref, alog_ref, dtb_ref, dsk_ref, nw_ref, ex_ref,
                o_ref, st_ref, *, L):
    @pl.when(pl.program_id(1) == 0)
    def _():
        st_ref[...] = jnp.zeros(st_ref.shape, F32)

    hd = SSM_HEAD_DIM
    gw = SSM_INNER // SSM_GROUPS
    dt = _softplus(sm_ref[...] + dtb_ref[...])
    la = -jnp.exp(alog_ref[...]) * dt
    g_cum = _dot_w3(_tri_incl(L), la)
    g_cum_t = g_cum.T
    ex = ex_ref[...]
    dt_e = _dot_x3(dt, ex)
    g_e = _dot_x3(g_cum, ex)
    g_last = g_e[L - 1:L, :]
    e_g = jnp.exp(g_e)
    e_dec = jnp.exp(g_last - g_e)
    e_last = jnp.exp(g_last)

    xs = xs_ref[...]
    xdt = xs * dt_e
    xdec = xdt * e_dec
    row = lax.broadcasted_iota(jnp.int32, (L, L), 0)
    col = lax.broadcasted_iota(jnp.int32, (L, L), 1)
    causal = row >= col
    lane = lax.broadcasted_iota(jnp.int32, (L, LANES), 1)
    first_half = lane < hd

    y_parts = []
    for g in range(SSM_GROUPS):
        b_g = bs_ref[:, g * SSM_STATE:(g + 1) * SSM_STATE]
        c_g = cs_ref[:, g * SSM_STATE:(g + 1) * SSM_STATE]
        cb = _dot_nt(c_g, b_g)
        st_g = st_ref[:, g * gw:(g + 1) * gw]
        y_off = _dot(c_g, st_g) * e_g[:, g * gw:(g + 1) * gw]
        st_ref[:, g * gw:(g + 1) * gw] = (st_g * e_last[:, g * gw:(g + 1) * gw]
                                          + _dot_tn(b_g, xdec[:, g * gw:(g + 1) * gw]))
        heads_per_group = SSM_HEADS // SSM_GROUPS
        for p in range(heads_per_group // 2):
            h0 = g * heads_per_group + 2 * p
            base = h0 * hd
            sc = []
            for h in (h0, h0 + 1):
                diff = g_cum[:, h:h + 1] - g_cum_t[h:h + 1, :]
                sc.append(cb * jnp.exp(jnp.where(causal, diff, -jnp.inf)))
            yd = _dot(jnp.concatenate(sc, axis=0), xdt[:, base:base + LANES])
            y_parts.append(jnp.where(first_half, yd[:L], yd[L:])
                           + y_off[:, base - g * gw:base - g * gw + LANES])
    y = jnp.concatenate(y_parts, axis=1) + dsk_ref[...] * xs
    y = y * _silu(z_ref[...])
    outs = []
    for g in range(SSM_GROUPS):
        seg = y[:, g * gw:(g + 1) * gw]
        outs.append(seg * lax.rsqrt(jnp.mean(seg * seg, axis=-1, keepdims=True) + NORM_EPS))
    o_ref[...] = (jnp.concatenate(outs, axis=1) * nw_ref[...]).astype(o_ref.dtype)


def _pad_lanes(v, offset=0, width=LANES):
    out = jnp.zeros((1, width), F32)
    return lax.dynamic_update_slice(out, v.reshape(1, -1).astype(F32), (0, offset))


def _ssd_branch(conv_out, proj, a_log, dt_bias, d_skip, norm_w, bsz, seq):
    L = SSD_CHUNK
    nc = seq // L
    ex = np.zeros((LANES, SSM_INNER), np.float32)
    for h in range(SSM_HEADS):
        ex[h, h * SSM_HEAD_DIM:(h + 1) * SSM_HEAD_DIM] = 1.0
    row = lambda b, c: b * nc + c
    vec = lambda w: pl.BlockSpec((1, w), lambda b, c: (0, 0))
    return pl.pallas_call(
        functools.partial(_ssd_kernel, L=L),
        grid=(bsz, nc),
        in_specs=[pl.BlockSpec((L, SSM_INNER), lambda b, c: (row(b, c), CV_XS // SSM_INNER)),
                  pl.BlockSpec((L, 256), lambda b, c: (row(b, c), CV_BS // 256)),
                  pl.BlockSpec((L, 256), lambda b, c: (row(b, c), CV_CS // 256)),
                  pl.BlockSpec((L, SSM_INNER), lambda b, c: (row(b, c), COL_ZSSM // SSM_INNER)),
                  pl.BlockSpec((L, LANES), lambda b, c: (row(b, c), COL_SMALL // LANES)),
                  vec(LANES), vec(LANES), vec(SSM_INNER), vec(SSM_INNER),
                  pl.BlockSpec((LANES, SSM_INNER), lambda b, c: (0, 0))],
        out_specs=pl.BlockSpec((L, SSM_INNER), lambda b, c: (row(b, c), 0)),
        out_shape=jax.ShapeDtypeStruct((bsz * seq, SSM_INNER), BF16),
        scratch_shapes=[pltpu.VMEM((SSM_STATE, SSM_INNER), F32)],
        compiler_params=_cparams("parallel", "arbitrary"),
    )(conv_out, conv_out, conv_out, proj, proj,
      _pad_lanes(a_log, SMALL_DT), _pad_lanes(dt_bias, SMALL_DT),
      jnp.repeat(d_skip.astype(F32), SSM_HEAD_DIM).reshape(1, SSM_INNER),
      norm_w.reshape(1, SSM_INNER).astype(F32), jnp.asarray(ex, BF16))


def _stack_tile(x, nsub):
    return x if nsub == 1 else jnp.concatenate([x] * nsub, axis=0)


def _stack_mask(x, nsub):
    if nsub == 1:
        return x
    lane = lax.broadcasted_iota(jnp.int32, x.shape, 1) // (LANES // nsub)
    return jnp.concatenate([jnp.where(lane == j, x, 0.0) for j in range(nsub)], axis=0)


def _unit_solve(n, x, steps, mm):
    for s in range(steps):
        x = x + mm(n, x)
        if s + 1 < steps:
            n = mm(n, n)
    return x


def _dplr_chunk(a_sc, r_sc, b_sc, k_sc, a_sr, r_sr, b_dec, k_dec, v, s_prev, p_last, *,
                L, nsub, dm_a=None, dm_r=None, mm=_dot):
    M = nsub * L
    rows = jnp.concatenate([_stack_tile(a_sc, nsub), _stack_tile(r_sc, nsub)], axis=0)
    b_st = _stack_mask(b_sc, nsub)
    k_st = _stack_mask(k_sc, nsub)
    v_st = _stack_mask(v, nsub)
    sc_b = _dot_nt(rows, b_st)
    sc_k = _dot_nt(rows, k_st)
    ri = lax.broadcasted_iota(jnp.int32, (M, M), 0)
    ci = lax.broadcasted_iota(jnp.int32, (M, M), 1)
    if nsub == 1:
        strict, lower = ci < ri, ci <= ri
    else:
        same = (ri // L) == (ci // L)
        strict = same & ((ci % L) < (ri % L))
        lower = same & ((ci % L) <= (ri % L))
    if dm_a is None:
        a_ab = jnp.where(strict, sc_b[:M], 0.0)
        a_ak = jnp.where(strict, sc_k[:M], 0.0)
        a_rb = jnp.where(lower, sc_b[M:], 0.0)
        a_rk = jnp.where(lower, sc_k[M:], 0.0)
    else:
        a_ab, a_ak = sc_b[:M] * dm_a, sc_k[:M] * dm_a
        a_rb, a_rk = sc_b[M:] * dm_r, sc_k[M:] * dm_r

    reads = _dot_nt(jnp.concatenate([_stack_tile(a_sr, nsub), _stack_tile(r_sr, nsub)], axis=0),
                    s_prev)
    a_h, r_h = reads[:M], reads[M:]
    if nsub > 1:
        rr = lax.broadcasted_iota(jnp.int32, (M, LANES), 0) // L
        ll = lax.broadcasted_iota(jnp.int32, (M, LANES), 1) // (LANES // nsub)
        own = rr == ll
        a_h = jnp.where(own, a_h, 0.0)
        r_h = jnp.where(own, r_h, 0.0)

    w0 = a_h + mm(a_ak, v_st)
    u = _unit_solve(a_ab, w0, int(math.log2(L)), mm)
    y_st = r_h + mm(a_rb, u) + _dot(a_rk, v_st)
    y = y_st[:L]
    for j in range(1, nsub):
        y = y + y_st[j * L:(j + 1) * L]
    s_new = s_prev * p_last + _dot_tn(jnp.concatenate([u, v_st], axis=0),
                                      jnp.concatenate([_stack_mask(b_dec, nsub),
                                                       _stack_mask(k_dec, nsub)], axis=0))
    return y, s_new


def _gdn_kernel(q_ref, k_ref, v_ref, z_ref, sm_ref, alog_ref, dtb_ref, nw_ref, o_ref, st_ref, *, L):
    @pl.when(pl.program_id(1) == 0)
    def _():
        st_ref[...] = jnp.zeros(st_ref.shape, F32)

    sm = sm_ref[...]
    g = -jnp.exp(alog_ref[...]) * _softplus(sm + dtb_ref[...])
    beta = _sigmoid(sm)
    g_cum = _dot_w3(_tri_incl(L), g)
    g_cum_t = jnp.concatenate([g_cum, jnp.zeros((LANES - L, LANES), F32)], axis=0).T
    row = lax.broadcasted_iota(jnp.int32, (L, L), 0)
    col = lax.broadcasted_iota(jnp.int32, (L, L), 1)
    nw = nw_ref[...]
    scale = GDN_HEAD_DIM ** -0.5
    for h in range(GDN_HEADS):
        sl = slice(h * LANES, (h + 1) * LANES)
        q = q_ref[:, sl]
        k = k_ref[:, sl]
        v = v_ref[:, sl]
        qn = q * (lax.rsqrt(jnp.sum(q * q, axis=-1, keepdims=True) + L2_EPS) * scale)
        kn = k * lax.rsqrt(jnp.sum(k * k, axis=-1, keepdims=True) + L2_EPS)
        gc = g_cum[:, SMALL_A + h:SMALL_A + h + 1]
        gr = g_cum_t[SMALL_A + h:SMALL_A + h + 1, 0:L]
        gs = g[:, SMALL_A + h:SMALL_A + h + 1]
        bt = beta[:, SMALL_B + h:SMALL_B + h + 1]
        g_prev = gc - gs
        g_end = g_cum[L - 1:L, SMALL_A + h:SMALL_A + h + 1]
        dm_a = jnp.exp(jnp.where(col < row, g_prev - gr, -jnp.inf))
        dm_r = jnp.exp(jnp.where(col <= row, gc - gr, -jnp.inf))
        b_vec = -(bt * jnp.exp(gs)) * kn
        k_vec = bt * kn
        e_end = jnp.exp(g_end - gc)
        y, s_new = _dplr_chunk(
            kn, qn, b_vec, k_vec,
            kn * jnp.exp(g_prev), qn * jnp.exp(gc),
            b_vec * e_end, k_vec * e_end,
            v, st_ref[h], jnp.exp(g_end), L=L, nsub=1, dm_a=dm_a, dm_r=dm_r)
        st_ref[h] = s_new
        yn = y * lax.rsqrt(jnp.mean(y * y, axis=-1, keepdims=True) + NORM_EPS) * nw
        o_ref[:, sl] = (yn * _silu(z_ref[:, sl])).astype(o_ref.dtype)


def _gdn_branch(conv_out, proj, a_log, dt_bias, norm_w, bsz, seq):
    L = DPLR_CHUNK
    nc = seq // L
    row = lambda b, c: b * nc + c
    vec = lambda w: pl.BlockSpec((1, w), lambda b, c: (0, 0))
    return pl.pallas_call(
        functools.partial(_gdn_kernel, L=L),
        grid=(bsz, nc),
        in_specs=[pl.BlockSpec((L, GDN_INNER), lambda b, c: (row(b, c), CV_Q // GDN_INNER)),
                  pl.BlockSpec((L, GDN_INNER), lambda b, c: (row(b, c), CV_K // GDN_INNER)),
                  pl.BlockSpec((L, GDN_INNER), lambda b, c: (row(b, c), CV_V // GDN_INNER)),
                  pl.BlockSpec((L, GDN_INNER), lambda b, c: (row(b, c), COL_ZGDN // GDN_INNER)),
                  pl.BlockSpec((L, LANES), lambda b, c: (row(b, c), COL_SMALL // LANES)),
                  vec(LANES), vec(LANES), vec(LANES)],
        out_specs=pl.BlockSpec((L, GDN_INNER), lambda b, c: (row(b, c), 0)),
        out_shape=jax.ShapeDtypeStruct((bsz * seq, GDN_INNER), BF16),
        scratch_shapes=[pltpu.VMEM((GDN_HEADS, LANES, LANES), F32)],
        compiler_params=_cparams("parallel", "arbitrary"),
    )(conv_out, conv_out, conv_out, proj, proj,
      _pad_lanes(a_log, SMALL_A), _pad_lanes(dt_bias, SMALL_A),
      norm_w.reshape(1, GDN_HEAD_DIM).astype(F32))


def _rwkv_kernel(rkv_ref, lora_ref, mu_ref, mul_ref, w0_ref, w2_ref, a0_ref, a2_ref, g2_ref,
                 kk_ref, ka_ref, rk_ref, lnw_ref, lnb_ref, o_ref, sh_ref, shl_ref, st_ref, *, L):
    @pl.when(pl.program_id(1) == 0)
    def _():
        st_ref[...] = jnp.zeros(st_ref.shape, F32)
        sh_ref[0:SUBLANES, :] = jnp.zeros((SUBLANES, sh_ref.shape[1]), F32)
        shl_ref[0:SUBLANES, :] = jnp.zeros((SUBLANES, shl_ref.shape[1]), F32)

    n = RWKV_INNER
    u = rkv_ref[...]
    ul = lora_ref[...]
    sh_ref[SUBLANES:SUBLANES + L, :] = u
    shl_ref[SUBLANES:SUBLANES + L, :] = ul
    u = u + (sh_ref[pl.ds(SUBLANES - 1, L), :] - u) * mu_ref[...]
    ul = ul + (shl_ref[pl.ds(SUBLANES - 1, L), :] - ul) * mul_ref[...]
    sh_ref[0:SUBLANES, :] = sh_ref[L:L + SUBLANES, :]
    shl_ref[0:SUBLANES, :] = shl_ref[L:L + SUBLANES, :]

    r, k, v = u[:, 0:n], u[:, n:2 * n], u[:, 2 * n:3 * n]
    lo = ul[:, 0:LANES]
    dg = ul[:, LANES:2 * LANES]
    w_log = -_softplus(-(w0_ref[...] + _dot_hl(jnp.tanh(lo), w2_ref[...]))) - 0.5
    lw = -jnp.exp(w_log)
    a_ic = _sigmoid(a0_ref[...] + _dot_hl(lo, a2_ref[...]))
    gate = _dot(_sigmoid(dg), g2_ref[...])
    g_cum = _dot_w3(_tri_incl(L), lw)
    kk_raw = k * kk_ref[...]
    k_mod = k * (1.0 + (a_ic - 1.0) * ka_ref[...])
    rkr = r * k_mod * rk_ref[...]
    ones_bd = _half_ones()
    inv_hd = 1.0 / RWKV_HEAD_DIM

    for p in range(RWKV_HEADS // 2):
        sl = slice(p * LANES, (p + 1) * LANES)
        kk = kk_raw[:, sl]
        kk = kk * lax.rsqrt(_dot_x3(kk * kk, ones_bd) + L2_EPS)
        gc = g_cum[:, sl]
        g_end = gc[L - 1:L, :]
        e_neg = jnp.exp(-gc)
        e_pos = jnp.exp(gc)
        e_end = jnp.exp(g_end - gc)
        a_vec = -kk
        b_vec = kk * a_ic[:, sl]
        k_vec = k_mod[:, sl]
        a_t = a_vec * jnp.exp(gc - lw[:, sl])
        r_t = r[:, sl] * e_pos
        y, s_new = _dplr_chunk(
            a_t, r_t, b_vec * e_neg, k_vec * e_neg,
            a_t, r_t, b_vec * e_end, k_vec * e_end,
            v[:, sl], st_ref[p], jnp.exp(g_end), L=L, nsub=2)
        st_ref[p] = s_new
        mean = _dot_x3(y, ones_bd) * inv_hd
        yc = y - mean
        var = _dot_x3(yc * yc, ones_bd) * inv_hd
        yn = yc * lax.rsqrt(var + RWKV_GN_EPS) * lnw_ref[:, sl] + lnb_ref[:, sl]
        bonus = _dot_x3(rkr[:, sl], ones_bd) * v[:, sl]
        o_ref[:, sl] = ((yn + bonus) * gate[:, sl]).astype(o_ref.dtype)


def _rwkv_branch(proj, mu, w0, w2, a0, a2, g2, k_k, k_a, r_k, ln_w, ln_b, bsz, seq):
    L = DPLR_CHUNK
    nc = seq // L
    n = RWKV_INNER
    row = lambda b, c: b * nc + c
    vec = lambda w: pl.BlockSpec((1, w), lambda b, c: (0, 0))
    mat = lambda: pl.BlockSpec((LANES, n), lambda b, c: (0, 0))
    w2p = jnp.concatenate([w2, jnp.zeros((LANES - RWKV_DECAY_LORA, n), F32)], axis=0)
    a2p = jnp.concatenate([jnp.zeros((RWKV_DECAY_LORA, n), F32), a2], axis=0)
    r1 = lambda t: t.reshape(1, -1).astype(F32)
    return pl.pallas_call(
        functools.partial(_rwkv_kernel, L=L),
        grid=(bsz, nc),
        in_specs=[pl.BlockSpec((L, 3 * n), lambda b, c: (row(b, c), COL_RKV // (3 * n))),
                  pl.BlockSpec((L, 256), lambda b, c: (row(b, c), COL_LORA // 256)),
                  vec(3 * n), vec(256), vec(n), mat(), vec(n), mat(), mat(),
                  vec(n), vec(n), vec(n), vec(n), vec(n)],
        out_specs=pl.BlockSpec((L, n), lambda b, c: (row(b, c), 0)),
        out_shape=jax.ShapeDtypeStruct((bsz * seq, n), BF16),
        scratch_shapes=[pltpu.VMEM((L + SUBLANES, 3 * n), F32),
                        pltpu.VMEM((L + SUBLANES, 256), F32),
                        pltpu.VMEM((RWKV_HEADS // 2, LANES, LANES), F32)],
        compiler_params=_cparams("parallel", "arbitrary"),
    )(proj, proj, r1(mu[:3 * n]), r1(mu[3 * n:]), r1(w0), w2p, r1(a0), a2p, g2.astype(F32),
      r1(k_k), r1(k_a), r1(r_k), r1(ln_w), r1(ln_b))


def _merge_kernel(x_ref, ys_ref, yg_ref, yr_ref, gate_ref, ps_ref, pg_ref, pr_ref, wo_ref, o_ref):
    d = D_MODEL
    m = _sigmoid(gate_ref[:, 0:d]) * jnp.dot(ys_ref[...], ps_ref[...], preferred_element_type=F32)
    m = m + _sigmoid(gate_ref[:, d:2 * d]) * jnp.dot(yg_ref[...], pg_ref[...], preferred_element_type=F32)
    m = m + _sigmoid(gate_ref[:, 2 * d:3 * d]) * jnp.dot(yr_ref[...], pr_ref[...], preferred_element_type=F32)
    o_ref[...] = x_ref[...] + jnp.dot(m.astype(BF16), wo_ref[...], preferred_element_type=F32)


def _merge(x, y_ssm, y_gdn, y_rwkv, proj, p_ssm, p_gdn, p_rwkv, w_out, tm=512):
    t, d = x.shape
    tok = lambda: pl.BlockSpec((tm, d), lambda i: (i, 0))
    wgt = lambda: pl.BlockSpec((d, d), lambda i: (0, 0))
    return pl.pallas_call(
        _merge_kernel,
        grid=(t // tm,),
        in_specs=[tok(), tok(), tok(), tok(),
                  pl.BlockSpec((tm, 3 * d), lambda i: (i, COL_GATE // (3 * d))),
                  wgt(), wgt(), wgt(), wgt()],
        out_specs=tok(),
        out_shape=jax.ShapeDtypeStruct((t, d), F32),
        compiler_params=_cparams("parallel"),
    )(x, y_ssm, y_gdn, y_rwkv, proj, p_ssm, p_gdn, p_rwkv, w_out)


def _ffn_kernel(x_ref, nw_ref, wg_ref, wu_ref, wd_ref, o_ref, h_ref, acc_ref):
    f = pl.program_id(1)

    @pl.when(f == 0)
    def _():
        x = x_ref[...]
        ms = jnp.mean(x * x, axis=-1, keepdims=True)
        h_ref[...] = (x * lax.rsqrt(ms + NORM_EPS) * nw_ref[...]).astype(BF16)
        acc_ref[...] = x

    h = h_ref[...]
    act = _silu(jnp.dot(h, wg_ref[...], preferred_element_type=F32)) * jnp.dot(
        h, wu_ref[...], preferred_element_type=F32)
    acc_ref[...] += jnp.dot(act.astype(BF16), wd_ref[...], preferred_element_type=F32)

    @pl.when(f == pl.num_programs(1) - 1)
    def _():
        o_ref[...] = acc_ref[...]


def _ffn_dense(x, norm_w, w_gate, w_up, w_down, tm=512, tf=1408):
    t, d = x.shape
    nf = w_gate.shape[1] // tf
    return pl.pallas_call(
        _ffn_kernel,
        grid=(t // tm, nf),
        in_specs=[pl.BlockSpec((tm, d), lambda i, f: (i, 0)),
                  pl.BlockSpec((1, d), lambda i, f: (0, 0)),
                  pl.BlockSpec((d, tf), lambda i, f: (0, f)),
                  pl.BlockSpec((d, tf), lambda i, f: (0, f)),
                  pl.BlockSpec((tf, d), lambda i, f: (f, 0))],
        out_specs=pl.BlockSpec((tm, d), lambda i, f: (i, 0)),
        out_shape=jax.ShapeDtypeStruct((t, d), F32),
        scratch_shapes=[pltpu.VMEM((tm, d), BF16), pltpu.VMEM((tm, d), F32)],
        compiler_params=_cparams("parallel", "arbitrary"),
    )(x, norm_w.reshape(1, d).astype(F32), w_gate, w_up, w_down)


def _moe_kernel(x_ref, nw_ref, rt_ref, wg_ref, wu_ref, wd_ref, fw_ref, o_ref, h_ref, acc_ref, cmb_ref,
                *, final_norm):
    e = pl.program_id(1)
    f = pl.program_id(2)
    lane = lax.broadcasted_iota(jnp.int32, cmb_ref.shape, 1)

    @pl.when((e == 0) & (f == 0))
    def _():
        x = x_ref[...]
        ms = jnp.mean(x * x, axis=-1, keepdims=True)
        hf = x * lax.rsqrt(ms + NORM_EPS) * nw_ref[...]
        h_ref[...] = hf.astype(BF16)
        acc_ref[...] = x
        logits = jnp.where(lane < N_EXPERTS, _dot_hl(hf, rt_ref[...]), -jnp.inf)
        m1 = jnp.max(logits, axis=-1, keepdims=True)
        i1 = jnp.min(jnp.where(logits == m1, lane, LANES), axis=-1, keepdims=True)
        rest = jnp.where(lane == i1, -jnp.inf, logits)
        m2 = jnp.max(rest, axis=-1, keepdims=True)
        i2 = jnp.min(jnp.where(rest == m2, lane, LANES), axis=-1, keepdims=True)
        e2 = jnp.exp(m2 - m1)
        p1 = 1.0 / (1.0 + e2)
        cmb_ref[...] = jnp.where(lane == i1, p1, 0.0) + jnp.where(lane == i2, e2 * p1, 0.0)

    h = h_ref[...]
    act = _silu(jnp.dot(h, wg_ref[0], preferred_element_type=F32)) * jnp.dot(
        h, wu_ref[0], preferred_element_type=F32)
    c_e = jnp.sum(jnp.where(lane == e, cmb_ref[...], 0.0), axis=-1, keepdims=True)
    acc_ref[...] += jnp.dot((act * c_e).astype(BF16), wd_ref[0], preferred_element_type=F32)

    @pl.when((e == pl.num_programs(1) - 1) & (f == pl.num_programs(2) - 1))
    def _():
        y = acc_ref[...]
        if final_norm:
            y = y * lax.rsqrt(jnp.mean(y * y, axis=-1, keepdims=True) + NORM_EPS) * fw_ref[...]
        o_ref[...] = y


def _ffn_moe(x, norm_w, router, w_gate, w_up, w_down, final_w, tm=512, tf=1408):
    t, d = x.shape
    ne, _, ff = w_gate.shape
    nf = ff // tf
    rt = jnp.concatenate([router.astype(F32), jnp.zeros((d, LANES - ne), F32)], axis=1)
    final_norm = final_w is not None
    fw = (final_w if final_norm else jnp.ones((d,), F32)).reshape(1, d).astype(F32)
    return pl.pallas_call(
        functools.partial(_moe_kernel, final_norm=final_norm),
        grid=(t // tm, ne, nf),
        in_specs=[pl.BlockSpec((tm, d), lambda i, e, f: (i, 0)),
                  pl.BlockSpec((1, d), lambda i, e, f: (0, 0)),
                  pl.BlockSpec((d, LANES), lambda i, e, f: (0, 0)),
                  pl.BlockSpec((1, d, tf), lambda i, e, f: (e, 0, f)),
                  pl.BlockSpec((1, d, tf), lambda i, e, f: (e, 0, f)),
                  pl.BlockSpec((1, tf, d), lambda i, e, f: (e, f, 0)),
                  pl.BlockSpec((1, d), lambda i, e, f: (0, 0))],
        out_specs=pl.BlockSpec((tm, d), lambda i, e, f: (i, 0)),
        out_shape=jax.ShapeDtypeStruct((t, d), F32),
        scratch_shapes=[pltpu.VMEM((tm, d), BF16), pltpu.VMEM((tm, d), F32),
                        pltpu.VMEM((tm, LANES), F32)],
        compiler_params=_cparams("parallel", "arbitrary", "arbitrary"),
    )(x, norm_w.reshape(1, d).astype(F32), rt, w_gate, w_up, w_down, fw)


def _reorder_w_in(w):
    d = w.shape[0]
    parts = [w[:, 0:1024], w[:, 1536:4608], w[:, 4608:5632], w[:, 5648:6672], w[:, 6688:9760],
             w[:, 10016:13088], w[:, 1024:1536], w[:, 9760:10016], w[:, 5632:5648], w[:, 6672:6688],
             jnp.zeros((d, N_PROJ - COL_SMALL - 32), w.dtype)]
    return jnp.concatenate(parts, axis=1).astype(BF16)


def _reorder_conv(c):
    return jnp.concatenate([c[..., 0:1024], c[..., 1536:4608], c[..., 1024:1536]], axis=-1).astype(F32)


def _token_mixer_layer(x, i, bsz, seq, p):
    h = _rmsnorm(x, p["attn_norm_w"][i], BF16)
    proj = _matmul(h, _reorder_w_in(p["w_in"][i]), tm=1024, tn=1024)
    conv_out = _conv_silu(proj, _reorder_conv(p["conv_w"][i]), _reorder_conv(p["conv_b"][i]), bsz, seq)
    y_ssm = _ssd_branch(conv_out, proj, p["ssm_a_log"][i], p["ssm_dt_bias"][i], p["ssm_d"][i],
                        p["ssm_norm_w"][i], bsz, seq)
    y_gdn = _gdn_branch(conv_out, proj, p["gdn_a_log"][i], p["gdn_dt_bias"][i], p["gdn_norm_w"][i],
                        bsz, seq)
    y_rwkv = _rwkv_branch(proj, p["rwkv_mu"][i], p["rwkv_w0"][i], p["rwkv_w2"][i], p["rwkv_a0"][i],
                          p["rwkv_a2"][i], p["rwkv_g2"][i], p["rwkv_k_k"][i], p["rwkv_k_a"][i],
                          p["rwkv_r_k"][i], p["rwkv_ln_w"][i], p["rwkv_ln_b"][i], bsz, seq)
    return _merge(x, y_ssm, y_gdn, y_rwkv, proj, p["proj_ssm"][i].astype(BF16),
                  p["proj_gdn"][i].astype(BF16), p["proj_rwkv"][i].astype(BF16),
                  p["w_out"][i].astype(BF16))


def kernel(x, attn_norm_w, w_in, conv_w, conv_b, ssm_a_log, ssm_dt_bias, ssm_d, ssm_norm_w, gdn_a_log, gdn_dt_bias, gdn_norm_w, rwkv_mu, rwkv_w0, rwkv_w2, rwkv_a0, rwkv_a2, rwkv_g2, rwkv_k_k, rwkv_k_a, rwkv_r_k, rwkv_ln_w, rwkv_ln_b, proj_ssm, proj_gdn, proj_rwkv, w_out, ffn_norm_w, dense_w_gate, dense_w_up, dense_w_down, moe_router, moe_w_gate, moe_w_up, moe_w_down, final_norm_w):
    p = dict(attn_norm_w=attn_norm_w, w_in=w_in, conv_w=conv_w, conv_b=conv_b, ssm_a_log=ssm_a_log,
             ssm_dt_bias=ssm_dt_bias, ssm_d=ssm_d, ssm_norm_w=ssm_norm_w, gdn_a_log=gdn_a_log,
             gdn_dt_bias=gdn_dt_bias, gdn_norm_w=gdn_norm_w, rwkv_mu=rwkv_mu, rwkv_w0=rwkv_w0,
             rwkv_w2=rwkv_w2, rwkv_a0=rwkv_a0, rwkv_a2=rwkv_a2, rwkv_g2=rwkv_g2, rwkv_k_k=rwkv_k_k,
             rwkv_k_a=rwkv_k_a, rwkv_r_k=rwkv_r_k, rwkv_ln_w=rwkv_ln_w, rwkv_ln_b=rwkv_ln_b,
             proj_ssm=proj_ssm, proj_gdn=proj_gdn, proj_rwkv=proj_rwkv, w_out=w_out)
    bsz, seq, d = x.shape
    depth = attn_norm_w.shape[0]
    xt = x.reshape(bsz * seq, d)
    for i in range(depth):
        xt = _token_mixer_layer(xt, i, bsz, seq, p)
        j = i // 2
        last = i == depth - 1
        if i % 2 == 0:
            xt = _ffn_dense(xt, ffn_norm_w[i], dense_w_gate[j].astype(BF16), dense_w_up[j].astype(BF16),
                            dense_w_down[j].astype(BF16))
            if last:
                xt = _rmsnorm(xt, final_norm_w, F32)
        else:
            xt = _ffn_moe(xt, ffn_norm_w[i], moe_router[j], moe_w_gate[j].astype(BF16),
                          moe_w_up[j].astype(BF16), moe_w_down[j].astype(BF16),
                          final_norm_w if last else None)
    return xt.reshape(bsz, seq, d)
```

```python
import functools
import math

import numpy as np
import jax
import jax.numpy as jnp
from jax import lax
from jax.experimental import pallas as pl
from jax.experimental.pallas import tpu as pltpu

F32 = jnp.float32
BF16 = jnp.bfloat16

LANES = 128
SUBLANES = 8
VMEM_LIMIT_BYTES = 56 * 1024 * 1024

D_MODEL = 1024
SSM_HEADS, SSM_HEAD_DIM, SSM_GROUPS, SSM_STATE = 16, 64, 2, 128
SSM_INNER = SSM_HEADS * SSM_HEAD_DIM
GDN_HEADS, GDN_HEAD_DIM = 8, 128
GDN_INNER = GDN_HEADS * GDN_HEAD_DIM
RWKV_HEADS, RWKV_HEAD_DIM = 16, 64
RWKV_INNER = RWKV_HEADS * RWKV_HEAD_DIM
RWKV_DECAY_LORA, RWKV_ICLR_LORA, RWKV_GATE_LORA = 64, 64, 128
RWKV_GN_EPS = 64e-5
CONV_K = 4
FFN_DIM = 2816
N_EXPERTS = 8
NORM_EPS = 1e-6
L2_EPS = 1e-6

COL_XS, COL_Q, COL_K, COL_V = 0, 1024, 2048, 3072
COL_ZSSM, COL_ZGDN = 4096, 5120
COL_RKV = 6144
COL_GATE = 9216
COL_BS, COL_CS = 12288, 12544
COL_LORA = 12800
COL_SMALL = 13056
N_PROJ = 13312
SMALL_DT, SMALL_A, SMALL_B = 0, 16, 24
CONV_CH = 4608
CV_XS, CV_Q, CV_K, CV_V, CV_BS, CV_CS = 0, 1024, 2048, 3072, 4096, 4352

SSD_CHUNK = 128
DPLR_CHUNK = 64


def _cparams(*sem):
    return pltpu.CompilerParams(dimension_semantics=sem, vmem_limit_bytes=VMEM_LIMIT_BYTES)


def _dot(a, b):
    return jnp.dot(a.astype(BF16), b.astype(BF16), preferred_element_type=F32)


def _dot_nt(a, b):
    return lax.dot_general(a.astype(BF16), b.astype(BF16), (((1,), (1,)), ((), ())),
                           preferred_element_type=F32)


def _dot_tn(a, b):
    return lax.dot_general(a.astype(BF16), b.astype(BF16), (((0,), (0,)), ((), ())),
                           preferred_element_type=F32)


def _split3(x):
    hi = x.astype(BF16)
    r1 = x - hi.astype(F32)
    mid = r1.astype(BF16)
    lo = (r1 - mid.astype(F32)).astype(BF16)
    return hi, mid, lo


def _dot_x3(x, w):
    hi, mid, lo = _split3(x)
    return _dot(hi, w) + _dot(mid, w) + _dot(lo, w)


def _dot_w3(w, x):
    hi, mid, lo = _split3(x)
    return _dot(w, hi) + _dot(w, mid) + _dot(w, lo)


def _dot_hl(a, b):
    ah = a.astype(BF16)
    al = (a - ah.astype(F32)).astype(BF16)
    bh = b.astype(BF16)
    bl = (b - bh.astype(F32)).astype(BF16)
    return _dot(ah, bh) + _dot(ah, bl) + _dot(al, bh)


def _sigmoid(x):
    return 1.0 / (1.0 + jnp.exp(-x))


def _silu(x):
    return x * _sigmoid(x)


def _softplus(x):
    return jnp.maximum(x, 0.0) + jnp.log(1.0 + jnp.exp(-jnp.abs(x)))


def _tri_incl(n):
    r = lax.broadcasted_iota(jnp.int32, (n, n), 0)
    c = lax.broadcasted_iota(jnp.int32, (n, n), 1)
    return jnp.where(r >= c, 1.0, 0.0).astype(BF16)


def _half_ones():
    r = lax.broadcasted_iota(jnp.int32, (LANES, LANES), 0)
    c = lax.broadcasted_iota(jnp.int32, (LANES, LANES), 1)
    return jnp.where((r // 64) == (c // 64), 1.0, 0.0).astype(BF16)


def _rmsnorm_kernel(x_ref, w_ref, o_ref):
    x = x_ref[...]
    ms = jnp.mean(x * x, axis=-1, keepdims=True)
    o_ref[...] = (x * lax.rsqrt(ms + NORM_EPS) * w_ref[...]).astype(o_ref.dtype)


def _rmsnorm(x, w, out_dtype, tm=512):
    t, d = x.shape
    return pl.pallas_call(
        _rmsnorm_kernel,
        grid=(t // tm,),
        in_specs=[pl.BlockSpec((tm, d), lambda i: (i, 0)),
                  pl.BlockSpec((1, d), lambda i: (0, 0))],
        out_specs=pl.BlockSpec((tm, d), lambda i: (i, 0)),
        out_shape=jax.ShapeDtypeStruct((t, d), out_dtype),
        compiler_params=_cparams("parallel"),
    )(x, w.reshape(1, d))


def _matmul_kernel(a_ref, w_ref, o_ref):
    o_ref[...] = jnp.dot(a_ref[...], w_ref[...], preferred_element_type=F32)


def _matmul(a, w, tm, tn):
    m, k = a.shape
    n = w.shape[1]
    return pl.pallas_call(
        _matmul_kernel,
        grid=(n // tn, m // tm),
        in_specs=[pl.BlockSpec((tm, k), lambda j, i: (i, 0)),
                  pl.BlockSpec((k, tn), lambda j, i: (0, j))],
        out_specs=pl.BlockSpec((tm, tn), lambda j, i: (i, j)),
        out_shape=jax.ShapeDtypeStruct((m, n), F32),
        compiler_params=_cparams("parallel", "parallel"),
    )(a, w)


def _conv_kernel(u_ref, w_ref, b_ref, o_ref, sh_ref, *, tt):
    @pl.when(pl.program_id(2) == 0)
    def _():
        sh_ref[0:SUBLANES, :] = jnp.zeros((SUBLANES, sh_ref.shape[1]), F32)

    u = u_ref[...]
    sh_ref[SUBLANES:SUBLANES + tt, :] = u
    acc = u * w_ref[CONV_K - 1:CONV_K, :] + b_ref[...]
    for k in range(1, CONV_K):
        acc = acc + sh_ref[pl.ds(SUBLANES - k, tt), :] * w_ref[CONV_K - 1 - k:CONV_K - k, :]
    o_ref[...] = _silu(acc)
    sh_ref[0:SUBLANES, :] = sh_ref[tt:tt + SUBLANES, :]


def _conv_silu(proj, conv_w, conv_b, bsz, seq, tt=256, tc=512):
    nblk = CONV_CH // tc
    first = (COL_V + 1024) // tc
    tail = COL_BS // tc
    tpb = seq // tt

    def in_map(b, j, t):
        return (b * tpb + t, jnp.where(j < first, j, tail + j - first))

    return pl.pallas_call(
        functools.partial(_conv_kernel, tt=tt),
        grid=(bsz, nblk, tpb),
        in_specs=[pl.BlockSpec((tt, tc), in_map),
                  pl.BlockSpec((CONV_K, tc), lambda b, j, t: (0, j)),
                  pl.BlockSpec((1, tc), lambda b, j, t: (0, j))],
        out_specs=pl.BlockSpec((tt, tc), lambda b, j, t: (b * tpb + t, j)),
        out_shape=jax.ShapeDtypeStruct((bsz * seq, CONV_CH), F32),
        scratch_shapes=[pltpu.VMEM((tt + SUBLANES, tc), F32)],
        compiler_params=_cparams("parallel", "parallel", "arbitrary"),
    )(proj, conv_w, conv_b.reshape(1, CONV_CH))


def _ssd_kernel(xs_ref, bs_ref, cs_ref, z_ref, sm_ref, alog_ref, dtb_ref, dsk_ref, nw_ref, ex_ref,
                o_ref, st_ref, *, L):
    @pl.when(pl.program_id(1) == 0)
    def _():
        st_ref[...] = jnp.zeros(st_ref.shape, F32)

    hd = SSM_HEAD_DIM
    gw = SSM_INNER // SSM_GROUPS
    dt = _softplus(sm_ref[...] + dtb_ref[...])
    la = -jnp.exp(alog_ref[...]) * dt
    g_cum = _dot_w3(_tri_incl(L), la)
    g_cum_t = g_cum.T
    ex = ex_ref[...]
    dt_e = _dot_x3(dt, ex)
    g_e = _dot_x3(g_cum, ex)
    g_last = g_e[L - 1:L, :]
    e_g = jnp.exp(g_e)
    e_dec = jnp.exp(g_last - g_e)
    e_last = jnp.exp(g_last)

    xs = xs_ref[...]
    xdt = xs * dt_e
    xdec = xdt * e_dec
    row = lax.broadcasted_iota(jnp.int32, (L, L), 0)
    col = lax.broadcasted_iota(jnp.int32, (L, L), 1)
    causal = row >= col
    lane = lax.broadcasted_iota(jnp.int32, (L, LANES), 1)
    first_half = lane < hd

    y_parts = []
    for g in range(SSM_GROUPS):
        b_g = bs_ref[:, g * SSM_STATE:(g + 1) * SSM_STATE]
        c_g = cs_ref[:, g * SSM_STATE:(g + 1) * SSM_STATE]
        cb = _dot_nt(c_g, b_g)
        st_g = st_ref[:, g * gw:(g + 1) * gw]
        y_off = _dot(c_g, st_g) * e_g[:, g * gw:(g + 1) * gw]
        st_ref[:, g * gw:(g + 1) * gw] = (st_g * e_last[:, g * gw:(g + 1) * gw]
                                          + _dot_tn(b_g, xdec[:, g * gw:(g + 1) * gw]))
        heads_per_group = SSM_HEADS // SSM_GROUPS
        for p in range(heads_per_group // 2):
            h0 = g * heads_per_group + 2 * p
            base = h0 * hd
            sc = []
            for h in (h0, h0 + 1):
                diff = g_cum[:, h:h + 1] - g_cum_t[h:h + 1, :]
                sc.append(cb * jnp.exp(jnp.where(causal, diff, -jnp.inf)))
            yd = _dot(jnp.concatenate(sc, axis=0), xdt[:, base:base + LANES])
            y_parts.append(jnp.where(first_half, yd[:L], yd[L:])
                           + y_off[:, base - g * gw:base - g * gw + LANES])
    y = jnp.concatenate(y_parts, axis=1) + dsk_ref[...] * xs
    y = y * _silu(z_ref[...])
    outs = []
    for g in range(SSM_GROUPS):
        seg = y[:, g * gw:(g + 1) * gw]
        outs.append(seg * lax.rsqrt(jnp.mean(seg * seg, axis=-1, keepdims=True) + NORM_EPS))
    o_ref[...] = (jnp.concatenate(outs, axis=1) * nw_ref[...]).astype(o_ref.dtype)


def _pad_lanes(v, offset=0, width=LANES):
    out = jnp.zeros((1, width), F32)
    return lax.dynamic_update_slice(out, v.reshape(1, -1).astype(F32), (0, offset))


def _ssd_branch(conv_out, proj, a_log, dt_bias, d_skip, norm_w, bsz, seq):
    L = SSD_CHUNK
    nc = seq // L
    ex = np.zeros((LANES, SSM_INNER), np.float32)
    for h in range(SSM_HEADS):
        ex[h, h * SSM_HEAD_DIM:(h + 1) * SSM_HEAD_DIM] = 1.0
    row = lambda b, c: b * nc + c
    vec = lambda w: pl.BlockSpec((1, w), lambda b, c: (0, 0))
    return pl.pallas_call(
        functools.partial(_ssd_kernel, L=L),
        grid=(bsz, nc),
        in_specs=[pl.BlockSpec((L, SSM_INNER), lambda b, c: (row(b, c), CV_XS // SSM_INNER)),
                  pl.BlockSpec((L, 256), lambda b, c: (row(b, c), CV_BS // 256)),
                  pl.BlockSpec((L, 256), lambda b, c: (row(b, c), CV_CS // 256)),
                  pl.BlockSpec((L, SSM_INNER), lambda b, c: (row(b, c), COL_ZSSM // SSM_INNER)),
                  pl.BlockSpec((L, LANES), lambda b, c: (row(b, c), COL_SMALL // LANES)),
                  vec(LANES), vec(LANES), vec(SSM_INNER), vec(SSM_INNER),
                  pl.BlockSpec((LANES, SSM_INNER), lambda b, c: (0, 0))],
        out_specs=pl.BlockSpec((L, SSM_INNER), lambda b, c: (row(b, c), 0)),
        out_shape=jax.ShapeDtypeStruct((bsz * seq, SSM_INNER), BF16),
        scratch_shapes=[pltpu.VMEM((SSM_STATE, SSM_INNER), F32)],
        compiler_params=_cparams("parallel", "arbitrary"),
    )(conv_out, conv_out, conv_out, proj, proj,
      _pad_lanes(a_log, SMALL_DT), _pad_lanes(dt_bias, SMALL_DT),
      jnp.repeat(d_skip.astype(F32), SSM_HEAD_DIM).reshape(1, SSM_INNER),
      norm_w.reshape(1, SSM_INNER).astype(F32), jnp.asarray(ex, BF16))


def _stack_tile(x, nsub):
    return x if nsub == 1 else jnp.concatenate([x] * nsub, axis=0)


def _stack_mask(x, nsub):
    if nsub == 1:
        return x
    lane = lax.broadcasted_iota(jnp.int32, x.shape, 1) // (LANES // nsub)
    return jnp.concatenate([jnp.where(lane == j, x, 0.0) for j in range(nsub)], axis=0)


def _dplr_chunks(units, *, L, nsub, mm=_dot):
    M = nsub * L
    nu = len(units)
    ri = lax.broadcasted_iota(jnp.int32, (M, M), 0)
    ci = lax.broadcasted_iota(jnp.int32, (M, M), 1)
    if nsub == 1:
        strict, lower = ci < ri, ci <= ri
        own = None
    else:
        same = (ri // L) == (ci // L)
        strict = same & ((ci % L) < (ri % L))
        lower = same & ((ci % L) <= (ri % L))
        rr = lax.broadcasted_iota(jnp.int32, (M, LANES), 0) // L
        ll = lax.broadcasted_iota(jnp.int32, (M, LANES), 1) // (LANES // nsub)
        own = rr == ll

    rows = [jnp.concatenate([_stack_tile(u["a_sc"], nsub), _stack_tile(u["r_sc"], nsub)], axis=0)
            for u in units]
    b_st = [_stack_mask(u["b_sc"], nsub) for u in units]
    k_st = [_stack_mask(u["k_sc"], nsub) for u in units]
    v_st = [_stack_mask(u["v"], nsub) for u in units]
    sc_b = [_dot_nt(rows[i], b_st[i]) for i in range(nu)]
    sc_k = [_dot_nt(rows[i], k_st[i]) for i in range(nu)]
    reads = [_dot_nt(jnp.concatenate([_stack_tile(u["a_sr"], nsub), _stack_tile(u["r_sr"], nsub)],
                                     axis=0), u["s_prev"]) for u in units]
    n_mat, a_ak, a_rb, a_rk = [], [], [], []
    for i, u in enumerate(units):
        if u.get("dm_a") is None:
            n_mat.append(jnp.where(strict, sc_b[i][:M], 0.0))
            a_ak.append(jnp.where(strict, sc_k[i][:M], 0.0))
            a_rb.append(jnp.where(lower, sc_b[i][M:], 0.0))
            a_rk.append(jnp.where(lower, sc_k[i][M:], 0.0))
        else:
            n_mat.append(sc_b[i][:M] * u["dm_a"])
            a_ak.append(sc_k[i][:M] * u["dm_a"])
            a_rb.append(sc_b[i][M:] * u["dm_r"])
            a_rk.append(sc_k[i][M:] * u["dm_r"])
    if own is None:
        a_h = [r[:M] for r in reads]
        r_h = [r[M:] for r in reads]
    else:
        a_h = [jnp.where(own, r[:M], 0.0) for r in reads]
        r_h = [jnp.where(own, r[M:], 0.0) for r in reads]

    x = [a_h[i] + mm(a_ak[i], v_st[i]) for i in range(nu)]
    y_part = [r_h[i] + _dot(a_rk[i], v_st[i]) for i in range(nu)]
    steps = int(math.log2(L))
    for s in range(steps):
        x = [x[i] + mm(n_mat[i], x[i]) for i in range(nu)]
        if s + 1 < steps:
            n_mat = [mm(n_mat[i], n_mat[i]) for i in range(nu)]
    y_st = [y_part[i] + mm(a_rb[i], x[i]) for i in range(nu)]
    ys = []
    for i in range(nu):
        y = y_st[i][:L]
        for j in range(1, nsub):
            y = y + y_st[i][j * L:(j + 1) * L]
        ys.append(y)
    s_new = [u["s_prev"] * u["p_last"]
             + _dot_tn(jnp.concatenate([x[i], v_st[i]], axis=0),
                       jnp.concatenate([_stack_mask(u["b_dec"], nsub), _stack_mask(u["k_dec"], nsub)], axis=0))
             for i, u in enumerate(units)]
    return ys, s_new


def _gdn_kernel(q_ref, k_ref, v_ref, z_ref, sm_ref, alog_ref, dtb_ref, nw_ref, o_ref, st_ref, *, L):
    @pl.when(pl.program_id(1) == 0)
    def _():
        st_ref[...] = jnp.zeros(st_ref.shape, F32)

    sm = sm_ref[...]
    g = -jnp.exp(alog_ref[...]) * _softplus(sm + dtb_ref[...])
    beta = _sigmoid(sm)
    g_cum = _dot_w3(_tri_incl(L), g)
    g_cum_t = jnp.concatenate([g_cum, jnp.zeros((LANES - L, LANES), F32)], axis=0).T
    row = lax.broadcasted_iota(jnp.int32, (L, L), 0)
    col = lax.broadcasted_iota(jnp.int32, (L, L), 1)
    nw = nw_ref[...]
    scale = GDN_HEAD_DIM ** -0.5
    units = []
    for h in range(GDN_HEADS):
        sl = slice(h * LANES, (h + 1) * LANES)
        q = q_ref[:, sl]
        k = k_ref[:, sl]
        qn = q * (lax.rsqrt(jnp.sum(q * q, axis=-1, keepdims=True) + L2_EPS) * scale)
        kn = k * lax.rsqrt(jnp.sum(k * k, axis=-1, keepdims=True) + L2_EPS)
        gc = g_cum[:, SMALL_A + h:SMALL_A + h + 1]
        gr = g_cum_t[SMALL_A + h:SMALL_A + h + 1, 0:L]
        gs = g[:, SMALL_A + h:SMALL_A + h + 1]
        bt = beta[:, SMALL_B + h:SMALL_B + h + 1]
        g_prev = gc - gs
        g_end = g_cum[L - 1:L, SMALL_A + h:SMALL_A + h + 1]
        b_vec = -(bt * jnp.exp(gs)) * kn
        k_vec = bt * kn
        e_end = jnp.exp(g_end - gc)
        units.append(dict(
            a_sc=kn, r_sc=qn, b_sc=b_vec, k_sc=k_vec,
            a_sr=kn * jnp.exp(g_prev), r_sr=qn * jnp.exp(gc),
            b_dec=b_vec * e_end, k_dec=k_vec * e_end,
            v=v_ref[:, sl], s_prev=st_ref[h], p_last=jnp.exp(g_end),
            dm_a=jnp.exp(jnp.where(col < row, g_prev - gr, -jnp.inf)),
            dm_r=jnp.exp(jnp.where(col <= row, gc - gr, -jnp.inf))))
    ys, s_new = _dplr_chunks(units, L=L, nsub=1)
    for h in range(GDN_HEADS):
        sl = slice(h * LANES, (h + 1) * LANES)
        st_ref[h] = s_new[h]
        y = ys[h]
        yn = y * lax.rsqrt(jnp.mean(y * y, axis=-1, keepdims=True) + NORM_EPS) * nw
        o_ref[:, sl] = (yn * _silu(z_ref[:, sl])).astype(o_ref.dtype)


def _gdn_branch(conv_out, proj, a_log, dt_bias, norm_w, bsz, seq):
    L = DPLR_CHUNK
    nc = seq // L
    row = lambda b, c: b * nc + c
    vec = lambda w: pl.BlockSpec((1, w), lambda b, c: (0, 0))
    return pl.pallas_call(
        functools.partial(_gdn_kernel, L=L),
        grid=(bsz, nc),
        in_specs=[pl.BlockSpec((L, GDN_INNER), lambda b, c: (row(b, c), CV_Q // GDN_INNER)),
                  pl.BlockSpec((L, GDN_INNER), lambda b, c: (row(b, c), CV_K // GDN_INNER)),
                  pl.BlockSpec((L, GDN_INNER), lambda b, c: (row(b, c), CV_V // GDN_INNER)),
                  pl.BlockSpec((L, GDN_INNER), lambda b, c: (row(b, c), COL_ZGDN // GDN_INNER)),
                  pl.BlockSpec((L, LANES), lambda b, c: (row(b, c), COL_SMALL // LANES)),
                  vec(LANES), vec(LANES), vec(LANES)],
        out_specs=pl.BlockSpec((L, GDN_INNER), lambda b, c: (row(b, c), 0)),
        out_shape=jax.ShapeDtypeStruct((bsz * seq, GDN_INNER), BF16),
        scratch_shapes=[pltpu.VMEM((GDN_HEADS, LANES, LANES), F32)],
        compiler_params=_cparams("parallel", "arbitrary"),
    )(conv_out, conv_out, conv_out, proj, proj,
      _pad_lanes(a_log, SMALL_A), _pad_lanes(dt_bias, SMALL_A),
      norm_w.reshape(1, GDN_HEAD_DIM).astype(F32))


def _rwkv_kernel(rkv_ref, lora_ref, mu_ref, mul_ref, w0_ref, w2_ref, a0_ref, a2_ref, g2_ref,
                 kk_ref, ka_ref, rk_ref, lnw_ref, lnb_ref, o_ref, sh_ref, shl_ref, st_ref, *, L):
    @pl.when(pl.program_id(1) == 0)
    def _():
        st_ref[...] = jnp.zeros(st_ref.shape, F32)
        sh_ref[0:SUBLANES, :] = jnp.zeros((SUBLANES, sh_ref.shape[1]), F32)
        shl_ref[0:SUBLANES, :] = jnp.zeros((SUBLANES, shl_ref.shape[1]), F32)

    n = RWKV_INNER
    u = rkv_ref[...]
    ul = lora_ref[...]
    sh_ref[SUBLANES:SUBLANES + L, :] = u
    shl_ref[SUBLANES:SUBLANES + L, :] = ul
    u = u + (sh_ref[pl.ds(SUBLANES - 1, L), :] - u) * mu_ref[...]
    ul = ul + (shl_ref[pl.ds(SUBLANES - 1, L), :] - ul) * mul_ref[...]
    sh_ref[0:SUBLANES, :] = sh_ref[L:L + SUBLANES, :]
    shl_ref[0:SUBLANES, :] = shl_ref[L:L + SUBLANES, :]

    r, k, v = u[:, 0:n], u[:, n:2 * n], u[:, 2 * n:3 * n]
    lo = ul[:, 0:LANES]
    dg = ul[:, LANES:2 * LANES]
    w_log = -_softplus(-(w0_ref[...] + _dot_hl(jnp.tanh(lo), w2_ref[...]))) - 0.5
    lw = -jnp.exp(w_log)
    a_ic = _sigmoid(a0_ref[...] + _dot_hl(lo, a2_ref[...]))
    gate = _dot(_sigmoid(dg), g2_ref[...])
    g_cum = _dot_w3(_tri_incl(L), lw)
    kk_raw = k * kk_ref[...]
    k_mod = k * (1.0 + (a_ic - 1.0) * ka_ref[...])
    rkr = r * k_mod * rk_ref[...]
    ones_bd = _half_ones()
    inv_hd = 1.0 / RWKV_HEAD_DIM

    units = []
    for p in range(RWKV_HEADS // 2):
        sl = slice(p * LANES, (p + 1) * LANES)
        kk = kk_raw[:, sl]
        kk = kk * lax.rsqrt(_dot_x3(kk * kk, ones_bd) + L2_EPS)
        gc = g_cum[:, sl]
        g_end = gc[L - 1:L, :]
        e_neg = jnp.exp(-gc)
        e_end = jnp.exp(g_end - gc)
        b_vec = kk * a_ic[:, sl]
        k_vec = k_mod[:, sl]
        a_t = -kk * jnp.exp(gc - lw[:, sl])
        r_t = r[:, sl] * jnp.exp(gc)
        units.append(dict(
            a_sc=a_t, r_sc=r_t, b_sc=b_vec * e_neg, k_sc=k_vec * e_neg,
            a_sr=a_t, r_sr=r_t, b_dec=b_vec * e_end, k_dec=k_vec * e_end,
            v=v[:, sl], s_prev=st_ref[p], p_last=jnp.exp(g_end)))
    ys, s_new = _dplr_chunks(units, L=L, nsub=2)
    for p in range(RWKV_HEADS // 2):
        sl = slice(p * LANES, (p + 1) * LANES)
        st_ref[p] = s_new[p]
        y = ys[p]
        mean = _dot_x3(y, ones_bd) * inv_hd
        yc = y - mean
        var = _dot_x3(yc * yc, ones_bd) * inv_hd
        yn = yc * lax.rsqrt(var + RWKV_GN_EPS) * lnw_ref[:, sl] + lnb_ref[:, sl]
        bonus = _dot_x3(rkr[:, sl], ones_bd) * v[:, sl]
        o_ref[:, sl] = ((yn + bonus) * gate[:, sl]).astype(o_ref.dtype)


def _rwkv_branch(proj, mu, w0, w2, a0, a2, g2, k_k, k_a, r_k, ln_w, ln_b, bsz, seq):
    L = DPLR_CHUNK
    nc = seq // L
    n = RWKV_INNER
    row = lambda b, c: b * nc + c
    vec = lambda w: pl.BlockSpec((1, w), lambda b, c: (0, 0))
    mat = lambda: pl.BlockSpec((LANES, n), lambda b, c: (0, 0))
    w2p = jnp.concatenate([w2, jnp.zeros((LANES - RWKV_DECAY_LORA, n), F32)], axis=0)
    a2p = jnp.concatenate([jnp.zeros((RWKV_DECAY_LORA, n), F32), a2], axis=0)
    r1 = lambda t: t.reshape(1, -1).astype(F32)
    return pl.pallas_call(
        functools.partial(_rwkv_kernel, L=L),
        grid=(bsz, nc),
        in_specs=[pl.BlockSpec((L, 3 * n), lambda b, c: (row(b, c), COL_RKV // (3 * n))),
                  pl.BlockSpec((L, 256), lambda b, c: (row(b, c), COL_LORA // 256)),
                  vec(3 * n), vec(256), vec(n), mat(), vec(n), mat(), mat(),
                  vec(n), vec(n), vec(n), vec(n), vec(n)],
        out_specs=pl.BlockSpec((L, n), lambda b, c: (row(b, c), 0)),
        out_shape=jax.ShapeDtypeStruct((bsz * seq, n), BF16),
        scratch_shapes=[pltpu.VMEM((L + SUBLANES, 3 * n), F32),
                        pltpu.VMEM((L + SUBLANES, 256), F32),
                        pltpu.VMEM((RWKV_HEADS // 2, LANES, LANES), F32)],
        compiler_params=_cparams("parallel", "arbitrary"),
    )(proj, proj, r1(mu[:3 * n]), r1(mu[3 * n:]), r1(w0), w2p, r1(a0), a2p, g2.astype(F32),
      r1(k_k), r1(k_a), r1(r_k), r1(ln_w), r1(ln_b))


def _merge_kernel(x_ref, ys_ref, yg_ref, yr_ref, gate_ref, ps_ref, pg_ref, pr_ref, wo_ref, o_ref):
    d = D_MODEL
    m = _sigmoid(gate_ref[:, 0:d]) * jnp.dot(ys_ref[...], ps_ref[...], preferred_element_type=F32)
    m = m + _sigmoid(gate_ref[:, d:2 * d]) * jnp.dot(yg_ref[...], pg_ref[...], preferred_element_type=F32)
    m = m + _sigmoid(gate_ref[:, 2 * d:3 * d]) * jnp.dot(yr_ref[...], pr_ref[...], preferred_element_type=F32)
    o_ref[...] = x_ref[...] + jnp.dot(m.astype(BF16), wo_ref[...], preferred_element_type=F32)


def _merge(x, y_ssm, y_gdn, y_rwkv, proj, p_ssm, p_gdn, p_rwkv, w_out, tm=512):
    t, d = x.shape
    tok = lambda: pl.BlockSpec((tm, d), lambda i: (i, 0))
    wgt = lambda: pl.BlockSpec((d, d), lambda i: (0, 0))
    return pl.pallas_call(
        _merge_kernel,
        grid=(t // tm,),
        in_specs=[tok(), tok(), tok(), tok(),
                  pl.BlockSpec((tm, 3 * d), lambda i: (i, COL_GATE // (3 * d))),
                  wgt(), wgt(), wgt(), wgt()],
        out_specs=tok(),
        out_shape=jax.ShapeDtypeStruct((t, d), F32),
        compiler_params=_cparams("parallel"),
    )(x, y_ssm, y_gdn, y_rwkv, proj, p_ssm, p_gdn, p_rwkv, w_out)


def _ffn_kernel(x_ref, nw_ref, wg_ref, wu_ref, wd_ref, o_ref, h_ref, acc_ref):
    f = pl.program_id(1)

    @pl.when(f == 0)
    def _():
        x = x_ref[...]
        ms = jnp.mean(x * x, axis=-1, keepdims=True)
        h_ref[...] = (x * lax.rsqrt(ms + NORM_EPS) * nw_ref[...]).astype(BF16)
        acc_ref[...] = x

    h = h_ref[...]
    act = _silu(jnp.dot(h, wg_ref[...], preferred_element_type=F32)) * jnp.dot(
        h, wu_ref[...], preferred_element_type=F32)
    acc_ref[...] += jnp.dot(act.astype(BF16), wd_ref[...], preferred_element_type=F32)

    @pl.when(f == pl.num_programs(1) - 1)
    def _():
        o_ref[...] = acc_ref[...]


def _ffn_dense(x, norm_w, w_gate, w_up, w_down, tm=512, tf=1408):
    t, d = x.shape
    nf = w_gate.shape[1] // tf
    return pl.pallas_call(
        _ffn_kernel,
        grid=(t // tm, nf),
        in_specs=[pl.BlockSpec((tm, d), lambda i, f: (i, 0)),
                  pl.BlockSpec((1, d), lambda i, f: (0, 0)),
                  pl.BlockSpec((d, tf), lambda i, f: (0, f)),
                  pl.BlockSpec((d, tf), lambda i, f: (0, f)),
                  pl.BlockSpec((tf, d), lambda i, f: (f, 0))],
        out_specs=pl.BlockSpec((tm, d), lambda i, f: (i, 0)),
        out_shape=jax.ShapeDtypeStruct((t, d), F32),
        scratch_shapes=[pltpu.VMEM((tm, d), BF16), pltpu.VMEM((tm, d), F32)],
        compiler_params=_cparams("parallel", "arbitrary"),
    )(x, norm_w.reshape(1, d).astype(F32), w_gate, w_up, w_down)


def _moe_kernel(x_ref, nw_ref, rt_ref, wg_ref, wu_ref, wd_ref, fw_ref, o_ref, h_ref, acc_ref, cmb_ref,
                *, final_norm):
    e = pl.program_id(1)
    f = pl.program_id(2)
    lane = lax.broadcasted_iota(jnp.int32, cmb_ref.shape, 1)

    @pl.when((e == 0) & (f == 0))
    def _():
        x = x_ref[...]
        ms = jnp.mean(x * x, axis=-1, keepdims=True)
        hf = x * lax.rsqrt(ms + NORM_EPS) * nw_ref[...]
        h_ref[...] = hf.astype(BF16)
        acc_ref[...] = x
        logits = jnp.where(lane < N_EXPERTS, _dot_hl(hf, rt_ref[...]), -jnp.inf)
        m1 = jnp.max(logits, axis=-1, keepdims=True)
        i1 = jnp.min(jnp.where(logits == m1, lane, LANES), axis=-1, keepdims=True)
        rest = jnp.where(lane == i1, -jnp.inf, logits)
        m2 = jnp.max(rest, axis=-1, keepdims=True)
        i2 = jnp.min(jnp.where(rest == m2, lane, LANES), axis=-1, keepdims=True)
        e2 = jnp.exp(m2 - m1)
        p1 = 1.0 / (1.0 + e2)
        cmb_ref[...] = jnp.where(lane == i1, p1, 0.0) + jnp.where(lane == i2, e2 * p1, 0.0)

    h = h_ref[...]
    act = _silu(jnp.dot(h, wg_ref[0], preferred_element_type=F32)) * jnp.dot(
        h, wu_ref[0], preferred_element_type=F32)
    c_e = jnp.sum(jnp.where(lane == e, cmb_ref[...], 0.0), axis=-1, keepdims=True)
    acc_ref[...] += jnp.dot((act * c_e).astype(BF16), wd_ref[0], preferred_element_type=F32)

    @pl.when((e == pl.num_programs(1) - 1) & (f == pl.num_programs(2) - 1))
    def _():
        y = acc_ref[...]
        if final_norm:
            y = y * lax.rsqrt(jnp.mean(y * y, axis=-1, keepdims=True) + NORM_EPS) * fw_ref[...]
        o_ref[...] = y


def _ffn_moe(x, norm_w, router, w_gate, w_up, w_down, final_w, tm=512, tf=1408):
    t, d = x.shape
    ne, _, ff = w_gate.shape
    nf = ff // tf
    rt = jnp.concatenate([router.astype(F32), jnp.zeros((d, LANES - ne), F32)], axis=1)
    final_norm = final_w is not None
    fw = (final_w if final_norm else jnp.ones((d,), F32)).reshape(1, d).astype(F32)
    return pl.pallas_call(
        functools.partial(_moe_kernel, final_norm=final_norm),
        grid=(t // tm, ne, nf),
        in_specs=[pl.BlockSpec((tm, d), lambda i, e, f: (i, 0)),
                  pl.BlockSpec((1, d), lambda i, e, f: (0, 0)),
                  pl.BlockSpec((d, LANES), lambda i, e, f: (0, 0)),
                  pl.BlockSpec((1, d, tf), lambda i, e, f: (e, 0, f)),
                  pl.BlockSpec((1, d, tf), lambda i, e, f: (e, 0, f)),
                  pl.BlockSpec((1, tf, d), lambda i, e, f: (e, f, 0)),
                  pl.BlockSpec((1, d), lambda i, e, f: (0, 0))],
        out_specs=pl.BlockSpec((tm, d), lambda i, e, f: (i, 0)),
        out_shape=jax.ShapeDtypeStruct((t, d), F32),
        scratch_shapes=[pltpu.VMEM((tm, d), BF16), pltpu.VMEM((tm, d), F32),
                        pltpu.VMEM((tm, LANES), F32)],
        compiler_params=_cparams("parallel", "arbitrary", "arbitrary"),
    )(x, norm_w.reshape(1, d).astype(F32), rt, w_gate, w_up, w_down, fw)


def _reorder_w_in(w):
    d = w.shape[0]
    parts = [w[:, 0:1024], w[:, 1536:4608], w[:, 4608:5632], w[:, 5648:6672], w[:, 6688:9760],
             w[:, 10016:13088], w[:, 1024:1536], w[:, 9760:10016], w[:, 5632:5648], w[:, 6672:6688],
             jnp.zeros((d, N_PROJ - COL_SMALL - 32), w.dtype)]
    return jnp.concatenate(parts, axis=1).astype(BF16)


def _reorder_conv(c):
    return jnp.concatenate([c[..., 0:1024], c[..., 1536:4608], c[..., 1024:1536]], axis=-1).astype(F32)


def _token_mixer_layer(x, i, bsz, seq, p):
    h = _rmsnorm(x, p["attn_norm_w"][i], BF16)
    proj = _matmul(h, _reorder_w_in(p["w_in"][i]), tm=1024, tn=1024)
    conv_out = _conv_silu(proj, _reorder_conv(p["conv_w"][i]), _reorder_conv(p["conv_b"][i]), bsz, seq)
    y_ssm = _ssd_branch(conv_out, proj, p["ssm_a_log"][i], p["ssm_dt_bias"][i], p["ssm_d"][i],
                        p["ssm_norm_w"][i], bsz, seq)
    y_gdn = _gdn_branch(conv_out, proj, p["gdn_a_log"][i], p["gdn_dt_bias"][i], p["gdn_norm_w"][i],
                        bsz, seq)
    y_rwkv = _rwkv_branch(proj, p["rwkv_mu"][i], p["rwkv_w0"][i], p["rwkv_w2"][i], p["rwkv_a0"][i],
                          p["rwkv_a2"][i], p["rwkv_g2"][i], p["rwkv_k_k"][i], p["rwkv_k_a"][i],
                          p["rwkv_r_k"][i], p["rwkv_ln_w"][i], p["rwkv_ln_b"][i], bsz, seq)
    return _merge(x, y_ssm, y_gdn, y_rwkv, proj, p["proj_ssm"][i].astype(BF16),
                  p["proj_gdn"][i].astype(BF16), p["proj_rwkv"][i].astype(BF16),
                  p["w_out"][i].astype(BF16))


def kernel(x, attn_norm_w, w_in, conv_w, conv_b, ssm_a_log, ssm_dt_bias, ssm_d, ssm_norm_w, gdn_a_log, gdn_dt_bias, gdn_norm_w, rwkv_mu, rwkv_w0, rwkv_w2, rwkv_a0, rwkv_a2, rwkv_g2, rwkv_k_k, rwkv_k_a, rwkv_r_k, rwkv_ln_w, rwkv_ln_b, proj_ssm, proj_gdn, proj_rwkv, w_out, ffn_norm_w, dense_w_gate, dense_w_up, dense_w_down, moe_router, moe_w_gate, moe_w_up, moe_w_down, final_norm_w):
    p = dict(attn_norm_w=attn_norm_w, w_in=w_in, conv_w=conv_w, conv_b=conv_b, ssm_a_log=ssm_a_log,
             ssm_dt_bias=ssm_dt_bias, ssm_d=ssm_d, ssm_norm_w=ssm_norm_w, gdn_a_log=gdn_a_log,
             gdn_dt_bias=gdn_dt_bias, gdn_norm_w=gdn_norm_w, rwkv_mu=rwkv_mu, rwkv_w0=rwkv_w0,
             rwkv_w2=rwkv_w2, rwkv_a0=rwkv_a0, rwkv_a2=rwkv_a2, rwkv_g2=rwkv_g2, rwkv_k_k=rwkv_k_k,
             rwkv_k_a=rwkv_k_a, rwkv_r_k=rwkv_r_k, rwkv_ln_w=rwkv_ln_w, rwkv_ln_b=rwkv_ln_b,
             proj_ssm=proj_ssm, proj_gdn=proj_gdn, proj_rwkv=proj_rwkv, w_out=w_out)
    bsz, seq, d = x.shape
    depth = attn_norm_w.shape[0]
    xt = x.reshape(bsz * seq, d)
    for i in range(depth):
        xt = _token_mixer_layer(xt, i, bsz, seq, p)
        j = i // 2
        last = i == depth - 1
        if i % 2 == 0:
            xt = _ffn_dense(xt, ffn_norm_w[i], dense_w_gate[j].astype(BF16), dense_w_up[j].astype(BF16),
                            dense_w_down[j].astype(BF16))
            if last:
                xt = _rmsnorm(xt, final_norm_w, F32)
        else:
            xt = _ffn_moe(xt, ffn_norm_w[i], moe_router[j], moe_w_gate[j].astype(BF16),
                          moe_w_up[j].astype(BF16), moe_w_down[j].astype(BF16),
                          final_norm_w if last else None)
    return xt.reshape(bsz, seq, d)
```

```python
import functools
import math

import numpy as np
import jax
import jax.numpy as jnp
from jax import lax
from jax.experimental import pallas as pl
from jax.experimental.pallas import tpu as pltpu

F32 = jnp.float32
BF16 = jnp.bfloat16

LANES = 128
SUBLANES = 8
VMEM_LIMIT_BYTES = 56 * 1024 * 1024

D_MODEL = 1024
SSM_HEADS, SSM_HEAD_DIM, SSM_GROUPS, SSM_STATE = 16, 64, 2, 128
SSM_INNER = SSM_HEADS * SSM_HEAD_DIM
GDN_HEADS, GDN_HEAD_DIM = 8, 128
GDN_INNER = GDN_HEADS * GDN_HEAD_DIM
RWKV_HEADS, RWKV_HEAD_DIM = 16, 64
RWKV_INNER = RWKV_HEADS * RWKV_HEAD_DIM
RWKV_DECAY_LORA, RWKV_ICLR_LORA, RWKV_GATE_LORA = 64, 64, 128
RWKV_GN_EPS = 64e-5
CONV_K = 4
FFN_DIM = 2816
N_EXPERTS = 8
NORM_EPS = 1e-6
L2_EPS = 1e-6

COL_XS, COL_Q, COL_K, COL_V = 0, 1024, 2048, 3072
COL_ZSSM, COL_ZGDN = 4096, 5120
COL_RKV = 6144
COL_GATE = 9216
COL_BS, COL_CS = 12288, 12544
COL_LORA = 12800
COL_SMALL = 13056
N_PROJ = 13312
SMALL_DT, SMALL_A, SMALL_B = 0, 16, 24

SSD_CHUNK = 128
DPLR_CHUNK = 64


def _cparams(*sem):
    return pltpu.CompilerParams(dimension_semantics=sem, vmem_limit_bytes=VMEM_LIMIT_BYTES)


def _dot(a, b):
    return jnp.dot(a.astype(BF16), b.astype(BF16), preferred_element_type=F32)


def _dot_nt(a, b):
    return lax.dot_general(a.astype(BF16), b.astype(BF16), (((1,), (1,)), ((), ())),
                           preferred_element_type=F32)


def _dot_tn(a, b):
    return lax.dot_general(a.astype(BF16), b.astype(BF16), (((0,), (0,)), ((), ())),
                           preferred_element_type=F32)


def _split3(x):
    hi = x.astype(BF16)
    r1 = x - hi.astype(F32)
    mid = r1.astype(BF16)
    lo = (r1 - mid.astype(F32)).astype(BF16)
    return hi, mid, lo


def _dot_x3(x, w):
    hi, mid, lo = _split3(x)
    return _dot(hi, w) + _dot(mid, w) + _dot(lo, w)


def _dot_w3(w, x):
    hi, mid, lo = _split3(x)
    return _dot(w, hi) + _dot(w, mid) + _dot(w, lo)


def _dot_hl(a, b):
    ah = a.astype(BF16)
    al = (a - ah.astype(F32)).astype(BF16)
    bh = b.astype(BF16)
    bl = (b - bh.astype(F32)).astype(BF16)
    return _dot(ah, bh) + _dot(ah, bl) + _dot(al, bh)


def _sigmoid(x):
    return 1.0 / (1.0 + jnp.exp(-x))


def _silu(x):
    return x * _sigmoid(x)


def _softplus(x):
    return jnp.maximum(x, 0.0) + jnp.log(1.0 + jnp.exp(-jnp.abs(x)))


def _tri_incl(n):
    r = lax.broadcasted_iota(jnp.int32, (n, n), 0)
    c = lax.broadcasted_iota(jnp.int32, (n, n), 1)
    return jnp.where(r >= c, 1.0, 0.0).astype(BF16)


def _half_ones():
    r = lax.broadcasted_iota(jnp.int32, (LANES, LANES), 0)
    c = lax.broadcasted_iota(jnp.int32, (LANES, LANES), 1)
    return jnp.where((r // 64) == (c // 64), 1.0, 0.0).astype(BF16)


def _rmsnorm_kernel(x_ref, w_ref, o_ref):
    x = x_ref[...]
    ms = jnp.mean(x * x, axis=-1, keepdims=True)
    o_ref[...] = (x * lax.rsqrt(ms + NORM_EPS) * w_ref[...]).astype(o_ref.dtype)


def _rmsnorm(x, w, out_dtype, tm=512):
    t, d = x.shape
    return pl.pallas_call(
        _rmsnorm_kernel,
        grid=(t // tm,),
        in_specs=[pl.BlockSpec((tm, d), lambda i: (i, 0)),
                  pl.BlockSpec((1, d), lambda i: (0, 0))],
        out_specs=pl.BlockSpec((tm, d), lambda i: (i, 0)),
        out_shape=jax.ShapeDtypeStruct((t, d), out_dtype),
        compiler_params=_cparams("parallel"),
    )(x, w.reshape(1, d))


def _inproj_kernel(a_ref, w_ref, cw_ref, cb_ref, cm_ref, o_ref, sh_ref, *, tm, tn, tiles_per_seq):
    j = pl.program_id(0)
    i = pl.program_id(1)
    lo = j * tn
    has_conv = (lo < COL_V + 1024) | ((lo + tn > COL_BS) & (lo < COL_LORA))
    acc = jnp.dot(a_ref[...], w_ref[...], preferred_element_type=F32)

    @pl.when(jnp.logical_not(has_conv))
    def _():
        o_ref[...] = acc

    @pl.when(has_conv)
    def _():
        @pl.when(i % tiles_per_seq == 0)
        def _():
            sh_ref[0:SUBLANES, :] = jnp.zeros((SUBLANES, tn), F32)

        sh_ref[SUBLANES:SUBLANES + tm, :] = acc
        c = acc * cw_ref[CONV_K - 1:CONV_K, :] + cb_ref[...]
        for k in range(1, CONV_K):
            c = c + sh_ref[pl.ds(SUBLANES - k, tm), :] * cw_ref[CONV_K - 1 - k:CONV_K - k, :]
        o_ref[...] = jnp.where(cm_ref[...] > 0.0, _silu(c), acc)
        sh_ref[0:SUBLANES, :] = sh_ref[tm:tm + SUBLANES, :]


def _conv_cols(c, fill):
    r = c.shape[0]
    z = lambda w: jnp.full((r, w), fill, F32)
    return jnp.concatenate([c[:, 0:1024], c[:, 1536:4608], z(COL_BS - (COL_V + 1024)), c[:, 1024:1536],
                            z(N_PROJ - COL_LORA)], axis=1).astype(F32)


def _inproj(a, w, conv_w, conv_b, seq, tm, tn):
    m, k = a.shape
    n = w.shape[1]
    cw = _conv_cols(conv_w, 0.0)
    cb = _conv_cols(conv_b.reshape(1, -1), 0.0)
    cm = _conv_cols(jnp.ones((1, conv_w.shape[1]), F32), 0.0)
    return pl.pallas_call(
        functools.partial(_inproj_kernel, tm=tm, tn=tn, tiles_per_seq=seq // tm),
        grid=(n // tn, m // tm),
        in_specs=[pl.BlockSpec((tm, k), lambda j, i: (i, 0)),
                  pl.BlockSpec((k, tn), lambda j, i: (0, j)),
                  pl.BlockSpec((CONV_K, tn), lambda j, i: (0, j)),
                  pl.BlockSpec((1, tn), lambda j, i: (0, j)),
                  pl.BlockSpec((1, tn), lambda j, i: (0, j))],
        out_specs=pl.BlockSpec((tm, tn), lambda j, i: (i, j)),
        out_shape=jax.ShapeDtypeStruct((m, n), F32),
        scratch_shapes=[pltpu.VMEM((tm + SUBLANES, tn), F32)],
        compiler_params=_cparams("parallel", "arbitrary"),
    )(a, w, cw, cb, cm)


def _ssd_kernel(xs_ref, bs_ref, cs_ref, z_ref, sm_ref, alog_ref, dtb_ref, dsk_ref, nw_ref, ex_ref,
                o_ref, st_ref, *, L):
    @pl.when(pl.program_id(1) == 0)
    def _():
        st_ref[...] = jnp.zeros(st_ref.shape, F32)

    hd = SSM_HEAD_DIM
    gw = SSM_INNER // SSM_GROUPS
    dt = _softplus(sm_ref[...] + dtb_ref[...])
    la = -jnp.exp(alog_ref[...]) * dt
    g_cum = _dot_w3(_tri_incl(L), la)
    g_cum_t = g_cum.T
    ex = ex_ref[...]
    dt_e = _dot_x3(dt, ex)
    g_e = _dot_x3(g_cum, ex)
    g_last = g_e[L - 1:L, :]
    e_g = jnp.exp(g_e)
    e_dec = jnp.exp(g_last - g_e)
    e_last = jnp.exp(g_last)

    xs = xs_ref[...]
    xdt = xs * dt_e
    xdec = xdt * e_dec
    row = lax.broadcasted_iota(jnp.int32, (L, L), 0)
    col = lax.broadcasted_iota(jnp.int32, (L, L), 1)
    causal = row >= col
    lane = lax.broadcasted_iota(jnp.int32, (L, LANES), 1)
    first_half = lane < hd

    y_parts = []
    for g in range(SSM_GROUPS):
        b_g = bs_ref[:, g * SSM_STATE:(g + 1) * SSM_STATE]
        c_g = cs_ref[:, g * SSM_STATE:(g + 1) * SSM_STATE]
        cb = _dot_nt(c_g, b_g)
        st_g = st_ref[:, g * gw:(g + 1) * gw]
        y_off = _dot(c_g, st_g) * e_g[:, g * gw:(g + 1) * gw]
        st_ref[:, g * gw:(g + 1) * gw] = (st_g * e_last[:, g * gw:(g + 1) * gw]
                                          + _dot_tn(b_g, xdec[:, g * gw:(g + 1) * gw]))
        heads_per_group = SSM_HEADS // SSM_GROUPS
        for p in range(heads_per_group // 2):
            h0 = g * heads_per_group + 2 * p
            base = h0 * hd
            sc = []
            for h in (h0, h0 + 1):
                diff = g_cum[:, h:h + 1] - g_cum_t[h:h + 1, :]
                sc.append(cb * jnp.exp(jnp.where(causal, diff, -jnp.inf)))
            yd = _dot(jnp.concatenate(sc, axis=0), xdt[:, base:base + LANES])
            y_parts.append(jnp.where(first_half, yd[:L], yd[L:])
                           + y_off[:, base - g * gw:base - g * gw + LANES])
    y = jnp.concatenate(y_parts, axis=1) + dsk_ref[...] * xs
    y = y * _silu(z_ref[...])
    outs = []
    for g in range(SSM_GROUPS):
        seg = y[:, g * gw:(g + 1) * gw]
        outs.append(seg * lax.rsqrt(jnp.mean(seg * seg, axis=-1, keepdims=True) + NORM_EPS))
    o_ref[...] = (jnp.concatenate(outs, axis=1) * nw_ref[...]).astype(o_ref.dtype)


def _pad_lanes(v, offset=0, width=LANES):
    out = jnp.zeros((1, width), F32)
    return lax.dynamic_update_slice(out, v.reshape(1, -1).astype(F32), (0, offset))


def _ssd_branch(proj, a_log, dt_bias, d_skip, norm_w, bsz, seq):
    L = SSD_CHUNK
    nc = seq // L
    ex = np.zeros((LANES, SSM_INNER), np.float32)
    for h in range(SSM_HEADS):
        ex[h, h * SSM_HEAD_DIM:(h + 1) * SSM_HEAD_DIM] = 1.0
    row = lambda b, c: b * nc + c
    vec = lambda w: pl.BlockSpec((1, w), lambda b, c: (0, 0))
    return pl.pallas_call(
        functools.partial(_ssd_kernel, L=L),
        grid=(bsz, nc),
        in_specs=[pl.BlockSpec((L, SSM_INNER), lambda b, c: (row(b, c), COL_XS // SSM_INNER)),
                  pl.BlockSpec((L, 256), lambda b, c: (row(b, c), COL_BS // 256)),
                  pl.BlockSpec((L, 256), lambda b, c: (row(b, c), COL_CS // 256)),
                  pl.BlockSpec((L, SSM_INNER), lambda b, c: (row(b, c), COL_ZSSM // SSM_INNER)),
                  pl.BlockSpec((L, LANES), lambda b, c: (row(b, c), COL_SMALL // LANES)),
                  vec(LANES), vec(LANES), vec(SSM_INNER), vec(SSM_INNER),
                  pl.BlockSpec((LANES, SSM_INNER), lambda b, c: (0, 0))],
        out_specs=pl.BlockSpec((L, SSM_INNER), lambda b, c: (row(b, c), 0)),
        out_shape=jax.ShapeDtypeStruct((bsz * seq, SSM_INNER), BF16),
        scratch_shapes=[pltpu.VMEM((SSM_STATE, SSM_INNER), F32)],
        compiler_params=_cparams("parallel", "arbitrary"),
    )(proj, proj, proj, proj, proj,
      _pad_lanes(a_log, SMALL_DT), _pad_lanes(dt_bias, SMALL_DT),
      jnp.repeat(d_skip.astype(F32), SSM_HEAD_DIM).reshape(1, SSM_INNER),
      norm_w.reshape(1, SSM_INNER).astype(F32), jnp.asarray(ex, BF16))


def _stack_tile(x, nsub):
    return x if nsub == 1 else jnp.concatenate([x] * nsub, axis=0)


def _stack_mask(x, nsub):
    if nsub == 1:
        return x
    lane = lax.broadcasted_iota(jnp.int32, x.shape, 1) // (LANES // nsub)
    return jnp.concatenate([jnp.where(lane == j, x, 0.0) for j in range(nsub)], axis=0)


def _dplr_chunks(units, *, L, nsub, mm=_dot):
    M = nsub * L
    nu = len(units)
    ri = lax.broadcasted_iota(jnp.int32, (M, M), 0)
    ci = lax.broadcasted_iota(jnp.int32, (M, M), 1)
    if nsub == 1:
        strict, lower = ci < ri, ci <= ri
        own = None
    else:
        same = (ri // L) == (ci // L)
        strict = same & ((ci % L) < (ri % L))
        lower = same & ((ci % L) <= (ri % L))
        rr = lax.broadcasted_iota(jnp.int32, (M, LANES), 0) // L
        ll = lax.broadcasted_iota(jnp.int32, (M, LANES), 1) // (LANES // nsub)
        own = rr == ll

    rows = [jnp.concatenate([_stack_tile(u["a_sc"], nsub), _stack_tile(u["r_sc"], nsub)], axis=0)
            for u in units]
    b_st = [_stack_mask(u["b_sc"], nsub) for u in units]
    k_st = [_stack_mask(u["k_sc"], nsub) for u in units]
    v_st = [_stack_mask(u["v"], nsub) for u in units]
    if M % LANES == 0:
        sc = [_dot_nt(rows[i], jnp.concatenate([b_st[i], k_st[i]], axis=0)) for i in range(nu)]
        sc_b = [t[:, :M] for t in sc]
        sc_k = [t[:, M:] for t in sc]
    else:
        sc_b = [_dot_nt(rows[i], b_st[i]) for i in range(nu)]
        sc_k = [_dot_nt(rows[i], k_st[i]) for i in range(nu)]
    reads = [_dot_nt(jnp.concatenate([_stack_tile(u["a_sr"], nsub), _stack_tile(u["r_sr"], nsub)],
                                     axis=0), u["s_prev"]) for u in units]
    n_mat, a_ak, a_rb, a_rk = [], [], [], []
    for i, u in enumerate(units):
        if u.get("dm_a") is None:
            n_mat.append(jnp.where(strict, sc_b[i][:M], 0.0))
            a_ak.append(jnp.where(strict, sc_k[i][:M], 0.0))
            a_rb.append(jnp.where(lower, sc_b[i][M:], 0.0))
            a_rk.append(jnp.where(lower, sc_k[i][M:], 0.0))
        else:
            n_mat.append(sc_b[i][:M] * u["dm_a"])
            a_ak.append(sc_k[i][:M] * u["dm_a"])
            a_rb.append(sc_b[i][M:] * u["dm_r"])
            a_rk.append(sc_k[i][M:] * u["dm_r"])
    if own is None:
        a_h = [r[:M] for r in reads]
        r_h = [r[M:] for r in reads]
    else:
        a_h = [jnp.where(own, r[:M], 0.0) for r in reads]
        r_h = [jnp.where(own, r[M:], 0.0) for r in reads]

    av = [mm(jnp.concatenate([a_ak[i], a_rk[i]], axis=0), v_st[i]) for i in range(nu)]
    x = [a_h[i] + av[i][:M] for i in range(nu)]
    y_part = [r_h[i] + av[i][M:] for i in range(nu)]
    steps = int(math.log2(L))
    for s in range(steps):
        if s + 1 < steps:
            t = [mm(n_mat[i], jnp.concatenate([x[i], n_mat[i]], axis=1)) for i in range(nu)]
            x = [x[i] + t[i][:, :LANES] for i in range(nu)]
            n_mat = [t[i][:, LANES:] for i in range(nu)]
        else:
            x = [x[i] + mm(n_mat[i], x[i]) for i in range(nu)]
    y_st = [y_part[i] + mm(a_rb[i], x[i]) for i in range(nu)]
    ys = []
    for i in range(nu):
        y = y_st[i][:L]
        for j in range(1, nsub):
            y = y + y_st[i][j * L:(j + 1) * L]
        ys.append(y)
    s_new = [u["s_prev"] * u["p_last"]
             + _dot_tn(jnp.concatenate([x[i], v_st[i]], axis=0),
                       jnp.concatenate([_stack_mask(u["b_dec"], nsub), _stack_mask(u["k_dec"], nsub)], axis=0))
             for i, u in enumerate(units)]
    return ys, s_new


def _gdn_kernel(q_ref, k_ref, v_ref, z_ref, sm_ref, alog_ref, dtb_ref, nw_ref, o_ref, st_ref, *, L):
    @pl.when(pl.program_id(1) == 0)
    def _():
        st_ref[...] = jnp.zeros(st_ref.shape, F32)

    sm = sm_ref[...]
    g = -jnp.exp(alog_ref[...]) * _softplus(sm + dtb_ref[...])
    beta = _sigmoid(sm)
    g_cum = _dot_w3(_tri_incl(L), g)
    g_cum_t = jnp.concatenate([g_cum, jnp.zeros((LANES - L, LANES), F32)], axis=0).T
    row = lax.broadcasted_iota(jnp.int32, (L, L), 0)
    col = lax.broadcasted_iota(jnp.int32, (L, L), 1)
    nw = nw_ref[...]
    scale = GDN_HEAD_DIM ** -0.5
    units = []
    for h in range(GDN_HEADS):
        sl = slice(h * LANES, (h + 1) * LANES)
        q = q_ref[:, sl]
        k = k_ref[:, sl]
        qn = q * (lax.rsqrt(jnp.sum(q * q, axis=-1, keepdims=True) + L2_EPS) * scale)
        kn = k * lax.rsqrt(jnp.sum(k * k, axis=-1, keepdims=True) + L2_EPS)
        gc = g_cum[:, SMALL_A + h:SMALL_A + h + 1]
        gr = g_cum_t[SMALL_A + h:SMALL_A + h + 1, 0:L]
        gs = g[:, SMALL_A + h:SMALL_A + h + 1]
        bt = beta[:, SMALL_B + h:SMALL_B + h + 1]
        g_prev = gc - gs
        g_end = g_cum[L - 1:L, SMALL_A + h:SMALL_A + h + 1]
        b_vec = -(bt * jnp.exp(gs)) * kn
        k_vec = bt * kn
        e_end = jnp.exp(g_end - gc)
        units.append(dict(
            a_sc=kn, r_sc=qn, b_sc=b_vec, k_sc=k_vec,
            a_sr=kn * jnp.exp(g_prev), r_sr=qn * jnp.exp(gc),
            b_dec=b_vec * e_end, k_dec=k_vec * e_end,
            v=v_ref[:, sl], s_prev=st_ref[h], p_last=jnp.exp(g_end),
            dm_a=jnp.exp(jnp.where(col < row, g_prev - gr, -jnp.inf)),
            dm_r=jnp.exp(jnp.where(col <= row, gc - gr, -jnp.inf))))
    ys, s_new = _dplr_chunks(units, L=L, nsub=1)
    for h in range(GDN_HEADS):
        sl = slice(h * LANES, (h + 1) * LANES)
        st_ref[h] = s_new[h]
        y = ys[h]
        yn = y * lax.rsqrt(jnp.mean(y * y, axis=-1, keepdims=True) + NORM_EPS) * nw
        o_ref[:, sl] = (yn * _silu(z_ref[:, sl])).astype(o_ref.dtype)


def _gdn_branch(proj, a_log, dt_bias, norm_w, bsz, seq):
    L = DPLR_CHUNK
    nc = seq // L
    row = lambda b, c: b * nc + c
    vec = lambda w: pl.BlockSpec((1, w), lambda b, c: (0, 0))
    return pl.pallas_call(
        functools.partial(_gdn_kernel, L=L),
        grid=(bsz, nc),
        in_specs=[pl.BlockSpec((L, GDN_INNER), lambda b, c: (row(b, c), COL_Q // GDN_INNER)),
                  pl.BlockSpec((L, GDN_INNER), lambda b, c: (row(b, c), COL_K // GDN_INNER)),
                  pl.BlockSpec((L, GDN_INNER), lambda b, c: (row(b, c), COL_V // GDN_INNER)),
                  pl.BlockSpec((L, GDN_INNER), lambda b, c: (row(b, c), COL_ZGDN // GDN_INNER)),
                  pl.BlockSpec((L, LANES), lambda b, c: (row(b, c), COL_SMALL // LANES)),
                  vec(LANES), vec(LANES), vec(LANES)],
        out_specs=pl.BlockSpec((L, GDN_INNER), lambda b, c: (row(b, c), 0)),
        out_shape=jax.ShapeDtypeStruct((bsz * seq, GDN_INNER), BF16),
        scratch_shapes=[pltpu.VMEM((GDN_HEADS, LANES, LANES), F32)],
        compiler_params=_cparams("parallel", "arbitrary"),
    )(proj, proj, proj, proj, proj,
      _pad_lanes(a_log, SMALL_A), _pad_lanes(dt_bias, SMALL_A),
      norm_w.reshape(1, GDN_HEAD_DIM).astype(F32))


def _rwkv_kernel(rkv_ref, lora_ref, mu_ref, mul_ref, w0_ref, w2_ref, a0_ref, a2_ref, g2_ref,
                 kk_ref, ka_ref, rk_ref, lnw_ref, lnb_ref, o_ref, sh_ref, shl_ref, st_ref, *, L):
    @pl.when(pl.program_id(1) == 0)
    def _():
        st_ref[...] = jnp.zeros(st_ref.shape, F32)
        sh_ref[0:SUBLANES, :] = jnp.zeros((SUBLANES, sh_ref.shape[1]), F32)
        shl_ref[0:SUBLANES, :] = jnp.zeros((SUBLANES, shl_ref.shape[1]), F32)

    n = RWKV_INNER
    u = rkv_ref[...]
    ul = lora_ref[...]
    sh_ref[SUBLANES:SUBLANES + L, :] = u
    shl_ref[SUBLANES:SUBLANES + L, :] = ul
    u = u + (sh_ref[pl.ds(SUBLANES - 1, L), :] - u) * mu_ref[...]
    ul = ul + (shl_ref[pl.ds(SUBLANES - 1, L), :] - ul) * mul_ref[...]
    sh_ref[0:SUBLANES, :] = sh_ref[L:L + SUBLANES, :]
    shl_ref[0:SUBLANES, :] = shl_ref[L:L + SUBLANES, :]

    r, k, v = u[:, 0:n], u[:, n:2 * n], u[:, 2 * n:3 * n]
    lo = ul[:, 0:LANES]
    dg = ul[:, LANES:2 * LANES]
    w_log = -_softplus(-(w0_ref[...] + _dot_hl(jnp.tanh(lo), w2_ref[...]))) - 0.5
    lw = -jnp.exp(w_log)
    a_ic = _sigmoid(a0_ref[...] + _dot_hl(lo, a2_ref[...]))
    gate = _dot(_sigmoid(dg), g2_ref[...])
    g_cum = _dot_w3(_tri_incl(L), lw)
    kk_raw = k * kk_ref[...]
    k_mod = k * (1.0 + (a_ic - 1.0) * ka_ref[...])
    rkr = r * k_mod * rk_ref[...]
    ones_bd = _half_ones()
    inv_hd = 1.0 / RWKV_HEAD_DIM

    npair = RWKV_HEADS // 2

    def head_sums(t):
        st = jnp.concatenate([t[:, p * LANES:(p + 1) * LANES] for p in range(npair)], axis=0)
        sm = _dot(st, ones_bd)
        return jnp.concatenate([sm[p * L:(p + 1) * L] for p in range(npair)], axis=1)

    kk_all = kk_raw * lax.rsqrt(head_sums(kk_raw * kk_raw) + L2_EPS)
    bonus_dot = head_sums(rkr)
    units = []
    for p in range(npair):
        sl = slice(p * LANES, (p + 1) * LANES)
        kk = kk_all[:, sl]
        gc = g_cum[:, sl]
        g_end = gc[L - 1:L, :]
        e_neg = jnp.exp(-gc)
        e_end = jnp.exp(g_end - gc)
        b_vec = kk * a_ic[:, sl]
        k_vec = k_mod[:, sl]
        a_t = -kk * jnp.exp(gc - lw[:, sl])
        r_t = r[:, sl] * jnp.exp(gc)
        units.append(dict(
            a_sc=a_t, r_sc=r_t, b_sc=b_vec * e_neg, k_sc=k_vec * e_neg,
            a_sr=a_t, r_sr=r_t, b_dec=b_vec * e_end, k_dec=k_vec * e_end,
            v=v[:, sl], s_prev=st_ref[p], p_last=jnp.exp(g_end)))
    ys, s_new = _dplr_chunks(units, L=L, nsub=2)
    for p in range(npair):
        st_ref[p] = s_new[p]
    y = jnp.concatenate(ys, axis=1)
    yc = y - head_sums(y) * inv_hd
    var = head_sums(yc * yc) * inv_hd
    yn = yc * lax.rsqrt(var + RWKV_GN_EPS) * lnw_ref[...] + lnb_ref[...]
    o_ref[...] = ((yn + bonus_dot * v) * gate).astype(o_ref.dtype)


def _rwkv_branch(proj, mu, w0, w2, a0, a2, g2, k_k, k_a, r_k, ln_w, ln_b, bsz, seq):
    L = DPLR_CHUNK
    nc = seq // L
    n = RWKV_INNER
    row = lambda b, c: b * nc + c
    vec = lambda w: pl.BlockSpec((1, w), lambda b, c: (0, 0))
    mat = lambda: pl.BlockSpec((LANES, n), lambda b, c: (0, 0))
    w2p = jnp.concatenate([w2, jnp.zeros((LANES - RWKV_DECAY_LORA, n), F32)], axis=0)
    a2p = jnp.concatenate([jnp.zeros((RWKV_DECAY_LORA, n), F32), a2], axis=0)
    r1 = lambda t: t.reshape(1, -1).astype(F32)
    return pl.pallas_call(
        functools.partial(_rwkv_kernel, L=L),
        grid=(bsz, nc),
        in_specs=[pl.BlockSpec((L, 3 * n), lambda b, c: (row(b, c), COL_RKV // (3 * n))),
                  pl.BlockSpec((L, 256), lambda b, c: (row(b, c), COL_LORA // 256)),
                  vec(3 * n), vec(256), vec(n), mat(), vec(n), mat(), mat(),
                  vec(n), vec(n), vec(n), vec(n), vec(n)],
        out_specs=pl.BlockSpec((L, n), lambda b, c: (row(b, c), 0)),
        out_shape=jax.ShapeDtypeStruct((bsz * seq, n), BF16),
        scratch_shapes=[pltpu.VMEM((L + SUBLANES, 3 * n), F32),
                        pltpu.VMEM((L + SUBLANES, 256), F32),
                        pltpu.VMEM((RWKV_HEADS // 2, LANES, LANES), F32)],
        compiler_params=_cparams("parallel", "arbitrary"),
    )(proj, proj, r1(mu[:3 * n]), r1(mu[3 * n:]), r1(w0), w2p, r1(a0), a2p, g2.astype(F32),
      r1(k_k), r1(k_a), r1(r_k), r1(ln_w), r1(ln_b))


def _merge_kernel(x_ref, ys_ref, yg_ref, yr_ref, gate_ref, ps_ref, pg_ref, pr_ref, wo_ref, o_ref):
    d = D_MODEL
    m = _sigmoid(gate_ref[:, 0:d]) * jnp.dot(ys_ref[...], ps_ref[...], preferred_element_type=F32)
    m = m + _sigmoid(gate_ref[:, d:2 * d]) * jnp.dot(yg_ref[...], pg_ref[...], preferred_element_type=F32)
    m = m + _sigmoid(gate_ref[:, 2 * d:3 * d]) * jnp.dot(yr_ref[...], pr_ref[...], preferred_element_type=F32)
    o_ref[...] = x_ref[...] + jnp.dot(m.astype(BF16), wo_ref[...], preferred_element_type=F32)


def _merge(x, y_ssm, y_gdn, y_rwkv, proj, p_ssm, p_gdn, p_rwkv, w_out, tm=512):
    t, d = x.shape
    tok = lambda: pl.BlockSpec((tm, d), lambda i: (i, 0))
    wgt = lambda: pl.BlockSpec((d, d), lambda i: (0, 0))
    return pl.pallas_call(
        _merge_kernel,
        grid=(t // tm,),
        in_specs=[tok(), tok(), tok(), tok(),
                  pl.BlockSpec((tm, 3 * d), lambda i: (i, COL_GATE // (3 * d))),
                  wgt(), wgt(), wgt(), wgt()],
        out_specs=tok(),
        out_shape=jax.ShapeDtypeStruct((t, d), F32),
        compiler_params=_cparams("parallel"),
    )(x, y_ssm, y_gdn, y_rwkv, proj, p_ssm, p_gdn, p_rwkv, w_out)


def _ffn_kernel(x_ref, nw_ref, wg_ref, wu_ref, wd_ref, o_ref, h_ref, acc_ref):
    f = pl.program_id(1)

    @pl.when(f == 0)
    def _():
        x = x_ref[...]
        ms = jnp.mean(x * x, axis=-1, keepdims=True)
        h_ref[...] = (x * lax.rsqrt(ms + NORM_EPS) * nw_ref[...]).astype(BF16)
        acc_ref[...] = x

    h = h_ref[...]
    act = _silu(jnp.dot(h, wg_ref[...], preferred_element_type=F32)) * jnp.dot(
        h, wu_ref[...], preferred_element_type=F32)
    acc_ref[...] += jnp.dot(act.astype(BF16), wd_ref[...], preferred_element_type=F32)

    @pl.when(f == pl.num_programs(1) - 1)
    def _():
        o_ref[...] = acc_ref[...]


def _ffn_dense(x, norm_w, w_gate, w_up, w_down, tm=512, tf=1408):
    t, d = x.shape
    nf = w_gate.shape[1] // tf
    return pl.pallas_call(
        _ffn_kernel,
        grid=(t // tm, nf),
        in_specs=[pl.BlockSpec((tm, d), lambda i, f: (i, 0)),
                  pl.BlockSpec((1, d), lambda i, f: (0, 0)),
                  pl.BlockSpec((d, tf), lambda i, f: (0, f)),
                  pl.BlockSpec((d, tf), lambda i, f: (0, f)),
                  pl.BlockSpec((tf, d), lambda i, f: (f, 0))],
        out_specs=pl.BlockSpec((tm, d), lambda i, f: (i, 0)),
        out_shape=jax.ShapeDtypeStruct((t, d), F32),
        scratch_shapes=[pltpu.VMEM((tm, d), BF16), pltpu.VMEM((tm, d), F32)],
        compiler_params=_cparams("parallel", "arbitrary"),
    )(x, norm_w.reshape(1, d).astype(F32), w_gate, w_up, w_down)


MOE_TM = 1024
MOE_BLK = 128
MOE_ROW_CHUNK = 512


def _route_kernel(x_ref, nw_ref, rt_ref, h_ref, cmb_ref, cnt_ref):
    x = x_ref[...]
    lane = lax.broadcasted_iota(jnp.int32, cmb_ref.shape, 1)
    ms = jnp.mean(x * x, axis=-1, keepdims=True)
    hf = x * lax.rsqrt(ms + NORM_EPS) * nw_ref[...]
    h_ref[...] = hf.astype(BF16)
    logits = jnp.where(lane < N_EXPERTS, _dot_hl(hf, rt_ref[...]), -jnp.inf)
    m1 = jnp.max(logits, axis=-1, keepdims=True)
    i1 = jnp.min(jnp.where(logits == m1, lane, LANES), axis=-1, keepdims=True)
    rest = jnp.where(lane == i1, -jnp.inf, logits)
    m2 = jnp.max(rest, axis=-1, keepdims=True)
    i2 = jnp.min(jnp.where(rest == m2, lane, LANES), axis=-1, keepdims=True)
    e2 = jnp.exp(m2 - m1)
    p1 = 1.0 / (1.0 + e2)
    cmb = jnp.where(lane == i1, p1, 0.0) + jnp.where(lane == i2, e2 * p1, 0.0)
    cmb_ref[...] = cmb
    cnt_ref[0] = jnp.sum(jnp.where(cmb > 0.0, 1.0, 0.0), axis=0, keepdims=True).astype(jnp.int32)


def _moe_route(x, norm_w, router, tm):
    t, d = x.shape
    ne = router.shape[1]
    rt = jnp.concatenate([router.astype(F32), jnp.zeros((d, LANES - ne), F32)], axis=1)
    return pl.pallas_call(
        _route_kernel,
        grid=(t // tm,),
        in_specs=[pl.BlockSpec((tm, d), lambda i: (i, 0)),
                  pl.BlockSpec((1, d), lambda i: (0, 0)),
                  pl.BlockSpec((d, LANES), lambda i: (0, 0))],
        out_specs=[pl.BlockSpec((tm, d), lambda i: (i, 0)),
                   pl.BlockSpec((tm, LANES), lambda i: (i, 0)),
                   pl.BlockSpec((1, 1, LANES), lambda i: (i, 0, 0))],
        out_shape=[jax.ShapeDtypeStruct((t, d), BF16),
                   jax.ShapeDtypeStruct((t, LANES), F32),
                   jax.ShapeDtypeStruct((t // tm, 1, LANES), jnp.int32)],
        compiler_params=_cparams("parallel"),
    )(x, norm_w.reshape(1, d).astype(F32), rt)


def _moe_kernel(cnt_ref, h_ref, cmb_ref, wg_ref, wu_ref, wd_ref, o_ref, rank_ref, xc_ref, yc_ref, *, tm):
    i = pl.program_id(0)
    e = pl.program_id(1)
    f = pl.program_id(2)
    last_f = f == pl.num_programs(2) - 1
    lane = lax.broadcasted_iota(jnp.int32, (tm, LANES), 1)

    @pl.when((e == 0) & (f == 0))
    def _():
        o_ref[...] = jnp.zeros(o_ref.shape, F32)
        r = lax.broadcasted_iota(jnp.int32, (tm, tm), 0)
        c = lax.broadcasted_iota(jnp.int32, (tm, tm), 1)
        before = jnp.where(c < r, 1.0, 0.0).astype(BF16)
        sel = jnp.where(cmb_ref[...] > 0.0, 1.0, 0.0).astype(BF16)
        rank_ref[...] = jnp.dot(before, sel, preferred_element_type=F32)

    cnt = cnt_ref[i, e]
    nblk = (cnt + MOE_BLK - 1) // MOE_BLK
    for nb in range(1, tm // MOE_BLK + 1):
        rows = nb * MOE_BLK

        @pl.when(nblk == nb)
        def _(rows=rows):
            c_e = jnp.sum(jnp.where(lane == e, cmb_ref[...], 0.0), axis=-1, keepdims=True)
            r_e = jnp.sum(jnp.where(lane == e, rank_ref[...], 0.0), axis=-1, keepdims=True)
            key = jnp.where(c_e > 0.0, r_e, -1.0)
            slot = lax.broadcasted_iota(jnp.int32, (tm, rows), 1).astype(F32)
            onehot = jnp.where(key == slot, 1.0, 0.0).astype(BF16)

            @pl.when(f == 0)
            def _():
                xc_ref[0:rows, :] = _dot_tn(onehot, h_ref[...]).astype(BF16)
                yc_ref[0:rows, :] = jnp.zeros((rows, yc_ref.shape[1]), F32)

            for r0 in range(0, rows, MOE_ROW_CHUNK):
                rs = slice(r0, min(rows, r0 + MOE_ROW_CHUNK))
                xc = xc_ref[rs, :]
                act = _silu(jnp.dot(xc, wg_ref[0], preferred_element_type=F32)) * jnp.dot(
                    xc, wu_ref[0], preferred_element_type=F32)
                yc_ref[rs, :] += jnp.dot(act.astype(BF16), wd_ref[0], preferred_element_type=F32)

            @pl.when(last_f)
            def _():
                o_ref[...] += c_e * jnp.dot(onehot, yc_ref[0:rows, :].astype(BF16),
                                            preferred_element_type=F32)


def _add_norm_kernel(x_ref, y_ref, w_ref, o_ref, *, final_norm):
    y = x_ref[...] + y_ref[...]
    if final_norm:
        y = y * lax.rsqrt(jnp.mean(y * y, axis=-1, keepdims=True) + NORM_EPS) * w_ref[...]
    o_ref[...] = y


def _ffn_moe(x, norm_w, router, w_gate, w_up, w_down, final_w, tm=MOE_TM, tf=1408, tr=512):
    t, d = x.shape
    ne, _, ff = w_gate.shape
    nf = ff // tf
    h, cmb, cnt = _moe_route(x, norm_w, router, tm)
    grid_spec = pltpu.PrefetchScalarGridSpec(
        num_scalar_prefetch=1,
        grid=(t // tm, ne, nf),
        in_specs=[pl.BlockSpec((tm, d), lambda i, e, f, c: (i, 0)),
                  pl.BlockSpec((tm, LANES), lambda i, e, f, c: (i, 0)),
                  pl.BlockSpec((1, d, tf), lambda i, e, f, c: (e, 0, f)),
                  pl.BlockSpec((1, d, tf), lambda i, e, f, c: (e, 0, f)),
                  pl.BlockSpec((1, tf, d), lambda i, e, f, c: (e, f, 0))],
        out_specs=pl.BlockSpec((tm, d), lambda i, e, f, c: (i, 0)),
        scratch_shapes=[pltpu.VMEM((tm, LANES), F32), pltpu.VMEM((tm, d), BF16), pltpu.VMEM((tm, d), F32)])
    y = pl.pallas_call(
        functools.partial(_moe_kernel, tm=tm),
        grid_spec=grid_spec,
        out_shape=jax.ShapeDtypeStruct((t, d), F32),
        compiler_params=_cparams("parallel", "arbitrary", "arbitrary"),
    )(cnt.reshape(t // tm, LANES), h, cmb, w_gate, w_up, w_down)
    final_norm = final_w is not None
    fw = (final_w if final_norm else jnp.ones((d,), F32)).reshape(1, d).astype(F32)
    tr = min(tr, t)
    return pl.pallas_call(
        functools.partial(_add_norm_kernel, final_norm=final_norm),
        grid=(t // tr,),
        in_specs=[pl.BlockSpec((tr, d), lambda i: (i, 0)),
                  pl.BlockSpec((tr, d), lambda i: (i, 0)),
                  pl.BlockSpec((1, d), lambda i: (0, 0))],
        out_specs=pl.BlockSpec((tr, d), lambda i: (i, 0)),
        out_shape=jax.ShapeDtypeStruct((t, d), F32),
        compiler_params=_cparams("parallel"),
    )(x, y, fw)


def _reorder_w_in(w):
    d = w.shape[0]
    parts = [w[:, 0:1024], w[:, 1536:4608], w[:, 4608:5632], w[:, 5648:6672], w[:, 6688:9760],
             w[:, 10016:13088], w[:, 1024:1536], w[:, 9760:10016], w[:, 5632:5648], w[:, 6672:6688],
             jnp.zeros((d, N_PROJ - COL_SMALL - 32), w.dtype)]
    return jnp.concatenate(parts, axis=1).astype(BF16)


def _token_mixer_layer(x, i, bsz, seq, p):
    h = _rmsnorm(x, p["attn_norm_w"][i], BF16)
    proj = _inproj(h, _reorder_w_in(p["w_in"][i]), p["conv_w"][i], p["conv_b"][i], seq, tm=1024, tn=1024)
    y_ssm = _ssd_branch(proj, p["ssm_a_log"][i], p["ssm_dt_bias"][i], p["ssm_d"][i],
                        p["ssm_norm_w"][i], bsz, seq)
    y_gdn = _gdn_branch(proj, p["gdn_a_log"][i], p["gdn_dt_bias"][i], p["gdn_norm_w"][i],
                        bsz, seq)
    y_rwkv = _rwkv_branch(proj, p["rwkv_mu"][i], p["rwkv_w0"][i], p["rwkv_w2"][i], p["rwkv_a0"][i],
                          p["rwkv_a2"][i], p["rwkv_g2"][i], p["rwkv_k_k"][i], p["rwkv_k_a"][i],
                          p["rwkv_r_k"][i], p["rwkv_ln_w"][i], p["rwkv_ln_b"][i], bsz, seq)
    return _merge(x, y_ssm, y_gdn, y_rwkv, proj, p["proj_ssm"][i].astype(BF16),
                  p["proj_gdn"][i].astype(BF16), p["proj_rwkv"][i].astype(BF16),
                  p["w_out"][i].astype(BF16))


def kernel(x, attn_norm_w, w_in, conv_w, conv_b, ssm_a_log, ssm_dt_bias, ssm_d, ssm_norm_w, gdn_a_log, gdn_dt_bias, gdn_norm_w, rwkv_mu, rwkv_w0, rwkv_w2, rwkv_a0, rwkv_a2, rwkv_g2, rwkv_k_k, rwkv_k_a, rwkv_r_k, rwkv_ln_w, rwkv_ln_b, proj_ssm, proj_gdn, proj_rwkv, w_out, ffn_norm_w, dense_w_gate, dense_w_up, dense_w_down, moe_router, moe_w_gate, moe_w_up, moe_w_down, final_norm_w):
    p = dict(attn_norm_w=attn_norm_w, w_in=w_in, conv_w=conv_w, conv_b=conv_b, ssm_a_log=ssm_a_log,
             ssm_dt_bias=ssm_dt_bias, ssm_d=ssm_d, ssm_norm_w=ssm_norm_w, gdn_a_log=gdn_a_log,
             gdn_dt_bias=gdn_dt_bias, gdn_norm_w=gdn_norm_w, rwkv_mu=rwkv_mu, rwkv_w0=rwkv_w0,
             rwkv_w2=rwkv_w2, rwkv_a0=rwkv_a0, rwkv_a2=rwkv_a2, rwkv_g2=rwkv_g2, rwkv_k_k=rwkv_k_k,
             rwkv_k_a=rwkv_k_a, rwkv_r_k=rwkv_r_k, rwkv_ln_w=rwkv_ln_w, rwkv_ln_b=rwkv_ln_b,
             proj_ssm=proj_ssm, proj_gdn=proj_gdn, proj_rwkv=proj_rwkv, w_out=w_out)
    bsz, seq, d = x.shape
    depth = attn_norm_w.shape[0]
    xt = x.reshape(bsz * seq, d)
    for i in range(depth):
        xt = _token_mixer_layer(xt, i, bsz, seq, p)
        j = i // 2
        last = i == depth - 1
        if i % 2 == 0:
            xt = _ffn_dense(xt, ffn_norm_w[i], dense_w_gate[j].astype(BF16), dense_w_up[j].astype(BF16),
                            dense_w_down[j].astype(BF16))
            if last:
                xt = _rmsnorm(xt, final_norm_w, F32)
        else:
            xt = _ffn_moe(xt, ffn_norm_w[i], moe_router[j], moe_w_gate[j].astype(BF16),
                          moe_w_up[j].astype(BF16), moe_w_down[j].astype(BF16),
                          final_norm_w if last else None)
    return xt.reshape(bsz, seq, d)
```

```python
import functools
import math

import numpy as np
import jax
import jax.numpy as jnp
from jax import lax
from jax.experimental import pallas as pl
from jax.experimental.pallas import tpu as pltpu

F32 = jnp.float32
BF16 = jnp.bfloat16

LANES = 128
SUBLANES = 8
VMEM_LIMIT_BYTES = 56 * 1024 * 1024

D_MODEL = 1024
SSM_HEADS, SSM_HEAD_DIM, SSM_GROUPS, SSM_STATE = 16, 64, 2, 128
SSM_INNER = SSM_HEADS * SSM_HEAD_DIM
GDN_HEADS, GDN_HEAD_DIM = 8, 128
GDN_INNER = GDN_HEADS * GDN_HEAD_DIM
RWKV_HEADS, RWKV_HEAD_DIM = 16, 64
RWKV_INNER = RWKV_HEADS * RWKV_HEAD_DIM
RWKV_DECAY_LORA, RWKV_ICLR_LORA, RWKV_GATE_LORA = 64, 64, 128
RWKV_GN_EPS = 64e-5
CONV_K = 4
FFN_DIM = 2816
N_EXPERTS = 8
NORM_EPS = 1e-6
L2_EPS = 1e-6

COL_XS, COL_Q, COL_K, COL_V = 0, 1024, 2048, 3072
COL_ZSSM, COL_ZGDN = 4096, 5120
COL_RKV = 6144
COL_GATE = 9216
COL_BS, COL_CS = 12288, 12544
COL_LORA = 12800
COL_SMALL = 13056
N_PROJ = 13312
SMALL_DT, SMALL_A, SMALL_B = 0, 16, 24

SSD_CHUNK = 128
DPLR_CHUNK = 64
SEQ_PER_STEP = 2


def _cparams(*sem):
    return pltpu.CompilerParams(dimension_semantics=sem, vmem_limit_bytes=VMEM_LIMIT_BYTES)


def _dot(a, b):
    return jnp.dot(a.astype(BF16), b.astype(BF16), preferred_element_type=F32)


def _dot_nt(a, b):
    return lax.dot_general(a.astype(BF16), b.astype(BF16), (((1,), (1,)), ((), ())),
                           preferred_element_type=F32)


def _dot_tn(a, b):
    return lax.dot_general(a.astype(BF16), b.astype(BF16), (((0,), (0,)), ((), ())),
                           preferred_element_type=F32)


def _split3(x):
    hi = x.astype(BF16)
    r1 = x - hi.astype(F32)
    mid = r1.astype(BF16)
    lo = (r1 - mid.astype(F32)).astype(BF16)
    return hi, mid, lo


def _dot_x3(x, w):
    hi, mid, lo = _split3(x)
    return _dot(hi, w) + _dot(mid, w) + _dot(lo, w)


def _dot_w3(w, x):
    hi, mid, lo = _split3(x)
    return _dot(w, hi) + _dot(w, mid) + _dot(w, lo)


def _dot_hl(a, b):
    ah = a.astype(BF16)
    al = (a - ah.astype(F32)).astype(BF16)
    bh = b.astype(BF16)
    bl = (b - bh.astype(F32)).astype(BF16)
    return _dot(ah, bh) + _dot(ah, bl) + _dot(al, bh)


def _sigmoid(x):
    return 1.0 / (1.0 + jnp.exp(-x))


def _silu(x):
    return x * _sigmoid(x)


def _softplus(x):
    return jnp.maximum(x, 0.0) + jnp.log(1.0 + jnp.exp(-jnp.abs(x)))


def _tri_incl(n):
    r = lax.broadcasted_iota(jnp.int32, (n, n), 0)
    c = lax.broadcasted_iota(jnp.int32, (n, n), 1)
    return jnp.where(r >= c, 1.0, 0.0).astype(BF16)


def _half_ones():
    r = lax.broadcasted_iota(jnp.int32, (LANES, LANES), 0)
    c = lax.broadcasted_iota(jnp.int32, (LANES, LANES), 1)
    return jnp.where((r // 64) == (c // 64), 1.0, 0.0).astype(BF16)


def _rmsnorm_kernel(x_ref, w_ref, o_ref):
    x = x_ref[...]
    ms = jnp.mean(x * x, axis=-1, keepdims=True)
    o_ref[...] = (x * lax.rsqrt(ms + NORM_EPS) * w_ref[...]).astype(o_ref.dtype)


def _rmsnorm(x, w, out_dtype, tm=512):
    t, d = x.shape
    return pl.pallas_call(
        _rmsnorm_kernel,
        grid=(t // tm,),
        in_specs=[pl.BlockSpec((tm, d), lambda i: (i, 0)),
                  pl.BlockSpec((1, d), lambda i: (0, 0))],
        out_specs=pl.BlockSpec((tm, d), lambda i: (i, 0)),
        out_shape=jax.ShapeDtypeStruct((t, d), out_dtype),
        compiler_params=_cparams("parallel"),
    )(x, w.reshape(1, d))


def _inproj_kernel(a_ref, w_ref, cw_ref, cb_ref, cm_ref, o_ref, sh_ref, *, tm, tn, tiles_per_seq):
    j = pl.program_id(0)
    i = pl.program_id(1)
    lo = j * tn
    has_conv = (lo < COL_V + 1024) | ((lo + tn > COL_BS) & (lo < COL_LORA))
    acc = jnp.dot(a_ref[...], w_ref[...], preferred_element_type=F32)

    @pl.when(jnp.logical_not(has_conv))
    def _():
        o_ref[...] = acc

    @pl.when(has_conv)
    def _():
        @pl.when(i % tiles_per_seq == 0)
        def _():
            sh_ref[0:SUBLANES, :] = jnp.zeros((SUBLANES, tn), F32)

        sh_ref[SUBLANES:SUBLANES + tm, :] = acc
        c = acc * cw_ref[CONV_K - 1:CONV_K, :] + cb_ref[...]
        for k in range(1, CONV_K):
            c = c + sh_ref[pl.ds(SUBLANES - k, tm), :] * cw_ref[CONV_K - 1 - k:CONV_K - k, :]
        o_ref[...] = jnp.where(cm_ref[...] > 0.0, _silu(c), acc)
        sh_ref[0:SUBLANES, :] = sh_ref[tm:tm + SUBLANES, :]


def _conv_cols(c, fill):
    r = c.shape[0]
    z = lambda w: jnp.full((r, w), fill, F32)
    return jnp.concatenate([c[:, 0:1024], c[:, 1536:4608], z(COL_BS - (COL_V + 1024)), c[:, 1024:1536],
                            z(N_PROJ - COL_LORA)], axis=1).astype(F32)


def _inproj(a, w, conv_w, conv_b, seq, tm, tn):
    m, k = a.shape
    n = w.shape[1]
    cw = _conv_cols(conv_w, 0.0)
    cb = _conv_cols(conv_b.reshape(1, -1), 0.0)
    cm = _conv_cols(jnp.ones((1, conv_w.shape[1]), F32), 0.0)
    return pl.pallas_call(
        functools.partial(_inproj_kernel, tm=tm, tn=tn, tiles_per_seq=seq // tm),
        grid=(n // tn, m // tm),
        in_specs=[pl.BlockSpec((tm, k), lambda j, i: (i, 0)),
                  pl.BlockSpec((k, tn), lambda j, i: (0, j)),
                  pl.BlockSpec((CONV_K, tn), lambda j, i: (0, j)),
                  pl.BlockSpec((1, tn), lambda j, i: (0, j)),
                  pl.BlockSpec((1, tn), lambda j, i: (0, j))],
        out_specs=pl.BlockSpec((tm, tn), lambda j, i: (i, j)),
        out_shape=jax.ShapeDtypeStruct((m, n), F32),
        scratch_shapes=[pltpu.VMEM((tm + SUBLANES, tn), F32)],
        compiler_params=_cparams("parallel", "arbitrary"),
    )(a, w, cw, cb, cm)


def _ssd_kernel(xs_ref, bs_ref, cs_ref, z_ref, sm_ref, alog_ref, dtb_ref, dsk_ref, nw_ref, ex_ref,
                o_ref, st_ref, *, L):
    @pl.when(pl.program_id(1) == 0)
    def _():
        st_ref[...] = jnp.zeros(st_ref.shape, F32)

    hd = SSM_HEAD_DIM
    gw = SSM_INNER // SSM_GROUPS
    dt = _softplus(sm_ref[...] + dtb_ref[...])
    la = -jnp.exp(alog_ref[...]) * dt
    g_cum = _dot_w3(_tri_incl(L), la)
    g_cum_t = g_cum.T
    ex = ex_ref[...]
    dt_e = _dot_x3(dt, ex)
    g_e = _dot_x3(g_cum, ex)
    g_last = g_e[L - 1:L, :]
    e_g = jnp.exp(g_e)
    e_dec = jnp.exp(g_last - g_e)
    e_last = jnp.exp(g_last)

    xs = xs_ref[...]
    xdt = xs * dt_e
    xdec = xdt * e_dec
    row = lax.broadcasted_iota(jnp.int32, (L, L), 0)
    col = lax.broadcasted_iota(jnp.int32, (L, L), 1)
    causal = row >= col
    lane = lax.broadcasted_iota(jnp.int32, (L, LANES), 1)
    first_half = lane < hd

    y_parts = []
    for g in range(SSM_GROUPS):
        b_g = bs_ref[:, g * SSM_STATE:(g + 1) * SSM_STATE]
        c_g = cs_ref[:, g * SSM_STATE:(g + 1) * SSM_STATE]
        cb = _dot_nt(c_g, b_g)
        st_g = st_ref[:, g * gw:(g + 1) * gw]
        y_off = _dot(c_g, st_g) * e_g[:, g * gw:(g + 1) * gw]
        st_ref[:, g * gw:(g + 1) * gw] = (st_g * e_last[:, g * gw:(g + 1) * gw]
                                          + _dot_tn(b_g, xdec[:, g * gw:(g + 1) * gw]))
        heads_per_group = SSM_HEADS // SSM_GROUPS
        for p in range(heads_per_group // 2):
            h0 = g * heads_per_group + 2 * p
            base = h0 * hd
            sc = []
            for h in (h0, h0 + 1):
                diff = g_cum[:, h:h + 1] - g_cum_t[h:h + 1, :]
                sc.append(cb * jnp.exp(jnp.where(causal, diff, -jnp.inf)))
            yd = _dot(jnp.concatenate(sc, axis=0), xdt[:, base:base + LANES])
            y_parts.append(jnp.where(first_half, yd[:L], yd[L:])
                           + y_off[:, base - g * gw:base - g * gw + LANES])
    y = jnp.concatenate(y_parts, axis=1) + dsk_ref[...] * xs
    y = y * _silu(z_ref[...])
    outs = []
    for g in range(SSM_GROUPS):
        seg = y[:, g * gw:(g + 1) * gw]
        outs.append(seg * lax.rsqrt(jnp.mean(seg * seg, axis=-1, keepdims=True) + NORM_EPS))
    o_ref[...] = (jnp.concatenate(outs, axis=1) * nw_ref[...]).astype(o_ref.dtype)


def _pad_lanes(v, offset=0, width=LANES):
    out = jnp.zeros((1, width), F32)
    return lax.dynamic_update_slice(out, v.reshape(1, -1).astype(F32), (0, offset))


def _ssd_branch(proj, a_log, dt_bias, d_skip, norm_w, bsz, seq):
    L = SSD_CHUNK
    nc = seq // L
    ex = np.zeros((LANES, SSM_INNER), np.float32)
    for h in range(SSM_HEADS):
        ex[h, h * SSM_HEAD_DIM:(h + 1) * SSM_HEAD_DIM] = 1.0
    row = lambda b, c: b * nc + c
    vec = lambda w: pl.BlockSpec((1, w), lambda b, c: (0, 0))
    return pl.pallas_call(
        functools.partial(_ssd_kernel, L=L),
        grid=(bsz, nc),
        in_specs=[pl.BlockSpec((L, SSM_INNER), lambda b, c: (row(b, c), COL_XS // SSM_INNER)),
                  pl.BlockSpec((L, 256), lambda b, c: (row(b, c), COL_BS // 256)),
                  pl.BlockSpec((L, 256), lambda b, c: (row(b, c), COL_CS // 256)),
                  pl.BlockSpec((L, SSM_INNER), lambda b, c: (row(b, c), COL_ZSSM // SSM_INNER)),
                  pl.BlockSpec((L, LANES), lambda b, c: (row(b, c), COL_SMALL // LANES)),
                  vec(LANES), vec(LANES), vec(SSM_INNER), vec(SSM_INNER),
                  pl.BlockSpec((LANES, SSM_INNER), lambda b, c: (0, 0))],
        out_specs=pl.BlockSpec((L, SSM_INNER), lambda b, c: (row(b, c), 0)),
        out_shape=jax.ShapeDtypeStruct((bsz * seq, SSM_INNER), BF16),
        scratch_shapes=[pltpu.VMEM((SSM_STATE, SSM_INNER), F32)],
        compiler_params=_cparams("parallel", "arbitrary"),
    )(proj, proj, proj, proj, proj,
      _pad_lanes(a_log, SMALL_DT), _pad_lanes(dt_bias, SMALL_DT),
      jnp.repeat(d_skip.astype(F32), SSM_HEAD_DIM).reshape(1, SSM_INNER),
      norm_w.reshape(1, SSM_INNER).astype(F32), jnp.asarray(ex, BF16))


def _stack_tile(x, nsub):
    return x if nsub == 1 else jnp.concatenate([x] * nsub, axis=0)


def _stack_mask(x, nsub):
    if nsub == 1:
        return x
    lane = lax.broadcasted_iota(jnp.int32, x.shape, 1) // (LANES // nsub)
    return jnp.concatenate([jnp.where(lane == j, x, 0.0) for j in range(nsub)], axis=0)


def _dplr_chunks(units, *, L, nsub, mm=_dot):
    M = nsub * L
    nu = len(units)
    ri = lax.broadcasted_iota(jnp.int32, (M, M), 0)
    ci = lax.broadcasted_iota(jnp.int32, (M, M), 1)
    if nsub == 1:
        strict, lower = ci < ri, ci <= ri
        own = None
    else:
        same = (ri // L) == (ci // L)
        strict = same & ((ci % L) < (ri % L))
        lower = same & ((ci % L) <= (ri % L))
        rr = lax.broadcasted_iota(jnp.int32, (M, LANES), 0) // L
        ll = lax.broadcasted_iota(jnp.int32, (M, LANES), 1) // (LANES // nsub)
        own = rr == ll

    rows = [jnp.concatenate([_stack_tile(u["a_sc"], nsub), _stack_tile(u["r_sc"], nsub)], axis=0)
            for u in units]
    b_st = [_stack_mask(u["b_sc"], nsub) for u in units]
    k_st = [_stack_mask(u["k_sc"], nsub) for u in units]
    v_st = [_stack_mask(u["v"], nsub) for u in units]
    if M % LANES == 0:
        sc = [_dot_nt(rows[i], jnp.concatenate([b_st[i], k_st[i]], axis=0)) for i in range(nu)]
        sc_b = [t[:, :M] for t in sc]
        sc_k = [t[:, M:] for t in sc]
    else:
        sc_b = [_dot_nt(rows[i], b_st[i]) for i in range(nu)]
        sc_k = [_dot_nt(rows[i], k_st[i]) for i in range(nu)]
    reads = [_dot_nt(jnp.concatenate([_stack_tile(u["a_sr"], nsub), _stack_tile(u["r_sr"], nsub)],
                                     axis=0), u["s_prev"]) for u in units]
    n_mat, a_ak, a_rb, a_rk = [], [], [], []
    for i, u in enumerate(units):
        if u.get("dm_a") is None:
            n_mat.append(jnp.where(strict, sc_b[i][:M], 0.0))
            a_ak.append(jnp.where(strict, sc_k[i][:M], 0.0))
            a_rb.append(jnp.where(lower, sc_b[i][M:], 0.0))
            a_rk.append(jnp.where(lower, sc_k[i][M:], 0.0))
        else:
            n_mat.append(sc_b[i][:M] * u["dm_a"])
            a_ak.append(sc_k[i][:M] * u["dm_a"])
            a_rb.append(sc_b[i][M:] * u["dm_r"])
            a_rk.append(sc_k[i][M:] * u["dm_r"])
    if own is None:
        a_h = [r[:M] for r in reads]
        r_h = [r[M:] for r in reads]
    else:
        a_h = [jnp.where(own, r[:M], 0.0) for r in reads]
        r_h = [jnp.where(own, r[M:], 0.0) for r in reads]

    av = [mm(jnp.concatenate([a_ak[i], a_rk[i]], axis=0), v_st[i]) for i in range(nu)]
    x = [a_h[i] + av[i][:M] for i in range(nu)]
    y_part = [r_h[i] + av[i][M:] for i in range(nu)]
    steps = int(math.log2(L))
    for s in range(steps):
        if s + 1 < steps:
            t = [mm(n_mat[i], jnp.concatenate([x[i], n_mat[i]], axis=1)) for i in range(nu)]
            x = [x[i] + t[i][:, :LANES] for i in range(nu)]
            n_mat = [t[i][:, LANES:] for i in range(nu)]
        else:
            x = [x[i] + mm(n_mat[i], x[i]) for i in range(nu)]
    y_st = [y_part[i] + mm(a_rb[i], x[i]) for i in range(nu)]
    ys = []
    for i in range(nu):
        y = y_st[i][:L]
        for j in range(1, nsub):
            y = y + y_st[i][j * L:(j + 1) * L]
        ys.append(y)
    s_new = [u["s_prev"] * u["p_last"]
             + _dot_tn(jnp.concatenate([x[i], v_st[i]], axis=0),
                       jnp.concatenate([_stack_mask(u["b_dec"], nsub), _stack_mask(u["k_dec"], nsub)], axis=0))
             for i, u in enumerate(units)]
    return ys, s_new


def _gdn_kernel(q_ref, k_ref, v_ref, z_ref, sm_ref, alog_ref, dtb_ref, nw_ref, o_ref, st_ref, *, L, nseq):
    @pl.when(pl.program_id(1) == 0)
    def _():
        st_ref[...] = jnp.zeros(st_ref.shape, F32)

    row = lax.broadcasted_iota(jnp.int32, (L, L), 0)
    col = lax.broadcasted_iota(jnp.int32, (L, L), 1)
    nw = nw_ref[...]
    scale = GDN_HEAD_DIM ** -0.5
    units = []
    for s in range(nseq):
        sm = sm_ref[s]
        g = -jnp.exp(alog_ref[...]) * _softplus(sm + dtb_ref[...])
        beta = _sigmoid(sm)
        g_cum = _dot_w3(_tri_incl(L), g)
        g_cum_t = jnp.concatenate([g_cum, jnp.zeros((LANES - L, LANES), F32)], axis=0).T
        for h in range(GDN_HEADS):
            sl = slice(h * LANES, (h + 1) * LANES)
            q = q_ref[s, :, sl]
            k = k_ref[s, :, sl]
            qn = q * (lax.rsqrt(jnp.sum(q * q, axis=-1, keepdims=True) + L2_EPS) * scale)
            kn = k * lax.rsqrt(jnp.sum(k * k, axis=-1, keepdims=True) + L2_EPS)
            gc = g_cum[:, SMALL_A + h:SMALL_A + h + 1]
            gr = g_cum_t[SMALL_A + h:SMALL_A + h + 1, 0:L]
            gs = g[:, SMALL_A + h:SMALL_A + h + 1]
            bt = beta[:, SMALL_B + h:SMALL_B + h + 1]
            g_prev = gc - gs
            g_end = g_cum[L - 1:L, SMALL_A + h:SMALL_A + h + 1]
            b_vec = -(bt * jnp.exp(gs)) * kn
            k_vec = bt * kn
            e_end = jnp.exp(g_end - gc)
            units.append(dict(
                a_sc=kn, r_sc=qn, b_sc=b_vec, k_sc=k_vec,
                a_sr=kn * jnp.exp(g_prev), r_sr=qn * jnp.exp(gc),
                b_dec=b_vec * e_end, k_dec=k_vec * e_end,
                v=v_ref[s, :, sl], s_prev=st_ref[s, h], p_last=jnp.exp(g_end),
                dm_a=jnp.exp(jnp.where(col < row, g_prev - gr, -jnp.inf)),
                dm_r=jnp.exp(jnp.where(col <= row, gc - gr, -jnp.inf))))
    ys, s_new = _dplr_chunks(units, L=L, nsub=1)
    for s in range(nseq):
        for h in range(GDN_HEADS):
            sl = slice(h * LANES, (h + 1) * LANES)
            st_ref[s, h] = s_new[s * GDN_HEADS + h]
            y = ys[s * GDN_HEADS + h]
            yn = y * lax.rsqrt(jnp.mean(y * y, axis=-1, keepdims=True) + NORM_EPS) * nw
            o_ref[s, :, sl] = (yn * _silu(z_ref[s, :, sl])).astype(o_ref.dtype)


def _gdn_branch(proj, a_log, dt_bias, norm_w, bsz, seq):
    L = DPLR_CHUNK
    nseq = SEQ_PER_STEP if bsz % SEQ_PER_STEP == 0 else 1
    proj3 = proj.reshape(bsz, seq, N_PROJ)
    vec = lambda w: pl.BlockSpec((1, w), lambda b, c: (0, 0))
    tok = lambda width, col: pl.BlockSpec((nseq, L, width), lambda b, c: (b, c, col // width))
    out = pl.pallas_call(
        functools.partial(_gdn_kernel, L=L, nseq=nseq),
        grid=(bsz // nseq, seq // L),
        in_specs=[tok(GDN_INNER, COL_Q), tok(GDN_INNER, COL_K), tok(GDN_INNER, COL_V),
                  tok(GDN_INNER, COL_ZGDN), tok(LANES, COL_SMALL),
                  vec(LANES), vec(LANES), vec(LANES)],
        out_specs=pl.BlockSpec((nseq, L, GDN_INNER), lambda b, c: (b, c, 0)),
        out_shape=jax.ShapeDtypeStruct((bsz, seq, GDN_INNER), BF16),
        scratch_shapes=[pltpu.VMEM((nseq, GDN_HEADS, LANES, LANES), F32)],
        compiler_params=_cparams("parallel", "arbitrary"),
    )(proj3, proj3, proj3, proj3, proj3,
      _pad_lanes(a_log, SMALL_A), _pad_lanes(dt_bias, SMALL_A),
      norm_w.reshape(1, GDN_HEAD_DIM).astype(F32))
    return out.reshape(bsz * seq, GDN_INNER)


def _rwkv_kernel(rkv_ref, lora_ref, mu_ref, mul_ref, w0_ref, w2_ref, a0_ref, a2_ref, g2_ref,
                 kk_ref, ka_ref, rk_ref, lnw_ref, lnb_ref, o_ref, sh_ref, shl_ref, st_ref, *, L, nseq):
    @pl.when(pl.program_id(1) == 0)
    def _():
        st_ref[...] = jnp.zeros(st_ref.shape, F32)
        for s in range(nseq):
            sh_ref[s, 0:SUBLANES, :] = jnp.zeros((SUBLANES, sh_ref.shape[2]), F32)
            shl_ref[s, 0:SUBLANES, :] = jnp.zeros((SUBLANES, shl_ref.shape[2]), F32)

    n = RWKV_INNER
    npair = RWKV_HEADS // 2
    ones_bd = _half_ones()
    inv_hd = 1.0 / RWKV_HEAD_DIM

    def head_sums(t):
        st = jnp.concatenate([t[:, p * LANES:(p + 1) * LANES] for p in range(npair)], axis=0)
        sm = _dot(st, ones_bd)
        return jnp.concatenate([sm[p * L:(p + 1) * L] for p in range(npair)], axis=1)

    units, tails = [], []
    for s in range(nseq):
        u = rkv_ref[s]
        ul = lora_ref[s]
        sh_ref[s, SUBLANES:SUBLANES + L, :] = u
        shl_ref[s, SUBLANES:SUBLANES + L, :] = ul
        u = u + (sh_ref[s, pl.ds(SUBLANES - 1, L), :] - u) * mu_ref[...]
        ul = ul + (shl_ref[s, pl.ds(SUBLANES - 1, L), :] - ul) * mul_ref[...]
        sh_ref[s, 0:SUBLANES, :] = sh_ref[s, L:L + SUBLANES, :]
        shl_ref[s, 0:SUBLANES, :] = shl_ref[s, L:L + SUBLANES, :]

        r, k, v = u[:, 0:n], u[:, n:2 * n], u[:, 2 * n:3 * n]
        lo = ul[:, 0:LANES]
        dg = ul[:, LANES:2 * LANES]
        w_log = -_softplus(-(w0_ref[...] + _dot_hl(jnp.tanh(lo), w2_ref[...]))) - 0.5
        lw = -jnp.exp(w_log)
        a_ic = _sigmoid(a0_ref[...] + _dot_hl(lo, a2_ref[...]))
        gate = _dot(_sigmoid(dg), g2_ref[...])
        g_cum = _dot_w3(_tri_incl(L), lw)
        kk_raw = k * kk_ref[...]
        k_mod = k * (1.0 + (a_ic - 1.0) * ka_ref[...])
        kk_all = kk_raw * lax.rsqrt(head_sums(kk_raw * kk_raw) + L2_EPS)
        tails.append((head_sums(r * k_mod * rk_ref[...]) * v, gate))
        for p in range(npair):
            sl = slice(p * LANES, (p + 1) * LANES)
            kk = kk_all[:, sl]
            gc = g_cum[:, sl]
            g_end = gc[L - 1:L, :]
            e_neg = jnp.exp(-gc)
            e_end = jnp.exp(g_end - gc)
            b_vec = kk * a_ic[:, sl]
            k_vec = k_mod[:, sl]
            a_t = -kk * jnp.exp(gc - lw[:, sl])
            r_t = r[:, sl] * jnp.exp(gc)
            units.append(dict(
                a_sc=a_t, r_sc=r_t, b_sc=b_vec * e_neg, k_sc=k_vec * e_neg,
                a_sr=a_t, r_sr=r_t, b_dec=b_vec * e_end, k_dec=k_vec * e_end,
                v=v[:, sl], s_prev=st_ref[s, p], p_last=jnp.exp(g_end)))
    ys, s_new = _dplr_chunks(units, L=L, nsub=2)
    for s in range(nseq):
        for p in range(npair):
            st_ref[s, p] = s_new[s * npair + p]
        y = jnp.concatenate(ys[s * npair:(s + 1) * npair], axis=1)
        yc = y - head_sums(y) * inv_hd
        var = head_sums(yc * yc) * inv_hd
        yn = yc * lax.rsqrt(var + RWKV_GN_EPS) * lnw_ref[...] + lnb_ref[...]
        bonus, gate = tails[s]
        o_ref[s] = ((yn + bonus) * gate).astype(o_ref.dtype)


def _rwkv_branch(proj, mu, w0, w2, a0, a2, g2, k_k, k_a, r_k, ln_w, ln_b, bsz, seq):
    L = DPLR_CHUNK
    n = RWKV_INNER
    nseq = SEQ_PER_STEP if bsz % SEQ_PER_STEP == 0 else 1
    proj3 = proj.reshape(bsz, seq, N_PROJ)
    vec = lambda w: pl.BlockSpec((1, w), lambda b, c: (0, 0))
    mat = lambda: pl.BlockSpec((LANES, n), lambda b, c: (0, 0))
    tok = lambda width, col: pl.BlockSpec((nseq, L, width), lambda b, c: (b, c, col // width))
    w2p = jnp.concatenate([w2, jnp.zeros((LANES - RWKV_DECAY_LORA, n), F32)], axis=0)
    a2p = jnp.concatenate([jnp.zeros((RWKV_DECAY_LORA, n), F32), a2], axis=0)
    r1 = lambda t: t.reshape(1, -1).astype(F32)
    out = pl.pallas_call(
        functools.partial(_rwkv_kernel, L=L, nseq=nseq),
        grid=(bsz // nseq, seq // L),
        in_specs=[tok(3 * n, COL_RKV), tok(256, COL_LORA),
                  vec(3 * n), vec(256), vec(n), mat(), vec(n), mat(), mat(),
                  vec(n), vec(n), vec(n), vec(n), vec(n)],
        out_specs=pl.BlockSpec((nseq, L, n), lambda b, c: (b, c, 0)),
        out_shape=jax.ShapeDtypeStruct((bsz, seq, n), BF16),
        scratch_shapes=[pltpu.VMEM((nseq, L + SUBLANES, 3 * n), F32),
                        pltpu.VMEM((nseq, L + SUBLANES, 256), F32),
                        pltpu.VMEM((nseq, RWKV_HEADS // 2, LANES, LANES), F32)],
        compiler_params=_cparams("parallel", "arbitrary"),
    )(proj3, proj3, r1(mu[:3 * n]), r1(mu[3 * n:]), r1(w0), w2p, r1(a0), a2p, g2.astype(F32),
      r1(k_k), r1(k_a), r1(r_k), r1(ln_w), r1(ln_b))
    return out.reshape(bsz * seq, n)


def _merge_kernel(x_ref, ys_ref, yg_ref, yr_ref, gate_ref, ps_ref, pg_ref, pr_ref, wo_ref, o_ref):
    d = D_MODEL
    m = _sigmoid(gate_ref[:, 0:d]) * jnp.dot(ys_ref[...], ps_ref[...], preferred_element_type=F32)
    m = m + _sigmoid(gate_ref[:, d:2 * d]) * jnp.dot(yg_ref[...], pg_ref[...], preferred_element_type=F32)
    m = m + _sigmoid(gate_ref[:, 2 * d:3 * d]) * jnp.dot(yr_ref[...], pr_ref[...], preferred_element_type=F32)
    o_ref[...] = x_ref[...] + jnp.dot(m.astype(BF16), wo_ref[...], preferred_element_type=F32)


def _merge(x, y_ssm, y_gdn, y_rwkv, proj, p_ssm, p_gdn, p_rwkv, w_out, tm=512):
    t, d = x.shape
    tok = lambda: pl.BlockSpec((tm, d), lambda i: (i, 0))
    wgt = lambda: pl.BlockSpec((d, d), lambda i: (0, 0))
    return pl.pallas_call(
        _merge_kernel,
        grid=(t // tm,),
        in_specs=[tok(), tok(), tok(), tok(),
                  pl.BlockSpec((tm, 3 * d), lambda i: (i, COL_GATE // (3 * d))),
                  wgt(), wgt(), wgt(), wgt()],
        out_specs=tok(),
        out_shape=jax.ShapeDtypeStruct((t, d), F32),
        compiler_params=_cparams("parallel"),
    )(x, y_ssm, y_gdn, y_rwkv, proj, p_ssm, p_gdn, p_rwkv, w_out)


def _ffn_kernel(x_ref, nw_ref, wg_ref, wu_ref, wd_ref, o_ref, h_ref, acc_ref):
    f = pl.program_id(1)

    @pl.when(f == 0)
    def _():
        x = x_ref[...]
        ms = jnp.mean(x * x, axis=-1, keepdims=True)
        h_ref[...] = (x * lax.rsqrt(ms + NORM_EPS) * nw_ref[...]).astype(BF16)
        acc_ref[...] = x

    h = h_ref[...]
    act = _silu(jnp.dot(h, wg_ref[...], preferred_element_type=F32)) * jnp.dot(
        h, wu_ref[...], preferred_element_type=F32)
    acc_ref[...] += jnp.dot(act.astype(BF16), wd_ref[...], preferred_element_type=F32)

    @pl.when(f == pl.num_programs(1) - 1)
    def _():
        o_ref[...] = acc_ref[...]


def _ffn_dense(x, norm_w, w_gate, w_up, w_down, tm=512, tf=1408):
    t, d = x.shape
    nf = w_gate.shape[1] // tf
    return pl.pallas_call(
        _ffn_kernel,
        grid=(t // tm, nf),
        in_specs=[pl.BlockSpec((tm, d), lambda i, f: (i, 0)),
                  pl.BlockSpec((1, d), lambda i, f: (0, 0)),
                  pl.BlockSpec((d, tf), lambda i, f: (0, f)),
                  pl.BlockSpec((d, tf), lambda i, f: (0, f)),
                  pl.BlockSpec((tf, d), lambda i, f: (f, 0))],
        out_specs=pl.BlockSpec((tm, d), lambda i, f: (i, 0)),
        out_shape=jax.ShapeDtypeStruct((t, d), F32),
        scratch_shapes=[pltpu.VMEM((tm, d), BF16), pltpu.VMEM((tm, d), F32)],
        compiler_params=_cparams("parallel", "arbitrary"),
    )(x, norm_w.reshape(1, d).astype(F32), w_gate, w_up, w_down)


MOE_TM = 1024
MOE_BLK = 128
MOE_FIRST_ROWS = 320


def _route_kernel(x_ref, nw_ref, rt_ref, h_ref, cmb_ref, cnt_ref):
    x = x_ref[...]
    lane = lax.broadcasted_iota(jnp.int32, cmb_ref.shape, 1)
    ms = jnp.mean(x * x, axis=-1, keepdims=True)
    hf = x * lax.rsqrt(ms + NORM_EPS) * nw_ref[...]
    h_ref[...] = hf.astype(BF16)
    logits = jnp.where(lane < N_EXPERTS, _dot_hl(hf, rt_ref[...]), -jnp.inf)
    m1 = jnp.max(logits, axis=-1, keepdims=True)
    i1 = jnp.min(jnp.where(logits == m1, lane, LANES), axis=-1, keepdims=True)
    rest = jnp.where(lane == i1, -jnp.inf, logits)
    m2 = jnp.max(rest, axis=-1, keepdims=True)
    i2 = jnp.min(jnp.where(rest == m2, lane, LANES), axis=-1, keepdims=True)
    e2 = jnp.exp(m2 - m1)
    p1 = 1.0 / (1.0 + e2)
    cmb = jnp.where(lane == i1, p1, 0.0) + jnp.where(lane == i2, e2 * p1, 0.0)
    cmb_ref[...] = cmb
    cnt_ref[0] = jnp.sum(jnp.where(cmb > 0.0, 1.0, 0.0), axis=0, keepdims=True).astype(jnp.int32)


def _moe_route(x, norm_w, router, tm):
    t, d = x.shape
    ne = router.shape[1]
    rt = jnp.concatenate([router.astype(F32), jnp.zeros((d, LANES - ne), F32)], axis=1)
    return pl.pallas_call(
        _route_kernel,
        grid=(t // tm,),
        in_specs=[pl.BlockSpec((tm, d), lambda i: (i, 0)),
                  pl.BlockSpec((1, d), lambda i: (0, 0)),
                  pl.BlockSpec((d, LANES), lambda i: (0, 0))],
        out_specs=[pl.BlockSpec((tm, d), lambda i: (i, 0)),
                   pl.BlockSpec((tm, LANES), lambda i: (i, 0)),
                   pl.BlockSpec((1, 1, LANES), lambda i: (i, 0, 0))],
        out_shape=[jax.ShapeDtypeStruct((t, d), BF16),
                   jax.ShapeDtypeStruct((t, LANES), F32),
                   jax.ShapeDtypeStruct((t // tm, 1, LANES), jnp.int32)],
        compiler_params=_cparams("parallel"),
    )(x, norm_w.reshape(1, d).astype(F32), rt)


def _moe_kernel(cnt_ref, h_ref, cmb_ref, wg_ref, wu_ref, wd_ref, o_ref, rank_ref, xc_ref, yc_ref, *, tm):
    i = pl.program_id(0)
    e = pl.program_id(1)
    f = pl.program_id(2)
    last_f = f == pl.num_programs(2) - 1
    lane = lax.broadcasted_iota(jnp.int32, (tm, LANES), 1)

    @pl.when((e == 0) & (f == 0))
    def _():
        o_ref[...] = jnp.zeros(o_ref.shape, F32)
        r = lax.broadcasted_iota(jnp.int32, (tm, tm), 0)
        c = lax.broadcasted_iota(jnp.int32, (tm, tm), 1)
        before = jnp.where(c < r, 1.0, 0.0).astype(BF16)
        sel = jnp.where(cmb_ref[...] > 0.0, 1.0, 0.0).astype(BF16)
        rank_ref[...] = jnp.dot(before, sel, preferred_element_type=F32)

    cnt = cnt_ref[i, e]

    def process(off, rows):
        c_e = jnp.sum(jnp.where(lane == e, cmb_ref[...], 0.0), axis=-1, keepdims=True)
        r_e = jnp.sum(jnp.where(lane == e, rank_ref[...], 0.0), axis=-1, keepdims=True)
        key = jnp.where(c_e > 0.0, r_e, -1.0) - jnp.asarray(off, F32)
        slot = lax.broadcasted_iota(jnp.int32, (tm, rows), 1).astype(F32)
        onehot = jnp.where(key == slot, 1.0, 0.0).astype(BF16)
        rs = pl.ds(off, rows)

        @pl.when(f == 0)
        def _():
            xc_ref[rs, :] = _dot_tn(onehot, h_ref[...]).astype(BF16)
            yc_ref[rs, :] = jnp.zeros((rows, yc_ref.shape[1]), F32)

        xc = xc_ref[rs, :]
        act = _silu(jnp.dot(xc, wg_ref[0], preferred_element_type=F32)) * jnp.dot(
            xc, wu_ref[0], preferred_element_type=F32)
        yc_ref[rs, :] += jnp.dot(act.astype(BF16), wd_ref[0], preferred_element_type=F32)

        @pl.when(last_f)
        def _():
            o_ref[...] += c_e * jnp.dot(onehot, yc_ref[rs, :].astype(BF16), preferred_element_type=F32)

    @pl.when(cnt > 0)
    def _():
        process(0, MOE_FIRST_ROWS)

    def extra(b, carry):
        process(pl.multiple_of(MOE_FIRST_ROWS + b * MOE_BLK, 64), MOE_BLK)
        return carry

    lax.fori_loop(0, (jnp.maximum(cnt - MOE_FIRST_ROWS, 0) + MOE_BLK - 1) // MOE_BLK, extra, 0)


def _add_norm_kernel(x_ref, y_ref, w_ref, o_ref, *, final_norm):
    y = x_ref[...] + y_ref[...]
    if final_norm:
        y = y * lax.rsqrt(jnp.mean(y * y, axis=-1, keepdims=True) + NORM_EPS) * w_ref[...]
    o_ref[...] = y


def _ffn_moe(x, norm_w, router, w_gate, w_up, w_down, final_w, tm=MOE_TM, tf=1408, tr=512):
    t, d = x.shape
    ne, _, ff = w_gate.shape
    nf = ff // tf
    h, cmb, cnt = _moe_route(x, norm_w, router, tm)
    cap = MOE_FIRST_ROWS + -(-max(tm - MOE_FIRST_ROWS, 0) // MOE_BLK) * MOE_BLK
    grid_spec = pltpu.PrefetchScalarGridSpec(
        num_scalar_prefetch=1,
        grid=(t // tm, ne, nf),
        in_specs=[pl.BlockSpec((tm, d), lambda i, e, f, c: (i, 0)),
                  pl.BlockSpec((tm, LANES), lambda i, e, f, c: (i, 0)),
                  pl.BlockSpec((1, d, tf), lambda i, e, f, c: (e, 0, f)),
                  pl.BlockSpec((1, d, tf), lambda i, e, f, c: (e, 0, f)),
                  pl.BlockSpec((1, tf, d), lambda i, e, f, c: (e, f, 0))],
        out_specs=pl.BlockSpec((tm, d), lambda i, e, f, c: (i, 0)),
        scratch_shapes=[pltpu.VMEM((tm, LANES), F32), pltpu.VMEM((cap, d), BF16), pltpu.VMEM((cap, d), F32)])
    y = pl.pallas_call(
        functools.partial(_moe_kernel, tm=tm),
        grid_spec=grid_spec,
        out_shape=jax.ShapeDtypeStruct((t, d), F32),
        compiler_params=_cparams("parallel", "arbitrary", "arbitrary"),
    )(cnt.reshape(t // tm, LANES), h, cmb, w_gate, w_up, w_down)
    final_norm = final_w is not None
    fw = (final_w if final_norm else jnp.ones((d,), F32)).reshape(1, d).astype(F32)
    tr = min(tr, t)
    return pl.pallas_call(
        functools.partial(_add_norm_kernel, final_norm=final_norm),
        grid=(t // tr,),
        in_specs=[pl.BlockSpec((tr, d), lambda i: (i, 0)),
                  pl.BlockSpec((tr, d), lambda i: (i, 0)),
                  pl.BlockSpec((1, d), lambda i: (0, 0))],
        out_specs=pl.BlockSpec((tr, d), lambda i: (i, 0)),
        out_shape=jax.ShapeDtypeStruct((t, d), F32),
        compiler_params=_cparams("parallel"),
    )(x, y, fw)


def _reorder_w_in(w):
    d = w.shape[0]
    parts = [w[:, 0:1024], w[:, 1536:4608], w[:, 4608:5632], w[:, 5648:6672], w[:, 6688:9760],
             w[:, 10016:13088], w[:, 1024:1536], w[:, 9760:10016], w[:, 5632:5648], w[:, 6672:6688],
             jnp.zeros((d, N_PROJ - COL_SMALL - 32), w.dtype)]
    return jnp.concatenate(parts, axis=1).astype(BF16)


def _token_mixer_layer(x, i, bsz, seq, p):
    h = _rmsnorm(x, p["attn_norm_w"][i], BF16)
    proj = _inproj(h, _reorder_w_in(p["w_in"][i]), p["conv_w"][i], p["conv_b"][i], seq, tm=1024, tn=1024)
    y_ssm = _ssd_branch(proj, p["ssm_a_log"][i], p["ssm_dt_bias"][i], p["ssm_d"][i],
                        p["ssm_norm_w"][i], bsz, seq)
    y_gdn = _gdn_branch(proj, p["gdn_a_log"][i], p["gdn_dt_bias"][i], p["gdn_norm_w"][i],
                        bsz, seq)
    y_rwkv = _rwkv_branch(proj, p["rwkv_mu"][i], p["rwkv_w0"][i], p["rwkv_w2"][i], p["rwkv_a0"][i],
                          p["rwkv_a2"][i], p["rwkv_g2"][i], p["rwkv_k_k"][i], p["rwkv_k_a"][i],
                          p["rwkv_r_k"][i], p["rwkv_ln_w"][i], p["rwkv_ln_b"][i], bsz, seq)
    return _merge(x, y_ssm, y_gdn, y_rwkv, proj, p["proj_ssm"][i].astype(BF16),
                  p["proj_gdn"][i].astype(BF16), p["proj_rwkv"][i].astype(BF16),
                  p["w_out"][i].astype(BF16))


def kernel(x, attn_norm_w, w_in, conv_w, conv_b, ssm_a_log, ssm_dt_bias, ssm_d, ssm_norm_w, gdn_a_log, gdn_dt_bias, gdn_norm_w, rwkv_mu, rwkv_w0, rwkv_w2, rwkv_a0, rwkv_a2, rwkv_g2, rwkv_k_k, rwkv_k_a, rwkv_r_k, rwkv_ln_w, rwkv_ln_b, proj_ssm, proj_gdn, proj_rwkv, w_out, ffn_norm_w, dense_w_gate, dense_w_up, dense_w_down, moe_router, moe_w_gate, moe_w_up, moe_w_down, final_norm_w):
    p = dict(attn_norm_w=attn_norm_w, w_in=w_in, conv_w=conv_w, conv_b=conv_b, ssm_a_log=ssm_a_log,
             ssm_dt_bias=ssm_dt_bias, ssm_d=ssm_d, ssm_norm_w=ssm_norm_w, gdn_a_log=gdn_a_log,
             gdn_dt_bias=gdn_dt_bias, gdn_norm_w=gdn_norm_w, rwkv_mu=rwkv_mu, rwkv_w0=rwkv_w0,
             rwkv_w2=rwkv_w2, rwkv_a0=rwkv_a0, rwkv_a2=rwkv_a2, rwkv_g2=rwkv_g2, rwkv_k_k=rwkv_k_k,
             rwkv_k_a=rwkv_k_a, rwkv_r_k=rwkv_r_k, rwkv_ln_w=rwkv_ln_w, rwkv_ln_b=rwkv_ln_b,
             proj_ssm=proj_ssm, proj_gdn=proj_gdn, proj_rwkv=proj_rwkv, w_out=w_out)
    bsz, seq, d = x.shape
    depth = attn_norm_w.shape[0]
    xt = x.reshape(bsz * seq, d)
    for i in range(depth):
        xt = _token_mixer_layer(xt, i, bsz, seq, p)
        j = i // 2
        last = i == depth - 1
        if i % 2 == 0:
            xt = _ffn_dense(xt, ffn_norm_w[i], dense_w_gate[j].astype(BF16), dense_w_up[j].astype(BF16),
                            dense_w_down[j].astype(BF16))
            if last:
                xt = _rmsnorm(xt, final_norm_w, F32)
        else:
            xt = _ffn_moe(xt, ffn_norm_w[i], moe_router[j], moe_w_gate[j].astype(BF16),
                          moe_w_up[j].astype(BF16), moe_w_down[j].astype(BF16),
                          final_norm_w if last else None)
    return xt.reshape(bsz, seq, d)
```

```python
import functools
import math

import numpy as np
import jax
import jax.numpy as jnp
from jax import lax
from jax.experimental import pallas as pl
from jax.experimental.pallas import tpu as pltpu

F32 = jnp.float32
BF16 = jnp.bfloat16

LANES = 128
SUBLANES = 8
VMEM_LIMIT_BYTES = 56 * 1024 * 1024

D_MODEL = 1024
SSM_HEADS, SSM_HEAD_DIM, SSM_GROUPS, SSM_STATE = 16, 64, 2, 128
SSM_INNER = SSM_HEADS * SSM_HEAD_DIM
GDN_HEADS, GDN_HEAD_DIM = 8, 128
GDN_INNER = GDN_HEADS * GDN_HEAD_DIM
RWKV_HEADS, RWKV_HEAD_DIM = 16, 64
RWKV_INNER = RWKV_HEADS * RWKV_HEAD_DIM
RWKV_DECAY_LORA, RWKV_ICLR_LORA, RWKV_GATE_LORA = 64, 64, 128
RWKV_GN_EPS = 64e-5
CONV_K = 4
FFN_DIM = 2816
N_EXPERTS = 8
NORM_EPS = 1e-6
L2_EPS = 1e-6

PC_XS, PC_Q, PC_K, PC_V = 0, 1024, 2048, 3072
PC_BS, PC_CS = 4096, 4352
PC_CONV_END = 4608
PC_LORA = 4608
PC_SMALL = 4864
N_PC = 5120
PN_RKV = 0
PN_GATE = 3072
PN_ZSSM, PN_ZGDN = 6144, 7168
N_PN = 8192
SMALL_DT, SMALL_A, SMALL_B = 0, 16, 24

SSD_CHUNK = 128
DPLR_CHUNK = 64
SEQ_PER_STEP = 2
INPROJ_ROW_CHUNK = 256


def _cparams(*sem):
    return pltpu.CompilerParams(dimension_semantics=sem, vmem_limit_bytes=VMEM_LIMIT_BYTES)


def _dot(a, b):
    return jnp.dot(a.astype(BF16), b.astype(BF16), preferred_element_type=F32)


def _dot_nt(a, b):
    return lax.dot_general(a.astype(BF16), b.astype(BF16), (((1,), (1,)), ((), ())),
                           preferred_element_type=F32)


def _dot_tn(a, b):
    return lax.dot_general(a.astype(BF16), b.astype(BF16), (((0,), (0,)), ((), ())),
                           preferred_element_type=F32)


def _split3(x):
    hi = x.astype(BF16)
    r1 = x - hi.astype(F32)
    mid = r1.astype(BF16)
    lo = (r1 - mid.astype(F32)).astype(BF16)
    return hi, mid, lo


def _dot_x3(x, w):
    hi, mid, lo = _split3(x)
    return _dot(hi, w) + _dot(mid, w) + _dot(lo, w)


def _dot_w3(w, x):
    hi, mid, lo = _split3(x)
    return _dot(w, hi) + _dot(w, mid) + _dot(w, lo)


def _dot_hl(a, b):
    ah = a.astype(BF16)
    al = (a - ah.astype(F32)).astype(BF16)
    bh = b.astype(BF16)
    bl = (b - bh.astype(F32)).astype(BF16)
    return _dot(ah, bh) + _dot(ah, bl) + _dot(al, bh)


def _sigmoid(x):
    return 0.5 * jnp.tanh(0.5 * x) + 0.5


def _silu(x):
    return x * _sigmoid(x)


def _softplus(x):
    return jnp.maximum(x, 0.0) + jnp.log(1.0 + jnp.exp(-jnp.abs(x)))


def _tri_incl(n):
    r = lax.broadcasted_iota(jnp.int32, (n, n), 0)
    c = lax.broadcasted_iota(jnp.int32, (n, n), 1)
    return jnp.where(r >= c, 1.0, 0.0).astype(BF16)


def _half_ones():
    r = lax.broadcasted_iota(jnp.int32, (LANES, LANES), 0)
    c = lax.broadcasted_iota(jnp.int32, (LANES, LANES), 1)
    return jnp.where((r // 64) == (c // 64), 1.0, 0.0).astype(BF16)


def _rmsnorm_kernel(x_ref, w_ref, o_ref):
    x = x_ref[...]
    ms = jnp.mean(x * x, axis=-1, keepdims=True)
    o_ref[...] = (x * lax.rsqrt(ms + NORM_EPS) * w_ref[...]).astype(o_ref.dtype)


def _rmsnorm(x, w, out_dtype, tm=512):
    t, d = x.shape
    return pl.pallas_call(
        _rmsnorm_kernel,
        grid=(t // tm,),
        in_specs=[pl.BlockSpec((tm, d), lambda i: (i, 0)),
                  pl.BlockSpec((1, d), lambda i: (0, 0))],
        out_specs=pl.BlockSpec((tm, d), lambda i: (i, 0)),
        out_shape=jax.ShapeDtypeStruct((t, d), out_dtype),
        compiler_params=_cparams("parallel"),
    )(x, w.reshape(1, d))


def _inproj_kernel(a_ref, wc_ref, wn_ref, cw_ref, cb_ref, cm_ref, oc_ref, on_ref, sh_ref, *,
                   tm, tiles_per_seq):
    i = pl.program_id(1)
    tc = oc_ref.shape[1]

    @pl.when(i % tiles_per_seq == 0)
    def _():
        sh_ref[0:SUBLANES, :] = jnp.zeros((SUBLANES, tc), F32)

    rc = min(tm, INPROJ_ROW_CHUNK)
    starts = list(range(0, tm, rc))
    acc = jnp.dot(a_ref[0:rc, :], wc_ref[...], preferred_element_type=F32)
    for n, r0 in enumerate(starts):
        sh_ref[SUBLANES + r0:SUBLANES + r0 + rc, :] = acc
        on_ref[r0:r0 + rc, :] = jnp.dot(a_ref[r0:r0 + rc, :], wn_ref[...], preferred_element_type=F32)
        if n + 1 < len(starts):
            acc_next = jnp.dot(a_ref[r0 + rc:r0 + 2 * rc, :], wc_ref[...], preferred_element_type=F32)
        c = acc * cw_ref[CONV_K - 1:CONV_K, :] + cb_ref[...]
        for k in range(1, CONV_K):
            c = c + sh_ref[pl.ds(SUBLANES + r0 - k, rc), :] * cw_ref[CONV_K - 1 - k:CONV_K - k, :]
        oc_ref[r0:r0 + rc, :] = jnp.where(cm_ref[...] > 0.0, _silu(c), acc)
        if n + 1 < len(starts):
            acc = acc_next
    sh_ref[0:SUBLANES, :] = sh_ref[tm:tm + SUBLANES, :]


def _conv_cols(c):
    r = c.shape[0]
    return jnp.concatenate([c[:, 0:1024], c[:, 1536:4608], c[:, 1024:1536],
                            jnp.zeros((r, N_PC - PC_CONV_END), F32)], axis=1).astype(F32)


def _inproj(a, wc, wn, conv_w, conv_b, seq, tm, nsteps=8):
    m, k = a.shape
    tc, tn = N_PC // nsteps, N_PN // nsteps
    cw = _conv_cols(conv_w)
    cb = _conv_cols(conv_b.reshape(1, -1))
    cm = _conv_cols(jnp.ones((1, conv_w.shape[1]), F32))
    return pl.pallas_call(
        functools.partial(_inproj_kernel, tm=tm, tiles_per_seq=seq // tm),
        grid=(nsteps, m // tm),
        in_specs=[pl.BlockSpec((tm, k), lambda j, i: (i, 0)),
                  pl.BlockSpec((k, tc), lambda j, i: (0, j)),
                  pl.BlockSpec((k, tn), lambda j, i: (0, j)),
                  pl.BlockSpec((CONV_K, tc), lambda j, i: (0, j)),
                  pl.BlockSpec((1, tc), lambda j, i: (0, j)),
                  pl.BlockSpec((1, tc), lambda j, i: (0, j))],
        out_specs=[pl.BlockSpec((tm, tc), lambda j, i: (i, j)),
                   pl.BlockSpec((tm, tn), lambda j, i: (i, j))],
        out_shape=[jax.ShapeDtypeStruct((m, N_PC), F32), jax.ShapeDtypeStruct((m, N_PN), F32)],
        scratch_shapes=[pltpu.VMEM((tm + SUBLANES, tc), F32)],
        compiler_params=_cparams("parallel", "arbitrary"),
    )(a, wc, wn, cw, cb, cm)


def _ssd_kernel(xs_ref, bs_ref, cs_ref, z_ref, sm_ref, alog_ref, dtb_ref, dsk_ref, nw_ref, ex_ref,
                o_ref, st_ref, *, L):
    @pl.when(pl.program_id(1) == 0)
    def _():
        st_ref[...] = jnp.zeros(st_ref.shape, F32)

    hd = SSM_HEAD_DIM
    gw = SSM_INNER // SSM_GROUPS
    dt = _softplus(sm_ref[...] + dtb_ref[...])
    la = -jnp.exp(alog_ref[...]) * dt
    g_cum = _dot_w3(_tri_incl(L), la)
    g_cum_t = g_cum.T
    ex = ex_ref[...]
    dt_e = _dot_x3(dt, ex)
    g_e = _dot_x3(g_cum, ex)
    g_last = g_e[L - 1:L, :]
    e_g = jnp.exp(g_e)
    e_dec = jnp.exp(g_last - g_e)
    e_last = jnp.exp(g_last)

    xs = xs_ref[...]
    xdt = xs * dt_e
    xdec = xdt * e_dec
    row = lax.broadcasted_iota(jnp.int32, (L, L), 0)
    col = lax.broadcasted_iota(jnp.int32, (L, L), 1)
    causal = row >= col
    lane = lax.broadcasted_iota(jnp.int32, (L, LANES), 1)
    first_half = lane < hd

    y_parts = []
    for g in range(SSM_GROUPS):
        b_g = bs_ref[:, g * SSM_STATE:(g + 1) * SSM_STATE]
        c_g = cs_ref[:, g * SSM_STATE:(g + 1) * SSM_STATE]
        cb = _dot_nt(c_g, b_g)
        st_g = st_ref[:, g * gw:(g + 1) * gw]
        y_off = _dot(c_g, st_g) * e_g[:, g * gw:(g + 1) * gw]
        st_ref[:, g * gw:(g + 1) * gw] = (st_g * e_last[:, g * gw:(g + 1) * gw]
                                          + _dot_tn(b_g, xdec[:, g * gw:(g + 1) * gw]))
        heads_per_group = SSM_HEADS // SSM_GROUPS
        for p in range(heads_per_group // 2):
            h0 = g * heads_per_group + 2 * p
            base = h0 * hd
            sc = []
            for h in (h0, h0 + 1):
                diff = g_cum[:, h:h + 1] - g_cum_t[h:h + 1, :]
                sc.append(cb * jnp.exp(jnp.where(causal, diff, -jnp.inf)))
            yd = _dot(jnp.concatenate(sc, axis=0), xdt[:, base:base + LANES])
            y_parts.append(jnp.where(first_half, yd[:L], yd[L:])
                           + y_off[:, base - g * gw:base - g * gw + LANES])
    y = jnp.concatenate(y_parts, axis=1) + dsk_ref[...] * xs
    y = y * _silu(z_ref[...])
    outs = []
    for g in range(SSM_GROUPS):
        seg = y[:, g * gw:(g + 1) * gw]
        outs.append(seg * lax.rsqrt(jnp.mean(seg * seg, axis=-1, keepdims=True) + NORM_EPS))
    o_ref[...] = (jnp.concatenate(outs, axis=1) * nw_ref[...]).astype(o_ref.dtype)


def _pad_lanes(v, offset=0, width=LANES):
    out = jnp.zeros((1, width), F32)
    return lax.dynamic_update_slice(out, v.reshape(1, -1).astype(F32), (0, offset))


def _ssd_branch(pc, pn, a_log, dt_bias, d_skip, norm_w, bsz, seq):
    L = SSD_CHUNK
    nc = seq // L
    ex = np.zeros((LANES, SSM_INNER), np.float32)
    for h in range(SSM_HEADS):
        ex[h, h * SSM_HEAD_DIM:(h + 1) * SSM_HEAD_DIM] = 1.0
    row = lambda b, c: b * nc + c
    vec = lambda w: pl.BlockSpec((1, w), lambda b, c: (0, 0))
    return pl.pallas_call(
        functools.partial(_ssd_kernel, L=L),
        grid=(bsz, nc),
        in_specs=[pl.BlockSpec((L, SSM_INNER), lambda b, c: (row(b, c), PC_XS // SSM_INNER)),
                  pl.BlockSpec((L, 256), lambda b, c: (row(b, c), PC_BS // 256)),
                  pl.BlockSpec((L, 256), lambda b, c: (row(b, c), PC_CS // 256)),
                  pl.BlockSpec((L, SSM_INNER), lambda b, c: (row(b, c), PN_ZSSM // SSM_INNER)),
                  pl.BlockSpec((L, LANES), lambda b, c: (row(b, c), PC_SMALL // LANES)),
                  vec(LANES), vec(LANES), vec(SSM_INNER), vec(SSM_INNER),
                  pl.BlockSpec((LANES, SSM_INNER), lambda b, c: (0, 0))],
        out_specs=pl.BlockSpec((L, SSM_INNER), lambda b, c: (row(b, c), 0)),
        out_shape=jax.ShapeDtypeStruct((bsz * seq, SSM_INNER), BF16),
        scratch_shapes=[pltpu.VMEM((SSM_STATE, SSM_INNER), F32)],
        compiler_params=_cparams("parallel", "arbitrary"),
    )(pc, pc, pc, pn, pc,
      _pad_lanes(a_log, SMALL_DT), _pad_lanes(dt_bias, SMALL_DT),
      jnp.repeat(d_skip.astype(F32), SSM_HEAD_DIM).reshape(1, SSM_INNER),
      norm_w.reshape(1, SSM_INNER).astype(F32), jnp.asarray(ex, BF16))


def _stack_tile(x, nsub):
    return x if nsub == 1 else jnp.concatenate([x] * nsub, axis=0)


def _stack_mask(x, nsub):
    if nsub == 1:
        return x
    lane = lax.broadcasted_iota(jnp.int32, x.shape, 1) // (LANES // nsub)
    return jnp.concatenate([jnp.where(lane == j, x, 0.0) for j in range(nsub)], axis=0)


def _dplr_chunks(units, *, L, nsub, mm=_dot):
    M = nsub * L
    nu = len(units)
    ri = lax.broadcasted_iota(jnp.int32, (M, M), 0)
    ci = lax.broadcasted_iota(jnp.int32, (M, M), 1)
    if nsub == 1:
        strict, lower = ci < ri, ci <= ri
        own = None
    else:
        same = (ri // L) == (ci // L)
        strict = same & ((ci % L) < (ri % L))
        lower = same & ((ci % L) <= (ri % L))
        rr = lax.broadcasted_iota(jnp.int32, (M, LANES), 0) // L
        ll = lax.broadcasted_iota(jnp.int32, (M, LANES), 1) // (LANES // nsub)
        own = rr == ll

    rows = [jnp.concatenate([_stack_tile(u["a_sc"], nsub), _stack_tile(u["r_sc"], nsub)], axis=0)
            for u in units]
    b_st = [_stack_mask(u["b_sc"], nsub) for u in units]
    k_st = [_stack_mask(u["k_sc"], nsub) for u in units]
    v_st = [_stack_mask(u["v"], nsub) for u in units]
    if M % LANES == 0:
        sc = [_dot_nt(rows[i], jnp.concatenate([b_st[i], k_st[i]], axis=0)) for i in range(nu)]
        sc_b = [t[:, :M] for t in sc]
        sc_k = [t[:, M:] for t in sc]
    else:
        sc_b = [_dot_nt(rows[i], b_st[i]) for i in range(nu)]
        sc_k = [_dot_nt(rows[i], k_st[i]) for i in range(nu)]
    reads = [_dot_nt(jnp.concatenate([_stack_tile(u["a_sr"], nsub), _stack_tile(u["r_sr"], nsub)],
                                     axis=0), u["s_prev"]) for u in units]
    n_mat, a_ak, a_rb, a_rk = [], [], [], []
    for i, u in enumerate(units):
        if u.get("dm_a") is None:
            n_mat.append(jnp.where(strict, sc_b[i][:M], 0.0))
            a_ak.append(jnp.where(strict, sc_k[i][:M], 0.0))
            a_rb.append(jnp.where(lower, sc_b[i][M:], 0.0))
            a_rk.append(jnp.where(lower, sc_k[i][M:], 0.0))
        else:
            n_mat.append(sc_b[i][:M] * u["dm_a"])
            a_ak.append(sc_k[i][:M] * u["dm_a"])
            a_rb.append(sc_b[i][M:] * u["dm_r"])
            a_rk.append(sc_k[i][M:] * u["dm_r"])
    if own is None:
        a_h = [r[:M] for r in reads]
        r_h = [r[M:] for r in reads]
    else:
        a_h = [jnp.where(own, r[:M], 0.0) for r in reads]
        r_h = [jnp.where(own, r[M:], 0.0) for r in reads]

    av = [mm(jnp.concatenate([a_ak[i], a_rk[i]], axis=0), v_st[i]) for i in range(nu)]
    x = [a_h[i] + av[i][:M] for i in range(nu)]
    y_part = [r_h[i] + av[i][M:] for i in range(nu)]
    steps = int(math.log2(L))
    for s in range(steps):
        if s + 1 < steps:
            t = [mm(n_mat[i], jnp.concatenate([x[i], n_mat[i]], axis=1)) for i in range(nu)]
            x = [x[i] + t[i][:, :LANES] for i in range(nu)]
            n_mat = [t[i][:, LANES:] for i in range(nu)]
        else:
            x = [x[i] + mm(n_mat[i], x[i]) for i in range(nu)]
    y_st = [y_part[i] + mm(a_rb[i], x[i]) for i in range(nu)]
    ys = []
    for i in range(nu):
        y = y_st[i][:L]
        for j in range(1, nsub):
            y = y + y_st[i][j * L:(j + 1) * L]
        ys.append(y)
    s_new = [u["s_prev"] * u["p_last"]
             + _dot_tn(jnp.concatenate([x[i], v_st[i]], axis=0),
                       jnp.concatenate([_stack_mask(u["b_dec"], nsub), _stack_mask(u["k_dec"], nsub)], axis=0))
             for i, u in enumerate(units)]
    return ys, s_new


def _gdn_kernel(q_ref, k_ref, v_ref, z_ref, sm_ref, alog_ref, dtb_ref, nw_ref, o_ref, st_ref, *, L, nseq):
    @pl.when(pl.program_id(1) == 0)
    def _():
        st_ref[...] = jnp.zeros(st_ref.shape, F32)

    row = lax.broadcasted_iota(jnp.int32, (L, L), 0)
    col = lax.broadcasted_iota(jnp.int32, (L, L), 1)
    nw = nw_ref[...]
    scale = GDN_HEAD_DIM ** -0.5
    units = []
    for s in range(nseq):
        sm = sm_ref[s]
        g = -jnp.exp(alog_ref[...]) * _softplus(sm + dtb_ref[...])
        beta = _sigmoid(sm)
        g_cum = _dot_w3(_tri_incl(L), g)
        g_cum_t = jnp.concatenate([g_cum, jnp.zeros((LANES - L, LANES), F32)], axis=0).T
        for h in range(GDN_HEADS):
            sl = slice(h * LANES, (h + 1) * LANES)
            q = q_ref[s, :, sl]
            k = k_ref[s, :, sl]
            qn = q * (lax.rsqrt(jnp.sum(q * q, axis=-1, keepdims=True) + L2_EPS) * scale)
            kn = k * lax.rsqrt(jnp.sum(k * k, axis=-1, keepdims=True) + L2_EPS)
            gc = g_cum[:, SMALL_A + h:SMALL_A + h + 1]
            gr = g_cum_t[SMALL_A + h:SMALL_A + h + 1, 0:L]
            gs = g[:, SMALL_A + h:SMALL_A + h + 1]
            bt = beta[:, SMALL_B + h:SMALL_B + h + 1]
            g_prev = gc - gs
            g_end = g_cum[L - 1:L, SMALL_A + h:SMALL_A + h + 1]
            b_vec = -(bt * jnp.exp(gs)) * kn
            k_vec = bt * kn
            e_end = jnp.exp(g_end - gc)
            units.append(dict(
                a_sc=kn, r_sc=qn, b_sc=b_vec, k_sc=k_vec,
                a_sr=kn * jnp.exp(g_prev), r_sr=qn * jnp.exp(gc),
                b_dec=b_vec * e_end, k_dec=k_vec * e_end,
                v=v_ref[s, :, sl], s_prev=st_ref[s, h], p_last=jnp.exp(g_end),
                dm_a=jnp.exp(jnp.where(col < row, g_prev - gr, -jnp.inf)),
                dm_r=jnp.exp(jnp.where(col <= row, gc - gr, -jnp.inf))))
    ys, s_new = _dplr_chunks(units, L=L, nsub=1)
    for s in range(nseq):
        for h in range(GDN_HEADS):
            sl = slice(h * LANES, (h + 1) * LANES)
            st_ref[s, h] = s_new[s * GDN_HEADS + h]
            y = ys[s * GDN_HEADS + h]
            yn = y * lax.rsqrt(jnp.mean(y * y, axis=-1, keepdims=True) + NORM_EPS) * nw
            o_ref[s, :, sl] = (yn * _silu(z_ref[s, :, sl])).astype(o_ref.dtype)


def _gdn_branch(pc, pn, a_log, dt_bias, norm_w, bsz, seq):
    L = DPLR_CHUNK
    nseq = SEQ_PER_STEP if bsz % SEQ_PER_STEP == 0 else 1
    pc3 = pc.reshape(bsz, seq, N_PC)
    pn3 = pn.reshape(bsz, seq, N_PN)
    vec = lambda w: pl.BlockSpec((1, w), lambda b, c: (0, 0))
    tok = lambda width, col: pl.BlockSpec((nseq, L, width), lambda b, c: (b, c, col // width))
    out = pl.pallas_call(
        functools.partial(_gdn_kernel, L=L, nseq=nseq),
        grid=(bsz // nseq, seq // L),
        in_specs=[tok(GDN_INNER, PC_Q), tok(GDN_INNER, PC_K), tok(GDN_INNER, PC_V),
                  tok(GDN_INNER, PN_ZGDN), tok(LANES, PC_SMALL),
                  vec(LANES), vec(LANES), vec(LANES)],
        out_specs=pl.BlockSpec((nseq, L, GDN_INNER), lambda b, c: (b, c, 0)),
        out_shape=jax.ShapeDtypeStruct((bsz, seq, GDN_INNER), BF16),
        scratch_shapes=[pltpu.VMEM((nseq, GDN_HEADS, LANES, LANES), F32)],
        compiler_params=_cparams("parallel", "arbitrary"),
    )(pc3, pc3, pc3, pn3, pc3,
      _pad_lanes(a_log, SMALL_A), _pad_lanes(dt_bias, SMALL_A),
      norm_w.reshape(1, GDN_HEAD_DIM).astype(F32))
    return out.reshape(bsz * seq, GDN_INNER)


def _rwkv_kernel(rkv_ref, lora_ref, mu_ref, mul_ref, w0_ref, w2_ref, a0_ref, a2_ref, g2_ref,
                 kk_ref, ka_ref, rk_ref, lnw_ref, lnb_ref, o_ref, sh_ref, shl_ref, st_ref, *, L, nseq):
    @pl.when(pl.program_id(1) == 0)
    def _():
        st_ref[...] = jnp.zeros(st_ref.shape, F32)
        for s in range(nseq):
            sh_ref[s, 0:SUBLANES, :] = jnp.zeros((SUBLANES, sh_ref.shape[2]), F32)
            shl_ref[s, 0:SUBLANES, :] = jnp.zeros((SUBLANES, shl_ref.shape[2]), F32)

    n = RWKV_INNER
    npair = RWKV_HEADS // 2
    ones_bd = _half_ones()
    inv_hd = 1.0 / RWKV_HEAD_DIM

    def head_sums(t):
        st = jnp.concatenate([t[:, p * LANES:(p + 1) * LANES] for p in range(npair)], axis=0)
        sm = _dot(st, ones_bd)
        return jnp.concatenate([sm[p * L:(p + 1) * L] for p in range(npair)], axis=1)

    units, tails = [], []
    for s in range(nseq):
        u = rkv_ref[s]
        ul = lora_ref[s]
        sh_ref[s, SUBLANES:SUBLANES + L, :] = u
        shl_ref[s, SUBLANES:SUBLANES + L, :] = ul
        u = u + (sh_ref[s, pl.ds(SUBLANES - 1, L), :] - u) * mu_ref[...]
        ul = ul + (shl_ref[s, pl.ds(SUBLANES - 1, L), :] - ul) * mul_ref[...]
        sh_ref[s, 0:SUBLANES, :] = sh_ref[s, L:L + SUBLANES, :]
        shl_ref[s, 0:SUBLANES, :] = shl_ref[s, L:L + SUBLANES, :]

        r, k, v = u[:, 0:n], u[:, n:2 * n], u[:, 2 * n:3 * n]
        lo = ul[:, 0:LANES]
        dg = ul[:, LANES:2 * LANES]
        w_log = -_softplus(-(w0_ref[...] + _dot_hl(jnp.tanh(lo), w2_ref[...]))) - 0.5
        lw = -jnp.exp(w_log)
        a_ic = _sigmoid(a0_ref[...] + _dot_hl(lo, a2_ref[...]))
        gate = _dot(_sigmoid(dg), g2_ref[...])
        g_cum = _dot_w3(_tri_incl(L), lw)
        kk_raw = k * kk_ref[...]
        k_mod = k * (1.0 + (a_ic - 1.0) * ka_ref[...])
        kk_all = kk_raw * lax.rsqrt(head_sums(kk_raw * kk_raw) + L2_EPS)
        tails.append((head_sums(r * k_mod * rk_ref[...]) * v, gate))
        for p in range(npair):
            sl = slice(p * LANES, (p + 1) * LANES)
            kk = kk_all[:, sl]
            gc = g_cum[:, sl]
            g_end = gc[L - 1:L, :]
            e_neg = jnp.exp(-gc)
            e_end = jnp.exp(g_end - gc)
            b_vec = kk * a_ic[:, sl]
            k_vec = k_mod[:, sl]
            a_t = -kk * jnp.exp(gc - lw[:, sl])
            r_t = r[:, sl] * jnp.exp(gc)
            units.append(dict(
                a_sc=a_t, r_sc=r_t, b_sc=b_vec * e_neg, k_sc=k_vec * e_neg,
                a_sr=a_t, r_sr=r_t, b_dec=b_vec * e_end, k_dec=k_vec * e_end,
                v=v[:, sl], s_prev=st_ref[s, p], p_last=jnp.exp(g_end)))
    ys, s_new = _dplr_chunks(units, L=L, nsub=2)
    for s in range(nseq):
        for p in range(npair):
            st_ref[s, p] = s_new[s * npair + p]
        y = jnp.concatenate(ys[s * npair:(s + 1) * npair], axis=1)
        yc = y - head_sums(y) * inv_hd
        var = head_sums(yc * yc) * inv_hd
        yn = yc * lax.rsqrt(var + RWKV_GN_EPS) * lnw_ref[...] + lnb_ref[...]
        bonus, gate = tails[s]
        o_ref[s] = ((yn + bonus) * gate).astype(o_ref.dtype)


def _rwkv_branch(pc, pn, mu, w0, w2, a0, a2, g2, k_k, k_a, r_k, ln_w, ln_b, bsz, seq):
    L = DPLR_CHUNK
    n = RWKV_INNER
    nseq = SEQ_PER_STEP if bsz % SEQ_PER_STEP == 0 else 1
    pc3 = pc.reshape(bsz, seq, N_PC)
    pn3 = pn.reshape(bsz, seq, N_PN)
    vec = lambda w: pl.BlockSpec((1, w), lambda b, c: (0, 0))
    mat = lambda: pl.BlockSpec((LANES, n), lambda b, c: (0, 0))
    tok = lambda width, col: pl.BlockSpec((nseq, L, width), lambda b, c: (b, c, col // width))
    w2p = jnp.concatenate([w2, jnp.zeros((LANES - RWKV_DECAY_LORA, n), F32)], axis=0)
    a2p = jnp.concatenate([jnp.zeros((RWKV_DECAY_LORA, n), F32), a2], axis=0)
    r1 = lambda t: t.reshape(1, -1).astype(F32)
    out = pl.pallas_call(
        functools.partial(_rwkv_kernel, L=L, nseq=nseq),
        grid=(bsz // nseq, seq // L),
        in_specs=[tok(3 * n, PN_RKV), tok(256, PC_LORA),
                  vec(3 * n), vec(256), vec(n), mat(), vec(n), mat(), mat(),
                  vec(n), vec(n), vec(n), vec(n), vec(n)],
        out_specs=pl.BlockSpec((nseq, L, n), lambda b, c: (b, c, 0)),
        out_shape=jax.ShapeDtypeStruct((bsz, seq, n), BF16),
        scratch_shapes=[pltpu.VMEM((nseq, L + SUBLANES, 3 * n), F32),
                        pltpu.VMEM((nseq, L + SUBLANES, 256), F32),
                        pltpu.VMEM((nseq, RWKV_HEADS // 2, LANES, LANES), F32)],
        compiler_params=_cparams("parallel", "arbitrary"),
    )(pn3, pc3, r1(mu[:3 * n]), r1(mu[3 * n:]), r1(w0), w2p, r1(a0), a2p, g2.astype(F32),
      r1(k_k), r1(k_a), r1(r_k), r1(ln_w), r1(ln_b))
    return out.reshape(bsz * seq, n)


def _merge_kernel(x_ref, ys_ref, yg_ref, yr_ref, gate_ref, ps_ref, pg_ref, pr_ref, wo_ref, o_ref):
    d = D_MODEL
    m = _sigmoid(gate_ref[:, 0:d]) * jnp.dot(ys_ref[...], ps_ref[...], preferred_element_type=F32)
    m = m + _sigmoid(gate_ref[:, d:2 * d]) * jnp.dot(yg_ref[...], pg_ref[...], preferred_element_type=F32)
    m = m + _sigmoid(gate_ref[:, 2 * d:3 * d]) * jnp.dot(yr_ref[...], pr_ref[...], preferred_element_type=F32)
    o_ref[...] = x_ref[...] + jnp.dot(m.astype(BF16), wo_ref[...], preferred_element_type=F32)


def _merge(x, y_ssm, y_gdn, y_rwkv, pn, p_ssm, p_gdn, p_rwkv, w_out, tm=512):
    t, d = x.shape
    tok = lambda: pl.BlockSpec((tm, d), lambda i: (i, 0))
    wgt = lambda: pl.BlockSpec((d, d), lambda i: (0, 0))
    return pl.pallas_call(
        _merge_kernel,
        grid=(t // tm,),
        in_specs=[tok(), tok(), tok(), tok(),
                  pl.BlockSpec((tm, 3 * d), lambda i: (i, PN_GATE // (3 * d))),
                  wgt(), wgt(), wgt(), wgt()],
        out_specs=tok(),
        out_shape=jax.ShapeDtypeStruct((t, d), F32),
        compiler_params=_cparams("parallel"),
    )(x, y_ssm, y_gdn, y_rwkv, pn, p_ssm, p_gdn, p_rwkv, w_out)


def _ffn_kernel(x_ref, nw_ref, wg_ref, wu_ref, wd_ref, o_ref, h_ref, acc_ref):
    f = pl.program_id(1)

    @pl.when(f == 0)
    def _():
        x = x_ref[...]
        ms = jnp.mean(x * x, axis=-1, keepdims=True)
        h_ref[...] = (x * lax.rsqrt(ms + NORM_EPS) * nw_ref[...]).astype(BF16)
        acc_ref[...] = x

    h = h_ref[...]
    act = _silu(jnp.dot(h, wg_ref[...], preferred_element_type=F32)) * jnp.dot(
        h, wu_ref[...], preferred_element_type=F32)
    acc_ref[...] += jnp.dot(act.astype(BF16), wd_ref[...], preferred_element_type=F32)

    @pl.when(f == pl.num_programs(1) - 1)
    def _():
        o_ref[...] = acc_ref[...]


def _ffn_dense(x, norm_w, w_gate, w_up, w_down, tm=512, tf=1408):
    t, d = x.shape
    nf = w_gate.shape[1] // tf
    return pl.pallas_call(
        _ffn_kernel,
        grid=(t // tm, nf),
        in_specs=[pl.BlockSpec((tm, d), lambda i, f: (i, 0)),
                  pl.BlockSpec((1, d), lambda i, f: (0, 0)),
                  pl.BlockSpec((d, tf), lambda i, f: (0, f)),
                  pl.BlockSpec((d, tf), lambda i, f: (0, f)),
                  pl.BlockSpec((tf, d), lambda i, f: (f, 0))],
        out_specs=pl.BlockSpec((tm, d), lambda i, f: (i, 0)),
        out_shape=jax.ShapeDtypeStruct((t, d), F32),
        scratch_shapes=[pltpu.VMEM((tm, d), BF16), pltpu.VMEM((tm, d), F32)],
        compiler_params=_cparams("parallel", "arbitrary"),
    )(x, norm_w.reshape(1, d).astype(F32), w_gate, w_up, w_down)


MOE_TM = 1024
MOE_BLK = 128
MOE_FIRST_ROWS = 320


def _route_kernel(x_ref, nw_ref, rt_ref, h_ref, cmb_ref, cnt_ref):
    x = x_ref[...]
    lane = lax.broadcasted_iota(jnp.int32, cmb_ref.shape, 1)
    ms = jnp.mean(x * x, axis=-1, keepdims=True)
    hf = x * lax.rsqrt(ms + NORM_EPS) * nw_ref[...]
    h_ref[...] = hf.astype(BF16)
    logits = jnp.where(lane < N_EXPERTS, _dot_hl(hf, rt_ref[...]), -jnp.inf)
    m1 = jnp.max(logits, axis=-1, keepdims=True)
    i1 = jnp.min(jnp.where(logits == m1, lane, LANES), axis=-1, keepdims=True)
    rest = jnp.where(lane == i1, -jnp.inf, logits)
    m2 = jnp.max(rest, axis=-1, keepdims=True)
    i2 = jnp.min(jnp.where(rest == m2, lane, LANES), axis=-1, keepdims=True)
    e2 = jnp.exp(m2 - m1)
    p1 = 1.0 / (1.0 + e2)
    cmb = jnp.where(lane == i1, p1, 0.0) + jnp.where(lane == i2, e2 * p1, 0.0)
    cmb_ref[...] = cmb
    cnt_ref[0] = jnp.sum(jnp.where(cmb > 0.0, 1.0, 0.0), axis=0, keepdims=True).astype(jnp.int32)


def _moe_route(x, norm_w, router, tm):
    t, d = x.shape
    ne = router.shape[1]
    rt = jnp.concatenate([router.astype(F32), jnp.zeros((d, LANES - ne), F32)], axis=1)
    return pl.pallas_call(
        _route_kernel,
        grid=(t // tm,),
        in_specs=[pl.BlockSpec((tm, d), lambda i: (i, 0)),
                  pl.BlockSpec((1, d), lambda i: (0, 0)),
                  pl.BlockSpec((d, LANES), lambda i: (0, 0))],
        out_specs=[pl.BlockSpec((tm, d), lambda i: (i, 0)),
                   pl.BlockSpec((tm, LANES), lambda i: (i, 0)),
                   pl.BlockSpec((1, 1, LANES), lambda i: (i, 0, 0))],
        out_shape=[jax.ShapeDtypeStruct((t, d), BF16),
                   jax.ShapeDtypeStruct((t, LANES), F32),
                   jax.ShapeDtypeStruct((t // tm, 1, LANES), jnp.int32)],
        compiler_params=_cparams("parallel"),
    )(x, norm_w.reshape(1, d).astype(F32), rt)


def _moe_kernel(cnt_ref, h_ref, cmb_ref, wg_ref, wu_ref, wd_ref, o_ref, rank_ref, xc_ref, yc_ref, *, tm):
    i = pl.program_id(0)
    e = pl.program_id(1)
    f = pl.program_id(2)
    last_f = f == pl.num_programs(2) - 1
    lane = lax.broadcasted_iota(jnp.int32, (tm, LANES), 1)

    @pl.when((e == 0) & (f == 0))
    def _():
        o_ref[...] = jnp.zeros(o_ref.shape, F32)
        r = lax.broadcasted_iota(jnp.int32, (tm, tm), 0)
        c = lax.broadcasted_iota(jnp.int32, (tm, tm), 1)
        before = jnp.where(c < r, 1.0, 0.0).astype(BF16)
        sel = jnp.where(cmb_ref[...] > 0.0, 1.0, 0.0).astype(BF16)
        rank_ref[...] = jnp.dot(before, sel, preferred_element_type=F32)

    cnt = cnt_ref[i, e]

    def process(off, rows):
        c_e = jnp.sum(jnp.where(lane == e, cmb_ref[...], 0.0), axis=-1, keepdims=True)
        r_e = jnp.sum(jnp.where(lane == e, rank_ref[...], 0.0), axis=-1, keepdims=True)
        key = jnp.where(c_e > 0.0, r_e, -1.0) - jnp.asarray(off, F32)
        slot = lax.broadcasted_iota(jnp.int32, (tm, rows), 1).astype(F32)
        onehot = jnp.where(key == slot, 1.0, 0.0).astype(BF16)
        rs = pl.ds(off, rows)

        @pl.when(f == 0)
        def _():
            xc_ref[rs, :] = _dot_tn(onehot, h_ref[...]).astype(BF16)
            yc_ref[rs, :] = jnp.zeros((rows, yc_ref.shape[1]), F32)

        xc = xc_ref[rs, :]
        act = _silu(jnp.dot(xc, wg_ref[0], preferred_element_type=F32)) * jnp.dot(
            xc, wu_ref[0], preferred_element_type=F32)
        yc_ref[rs, :] += jnp.dot(act.astype(BF16), wd_ref[0], preferred_element_type=F32)

        @pl.when(last_f)
        def _():
            o_ref[...] += c_e * jnp.dot(onehot, yc_ref[rs, :].astype(BF16), preferred_element_type=F32)

    @pl.when(cnt > 0)
    def _():
        process(0, MOE_FIRST_ROWS)

    def extra(b, carry):
        process(pl.multiple_of(MOE_FIRST_ROWS + b * MOE_BLK, 64), MOE_BLK)
        return carry

    lax.fori_loop(0, (jnp.maximum(cnt - MOE_FIRST_ROWS, 0) + MOE_BLK - 1) // MOE_BLK, extra, 0)


def _add_norm_kernel(x_ref, y_ref, w_ref, o_ref, *, final_norm):
    y = x_ref[...] + y_ref[...]
    if final_norm:
        y = y * lax.rsqrt(jnp.mean(y * y, axis=-1, keepdims=True) + NORM_EPS) * w_ref[...]
    o_ref[...] = y


def _ffn_moe(x, norm_w, router, w_gate, w_up, w_down, final_w, tm=MOE_TM, tf=1408, tr=512):
    t, d = x.shape
    ne, _, ff = w_gate.shape
    nf = ff // tf
    h, cmb, cnt = _moe_route(x, norm_w, router, tm)
    cap = MOE_FIRST_ROWS + -(-max(tm - MOE_FIRST_ROWS, 0) // MOE_BLK) * MOE_BLK
    grid_spec = pltpu.PrefetchScalarGridSpec(
        num_scalar_prefetch=1,
        grid=(t // tm, ne, nf),
        in_specs=[pl.BlockSpec((tm, d), lambda i, e, f, c: (i, 0)),
                  pl.BlockSpec((tm, LANES), lambda i, e, f, c: (i, 0)),
                  pl.BlockSpec((1, d, tf), lambda i, e, f, c: (e, 0, f)),
                  pl.BlockSpec((1, d, tf), lambda i, e, f, c: (e, 0, f)),
                  pl.BlockSpec((1, tf, d), lambda i, e, f, c: (e, f, 0))],
        out_specs=pl.BlockSpec((tm, d), lambda i, e, f, c: (i, 0)),
        scratch_shapes=[pltpu.VMEM((tm, LANES), F32), pltpu.VMEM((cap, d), BF16), pltpu.VMEM((cap, d), F32)])
    y = pl.pallas_call(
        functools.partial(_moe_kernel, tm=tm),
        grid_spec=grid_spec,
        out_shape=jax.ShapeDtypeStruct((t, d), F32),
        compiler_params=_cparams("parallel", "arbitrary", "arbitrary"),
    )(cnt.reshape(t // tm, LANES), h, cmb, w_gate, w_up, w_down)
    final_norm = final_w is not None
    fw = (final_w if final_norm else jnp.ones((d,), F32)).reshape(1, d).astype(F32)
    tr = min(tr, t)
    return pl.pallas_call(
        functools.partial(_add_norm_kernel, final_norm=final_norm),
        grid=(t // tr,),
        in_specs=[pl.BlockSpec((tr, d), lambda i: (i, 0)),
                  pl.BlockSpec((tr, d), lambda i: (i, 0)),
                  pl.BlockSpec((1, d), lambda i: (0, 0))],
        out_specs=pl.BlockSpec((tr, d), lambda i: (i, 0)),
        out_shape=jax.ShapeDtypeStruct((t, d), F32),
        compiler_params=_cparams("parallel"),
    )(x, y, fw)


def _reorder_w_in(w):
    d = w.shape[0]
    wc = [w[:, 0:1024], w[:, 1536:4608], w[:, 1024:1536], w[:, 9760:10016], w[:, 5632:5648],
          w[:, 6672:6688], jnp.zeros((d, N_PC - PC_SMALL - 32), w.dtype)]
    wn = [w[:, 6688:9760], w[:, 10016:13088], w[:, 4608:5632], w[:, 5648:6672]]
    return jnp.concatenate(wc, axis=1).astype(BF16), jnp.concatenate(wn, axis=1).astype(BF16)


def _token_mixer_layer(x, i, bsz, seq, p):
    h = _rmsnorm(x, p["attn_norm_w"][i], BF16)
    wc, wn = _reorder_w_in(p["w_in"][i])
    pc, pn = _inproj(h, wc, wn, p["conv_w"][i], p["conv_b"][i], seq, tm=1024)
    y_ssm = _ssd_branch(pc, pn, p["ssm_a_log"][i], p["ssm_dt_bias"][i], p["ssm_d"][i],
                        p["ssm_norm_w"][i], bsz, seq)
    y_gdn = _gdn_branch(pc, pn, p["gdn_a_log"][i], p["gdn_dt_bias"][i], p["gdn_norm_w"][i],
                        bsz, seq)
    y_rwkv = _rwkv_branch(pc, pn, p["rwkv_mu"][i], p["rwkv_w0"][i], p["rwkv_w2"][i], p["rwkv_a0"][i],
                          p["rwkv_a2"][i], p["rwkv_g2"][i], p["rwkv_k_k"][i], p["rwkv_k_a"][i],
                          p["rwkv_r_k"][i], p["rwkv_ln_w"][i], p["rwkv_ln_b"][i], bsz, seq)
    return _merge(x, y_ssm, y_gdn, y_rwkv, pn, p["proj_ssm"][i].astype(BF16),
                  p["proj_gdn"][i].astype(BF16), p["proj_rwkv"][i].astype(BF16),
                  p["w_out"][i].astype(BF16))


def kernel(x, attn_norm_w, w_in, conv_w, conv_b, ssm_a_log, ssm_dt_bias, ssm_d, ssm_norm_w, gdn_a_log, gdn_dt_bias, gdn_norm_w, rwkv_mu, rwkv_w0, rwkv_w2, rwkv_a0, rwkv_a2, rwkv_g2, rwkv_k_k, rwkv_k_a, rwkv_r_k, rwkv_ln_w, rwkv_ln_b, proj_ssm, proj_gdn, proj_rwkv, w_out, ffn_norm_w, dense_w_gate, dense_w_up, dense_w_down, moe_router, moe_w_gate, moe_w_up, moe_w_down, final_norm_w):
    p = dict(attn_norm_w=attn_norm_w, w_in=w_in, conv_w=conv_w, conv_b=conv_b, ssm_a_log=ssm_a_log,
             ssm_dt_bias=ssm_dt_bias, ssm_d=ssm_d, ssm_norm_w=ssm_norm_w, gdn_a_log=gdn_a_log,
             gdn_dt_bias=gdn_dt_bias, gdn_norm_w=gdn_norm_w, rwkv_mu=rwkv_mu, rwkv_w0=rwkv_w0,
             rwkv_w2=rwkv_w2, rwkv_a0=rwkv_a0, rwkv_a2=rwkv_a2, rwkv_g2=rwkv_g2, rwkv_k_k=rwkv_k_k,
             rwkv_k_a=rwkv_k_a, rwkv_r_k=rwkv_r_k, rwkv_ln_w=rwkv_ln_w, rwkv_ln_b=rwkv_ln_b,
             proj_ssm=proj_ssm, proj_gdn=proj_gdn, proj_rwkv=proj_rwkv, w_out=w_out)
    bsz, seq, d = x.shape
    depth = attn_norm_w.shape[0]
    xt = x.reshape(bsz * seq, d)
    for i in range(depth):
        xt = _token_mixer_layer(xt, i, bsz, seq, p)
        j = i // 2
        last = i == depth - 1
        if i % 2 == 0:
            xt = _ffn_dense(xt, ffn_norm_w[i], dense_w_gate[j].astype(BF16), dense_w_up[j].astype(BF16),
                            dense_w_down[j].astype(BF16))
            if last:
                xt = _rmsnorm(xt, final_norm_w, F32)
        else:
            xt = _ffn_moe(xt, ffn_norm_w[i], moe_router[j], moe_w_gate[j].astype(BF16),
                          moe_w_up[j].astype(BF16), moe_w_down[j].astype(BF16),
                          final_norm_w if last else None)
    return xt.reshape(bsz, seq, d)
```

```python
import functools
import math

import numpy as np
import jax
import jax.numpy as jnp
from jax import lax
from jax.experimental import pallas as pl
from jax.experimental.pallas import tpu as pltpu

F32 = jnp.float32
BF16 = jnp.bfloat16

LANES = 128
SUBLANES = 8
VMEM_LIMIT_BYTES = 56 * 1024 * 1024

D_MODEL = 1024
SSM_HEADS, SSM_HEAD_DIM, SSM_GROUPS, SSM_STATE = 16, 64, 2, 128
SSM_INNER = SSM_HEADS * SSM_HEAD_DIM
GDN_HEADS, GDN_HEAD_DIM = 8, 128
GDN_INNER = GDN_HEADS * GDN_HEAD_DIM
RWKV_HEADS, RWKV_HEAD_DIM = 16, 64
RWKV_INNER = RWKV_HEADS * RWKV_HEAD_DIM
RWKV_DECAY_LORA, RWKV_ICLR_LORA, RWKV_GATE_LORA = 64, 64, 128
RWKV_GN_EPS = 64e-5
CONV_K = 4
FFN_DIM = 2816
N_EXPERTS = 8
NORM_EPS = 1e-6
L2_EPS = 1e-6

PC_XS, PC_Q, PC_K, PC_V = 0, 1024, 2048, 3072
PC_BS, PC_CS = 4096, 4352
PC_CONV_END = 4608
PC_LORA = 4608
PC_SMALL = 4864
N_PC = 5120
PN_RKV = 0
PN_GATE = 3072
PN_ZSSM, PN_ZGDN = 6144, 7168
N_PN = 8192
SMALL_DT, SMALL_A, SMALL_B = 0, 16, 24

SSD_CHUNK = 128
DPLR_CHUNK = 64
SEQ_PER_STEP = 4
INPROJ_ROW_CHUNK = 512


def _cparams(*sem):
    return pltpu.CompilerParams(dimension_semantics=sem, vmem_limit_bytes=VMEM_LIMIT_BYTES)


def _dot(a, b):
    return jnp.dot(a.astype(BF16), b.astype(BF16), preferred_element_type=F32)


def _dot_nt(a, b):
    return lax.dot_general(a.astype(BF16), b.astype(BF16), (((1,), (1,)), ((), ())),
                           preferred_element_type=F32)


def _dot_tn(a, b):
    return lax.dot_general(a.astype(BF16), b.astype(BF16), (((0,), (0,)), ((), ())),
                           preferred_element_type=F32)


def _split3(x):
    hi = x.astype(BF16)
    r1 = x - hi.astype(F32)
    mid = r1.astype(BF16)
    lo = (r1 - mid.astype(F32)).astype(BF16)
    return hi, mid, lo


def _dot_x3(x, w):
    hi, mid, lo = _split3(x)
    return _dot(hi, w) + _dot(mid, w) + _dot(lo, w)


def _dot_w3(w, x):
    hi, mid, lo = _split3(x)
    return _dot(w, hi) + _dot(w, mid) + _dot(w, lo)


def _dot_hl(a, b):
    ah = a.astype(BF16)
    al = (a - ah.astype(F32)).astype(BF16)
    bh = b.astype(BF16)
    bl = (b - bh.astype(F32)).astype(BF16)
    return _dot(ah, bh) + _dot(ah, bl) + _dot(al, bh)


def _sigmoid(x):
    return 0.5 * jnp.tanh(0.5 * x) + 0.5


def _silu(x):
    return x * _sigmoid(x)


def _softplus(x):
    return jnp.maximum(x, 0.0) + jnp.log(1.0 + jnp.exp(-jnp.abs(x)))


def _tri_incl(n):
    r = lax.broadcasted_iota(jnp.int32, (n, n), 0)
    c = lax.broadcasted_iota(jnp.int32, (n, n), 1)
    return jnp.where(r >= c, 1.0, 0.0).astype(BF16)


def _half_ones():
    r = lax.broadcasted_iota(jnp.int32, (LANES, LANES), 0)
    c = lax.broadcasted_iota(jnp.int32, (LANES, LANES), 1)
    return jnp.where((r // 64) == (c // 64), 1.0, 0.0).astype(BF16)


def _rmsnorm_kernel(x_ref, w_ref, o_ref):
    x = x_ref[...]
    ms = jnp.mean(x * x, axis=-1, keepdims=True)
    o_ref[...] = (x * lax.rsqrt(ms + NORM_EPS) * w_ref[...]).astype(o_ref.dtype)


def _rmsnorm(x, w, out_dtype, tm=512):
    t, d = x.shape
    return pl.pallas_call(
        _rmsnorm_kernel,
        grid=(t // tm,),
        in_specs=[pl.BlockSpec((tm, d), lambda i: (i, 0)),
                  pl.BlockSpec((1, d), lambda i: (0, 0))],
        out_specs=pl.BlockSpec((tm, d), lambda i: (i, 0)),
        out_shape=jax.ShapeDtypeStruct((t, d), out_dtype),
        compiler_params=_cparams("parallel"),
    )(x, w.reshape(1, d))


def _inproj_kernel(a_ref, wc_ref, wn_ref, cw_ref, cb_ref, cm_ref, oc_ref, on_ref, sh_ref, *,
                   tm, tiles_per_seq):
    i = pl.program_id(1)
    tc = oc_ref.shape[1]

    @pl.when(i % tiles_per_seq == 0)
    def _():
        sh_ref[0:SUBLANES, :] = jnp.zeros((SUBLANES, tc), F32)

    rc = min(tm, INPROJ_ROW_CHUNK)
    starts = list(range(0, tm, rc))
    acc = jnp.dot(a_ref[0:rc, :], wc_ref[...], preferred_element_type=F32)
    for n, r0 in enumerate(starts):
        sh_ref[SUBLANES + r0:SUBLANES + r0 + rc, :] = acc
        on_ref[r0:r0 + rc, :] = jnp.dot(a_ref[r0:r0 + rc, :], wn_ref[...], preferred_element_type=F32)
        if n + 1 < len(starts):
            acc_next = jnp.dot(a_ref[r0 + rc:r0 + 2 * rc, :], wc_ref[...], preferred_element_type=F32)
        c = acc * cw_ref[CONV_K - 1:CONV_K, :] + cb_ref[...]
        for k in range(1, CONV_K):
            c = c + sh_ref[pl.ds(SUBLANES + r0 - k, rc), :] * cw_ref[CONV_K - 1 - k:CONV_K - k, :]
        oc_ref[r0:r0 + rc, :] = jnp.where(cm_ref[...] > 0.0, _silu(c), acc)
        if n + 1 < len(starts):
            acc = acc_next
    sh_ref[0:SUBLANES, :] = sh_ref[tm:tm + SUBLANES, :]


def _conv_cols(c):
    r = c.shape[0]
    return jnp.concatenate([c[:, 0:1024], c[:, 1536:4608], c[:, 1024:1536],
                            jnp.zeros((r, N_PC - PC_CONV_END), F32)], axis=1).astype(F32)


def _inproj(a, wc, wn, conv_w, conv_b, seq, tm, nsteps=8):
    m, k = a.shape
    tc, tn = N_PC // nsteps, N_PN // nsteps
    cw = _conv_cols(conv_w)
    cb = _conv_cols(conv_b.reshape(1, -1))
    cm = _conv_cols(jnp.ones((1, conv_w.shape[1]), F32))
    return pl.pallas_call(
        functools.partial(_inproj_kernel, tm=tm, tiles_per_seq=seq // tm),
        grid=(nsteps, m // tm),
        in_specs=[pl.BlockSpec((tm, k), lambda j, i: (i, 0)),
                  pl.BlockSpec((k, tc), lambda j, i: (0, j)),
                  pl.BlockSpec((k, tn), lambda j, i: (0, j)),
                  pl.BlockSpec((CONV_K, tc), lambda j, i: (0, j)),
                  pl.BlockSpec((1, tc), lambda j, i: (0, j)),
                  pl.BlockSpec((1, tc), lambda j, i: (0, j))],
        out_specs=[pl.BlockSpec((tm, tc), lambda j, i: (i, j)),
                   pl.BlockSpec((tm, tn), lambda j, i: (i, j))],
        out_shape=[jax.ShapeDtypeStruct((m, N_PC), F32), jax.ShapeDtypeStruct((m, N_PN), F32)],
        scratch_shapes=[pltpu.VMEM((tm + SUBLANES, tc), F32)],
        compiler_params=_cparams("parallel", "arbitrary"),
    )(a, wc, wn, cw, cb, cm)


def _ssd_kernel(xs_ref, bs_ref, cs_ref, z_ref, sm_ref, alog_ref, dtb_ref, dsk_ref, nw_ref, ex_ref,
                o_ref, st_ref, *, L):
    @pl.when(pl.program_id(1) == 0)
    def _():
        st_ref[...] = jnp.zeros(st_ref.shape, F32)

    hd = SSM_HEAD_DIM
    gw = SSM_INNER // SSM_GROUPS
    dt = _softplus(sm_ref[...] + dtb_ref[...])
    la = -jnp.exp(alog_ref[...]) * dt
    g_cum = _dot_w3(_tri_incl(L), la)
    g_cum_t = g_cum.T
    ex = ex_ref[...]
    dt_e = _dot_x3(dt, ex)
    g_e = _dot_x3(g_cum, ex)
    g_last = g_e[L - 1:L, :]
    e_g = jnp.exp(g_e)
    e_dec = jnp.exp(g_last - g_e)
    e_last = jnp.exp(g_last)

    xs = xs_ref[...]
    xdt = xs * dt_e
    xdec = xdt * e_dec
    row = lax.broadcasted_iota(jnp.int32, (L, L), 0)
    col = lax.broadcasted_iota(jnp.int32, (L, L), 1)
    causal = row >= col
    lane = lax.broadcasted_iota(jnp.int32, (L, LANES), 1)
    first_half = lane < hd

    y_parts = []
    for g in range(SSM_GROUPS):
        b_g = bs_ref[:, g * SSM_STATE:(g + 1) * SSM_STATE]
        c_g = cs_ref[:, g * SSM_STATE:(g + 1) * SSM_STATE]
        cb = _dot_nt(c_g, b_g)
        st_g = st_ref[:, g * gw:(g + 1) * gw]
        y_off = _dot(c_g, st_g) * e_g[:, g * gw:(g + 1) * gw]
        st_ref[:, g * gw:(g + 1) * gw] = (st_g * e_last[:, g * gw:(g + 1) * gw]
                                          + _dot_tn(b_g, xdec[:, g * gw:(g + 1) * gw]))
        heads_per_group = SSM_HEADS // SSM_GROUPS
        for p in range(heads_per_group // 2):
            h0 = g * heads_per_group + 2 * p
            base = h0 * hd
            sc = []
            for h in (h0, h0 + 1):
                diff = g_cum[:, h:h + 1] - g_cum_t[h:h + 1, :]
                sc.append(cb * jnp.exp(jnp.where(causal, diff, -jnp.inf)))
            yd = _dot(jnp.concatenate(sc, axis=0), xdt[:, base:base + LANES])
            y_parts.append(jnp.where(first_half, yd[:L], yd[L:])
                           + y_off[:, base - g * gw:base - g * gw + LANES])
    y = jnp.concatenate(y_parts, axis=1) + dsk_ref[...] * xs
    y = y * _silu(z_ref[...])
    outs = []
    for g in range(SSM_GROUPS):
        seg = y[:, g * gw:(g + 1) * gw]
        outs.append(seg * lax.rsqrt(jnp.mean(seg * seg, axis=-1, keepdims=True) + NORM_EPS))
    o_ref[...] = (jnp.concatenate(outs, axis=1) * nw_ref[...]).astype(o_ref.dtype)


def _pad_lanes(v, offset=0, width=LANES):
    out = jnp.zeros((1, width), F32)
    return lax.dynamic_update_slice(out, v.reshape(1, -1).astype(F32), (0, offset))


def _ssd_branch(pc, pn, a_log, dt_bias, d_skip, norm_w, bsz, seq):
    L = SSD_CHUNK
    nc = seq // L
    ex = np.zeros((LANES, SSM_INNER), np.float32)
    for h in range(SSM_HEADS):
        ex[h, h * SSM_HEAD_DIM:(h + 1) * SSM_HEAD_DIM] = 1.0
    row = lambda b, c: b * nc + c
    vec = lambda w: pl.BlockSpec((1, w), lambda b, c: (0, 0))
    return pl.pallas_call(
        functools.partial(_ssd_kernel, L=L),
        grid=(bsz, nc),
        in_specs=[pl.BlockSpec((L, SSM_INNER), lambda b, c: (row(b, c), PC_XS // SSM_INNER)),
                  pl.BlockSpec((L, 256), lambda b, c: (row(b, c), PC_BS // 256)),
                  pl.BlockSpec((L, 256), lambda b, c: (row(b, c), PC_CS // 256)),
                  pl.BlockSpec((L, SSM_INNER), lambda b, c: (row(b, c), PN_ZSSM // SSM_INNER)),
                  pl.BlockSpec((L, LANES), lambda b, c: (row(b, c), PC_SMALL // LANES)),
                  vec(LANES), vec(LANES), vec(SSM_INNER), vec(SSM_INNER),
                  pl.BlockSpec((LANES, SSM_INNER), lambda b, c: (0, 0))],
        out_specs=pl.BlockSpec((L, SSM_INNER), lambda b, c: (row(b, c), 0)),
        out_shape=jax.ShapeDtypeStruct((bsz * seq, SSM_INNER), BF16),
        scratch_shapes=[pltpu.VMEM((SSM_STATE, SSM_INNER), F32)],
        compiler_params=_cparams("parallel", "arbitrary"),
    )(pc, pc, pc, pn, pc,
      _pad_lanes(a_log, SMALL_DT), _pad_lanes(dt_bias, SMALL_DT),
      jnp.repeat(d_skip.astype(F32), SSM_HEAD_DIM).reshape(1, SSM_INNER),
      norm_w.reshape(1, SSM_INNER).astype(F32), jnp.asarray(ex, BF16))


def _stack_tile(x, nsub):
    return x if nsub == 1 else jnp.concatenate([x] * nsub, axis=0)


def _stack_mask(x, nsub):
    if nsub == 1:
        return x
    lane = lax.broadcasted_iota(jnp.int32, x.shape, 1) // (LANES // nsub)
    return jnp.concatenate([jnp.where(lane == j, x, 0.0) for j in range(nsub)], axis=0)


def _dplr_chunks(units, *, L, nsub, mm=_dot):
    M = nsub * L
    nu = len(units)
    ri = lax.broadcasted_iota(jnp.int32, (M, M), 0)
    ci = lax.broadcasted_iota(jnp.int32, (M, M), 1)
    if nsub == 1:
        strict, lower = ci < ri, ci <= ri
        own = None
    else:
        same = (ri // L) == (ci // L)
        strict = same & ((ci % L) < (ri % L))
        lower = same & ((ci % L) <= (ri % L))
        rr = lax.broadcasted_iota(jnp.int32, (M, LANES), 0) // L
        ll = lax.broadcasted_iota(jnp.int32, (M, LANES), 1) // (LANES // nsub)
        own = rr == ll

    rows = [jnp.concatenate([_stack_tile(u["a_sc"], nsub), _stack_tile(u["r_sc"], nsub)], axis=0)
            for u in units]
    b_st = [_stack_mask(u["b_sc"], nsub) for u in units]
    k_st = [_stack_mask(u["k_sc"], nsub) for u in units]
    v_st = [_stack_mask(u["v"], nsub) for u in units]
    if M % LANES == 0:
        sc = [_dot_nt(rows[i], jnp.concatenate([b_st[i], k_st[i]], axis=0)) for i in range(nu)]
        sc_b = [t[:, :M] for t in sc]
        sc_k = [t[:, M:] for t in sc]
    else:
        sc_b = [_dot_nt(rows[i], b_st[i]) for i in range(nu)]
        sc_k = [_dot_nt(rows[i], k_st[i]) for i in range(nu)]
    reads = [_dot_nt(jnp.concatenate([_stack_tile(u["a_sr"], nsub), _stack_tile(u["r_sr"], nsub)],
                                     axis=0), u["s_prev"]) for u in units]
    n_mat, a_ak, a_rb, a_rk = [], [], [], []
    for i, u in enumerate(units):
        if u.get("dm_a") is None:
            n_mat.append(jnp.where(strict, sc_b[i][:M], 0.0))
            a_ak.append(jnp.where(strict, sc_k[i][:M], 0.0))
            a_rb.append(jnp.where(lower, sc_b[i][M:], 0.0))
            a_rk.append(jnp.where(lower, sc_k[i][M:], 0.0))
        else:
            n_mat.append(sc_b[i][:M] * u["dm_a"])
            a_ak.append(sc_k[i][:M] * u["dm_a"])
            a_rb.append(sc_b[i][M:] * u["dm_r"])
            a_rk.append(sc_k[i][M:] * u["dm_r"])
    if own is None:
        a_h = [r[:M] for r in reads]
        r_h = [r[M:] for r in reads]
    else:
        a_h = [jnp.where(own, r[:M], 0.0) for r in reads]
        r_h = [jnp.where(own, r[M:], 0.0) for r in reads]

    av = [mm(jnp.concatenate([a_ak[i], a_rk[i]], axis=0), v_st[i]) for i in range(nu)]
    x = [a_h[i] + av[i][:M] for i in range(nu)]
    y_part = [r_h[i] + av[i][M:] for i in range(nu)]
    steps = int(math.log2(L))
    for s in range(steps):
        if s + 1 < steps:
            t = [mm(n_mat[i], jnp.concatenate([x[i], n_mat[i]], axis=1)) for i in range(nu)]
            x = [x[i] + t[i][:, :LANES] for i in range(nu)]
            n_mat = [t[i][:, LANES:] for i in range(nu)]
        else:
            x = [x[i] + mm(n_mat[i], x[i]) for i in range(nu)]
    y_st = [y_part[i] + mm(a_rb[i], x[i]) for i in range(nu)]
    ys = []
    for i in range(nu):
        y = y_st[i][:L]
        for j in range(1, nsub):
            y = y + y_st[i][j * L:(j + 1) * L]
        ys.append(y)
    s_new = [u["s_prev"] * u["p_last"]
             + _dot_tn(jnp.concatenate([x[i], v_st[i]], axis=0),
                       jnp.concatenate([_stack_mask(u["b_dec"], nsub), _stack_mask(u["k_dec"], nsub)], axis=0))
             for i, u in enumerate(units)]
    return ys, s_new


def _gdn_kernel(q_ref, k_ref, v_ref, z_ref, sm_ref, alog_ref, dtb_ref, nw_ref, o_ref, st_ref, *, L, nseq):
    @pl.when(pl.program_id(1) == 0)
    def _():
        st_ref[...] = jnp.zeros(st_ref.shape, F32)

    row = lax.broadcasted_iota(jnp.int32, (L, L), 0)
    col = lax.broadcasted_iota(jnp.int32, (L, L), 1)
    nw = nw_ref[...]
    scale = GDN_HEAD_DIM ** -0.5
    units = []
    for s in range(nseq):
        sm = sm_ref[s]
        g = -jnp.exp(alog_ref[...]) * _softplus(sm + dtb_ref[...])
        beta = _sigmoid(sm)
        g_cum = _dot_w3(_tri_incl(L), g)
        g_cum_t = jnp.concatenate([g_cum, jnp.zeros((LANES - L, LANES), F32)], axis=0).T
        for h in range(GDN_HEADS):
            sl = slice(h * LANES, (h + 1) * LANES)
            q = q_ref[s, :, sl]
            k = k_ref[s, :, sl]
            qn = q * (lax.rsqrt(jnp.sum(q * q, axis=-1, keepdims=True) + L2_EPS) * scale)
            kn = k * lax.rsqrt(jnp.sum(k * k, axis=-1, keepdims=True) + L2_EPS)
            gc = g_cum[:, SMALL_A + h:SMALL_A + h + 1]
            gr = g_cum_t[SMALL_A + h:SMALL_A + h + 1, 0:L]
            gs = g[:, SMALL_A + h:SMALL_A + h + 1]
            bt = beta[:, SMALL_B + h:SMALL_B + h + 1]
            g_prev = gc - gs
            g_end = g_cum[L - 1:L, SMALL_A + h:SMALL_A + h + 1]
            b_vec = -(bt * jnp.exp(gs)) * kn
            k_vec = bt * kn
            e_end = jnp.exp(g_end - gc)
            units.append(dict(
                a_sc=kn, r_sc=qn, b_sc=b_vec, k_sc=k_vec,
                a_sr=kn * jnp.exp(g_prev), r_sr=qn * jnp.exp(gc),
                b_dec=b_vec * e_end, k_dec=k_vec * e_end,
                v=v_ref[s, :, sl], s_prev=st_ref[s, h], p_last=jnp.exp(g_end),
                dm_a=jnp.exp(jnp.where(col < row, g_prev - gr, -jnp.inf)),
                dm_r=jnp.exp(jnp.where(col <= row, gc - gr, -jnp.inf))))
    ys, s_new = _dplr_chunks(units, L=L, nsub=1)
    for s in range(nseq):
        for h in range(GDN_HEADS):
            sl = slice(h * LANES, (h + 1) * LANES)
            st_ref[s, h] = s_new[s * GDN_HEADS + h]
            y = ys[s * GDN_HEADS + h]
            yn = y * lax.rsqrt(jnp.mean(y * y, axis=-1, keepdims=True) + NORM_EPS) * nw
            o_ref[s, :, sl] = (yn * _silu(z_ref[s, :, sl])).astype(o_ref.dtype)


def _gdn_branch(pc, pn, a_log, dt_bias, norm_w, bsz, seq):
    L = DPLR_CHUNK
    nseq = SEQ_PER_STEP if bsz % SEQ_PER_STEP == 0 else 1
    pc3 = pc.reshape(bsz, seq, N_PC)
    pn3 = pn.reshape(bsz, seq, N_PN)
    vec = lambda w: pl.BlockSpec((1, w), lambda b, c: (0, 0))
    tok = lambda width, col: pl.BlockSpec((nseq, L, width), lambda b, c: (b, c, col // width))
    out = pl.pallas_call(
        functools.partial(_gdn_kernel, L=L, nseq=nseq),
        grid=(bsz // nseq, seq // L),
        in_specs=[tok(GDN_INNER, PC_Q), tok(GDN_INNER, PC_K), tok(GDN_INNER, PC_V),
                  tok(GDN_INNER, PN_ZGDN), tok(LANES, PC_SMALL),
                  vec(LANES), vec(LANES), vec(LANES)],
        out_specs=pl.BlockSpec((nseq, L, GDN_INNER), lambda b, c: (b, c, 0)),
        out_shape=jax.ShapeDtypeStruct((bsz, seq, GDN_INNER), BF16),
        scratch_shapes=[pltpu.VMEM((nseq, GDN_HEADS, LANES, LANES), F32)],
        compiler_params=_cparams("parallel", "arbitrary"),
    )(pc3, pc3, pc3, pn3, pc3,
      _pad_lanes(a_log, SMALL_A), _pad_lanes(dt_bias, SMALL_A),
      norm_w.reshape(1, GDN_HEAD_DIM).astype(F32))
    return out.reshape(bsz * seq, GDN_INNER)


def _rwkv_kernel(rkv_ref, lora_ref, mu_ref, mul_ref, w0_ref, w2_ref, a0_ref, a2_ref, g2_ref,
                 kk_ref, ka_ref, rk_ref, lnw_ref, lnb_ref, o_ref, sh_ref, shl_ref, st_ref, *, L, nseq):
    @pl.when(pl.program_id(1) == 0)
    def _():
        st_ref[...] = jnp.zeros(st_ref.shape, F32)
        for s in range(nseq):
            sh_ref[s, 0:SUBLANES, :] = jnp.zeros((SUBLANES, sh_ref.shape[2]), F32)
            shl_ref[s, 0:SUBLANES, :] = jnp.zeros((SUBLANES, shl_ref.shape[2]), F32)

    n = RWKV_INNER
    npair = RWKV_HEADS // 2
    ones_bd = _half_ones()
    inv_hd = 1.0 / RWKV_HEAD_DIM

    def head_sums(t):
        st = jnp.concatenate([t[:, p * LANES:(p + 1) * LANES] for p in range(npair)], axis=0)
        sm = _dot(st, ones_bd)
        return jnp.concatenate([sm[p * L:(p + 1) * L] for p in range(npair)], axis=1)

    units, tails = [], []
    for s in range(nseq):
        u = rkv_ref[s]
        ul = lora_ref[s]
        sh_ref[s, SUBLANES:SUBLANES + L, :] = u
        shl_ref[s, SUBLANES:SUBLANES + L, :] = ul
        u = u + (sh_ref[s, pl.ds(SUBLANES - 1, L), :] - u) * mu_ref[...]
        ul = ul + (shl_ref[s, pl.ds(SUBLANES - 1, L), :] - ul) * mul_ref[...]
        sh_ref[s, 0:SUBLANES, :] = sh_ref[s, L:L + SUBLANES, :]
        shl_ref[s, 0:SUBLANES, :] = shl_ref[s, L:L + SUBLANES, :]

        r, k, v = u[:, 0:n], u[:, n:2 * n], u[:, 2 * n:3 * n]
        lo = ul[:, 0:LANES]
        dg = ul[:, LANES:2 * LANES]
        w_log = -_softplus(-(w0_ref[...] + _dot_hl(jnp.tanh(lo), w2_ref[...]))) - 0.5
        lw = -jnp.exp(w_log)
        a_ic = _sigmoid(a0_ref[...] + _dot_hl(lo, a2_ref[...]))
        gate = _dot(_sigmoid(dg), g2_ref[...])
        g_cum = _dot_w3(_tri_incl(L), lw)
        kk_raw = k * kk_ref[...]
        k_mod = k * (1.0 + (a_ic - 1.0) * ka_ref[...])
        kk_all = kk_raw * lax.rsqrt(head_sums(kk_raw * kk_raw) + L2_EPS)
        tails.append((head_sums(r * k_mod * rk_ref[...]) * v, gate))
        for p in range(npair):
            sl = slice(p * LANES, (p + 1) * LANES)
            kk = kk_all[:, sl]
            gc = g_cum[:, sl]
            g_end = gc[L - 1:L, :]
            e_neg = jnp.exp(-gc)
            e_end = jnp.exp(g_end - gc)
            b_vec = kk * a_ic[:, sl]
            k_vec = k_mod[:, sl]
            a_t = -kk * jnp.exp(gc - lw[:, sl])
            r_t = r[:, sl] * jnp.exp(gc)
            units.append(dict(
                a_sc=a_t, r_sc=r_t, b_sc=b_vec * e_neg, k_sc=k_vec * e_neg,
                a_sr=a_t, r_sr=r_t, b_dec=b_vec * e_end, k_dec=k_vec * e_end,
                v=v[:, sl], s_prev=st_ref[s, p], p_last=jnp.exp(g_end)))
    ys, s_new = _dplr_chunks(units, L=L, nsub=2)
    for s in range(nseq):
        for p in range(npair):
            st_ref[s, p] = s_new[s * npair + p]
        y = jnp.concatenate(ys[s * npair:(s + 1) * npair], axis=1)
        yc = y - head_sums(y) * inv_hd
        var = head_sums(yc * yc) * inv_hd
        yn = yc * lax.rsqrt(var + RWKV_GN_EPS) * lnw_ref[...] + lnb_ref[...]
        bonus, gate = tails[s]
        o_ref[s] = ((yn + bonus) * gate).astype(o_ref.dtype)


def _rwkv_branch(pc, pn, mu, w0, w2, a0, a2, g2, k_k, k_a, r_k, ln_w, ln_b, bsz, seq):
    L = DPLR_CHUNK
    n = RWKV_INNER
    nseq = SEQ_PER_STEP if bsz % SEQ_PER_STEP == 0 else 1
    pc3 = pc.reshape(bsz, seq, N_PC)
    pn3 = pn.reshape(bsz, seq, N_PN)
    vec = lambda w: pl.BlockSpec((1, w), lambda b, c: (0, 0))
    mat = lambda: pl.BlockSpec((LANES, n), lambda b, c: (0, 0))
    tok = lambda width, col: pl.BlockSpec((nseq, L, width), lambda b, c: (b, c, col // width))
    w2p = jnp.concatenate([w2, jnp.zeros((LANES - RWKV_DECAY_LORA, n), F32)], axis=0)
    a2p = jnp.concatenate([jnp.zeros((RWKV_DECAY_LORA, n), F32), a2], axis=0)
    r1 = lambda t: t.reshape(1, -1).astype(F32)
    out = pl.pallas_call(
        functools.partial(_rwkv_kernel, L=L, nseq=nseq),
        grid=(bsz // nseq, seq // L),
        in_specs=[tok(3 * n, PN_RKV), tok(256, PC_LORA),
                  vec(3 * n), vec(256), vec(n), mat(), vec(n), mat(), mat(),
                  vec(n), vec(n), vec(n), vec(n), vec(n)],
        out_specs=pl.BlockSpec((nseq, L, n), lambda b, c: (b, c, 0)),
        out_shape=jax.ShapeDtypeStruct((bsz, seq, n), BF16),
        scratch_shapes=[pltpu.VMEM((nseq, L + SUBLANES, 3 * n), F32),
                        pltpu.VMEM((nseq, L + SUBLANES, 256), F32),
                        pltpu.VMEM((nseq, RWKV_HEADS // 2, LANES, LANES), F32)],
        compiler_params=_cparams("parallel", "arbitrary"),
    )(pn3, pc3, r1(mu[:3 * n]), r1(mu[3 * n:]), r1(w0), w2p, r1(a0), a2p, g2.astype(F32),
      r1(k_k), r1(k_a), r1(r_k), r1(ln_w), r1(ln_b))
    return out.reshape(bsz * seq, n)


def _merge_kernel(x_ref, ys_ref, yg_ref, yr_ref, gate_ref, ps_ref, pg_ref, pr_ref, wo_ref, o_ref):
    d = D_MODEL
    m = _sigmoid(gate_ref[:, 0:d]) * jnp.dot(ys_ref[...], ps_ref[...], preferred_element_type=F32)
    m = m + _sigmoid(gate_ref[:, d:2 * d]) * jnp.dot(yg_ref[...], pg_ref[...], preferred_element_type=F32)
    m = m + _sigmoid(gate_ref[:, 2 * d:3 * d]) * jnp.dot(yr_ref[...], pr_ref[...], preferred_element_type=F32)
    o_ref[...] = x_ref[...] + jnp.dot(m.astype(BF16), wo_ref[...], preferred_element_type=F32)


def _merge(x, y_ssm, y_gdn, y_rwkv, pn, p_ssm, p_gdn, p_rwkv, w_out, tm=512):
    t, d = x.shape
    tok = lambda: pl.BlockSpec((tm, d), lambda i: (i, 0))
    wgt = lambda: pl.BlockSpec((d, d), lambda i: (0, 0))
    return pl.pallas_call(
        _merge_kernel,
        grid=(t // tm,),
        in_specs=[tok(), tok(), tok(), tok(),
                  pl.BlockSpec((tm, 3 * d), lambda i: (i, PN_GATE // (3 * d))),
                  wgt(), wgt(), wgt(), wgt()],
        out_specs=tok(),
        out_shape=jax.ShapeDtypeStruct((t, d), F32),
        compiler_params=_cparams("parallel"),
    )(x, y_ssm, y_gdn, y_rwkv, pn, p_ssm, p_gdn, p_rwkv, w_out)


def _ffn_kernel(x_ref, nw_ref, wg_ref, wu_ref, wd_ref, o_ref, h_ref, acc_ref):
    f = pl.program_id(1)

    @pl.when(f == 0)
    def _():
        x = x_ref[...]
        ms = jnp.mean(x * x, axis=-1, keepdims=True)
        h_ref[...] = (x * lax.rsqrt(ms + NORM_EPS) * nw_ref[...]).astype(BF16)
        acc_ref[...] = x

    h = h_ref[...]
    act = _silu(jnp.dot(h, wg_ref[...], preferred_element_type=F32)) * jnp.dot(
        h, wu_ref[...], preferred_element_type=F32)
    acc_ref[...] += jnp.dot(act.astype(BF16), wd_ref[...], preferred_element_type=F32)

    @pl.when(f == pl.num_programs(1) - 1)
    def _():
        o_ref[...] = acc_ref[...]


def _ffn_dense(x, norm_w, w_gate, w_up, w_down, tm=512, tf=1408):
    t, d = x.shape
    nf = w_gate.shape[1] // tf
    return pl.pallas_call(
        _ffn_kernel,
        grid=(t // tm, nf),
        in_specs=[pl.BlockSpec((tm, d), lambda i, f: (i, 0)),
                  pl.BlockSpec((1, d), lambda i, f: (0, 0)),
                  pl.BlockSpec((d, tf), lambda i, f: (0, f)),
                  pl.BlockSpec((d, tf), lambda i, f: (0, f)),
                  pl.BlockSpec((tf, d), lambda i, f: (f, 0))],
        out_specs=pl.BlockSpec((tm, d), lambda i, f: (i, 0)),
        out_shape=jax.ShapeDtypeStruct((t, d), F32),
        scratch_shapes=[pltpu.VMEM((tm, d), BF16), pltpu.VMEM((tm, d), F32)],
        compiler_params=_cparams("parallel", "arbitrary"),
    )(x, norm_w.reshape(1, d).astype(F32), w_gate, w_up, w_down)


MOE_TM = 1024
MOE_BLK = 128
MOE_FIRST_ROWS = 288


def _route_kernel(x_ref, nw_ref, rt_ref, h_ref, cmb_ref, cnt_ref):
    x = x_ref[...]
    lane = lax.broadcasted_iota(jnp.int32, cmb_ref.shape, 1)
    ms = jnp.mean(x * x, axis=-1, keepdims=True)
    hf = x * lax.rsqrt(ms + NORM_EPS) * nw_ref[...]
    h_ref[...] = hf.astype(BF16)
    logits = jnp.where(lane < N_EXPERTS, _dot_hl(hf, rt_ref[...]), -jnp.inf)
    m1 = jnp.max(logits, axis=-1, keepdims=True)
    i1 = jnp.min(jnp.where(logits == m1, lane, LANES), axis=-1, keepdims=True)
    rest = jnp.where(lane == i1, -jnp.inf, logits)
    m2 = jnp.max(rest, axis=-1, keepdims=True)
    i2 = jnp.min(jnp.where(rest == m2, lane, LANES), axis=-1, keepdims=True)
    e2 = jnp.exp(m2 - m1)
    p1 = 1.0 / (1.0 + e2)
    cmb = jnp.where(lane == i1, p1, 0.0) + jnp.where(lane == i2, e2 * p1, 0.0)
    cmb_ref[...] = cmb
    cnt_ref[0] = jnp.sum(jnp.where(cmb > 0.0, 1.0, 0.0), axis=0, keepdims=True).astype(jnp.int32)


def _moe_route(x, norm_w, router, tm):
    t, d = x.shape
    ne = router.shape[1]
    rt = jnp.concatenate([router.astype(F32), jnp.zeros((d, LANES - ne), F32)], axis=1)
    return pl.pallas_call(
        _route_kernel,
        grid=(t // tm,),
        in_specs=[pl.BlockSpec((tm, d), lambda i: (i, 0)),
                  pl.BlockSpec((1, d), lambda i: (0, 0)),
                  pl.BlockSpec((d, LANES), lambda i: (0, 0))],
        out_specs=[pl.BlockSpec((tm, d), lambda i: (i, 0)),
                   pl.BlockSpec((tm, LANES), lambda i: (i, 0)),
                   pl.BlockSpec((1, 1, LANES), lambda i: (i, 0, 0))],
        out_shape=[jax.ShapeDtypeStruct((t, d), BF16),
                   jax.ShapeDtypeStruct((t, LANES), F32),
                   jax.ShapeDtypeStruct((t // tm, 1, LANES), jnp.int32)],
        compiler_params=_cparams("parallel"),
    )(x, norm_w.reshape(1, d).astype(F32), rt)


def _moe_kernel(cnt_ref, h_ref, cmb_ref, wg_ref, wu_ref, wd_ref, o_ref, rank_ref, xc_ref, yc_ref, oh_ref,
                *, tm):
    i = pl.program_id(0)
    e = pl.program_id(1)
    f = pl.program_id(2)
    last_f = f == pl.num_programs(2) - 1
    lane = lax.broadcasted_iota(jnp.int32, (tm, LANES), 1)

    @pl.when((e == 0) & (f == 0))
    def _():
        o_ref[...] = jnp.zeros(o_ref.shape, F32)
        r = lax.broadcasted_iota(jnp.int32, (tm, tm), 0)
        c = lax.broadcasted_iota(jnp.int32, (tm, tm), 1)
        before = jnp.where(c < r, 1.0, 0.0).astype(BF16)
        sel = jnp.where(cmb_ref[...] > 0.0, 1.0, 0.0).astype(BF16)
        rank_ref[...] = jnp.dot(before, sel, preferred_element_type=F32)

    cnt = cnt_ref[i, e]

    def weight():
        return jnp.sum(jnp.where(lane == e, cmb_ref[...], 0.0), axis=-1, keepdims=True)

    def one_hot(off, rows):
        r_e = jnp.sum(jnp.where(lane == e, rank_ref[...], 0.0), axis=-1, keepdims=True)
        key = jnp.where(weight() > 0.0, r_e, -1.0) - jnp.asarray(off, F32)
        slot = lax.broadcasted_iota(jnp.int32, (tm, rows), 1).astype(F32)
        return jnp.where(key == slot, 1.0, 0.0).astype(BF16)

    def process(off, rows, cache):
        rs = pl.ds(off, rows)

        @pl.when(f == 0)
        def _():
            onehot = one_hot(off, rows)
            if cache:
                oh_ref[...] = onehot
            xc_ref[rs, :] = _dot_tn(onehot, h_ref[...]).astype(BF16)
            yc_ref[rs, :] = jnp.zeros((rows, yc_ref.shape[1]), F32)

        xc = xc_ref[rs, :]
        act = _silu(jnp.dot(xc, wg_ref[0], preferred_element_type=F32)) * jnp.dot(
            xc, wu_ref[0], preferred_element_type=F32)
        yc_ref[rs, :] += jnp.dot(act.astype(BF16), wd_ref[0], preferred_element_type=F32)

        @pl.when(last_f)
        def _():
            onehot = oh_ref[...] if cache else one_hot(off, rows)
            o_ref[...] += weight() * jnp.dot(onehot, yc_ref[rs, :].astype(BF16), preferred_element_type=F32)

    @pl.when(cnt > 0)
    def _():
        process(0, MOE_FIRST_ROWS, True)

    def extra(b, carry):
        process(pl.multiple_of(MOE_FIRST_ROWS + b * MOE_BLK, 32), MOE_BLK, False)
        return carry

    lax.fori_loop(0, (jnp.maximum(cnt - MOE_FIRST_ROWS, 0) + MOE_BLK - 1) // MOE_BLK, extra, 0)


def _add_norm_kernel(x_ref, y_ref, w_ref, o_ref, *, final_norm):
    y = x_ref[...] + y_ref[...]
    if final_norm:
        y = y * lax.rsqrt(jnp.mean(y * y, axis=-1, keepdims=True) + NORM_EPS) * w_ref[...]
    o_ref[...] = y


def _ffn_moe(x, norm_w, router, w_gate, w_up, w_down, final_w, tm=MOE_TM, tf=1408, tr=512):
    t, d = x.shape
    ne, _, ff = w_gate.shape
    nf = ff // tf
    h, cmb, cnt = _moe_route(x, norm_w, router, tm)
    cap = MOE_FIRST_ROWS + -(-max(tm - MOE_FIRST_ROWS, 0) // MOE_BLK) * MOE_BLK
    grid_spec = pltpu.PrefetchScalarGridSpec(
        num_scalar_prefetch=1,
        grid=(t // tm, ne, nf),
        in_specs=[pl.BlockSpec((tm, d), lambda i, e, f, c: (i, 0)),
                  pl.BlockSpec((tm, LANES), lambda i, e, f, c: (i, 0)),
                  pl.BlockSpec((1, d, tf), lambda i, e, f, c: (e, 0, f)),
                  pl.BlockSpec((1, d, tf), lambda i, e, f, c: (e, 0, f)),
                  pl.BlockSpec((1, tf, d), lambda i, e, f, c: (e, f, 0))],
        out_specs=pl.BlockSpec((tm, d), lambda i, e, f, c: (i, 0)),
        scratch_shapes=[pltpu.VMEM((tm, LANES), F32), pltpu.VMEM((cap, d), BF16), pltpu.VMEM((cap, d), F32),
                        pltpu.VMEM((tm, MOE_FIRST_ROWS), BF16)])
    y = pl.pallas_call(
        functools.partial(_moe_kernel, tm=tm),
        grid_spec=grid_spec,
        out_shape=jax.ShapeDtypeStruct((t, d), F32),
        compiler_params=_cparams("parallel", "arbitrary", "arbitrary"),
    )(cnt.reshape(t // tm, LANES), h, cmb, w_gate, w_up, w_down)
    final_norm = final_w is not None
    fw = (final_w if final_norm else jnp.ones((d,), F32)).reshape(1, d).astype(F32)
    tr = min(tr, t)
    return pl.pallas_call(
        functools.partial(_add_norm_kernel, final_norm=final_norm),
        grid=(t // tr,),
        in_specs=[pl.BlockSpec((tr, d), lambda i: (i, 0)),
                  pl.BlockSpec((tr, d), lambda i: (i, 0)),
                  pl.BlockSpec((1, d), lambda i: (0, 0))],
        out_specs=pl.BlockSpec((tr, d), lambda i: (i, 0)),
        out_shape=jax.ShapeDtypeStruct((t, d), F32),
        compiler_params=_cparams("parallel"),
    )(x, y, fw)


def _reorder_w_in(w):
    d = w.shape[0]
    wc = [w[:, 0:1024], w[:, 1536:4608], w[:, 1024:1536], w[:, 9760:10016], w[:, 5632:5648],
          w[:, 6672:6688], jnp.zeros((d, N_PC - PC_SMALL - 32), w.dtype)]
    wn = [w[:, 6688:9760], w[:, 10016:13088], w[:, 4608:5632], w[:, 5648:6672]]
    return jnp.concatenate(wc, axis=1).astype(BF16), jnp.concatenate(wn, axis=1).astype(BF16)


def _token_mixer_layer(x, i, bsz, seq, p):
    h = _rmsnorm(x, p["attn_norm_w"][i], BF16)
    wc, wn = _reorder_w_in(p["w_in"][i])
    pc, pn = _inproj(h, wc, wn, p["conv_w"][i], p["conv_b"][i], seq, tm=1024)
    y_ssm = _ssd_branch(pc, pn, p["ssm_a_log"][i], p["ssm_dt_bias"][i], p["ssm_d"][i],
                        p["ssm_norm_w"][i], bsz, seq)
    y_gdn = _gdn_branch(pc, pn, p["gdn_a_log"][i], p["gdn_dt_bias"][i], p["gdn_norm_w"][i],
                        bsz, seq)
    y_rwkv = _rwkv_branch(pc, pn, p["rwkv_mu"][i], p["rwkv_w0"][i], p["rwkv_w2"][i], p["rwkv_a0"][i],
                          p["rwkv_a2"][i], p["rwkv_g2"][i], p["rwkv_k_k"][i], p["rwkv_k_a"][i],
                          p["rwkv_r_k"][i], p["rwkv_ln_w"][i], p["rwkv_ln_b"][i], bsz, seq)
    return _merge(x, y_ssm, y_gdn, y_rwkv, pn, p["proj_ssm"][i].astype(BF16),
                  p["proj_gdn"][i].astype(BF16), p["proj_rwkv"][i].astype(BF16),
                  p["w_out"][i].astype(BF16))


def kernel(x, attn_norm_w, w_in, conv_w, conv_b, ssm_a_log, ssm_dt_bias, ssm_d, ssm_norm_w, gdn_a_log, gdn_dt_bias, gdn_norm_w, rwkv_mu, rwkv_w0, rwkv_w2, rwkv_a0, rwkv_a2, rwkv_g2, rwkv_k_k, rwkv_k_a, rwkv_r_k, rwkv_ln_w, rwkv_ln_b, proj_ssm, proj_gdn, proj_rwkv, w_out, ffn_norm_w, dense_w_gate, dense_w_up, dense_w_down, moe_router, moe_w_gate, moe_w_up, moe_w_down, final_norm_w):
    p = dict(attn_norm_w=attn_norm_w, w_in=w_in, conv_w=conv_w, conv_b=conv_b, ssm_a_log=ssm_a_log,
             ssm_dt_bias=ssm_dt_bias, ssm_d=ssm_d, ssm_norm_w=ssm_norm_w, gdn_a_log=gdn_a_log,
             gdn_dt_bias=gdn_dt_bias, gdn_norm_w=gdn_norm_w, rwkv_mu=rwkv_mu, rwkv_w0=rwkv_w0,
             rwkv_w2=rwkv_w2, rwkv_a0=rwkv_a0, rwkv_a2=rwkv_a2, rwkv_g2=rwkv_g2, rwkv_k_k=rwkv_k_k,
             rwkv_k_a=rwkv_k_a, rwkv_r_k=rwkv_r_k, rwkv_ln_w=rwkv_ln_w, rwkv_ln_b=rwkv_ln_b,
             proj_ssm=proj_ssm, proj_gdn=proj_gdn, proj_rwkv=proj_rwkv, w_out=w_out)
    bsz, seq, d = x.shape
    depth = attn_norm_w.shape[0]
    xt = x.reshape(bsz * seq, d)
    for i in range(depth):
        xt = _token_mixer_layer(xt, i, bsz, seq, p)
        j = i // 2
        last = i == depth - 1
        if i % 2 == 0:
            xt = _ffn_dense(xt, ffn_norm_w[i], dense_w_gate[j].astype(BF16), dense_w_up[j].astype(BF16),
                            dense_w_down[j].astype(BF16))
            if last:
                xt = _rmsnorm(xt, final_norm_w, F32)
        else:
            xt = _ffn_moe(xt, ffn_norm_w[i], moe_router[j], moe_w_gate[j].astype(BF16),
                          moe_w_up[j].astype(BF16), moe_w_down[j].astype(BF16),
                          final_norm_w if last else None)
    return xt.reshape(bsz, seq, d)
```

```python
import functools
import math

import numpy as np
import jax
import jax.numpy as jnp
from jax import lax
from jax.experimental import pallas as pl
from jax.experimental.pallas import tpu as pltpu

F32 = jnp.float32
BF16 = jnp.bfloat16

LANES = 128
SUBLANES = 8
VMEM_LIMIT_BYTES = 56 * 1024 * 1024

D_MODEL = 1024
SSM_HEADS, SSM_HEAD_DIM, SSM_GROUPS, SSM_STATE = 16, 64, 2, 128
SSM_INNER = SSM_HEADS * SSM_HEAD_DIM
GDN_HEADS, GDN_HEAD_DIM = 8, 128
GDN_INNER = GDN_HEADS * GDN_HEAD_DIM
RWKV_HEADS, RWKV_HEAD_DIM = 16, 64
RWKV_INNER = RWKV_HEADS * RWKV_HEAD_DIM
RWKV_DECAY_LORA, RWKV_ICLR_LORA, RWKV_GATE_LORA = 64, 64, 128
RWKV_GN_EPS = 64e-5
CONV_K = 4
FFN_DIM = 2816
N_EXPERTS = 8
NORM_EPS = 1e-6
L2_EPS = 1e-6

PC_XS, PC_Q, PC_K, PC_V = 0, 1024, 2048, 3072
PC_BS, PC_CS = 4096, 4352
PC_CONV_END = 4608
PC_LORA = 4608
PC_SMALL = 4864
N_PC = 5120
PN_RKV = 0
PN_GATE = 3072
PN_ZSSM, PN_ZGDN = 6144, 7168
N_PN = 8192
SMALL_DT, SMALL_A, SMALL_B = 0, 16, 24

SSD_CHUNK = 128
DPLR_CHUNK = 64
SEQ_PER_STEP = 4
INPROJ_ROW_CHUNK = 512


def _cparams(*sem):
    return pltpu.CompilerParams(dimension_semantics=sem, vmem_limit_bytes=VMEM_LIMIT_BYTES)


def _dot(a, b):
    return jnp.dot(a.astype(BF16), b.astype(BF16), preferred_element_type=F32)


def _dot_nt(a, b):
    return lax.dot_general(a.astype(BF16), b.astype(BF16), (((1,), (1,)), ((), ())),
                           preferred_element_type=F32)


def _dot_tn(a, b):
    return lax.dot_general(a.astype(BF16), b.astype(BF16), (((0,), (0,)), ((), ())),
                           preferred_element_type=F32)


def _split3(x):
    hi = x.astype(BF16)
    r1 = x - hi.astype(F32)
    mid = r1.astype(BF16)
    lo = (r1 - mid.astype(F32)).astype(BF16)
    return hi, mid, lo


def _dot_x3(x, w):
    hi, mid, lo = _split3(x)
    return _dot(hi, w) + _dot(mid, w) + _dot(lo, w)


def _dot_w3(w, x):
    hi, mid, lo = _split3(x)
    return _dot(w, hi) + _dot(w, mid) + _dot(w, lo)


def _dot_hl(a, b):
    ah = a.astype(BF16)
    al = (a - ah.astype(F32)).astype(BF16)
    bh = b.astype(BF16)
    bl = (b - bh.astype(F32)).astype(BF16)
    return _dot(ah, bh) + _dot(ah, bl) + _dot(al, bh)


def _sigmoid(x):
    return 0.5 * jnp.tanh(0.5 * x) + 0.5


def _silu(x):
    return x * _sigmoid(x)


def _softplus(x):
    return jnp.maximum(x, 0.0) + jnp.log(1.0 + jnp.exp(-jnp.abs(x)))


def _tri_incl(n):
    r = lax.broadcasted_iota(jnp.int32, (n, n), 0)
    c = lax.broadcasted_iota(jnp.int32, (n, n), 1)
    return jnp.where(r >= c, 1.0, 0.0).astype(BF16)


def _half_ones():
    r = lax.broadcasted_iota(jnp.int32, (LANES, LANES), 0)
    c = lax.broadcasted_iota(jnp.int32, (LANES, LANES), 1)
    return jnp.where((r // 64) == (c // 64), 1.0, 0.0).astype(BF16)


def _rmsnorm_kernel(x_ref, w_ref, o_ref):
    x = x_ref[...]
    ms = jnp.mean(x * x, axis=-1, keepdims=True)
    o_ref[...] = (x * lax.rsqrt(ms + NORM_EPS) * w_ref[...]).astype(o_ref.dtype)


def _rmsnorm(x, w, out_dtype, tm=512):
    t, d = x.shape
    return pl.pallas_call(
        _rmsnorm_kernel,
        grid=(t // tm,),
        in_specs=[pl.BlockSpec((tm, d), lambda i: (i, 0)),
                  pl.BlockSpec((1, d), lambda i: (0, 0))],
        out_specs=pl.BlockSpec((tm, d), lambda i: (i, 0)),
        out_shape=jax.ShapeDtypeStruct((t, d), out_dtype),
        compiler_params=_cparams("parallel"),
    )(x, w.reshape(1, d))


def _inproj_kernel(a_ref, wc_ref, wn_ref, cw_ref, cb_ref, cm_ref, oc_ref, on_ref, sh_ref, *,
                   tm, tiles_per_seq):
    i = pl.program_id(1)
    tc = oc_ref.shape[1]

    @pl.when(i % tiles_per_seq == 0)
    def _():
        sh_ref[0:SUBLANES, :] = jnp.zeros((SUBLANES, tc), F32)

    rc = min(tm, INPROJ_ROW_CHUNK)
    starts = list(range(0, tm, rc))
    acc = jnp.dot(a_ref[0:rc, :], wc_ref[...], preferred_element_type=F32)
    for n, r0 in enumerate(starts):
        sh_ref[SUBLANES + r0:SUBLANES + r0 + rc, :] = acc
        on_ref[r0:r0 + rc, :] = jnp.dot(a_ref[r0:r0 + rc, :], wn_ref[...], preferred_element_type=F32)
        if n + 1 < len(starts):
            acc_next = jnp.dot(a_ref[r0 + rc:r0 + 2 * rc, :], wc_ref[...], preferred_element_type=F32)
        c = acc * cw_ref[CONV_K - 1:CONV_K, :] + cb_ref[...]
        for k in range(1, CONV_K):
            c = c + sh_ref[pl.ds(SUBLANES + r0 - k, rc), :] * cw_ref[CONV_K - 1 - k:CONV_K - k, :]
        oc_ref[r0:r0 + rc, :] = jnp.where(cm_ref[...] > 0.0, _silu(c), acc)
        if n + 1 < len(starts):
            acc = acc_next
    sh_ref[0:SUBLANES, :] = sh_ref[tm:tm + SUBLANES, :]


def _conv_cols(c):
    r = c.shape[0]
    return jnp.concatenate([c[:, 0:1024], c[:, 1536:4608], c[:, 1024:1536],
                            jnp.zeros((r, N_PC - PC_CONV_END), F32)], axis=1).astype(F32)


def _inproj(a, wc, wn, conv_w, conv_b, seq, tm, nsteps=8):
    m, k = a.shape
    tc, tn = N_PC // nsteps, N_PN // nsteps
    cw = _conv_cols(conv_w)
    cb = _conv_cols(conv_b.reshape(1, -1))
    cm = _conv_cols(jnp.ones((1, conv_w.shape[1]), F32))
    return pl.pallas_call(
        functools.partial(_inproj_kernel, tm=tm, tiles_per_seq=seq // tm),
        grid=(nsteps, m // tm),
        in_specs=[pl.BlockSpec((tm, k), lambda j, i: (i, 0)),
                  pl.BlockSpec((k, tc), lambda j, i: (0, j)),
                  pl.BlockSpec((k, tn), lambda j, i: (0, j)),
                  pl.BlockSpec((CONV_K, tc), lambda j, i: (0, j)),
                  pl.BlockSpec((1, tc), lambda j, i: (0, j)),
                  pl.BlockSpec((1, tc), lambda j, i: (0, j))],
        out_specs=[pl.BlockSpec((tm, tc), lambda j, i: (i, j)),
                   pl.BlockSpec((tm, tn), lambda j, i: (i, j))],
        out_shape=[jax.ShapeDtypeStruct((m, N_PC), F32), jax.ShapeDtypeStruct((m, N_PN), F32)],
        scratch_shapes=[pltpu.VMEM((tm + SUBLANES, tc), F32)],
        compiler_params=_cparams("parallel", "arbitrary"),
    )(a, wc, wn, cw, cb, cm)


def _ssd_kernel(xs_ref, bs_ref, cs_ref, z_ref, sm_ref, alog_ref, dtb_ref, dsk_ref, nw_ref, ex_ref,
                o_ref, st_ref, *, L):
    @pl.when(pl.program_id(1) == 0)
    def _():
        st_ref[...] = jnp.zeros(st_ref.shape, F32)

    hd = SSM_HEAD_DIM
    gw = SSM_INNER // SSM_GROUPS
    dt = _softplus(sm_ref[...] + dtb_ref[...])
    la = -jnp.exp(alog_ref[...]) * dt
    g_cum = _dot_w3(_tri_incl(L), la)
    g_cum_t = g_cum.T
    ex = ex_ref[...]
    dt_e = _dot_x3(dt, ex)
    g_e = _dot_x3(g_cum, ex)
    g_last = g_e[L - 1:L, :]
    e_g = jnp.exp(g_e)
    e_dec = jnp.exp(g_last - g_e)
    e_last = jnp.exp(g_last)

    xs = xs_ref[...]
    xdt = xs * dt_e
    xdec = xdt * e_dec
    row = lax.broadcasted_iota(jnp.int32, (L, L), 0)
    col = lax.broadcasted_iota(jnp.int32, (L, L), 1)
    causal = row >= col
    lane = lax.broadcasted_iota(jnp.int32, (L, LANES), 1)
    first_half = lane < hd

    y_parts = []
    for g in range(SSM_GROUPS):
        b_g = bs_ref[:, g * SSM_STATE:(g + 1) * SSM_STATE]
        c_g = cs_ref[:, g * SSM_STATE:(g + 1) * SSM_STATE]
        cb = _dot_nt(c_g, b_g)
        st_g = st_ref[:, g * gw:(g + 1) * gw]
        y_off = _dot(c_g, st_g) * e_g[:, g * gw:(g + 1) * gw]
        st_ref[:, g * gw:(g + 1) * gw] = (st_g * e_last[:, g * gw:(g + 1) * gw]
                                          + _dot_tn(b_g, xdec[:, g * gw:(g + 1) * gw]))
        heads_per_group = SSM_HEADS // SSM_GROUPS
        for p in range(heads_per_group // 2):
            h0 = g * heads_per_group + 2 * p
            base = h0 * hd
            sc = []
            for h in (h0, h0 + 1):
                diff = g_cum[:, h:h + 1] - g_cum_t[h:h + 1, :]
                sc.append(cb * jnp.exp(jnp.where(causal, diff, -jnp.inf)))
            yd = _dot(jnp.concatenate(sc, axis=0), xdt[:, base:base + LANES])
            y_parts.append(jnp.where(first_half, yd[:L], yd[L:])
                           + y_off[:, base - g * gw:base - g * gw + LANES])
    y = jnp.concatenate(y_parts, axis=1) + dsk_ref[...] * xs
    y = y * _silu(z_ref[...])
    outs = []
    for g in range(SSM_GROUPS):
        seg = y[:, g * gw:(g + 1) * gw]
        outs.append(seg * lax.rsqrt(jnp.mean(seg * seg, axis=-1, keepdims=True) + NORM_EPS))
    o_ref[...] = (jnp.concatenate(outs, axis=1) * nw_ref[...]).astype(o_ref.dtype)


def _pad_lanes(v, offset=0, width=LANES):
    out = jnp.zeros((1, width), F32)
    return lax.dynamic_update_slice(out, v.reshape(1, -1).astype(F32), (0, offset))


def _ssd_branch(pc, pn, a_log, dt_bias, d_skip, norm_w, bsz, seq):
    L = SSD_CHUNK
    nc = seq // L
    ex = np.zeros((LANES, SSM_INNER), np.float32)
    for h in range(SSM_HEADS):
        ex[h, h * SSM_HEAD_DIM:(h + 1) * SSM_HEAD_DIM] = 1.0
    row = lambda b, c: b * nc + c
    vec = lambda w: pl.BlockSpec((1, w), lambda b, c: (0, 0))
    return pl.pallas_call(
        functools.partial(_ssd_kernel, L=L),
        grid=(bsz, nc),
        in_specs=[pl.BlockSpec((L, SSM_INNER), lambda b, c: (row(b, c), PC_XS // SSM_INNER)),
                  pl.BlockSpec((L, 256), lambda b, c: (row(b, c), PC_BS // 256)),
                  pl.BlockSpec((L, 256), lambda b, c: (row(b, c), PC_CS // 256)),
                  pl.BlockSpec((L, SSM_INNER), lambda b, c: (row(b, c), PN_ZSSM // SSM_INNER)),
                  pl.BlockSpec((L, LANES), lambda b, c: (row(b, c), PC_SMALL // LANES)),
                  vec(LANES), vec(LANES), vec(SSM_INNER), vec(SSM_INNER),
                  pl.BlockSpec((LANES, SSM_INNER), lambda b, c: (0, 0))],
        out_specs=pl.BlockSpec((L, SSM_INNER), lambda b, c: (row(b, c), 0)),
        out_shape=jax.ShapeDtypeStruct((bsz * seq, SSM_INNER), BF16),
        scratch_shapes=[pltpu.VMEM((SSM_STATE, SSM_INNER), F32)],
        compiler_params=_cparams("parallel", "arbitrary"),
    )(pc, pc, pc, pn, pc,
      _pad_lanes(a_log, SMALL_DT), _pad_lanes(dt_bias, SMALL_DT),
      jnp.repeat(d_skip.astype(F32), SSM_HEAD_DIM).reshape(1, SSM_INNER),
      norm_w.reshape(1, SSM_INNER).astype(F32), jnp.asarray(ex, BF16))


def _head_stack(x):
    half = x.shape[1] // 2
    lane = lax.broadcasted_iota(jnp.int32, x.shape, 1)
    return jnp.concatenate([jnp.where(lane < half, x, 0.0), jnp.where(lane >= half, x, 0.0)], axis=0)


def _dplr_pairs(units, *, L):
    nu = len(units)
    P = 2 * L
    vc = units[0]["v"].shape[1]
    row = lax.broadcasted_iota(jnp.int32, (L, P), 0)
    tcol = lax.broadcasted_iota(jnp.int32, (L, P), 1)
    first = tcol < L
    tcol = jnp.where(first, tcol, tcol - L)
    strict, lower = tcol < row, tcol <= row

    sc = [_dot_nt(jnp.concatenate([u["a_sc"], u["r_sc"]], axis=0),
                  jnp.concatenate([_head_stack(u["b_sc"]), _head_stack(u["k_sc"])], axis=0))
          for u in units]
    v_st = [_head_stack(u["v"]) for u in units]

    def read(u):
        rows = jnp.concatenate([u["a_sr"], u["r_sr"]], axis=0)
        st = u["s_prev"]
        if len(st) == 1:
            return _dot_nt(rows, st[0])
        kh = rows.shape[1] // 2
        return jnp.concatenate([_dot_nt(rows[:, :kh], st[0]), _dot_nt(rows[:, kh:], st[1])], axis=1)

    reads = [read(u) for u in units]
    n_mat, a_rb, aks = [], [], []
    for i, u in enumerate(units):
        sb, sk = sc[i][:, :P], sc[i][:, P:]
        if u.get("dm_a") is None:
            n_mat.append(jnp.where(strict, sb[:L], 0.0))
            a_rb.append(jnp.where(lower, sb[L:], 0.0))
            aks.append(jnp.concatenate([jnp.where(strict, sk[:L], 0.0), jnp.where(lower, sk[L:], 0.0)], axis=0))
        else:
            n_mat.append(sb[:L] * u["dm_a"])
            a_rb.append(sb[L:] * u["dm_r"])
            aks.append(jnp.concatenate([sk[:L] * u["dm_a"], sk[L:] * u["dm_r"]], axis=0))
    av = [_dot(aks[i], v_st[i]) for i in range(nu)]
    x = [reads[i][:L] + av[i][:L] for i in range(nu)]
    y_part = [reads[i][L:] + av[i][L:] for i in range(nu)]

    steps = int(math.log2(L))
    for s in range(steps):
        x_st = [_head_stack(x[i]) for i in range(nu)]
        if s + 1 < steps:
            t = [_dot(n_mat[i], jnp.concatenate(
                    [x_st[i], jnp.concatenate([jnp.where(first, n_mat[i], 0.0),
                                               jnp.where(first, 0.0, n_mat[i])], axis=0)], axis=1))
                 for i in range(nu)]
            x = [x[i] + t[i][:, :vc] for i in range(nu)]
            n_mat = [t[i][:, vc:] for i in range(nu)]
        else:
            x = [x[i] + _dot(n_mat[i], x_st[i]) for i in range(nu)]
    ys = [y_part[i] + _dot(a_rb[i], _head_stack(x[i])) for i in range(nu)]

    s_new = []
    for i, u in enumerate(units):
        xv = jnp.concatenate([x[i], u["v"]], axis=0)
        bk = jnp.concatenate([u["b_dec"], u["k_dec"]], axis=0)
        st, pl_ = u["s_prev"], u["p_last"]
        if len(st) == 1:
            upd = _dot_tn(xv, bk)
            ri = lax.broadcasted_iota(jnp.int32, upd.shape, 0) // (upd.shape[0] // 2)
            ci = lax.broadcasted_iota(jnp.int32, upd.shape, 1) // (upd.shape[1] // 2)
            s_new.append([st[0] * pl_[0] + jnp.where(ri == ci, upd, 0.0)])
        else:
            vh, kh = xv.shape[1] // 2, bk.shape[1] // 2
            s_new.append([st[0] * pl_[0] + _dot_tn(xv[:, :vh], bk[:, :kh]),
                          st[1] * pl_[1] + _dot_tn(xv[:, vh:], bk[:, kh:])])
    return ys, s_new


def _gdn_kernel(q_ref, k_ref, v_ref, z_ref, sm_ref, alog_ref, dtb_ref, nw_ref, o_ref, st_ref, *, L, nseq):
    @pl.when(pl.program_id(1) == 0)
    def _():
        st_ref[...] = jnp.zeros(st_ref.shape, F32)

    hd = GDN_HEAD_DIM
    row = lax.broadcasted_iota(jnp.int32, (L, 2 * L), 0)
    tcol = lax.broadcasted_iota(jnp.int32, (L, 2 * L), 1)
    first_t = tcol < L
    tcol = jnp.where(first_t, tcol, tcol - L)
    first_c = lax.broadcasted_iota(jnp.int32, (L, 2 * hd), 1) < hd
    nw = nw_ref[...]
    scale = hd ** -0.5
    units = []
    for s in range(nseq):
        sm = sm_ref[s]
        g = -jnp.exp(alog_ref[...]) * _softplus(sm + dtb_ref[...])
        beta = _sigmoid(sm)
        g_cum = _dot_w3(_tri_incl(L), g)
        g_cum_t = jnp.concatenate([g_cum] * (LANES // L), axis=0).T
        for p in range(GDN_HEADS // 2):
            sl = slice(2 * p * hd, (2 * p + 2) * hd)
            q = q_ref[s, :, sl]
            k = k_ref[s, :, sl]

            def per_head(t):
                return jnp.where(first_c, jnp.sum(t[:, :hd], axis=-1, keepdims=True),
                                 jnp.sum(t[:, hd:], axis=-1, keepdims=True))

            def col(arr, base):
                h0 = base + 2 * p
                return arr[:, h0:h0 + 1], arr[:, h0 + 1:h0 + 2]

            qn = q * (lax.rsqrt(per_head(q * q) + L2_EPS) * scale)
            kn = k * lax.rsqrt(per_head(k * k) + L2_EPS)
            gc0, gc1 = col(g_cum, SMALL_A)
            gs0, gs1 = col(g, SMALL_A)
            bt0, bt1 = col(beta, SMALL_B)
            ch = lambda a0, a1: jnp.where(first_c, a0, a1)
            tm_ = lambda a0, a1: jnp.where(first_t, a0, a1)
            gc, gs, bt = ch(gc0, gc1), ch(gs0, gs1), ch(bt0, bt1)
            g_prev = gc - gs
            h0 = SMALL_A + 2 * p
            gr = tm_(g_cum_t[h0:h0 + 1, :], g_cum_t[h0 + 1:h0 + 2, :])
            ge0, ge1 = g_cum[L - 1:L, h0:h0 + 1], g_cum[L - 1:L, h0 + 1:h0 + 2]
            e_end = jnp.exp(ch(ge0, ge1) - gc)
            b_vec = -(bt * jnp.exp(gs)) * kn
            k_vec = bt * kn
            units.append(dict(
                a_sc=kn, r_sc=qn, b_sc=b_vec, k_sc=k_vec,
                a_sr=kn * jnp.exp(g_prev), r_sr=qn * jnp.exp(gc),
                b_dec=b_vec * e_end, k_dec=k_vec * e_end,
                v=v_ref[s, :, sl], s_prev=[st_ref[s, 2 * p], st_ref[s, 2 * p + 1]],
                p_last=[jnp.exp(ge0), jnp.exp(ge1)],
                dm_a=jnp.exp(jnp.where(tcol < row, tm_(gc0 - gs0, gc1 - gs1) - gr, -jnp.inf)),
                dm_r=jnp.exp(jnp.where(tcol <= row, tm_(gc0, gc1) - gr, -jnp.inf))))
    ys, s_new = _dplr_pairs(units, L=L)
    npair = GDN_HEADS // 2
    for s in range(nseq):
        for p in range(npair):
            sl = slice(2 * p * hd, (2 * p + 2) * hd)
            st_ref[s, 2 * p] = s_new[s * npair + p][0]
            st_ref[s, 2 * p + 1] = s_new[s * npair + p][1]
            y = ys[s * npair + p]
            ms = jnp.where(first_c, jnp.mean(y[:, :hd] * y[:, :hd], axis=-1, keepdims=True),
                           jnp.mean(y[:, hd:] * y[:, hd:], axis=-1, keepdims=True))
            yn = y * lax.rsqrt(ms + NORM_EPS) * nw
            o_ref[s, :, sl] = (yn * _silu(z_ref[s, :, sl])).astype(o_ref.dtype)


def _gdn_branch(pc, pn, a_log, dt_bias, norm_w, bsz, seq):
    L = DPLR_CHUNK
    nseq = SEQ_PER_STEP if bsz % SEQ_PER_STEP == 0 else 1
    pc3 = pc.reshape(bsz, seq, N_PC)
    pn3 = pn.reshape(bsz, seq, N_PN)
    vec = lambda w: pl.BlockSpec((1, w), lambda b, c: (0, 0))
    tok = lambda width, col: pl.BlockSpec((nseq, L, width), lambda b, c: (b, c, col // width))
    out = pl.pallas_call(
        functools.partial(_gdn_kernel, L=L, nseq=nseq),
        grid=(bsz // nseq, seq // L),
        in_specs=[tok(GDN_INNER, PC_Q), tok(GDN_INNER, PC_K), tok(GDN_INNER, PC_V),
                  tok(GDN_INNER, PN_ZGDN), tok(LANES, PC_SMALL),
                  vec(LANES), vec(LANES), vec(2 * GDN_HEAD_DIM)],
        out_specs=pl.BlockSpec((nseq, L, GDN_INNER), lambda b, c: (b, c, 0)),
        out_shape=jax.ShapeDtypeStruct((bsz, seq, GDN_INNER), BF16),
        scratch_shapes=[pltpu.VMEM((nseq, GDN_HEADS, LANES, LANES), F32)],
        compiler_params=_cparams("parallel", "arbitrary"),
    )(pc3, pc3, pc3, pn3, pc3,
      _pad_lanes(a_log, SMALL_A), _pad_lanes(dt_bias, SMALL_A),
      jnp.tile(norm_w.reshape(1, GDN_HEAD_DIM).astype(F32), (1, 2)))
    return out.reshape(bsz * seq, GDN_INNER)


def _rwkv_kernel(rkv_ref, lora_ref, mu_ref, mul_ref, w0_ref, w2_ref, a0_ref, a2_ref, g2_ref,
                 kk_ref, ka_ref, rk_ref, lnw_ref, lnb_ref, o_ref, sh_ref, shl_ref, st_ref, *, L, nseq):
    @pl.when(pl.program_id(1) == 0)
    def _():
        st_ref[...] = jnp.zeros(st_ref.shape, F32)
        for s in range(nseq):
            sh_ref[s, 0:SUBLANES, :] = jnp.zeros((SUBLANES, sh_ref.shape[2]), F32)
            shl_ref[s, 0:SUBLANES, :] = jnp.zeros((SUBLANES, shl_ref.shape[2]), F32)

    n = RWKV_INNER
    npair = RWKV_HEADS // 2
    ones_bd = _half_ones()
    inv_hd = 1.0 / RWKV_HEAD_DIM

    def head_sums(t):
        st = jnp.concatenate([t[:, p * LANES:(p + 1) * LANES] for p in range(npair)], axis=0)
        sm = _dot(st, ones_bd)
        return jnp.concatenate([sm[p * L:(p + 1) * L] for p in range(npair)], axis=1)

    units, tails = [], []
    for s in range(nseq):
        u = rkv_ref[s]
        ul = lora_ref[s]
        sh_ref[s, SUBLANES:SUBLANES + L, :] = u
        shl_ref[s, SUBLANES:SUBLANES + L, :] = ul
        u = u + (sh_ref[s, pl.ds(SUBLANES - 1, L), :] - u) * mu_ref[...]
        ul = ul + (shl_ref[s, pl.ds(SUBLANES - 1, L), :] - ul) * mul_ref[...]
        sh_ref[s, 0:SUBLANES, :] = sh_ref[s, L:L + SUBLANES, :]
        shl_ref[s, 0:SUBLANES, :] = shl_ref[s, L:L + SUBLANES, :]

        r, k, v = u[:, 0:n], u[:, n:2 * n], u[:, 2 * n:3 * n]
        lo = ul[:, 0:LANES]
        dg = ul[:, LANES:2 * LANES]
        w_log = -_softplus(-(w0_ref[...] + _dot_hl(jnp.tanh(lo), w2_ref[...]))) - 0.5
        lw = -jnp.exp(w_log)
        a_ic = _sigmoid(a0_ref[...] + _dot_hl(lo, a2_ref[...]))
        gate = _dot(_sigmoid(dg), g2_ref[...])
        g_cum = _dot_w3(_tri_incl(L), lw)
        kk_raw = k * kk_ref[...]
        k_mod = k * (1.0 + (a_ic - 1.0) * ka_ref[...])
        kk_all = kk_raw * lax.rsqrt(head_sums(kk_raw * kk_raw) + L2_EPS)
        tails.append((head_sums(r * k_mod * rk_ref[...]) * v, gate))
        for p in range(npair):
            sl = slice(p * LANES, (p + 1) * LANES)
            kk = kk_all[:, sl]
            gc = g_cum[:, sl]
            g_end = gc[L - 1:L, :]
            e_neg = jnp.exp(-gc)
            e_end = jnp.exp(g_end - gc)
            b_vec = kk * a_ic[:, sl]
            k_vec = k_mod[:, sl]
            a_t = -kk * jnp.exp(gc - lw[:, sl])
            r_t = r[:, sl] * jnp.exp(gc)
            units.append(dict(
                a_sc=a_t, r_sc=r_t, b_sc=b_vec * e_neg, k_sc=k_vec * e_neg,
                a_sr=a_t, r_sr=r_t, b_dec=b_vec * e_end, k_dec=k_vec * e_end,
                v=v[:, sl], s_prev=[st_ref[s, p]], p_last=[jnp.exp(g_end)]))
    ys, s_new = _dplr_pairs(units, L=L)
    for s in range(nseq):
        for p in range(npair):
            st_ref[s, p] = s_new[s * npair + p][0]
        y = jnp.concatenate(ys[s * npair:(s + 1) * npair], axis=1)
        yc = y - head_sums(y) * inv_hd
        var = head_sums(yc * yc) * inv_hd
        yn = yc * lax.rsqrt(var + RWKV_GN_EPS) * lnw_ref[...] + lnb_ref[...]
        bonus, gate = tails[s]
        o_ref[s] = ((yn + bonus) * gate).astype(o_ref.dtype)


def _rwkv_branch(pc, pn, mu, w0, w2, a0, a2, g2, k_k, k_a, r_k, ln_w, ln_b, bsz, seq):
    L = DPLR_CHUNK
    n = RWKV_INNER
    nseq = SEQ_PER_STEP if bsz % SEQ_PER_STEP == 0 else 1
    pc3 = pc.reshape(bsz, seq, N_PC)
    pn3 = pn.reshape(bsz, seq, N_PN)
    vec = lambda w: pl.BlockSpec((1, w), lambda b, c: (0, 0))
    mat = lambda: pl.BlockSpec((LANES, n), lambda b, c: (0, 0))
    tok = lambda width, col: pl.BlockSpec((nseq, L, width), lambda b, c: (b, c, col // width))
    w2p = jnp.concatenate([w2, jnp.zeros((LANES - RWKV_DECAY_LORA, n), F32)], axis=0)
    a2p = jnp.concatenate([jnp.zeros((RWKV_DECAY_LORA, n), F32), a2], axis=0)
    r1 = lambda t: t.reshape(1, -1).astype(F32)
    out = pl.pallas_call(
        functools.partial(_rwkv_kernel, L=L, nseq=nseq),
        grid=(bsz // nseq, seq // L),
        in_specs=[tok(3 * n, PN_RKV), tok(256, PC_LORA),
                  vec(3 * n), vec(256), vec(n), mat(), vec(n), mat(), mat(),
                  vec(n), vec(n), vec(n), vec(n), vec(n)],
        out_specs=pl.BlockSpec((nseq, L, n), lambda b, c: (b, c, 0)),
        out_shape=jax.ShapeDtypeStruct((bsz, seq, n), BF16),
        scratch_shapes=[pltpu.VMEM((nseq, L + SUBLANES, 3 * n), F32),
                        pltpu.VMEM((nseq, L + SUBLANES, 256), F32),
                        pltpu.VMEM((nseq, RWKV_HEADS // 2, LANES, LANES), F32)],
        compiler_params=_cparams("parallel", "arbitrary"),
    )(pn3, pc3, r1(mu[:3 * n]), r1(mu[3 * n:]), r1(w0), w2p, r1(a0), a2p, g2.astype(F32),
      r1(k_k), r1(k_a), r1(r_k), r1(ln_w), r1(ln_b))
    return out.reshape(bsz * seq, n)


def _merge_kernel(x_ref, ys_ref, yg_ref, yr_ref, gate_ref, ps_ref, pg_ref, pr_ref, wo_ref, o_ref):
    d = D_MODEL
    m = _sigmoid(gate_ref[:, 0:d]) * jnp.dot(ys_ref[...], ps_ref[...], preferred_element_type=F32)
    m = m + _sigmoid(gate_ref[:, d:2 * d]) * jnp.dot(yg_ref[...], pg_ref[...], preferred_element_type=F32)
    m = m + _sigmoid(gate_ref[:, 2 * d:3 * d]) * jnp.dot(yr_ref[...], pr_ref[...], preferred_element_type=F32)
    o_ref[...] = x_ref[...] + jnp.dot(m.astype(BF16), wo_ref[...], preferred_element_type=F32)


def _merge(x, y_ssm, y_gdn, y_rwkv, pn, p_ssm, p_gdn, p_rwkv, w_out, tm=512):
    t, d = x.shape
    tok = lambda: pl.BlockSpec((tm, d), lambda i: (i, 0))
    wgt = lambda: pl.BlockSpec((d, d), lambda i: (0, 0))
    return pl.pallas_call(
        _merge_kernel,
        grid=(t // tm,),
        in_specs=[tok(), tok(), tok(), tok(),
                  pl.BlockSpec((tm, 3 * d), lambda i: (i, PN_GATE // (3 * d))),
                  wgt(), wgt(), wgt(), wgt()],
        out_specs=tok(),
        out_shape=jax.ShapeDtypeStruct((t, d), F32),
        compiler_params=_cparams("parallel"),
    )(x, y_ssm, y_gdn, y_rwkv, pn, p_ssm, p_gdn, p_rwkv, w_out)


def _ffn_kernel(x_ref, nw_ref, wg_ref, wu_ref, wd_ref, o_ref, h_ref, acc_ref):
    f = pl.program_id(1)

    @pl.when(f == 0)
    def _():
        x = x_ref[...]
        ms = jnp.mean(x * x, axis=-1, keepdims=True)
        h_ref[...] = (x * lax.rsqrt(ms + NORM_EPS) * nw_ref[...]).astype(BF16)
        acc_ref[...] = x

    h = h_ref[...]
    act = _silu(jnp.dot(h, wg_ref[...], preferred_element_type=F32)) * jnp.dot(
        h, wu_ref[...], preferred_element_type=F32)
    acc_ref[...] += jnp.dot(act.astype(BF16), wd_ref[...], preferred_element_type=F32)

    @pl.when(f == pl.num_programs(1) - 1)
    def _():
        o_ref[...] = acc_ref[...]


def _ffn_dense(x, norm_w, w_gate, w_up, w_down, tm=512, tf=1408):
    t, d = x.shape
    nf = w_gate.shape[1] // tf
    return pl.pallas_call(
        _ffn_kernel,
        grid=(t // tm, nf),
        in_specs=[pl.BlockSpec((tm, d), lambda i, f: (i, 0)),
                  pl.BlockSpec((1, d), lambda i, f: (0, 0)),
                  pl.BlockSpec((d, tf), lambda i, f: (0, f)),
                  pl.BlockSpec((d, tf), lambda i, f: (0, f)),
                  pl.BlockSpec((tf, d), lambda i, f: (f, 0))],
        out_specs=pl.BlockSpec((tm, d), lambda i, f: (i, 0)),
        out_shape=jax.ShapeDtypeStruct((t, d), F32),
        scratch_shapes=[pltpu.VMEM((tm, d), BF16), pltpu.VMEM((tm, d), F32)],
        compiler_params=_cparams("parallel", "arbitrary"),
    )(x, norm_w.reshape(1, d).astype(F32), w_gate, w_up, w_down)


MOE_TM = 1024
MOE_BLK = 128
MOE_FIRST_ROWS = 288


def _route_kernel(x_ref, nw_ref, rt_ref, h_ref, cmb_ref, cnt_ref):
    x = x_ref[...]
    lane = lax.broadcasted_iota(jnp.int32, cmb_ref.shape, 1)
    ms = jnp.mean(x * x, axis=-1, keepdims=True)
    hf = x * lax.rsqrt(ms + NORM_EPS) * nw_ref[...]
    h_ref[...] = hf.astype(BF16)
    logits = jnp.where(lane < N_EXPERTS, _dot_hl(hf, rt_ref[...]), -jnp.inf)
    m1 = jnp.max(logits, axis=-1, keepdims=True)
    i1 = jnp.min(jnp.where(logits == m1, lane, LANES), axis=-1, keepdims=True)
    rest = jnp.where(lane == i1, -jnp.inf, logits)
    m2 = jnp.max(rest, axis=-1, keepdims=True)
    i2 = jnp.min(jnp.where(rest == m2, lane, LANES), axis=-1, keepdims=True)
    e2 = jnp.exp(m2 - m1)
    p1 = 1.0 / (1.0 + e2)
    cmb = jnp.where(lane == i1, p1, 0.0) + jnp.where(lane == i2, e2 * p1, 0.0)
    cmb_ref[...] = cmb
    cnt_ref[0] = jnp.sum(jnp.where(cmb > 0.0, 1.0, 0.0), axis=0, keepdims=True).astype(jnp.int32)


def _moe_route(x, norm_w, router, tm):
    t, d = x.shape
    ne = router.shape[1]
    rt = jnp.concatenate([router.astype(F32), jnp.zeros((d, LANES - ne), F32)], axis=1)
    return pl.pallas_call(
        _route_kernel,
        grid=(t // tm,),
        in_specs=[pl.BlockSpec((tm, d), lambda i: (i, 0)),
                  pl.BlockSpec((1, d), lambda i: (0, 0)),
                  pl.BlockSpec((d, LANES), lambda i: (0, 0))],
        out_specs=[pl.BlockSpec((tm, d), lambda i: (i, 0)),
                   pl.BlockSpec((tm, LANES), lambda i: (i, 0)),
                   pl.BlockSpec((1, 1, LANES), lambda i: (i, 0, 0))],
        out_shape=[jax.ShapeDtypeStruct((t, d), BF16),
                   jax.ShapeDtypeStruct((t, LANES), F32),
                   jax.ShapeDtypeStruct((t // tm, 1, LANES), jnp.int32)],
        compiler_params=_cparams("parallel"),
    )(x, norm_w.reshape(1, d).astype(F32), rt)


def _moe_kernel(cnt_ref, h_ref, cmb_ref, wg_ref, wu_ref, wd_ref, o_ref, rank_ref, xc_ref, yc_ref, oh_ref,
                *, tm):
    i = pl.program_id(0)
    e = pl.program_id(1)
    f = pl.program_id(2)
    last_f = f == pl.num_programs(2) - 1
    lane = lax.broadcasted_iota(jnp.int32, (tm, LANES), 1)

    @pl.when((e == 0) & (f == 0))
    def _():
        o_ref[...] = jnp.zeros(o_ref.shape, F32)
        r = lax.broadcasted_iota(jnp.int32, (tm, tm), 0)
        c = lax.broadcasted_iota(jnp.int32, (tm, tm), 1)
        before = jnp.where(c < r, 1.0, 0.0).astype(BF16)
        sel = jnp.where(cmb_ref[...] > 0.0, 1.0, 0.0).astype(BF16)
        rank_ref[...] = jnp.dot(before, sel, preferred_element_type=F32)

    cnt = cnt_ref[i, e]

    def weight():
        return jnp.sum(jnp.where(lane == e, cmb_ref[...], 0.0), axis=-1, keepdims=True)

    def one_hot(off, rows):
        r_e = jnp.sum(jnp.where(lane == e, rank_ref[...], 0.0), axis=-1, keepdims=True)
        key = jnp.where(weight() > 0.0, r_e, -1.0) - jnp.asarray(off, F32)
        slot = lax.broadcasted_iota(jnp.int32, (tm, rows), 1).astype(F32)
        return jnp.where(key == slot, 1.0, 0.0).astype(BF16)

    def process(off, rows, cache):
        rs = pl.ds(off, rows)

        @pl.when(f == 0)
        def _():
            onehot = one_hot(off, rows)
            if cache:
                oh_ref[...] = onehot
            xc_ref[rs, :] = _dot_tn(onehot, h_ref[...]).astype(BF16)
            yc_ref[rs, :] = jnp.zeros((rows, yc_ref.shape[1]), F32)

        xc = xc_ref[rs, :]
        act = _silu(jnp.dot(xc, wg_ref[0], preferred_element_type=F32)) * jnp.dot(
            xc, wu_ref[0], preferred_element_type=F32)
        yc_ref[rs, :] += jnp.dot(act.astype(BF16), wd_ref[0], preferred_element_type=F32)

        @pl.when(last_f)
        def _():
            onehot = oh_ref[...] if cache else one_hot(off, rows)
            o_ref[...] += weight() * jnp.dot(onehot, yc_ref[rs, :].astype(BF16), preferred_element_type=F32)

    @pl.when(cnt > 0)
    def _():
        process(0, MOE_FIRST_ROWS, True)

    def extra(b, carry):
        process(pl.multiple_of(MOE_FIRST_ROWS + b * MOE_BLK, 32), MOE_BLK, False)
        return carry

    lax.fori_loop(0, (jnp.maximum(cnt - MOE_FIRST_ROWS, 0) + MOE_BLK - 1) // MOE_BLK, extra, 0)


def _add_norm_kernel(x_ref, y_ref, w_ref, o_ref, *, final_norm):
    y = x_ref[...] + y_ref[...]
    if final_norm:
        y = y * lax.rsqrt(jnp.mean(y * y, axis=-1, keepdims=True) + NORM_EPS) * w_ref[...]
    o_ref[...] = y


def _ffn_moe(x, norm_w, router, w_gate, w_up, w_down, final_w, tm=MOE_TM, tf=1408, tr=512):
    t, d = x.shape
    ne, _, ff = w_gate.shape
    nf = ff // tf
    h, cmb, cnt = _moe_route(x, norm_w, router, tm)
    cap = MOE_FIRST_ROWS + -(-max(tm - MOE_FIRST_ROWS, 0) // MOE_BLK) * MOE_BLK
    grid_spec = pltpu.PrefetchScalarGridSpec(
        num_scalar_prefetch=1,
        grid=(t // tm, ne, nf),
        in_specs=[pl.BlockSpec((tm, d), lambda i, e, f, c: (i, 0)),
                  pl.BlockSpec((tm, LANES), lambda i, e, f, c: (i, 0)),
                  pl.BlockSpec((1, d, tf), lambda i, e, f, c: (e, 0, f)),
                  pl.BlockSpec((1, d, tf), lambda i, e, f, c: (e, 0, f)),
                  pl.BlockSpec((1, tf, d), lambda i, e, f, c: (e, f, 0))],
        out_specs=pl.BlockSpec((tm, d), lambda i, e, f, c: (i, 0)),
        scratch_shapes=[pltpu.VMEM((tm, LANES), F32), pltpu.VMEM((cap, d), BF16), pltpu.VMEM((cap, d), F32),
                        pltpu.VMEM((tm, MOE_FIRST_ROWS), BF16)])
    y = pl.pallas_call(
        functools.partial(_moe_kernel, tm=tm),
        grid_spec=grid_spec,
        out_shape=jax.ShapeDtypeStruct((t, d), F32),
        compiler_params=_cparams("parallel", "arbitrary", "arbitrary"),
    )(cnt.reshape(t // tm, LANES), h, cmb, w_gate, w_up, w_down)
    final_norm = final_w is not None
    fw = (final_w if final_norm else jnp.ones((d,), F32)).reshape(1, d).astype(F32)
    tr = min(tr, t)
    return pl.pallas_call(
        functools.partial(_add_norm_kernel, final_norm=final_norm),
        grid=(t // tr,),
        in_specs=[pl.BlockSpec((tr, d), lambda i: (i, 0)),
                  pl.BlockSpec((tr, d), lambda i: (i, 0)),
                  pl.BlockSpec((1, d), lambda i: (0, 0))],
        out_specs=pl.BlockSpec((tr, d), lambda i: (i, 0)),
        out_shape=jax.ShapeDtypeStruct((t, d), F32),
        compiler_params=_cparams("parallel"),
    )(x, y, fw)


def _reorder_w_in(w):
    d = w.shape[0]
    wc = [w[:, 0:1024], w[:, 1536:4608], w[:, 1024:1536], w[:, 9760:10016], w[:, 5632:5648],
          w[:, 6672:6688], jnp.zeros((d, N_PC - PC_SMALL - 32), w.dtype)]
    wn = [w[:, 6688:9760], w[:, 10016:13088], w[:, 4608:5632], w[:, 5648:6672]]
    return jnp.concatenate(wc, axis=1).astype(BF16), jnp.concatenate(wn, axis=1).astype(BF16)


def _token_mixer_layer(x, i, bsz, seq, p):
    h = _rmsnorm(x, p["attn_norm_w"][i], BF16)
    wc, wn = _reorder_w_in(p["w_in"][i])
    pc, pn = _inproj(h, wc, wn, p["conv_w"][i], p["conv_b"][i], seq, tm=1024)
    y_ssm = _ssd_branch(pc, pn, p["ssm_a_log"][i], p["ssm_dt_bias"][i], p["ssm_d"][i],
                        p["ssm_norm_w"][i], bsz, seq)
    y_gdn = _gdn_branch(pc, pn, p["gdn_a_log"][i], p["gdn_dt_bias"][i], p["gdn_norm_w"][i],
                        bsz, seq)
    y_rwkv = _rwkv_branch(pc, pn, p["rwkv_mu"][i], p["rwkv_w0"][i], p["rwkv_w2"][i], p["rwkv_a0"][i],
                          p["rwkv_a2"][i], p["rwkv_g2"][i], p["rwkv_k_k"][i], p["rwkv_k_a"][i],
                          p["rwkv_r_k"][i], p["rwkv_ln_w"][i], p["rwkv_ln_b"][i], bsz, seq)
    return _merge(x, y_ssm, y_gdn, y_rwkv, pn, p["proj_ssm"][i].astype(BF16),
                  p["proj_gdn"][i].astype(BF16), p["proj_rwkv"][i].astype(BF16),
                  p["w_out"][i].astype(BF16))


def kernel(x, attn_norm_w, w_in, conv_w, conv_b, ssm_a_log, ssm_dt_bias, ssm_d, ssm_norm_w, gdn_a_log, gdn_dt_bias, gdn_norm_w, rwkv_mu, rwkv_w0, rwkv_w2, rwkv_a0, rwkv_a2, rwkv_g2, rwkv_k_k, rwkv_k_a, rwkv_r_k, rwkv_ln_w, rwkv_ln_b, proj_ssm, proj_gdn, proj_rwkv, w_out, ffn_norm_w, dense_w_gate, dense_w_up, dense_w_down, moe_router, moe_w_gate, moe_w_up, moe_w_down, final_norm_w):
    p = dict(attn_norm_w=attn_norm_w, w_in=w_in, conv_w=conv_w, conv_b=conv_b, ssm_a_log=ssm_a_log,
             ssm_dt_bias=ssm_dt_bias, ssm_d=ssm_d, ssm_norm_w=ssm_norm_w, gdn_a_log=gdn_a_log,
             gdn_dt_bias=gdn_dt_bias, gdn_norm_w=gdn_norm_w, rwkv_mu=rwkv_mu, rwkv_w0=rwkv_w0,
             rwkv_w2=rwkv_w2, rwkv_a0=rwkv_a0, rwkv_a2=rwkv_a2, rwkv_g2=rwkv_g2, rwkv_k_k=rwkv_k_k,
             rwkv_k_a=rwkv_k_a, rwkv_r_k=rwkv_r_k, rwkv_ln_w=rwkv_ln_w, rwkv_ln_b=rwkv_ln_b,
             proj_ssm=proj_ssm, proj_gdn=proj_gdn, proj_rwkv=proj_rwkv, w_out=w_out)
    bsz, seq, d = x.shape
    depth = attn_norm_w.shape[0]
    xt = x.reshape(bsz * seq, d)
    for i in range(depth):
        xt = _token_mixer_layer(xt, i, bsz, seq, p)
        j = i // 2
        last = i == depth - 1
        if i % 2 == 0:
            xt = _ffn_dense(xt, ffn_norm_w[i], dense_w_gate[j].astype(BF16), dense_w_up[j].astype(BF16),
                            dense_w_down[j].astype(BF16))
            if last:
                xt = _rmsnorm(xt, final_norm_w, F32)
        else:
            xt = _ffn_moe(xt, ffn_norm_w[i], moe_router[j], moe_w_gate[j].astype(BF16),
                          moe_w_up[j].astype(BF16), moe_w_down[j].astype(BF16),
                          final_norm_w if last else None)
    return xt.reshape(bsz, seq, d)
```

```python
import functools
import math

import numpy as np
import jax
import jax.numpy as jnp
from jax import lax
from jax.experimental import pallas as pl
from jax.experimental.pallas import tpu as pltpu

F32 = jnp.float32
BF16 = jnp.bfloat16

LANES = 128
SUBLANES = 8
VMEM_LIMIT_BYTES = 56 * 1024 * 1024

D_MODEL = 1024
SSM_HEADS, SSM_HEAD_DIM, SSM_GROUPS, SSM_STATE = 16, 64, 2, 128
SSM_INNER = SSM_HEADS * SSM_HEAD_DIM
GDN_HEADS, GDN_HEAD_DIM = 8, 128
GDN_INNER = GDN_HEADS * GDN_HEAD_DIM
RWKV_HEADS, RWKV_HEAD_DIM = 16, 64
RWKV_INNER = RWKV_HEADS * RWKV_HEAD_DIM
RWKV_DECAY_LORA, RWKV_ICLR_LORA, RWKV_GATE_LORA = 64, 64, 128
RWKV_GN_EPS = 64e-5
CONV_K = 4
FFN_DIM = 2816
N_EXPERTS = 8
NORM_EPS = 1e-6
L2_EPS = 1e-6

PC_XS, PC_Q, PC_K, PC_V = 0, 1024, 2048, 3072
PC_BS, PC_CS = 4096, 4352
PC_CONV_END = 4608
PC_LORA = 4608
PC_SMALL = 4864
N_PC = 5120
PN_RKV = 0
PN_GATE = 3072
PN_ZSSM, PN_ZGDN = 6144, 7168
N_PN = 8192
SMALL_DT, SMALL_A, SMALL_B = 0, 16, 24

SSD_CHUNK = 128
DPLR_CHUNK = 64
SEQ_PER_STEP = 4
INPROJ_ROW_CHUNK = 512


def _cparams(*sem):
    return pltpu.CompilerParams(dimension_semantics=sem, vmem_limit_bytes=VMEM_LIMIT_BYTES)


def _dot(a, b):
    return jnp.dot(a.astype(BF16), b.astype(BF16), preferred_element_type=F32)


def _dot_nt(a, b):
    return lax.dot_general(a.astype(BF16), b.astype(BF16), (((1,), (1,)), ((), ())),
                           preferred_element_type=F32)


def _dot_tn(a, b):
    return lax.dot_general(a.astype(BF16), b.astype(BF16), (((0,), (0,)), ((), ())),
                           preferred_element_type=F32)


def _split3(x):
    hi = x.astype(BF16)
    r1 = x - hi.astype(F32)
    mid = r1.astype(BF16)
    lo = (r1 - mid.astype(F32)).astype(BF16)
    return hi, mid, lo


def _dot_x3(x, w):
    hi, mid, lo = _split3(x)
    return _dot(hi, w) + _dot(mid, w) + _dot(lo, w)


def _dot_w3(w, x):
    hi, mid, lo = _split3(x)
    return _dot(w, hi) + _dot(w, mid) + _dot(w, lo)


def _dot_hl(a, b):
    ah = a.astype(BF16)
    al = (a - ah.astype(F32)).astype(BF16)
    bh = b.astype(BF16)
    bl = (b - bh.astype(F32)).astype(BF16)
    return _dot(ah, bh) + _dot(ah, bl) + _dot(al, bh)


def _sigmoid(x):
    return 0.5 * jnp.tanh(0.5 * x) + 0.5


def _silu(x):
    return x * _sigmoid(x)


def _softplus(x):
    return jnp.maximum(x, 0.0) + jnp.log(1.0 + jnp.exp(-jnp.abs(x)))


def _tri_incl(n):
    r = lax.broadcasted_iota(jnp.int32, (n, n), 0)
    c = lax.broadcasted_iota(jnp.int32, (n, n), 1)
    return jnp.where(r >= c, 1.0, 0.0).astype(BF16)


def _half_ones():
    r = lax.broadcasted_iota(jnp.int32, (LANES, LANES), 0)
    c = lax.broadcasted_iota(jnp.int32, (LANES, LANES), 1)
    return jnp.where((r // 64) == (c // 64), 1.0, 0.0).astype(BF16)


def _rmsnorm_kernel(x_ref, w_ref, o_ref):
    x = x_ref[...]
    ms = jnp.mean(x * x, axis=-1, keepdims=True)
    o_ref[...] = (x * lax.rsqrt(ms + NORM_EPS) * w_ref[...]).astype(o_ref.dtype)


def _rmsnorm(x, w, out_dtype, tm=512):
    t, d = x.shape
    return pl.pallas_call(
        _rmsnorm_kernel,
        grid=(t // tm,),
        in_specs=[pl.BlockSpec((tm, d), lambda i: (i, 0)),
                  pl.BlockSpec((1, d), lambda i: (0, 0))],
        out_specs=pl.BlockSpec((tm, d), lambda i: (i, 0)),
        out_shape=jax.ShapeDtypeStruct((t, d), out_dtype),
        compiler_params=_cparams("parallel"),
    )(x, w.reshape(1, d))


def _inproj_kernel(a_ref, wc_ref, wn_ref, cw_ref, cb_ref, cm_ref, oc_ref, on_ref, sh_ref, *,
                   tm, tiles_per_seq):
    i = pl.program_id(1)
    tc = oc_ref.shape[1]

    @pl.when(i % tiles_per_seq == 0)
    def _():
        sh_ref[0:SUBLANES, :] = jnp.zeros((SUBLANES, tc), F32)

    rc = min(tm, INPROJ_ROW_CHUNK)
    starts = list(range(0, tm, rc))
    acc = jnp.dot(a_ref[0:rc, :], wc_ref[...], preferred_element_type=F32)
    for n, r0 in enumerate(starts):
        sh_ref[SUBLANES + r0:SUBLANES + r0 + rc, :] = acc
        on_ref[r0:r0 + rc, :] = jnp.dot(a_ref[r0:r0 + rc, :], wn_ref[...], preferred_element_type=F32)
        if n + 1 < len(starts):
            acc_next = jnp.dot(a_ref[r0 + rc:r0 + 2 * rc, :], wc_ref[...], preferred_element_type=F32)
        c = acc * cw_ref[CONV_K - 1:CONV_K, :] + cb_ref[...]
        for k in range(1, CONV_K):
            c = c + sh_ref[pl.ds(SUBLANES + r0 - k, rc), :] * cw_ref[CONV_K - 1 - k:CONV_K - k, :]
        oc_ref[r0:r0 + rc, :] = jnp.where(cm_ref[...] > 0.0, _silu(c), acc)
        if n + 1 < len(starts):
            acc = acc_next
    sh_ref[0:SUBLANES, :] = sh_ref[tm:tm + SUBLANES, :]


def _conv_cols(c):
    r = c.shape[0]
    return jnp.concatenate([c[:, 0:1024], c[:, 1536:4608], c[:, 1024:1536],
                            jnp.zeros((r, N_PC - PC_CONV_END), F32)], axis=1).astype(F32)


def _inproj(a, wc, wn, conv_w, conv_b, seq, tm, nsteps=8):
    m, k = a.shape
    tc, tn = N_PC // nsteps, N_PN // nsteps
    cw = _conv_cols(conv_w)
    cb = _conv_cols(conv_b.reshape(1, -1))
    cm = _conv_cols(jnp.ones((1, conv_w.shape[1]), F32))
    return pl.pallas_call(
        functools.partial(_inproj_kernel, tm=tm, tiles_per_seq=seq // tm),
        grid=(nsteps, m // tm),
        in_specs=[pl.BlockSpec((tm, k), lambda j, i: (i, 0)),
                  pl.BlockSpec((k, tc), lambda j, i: (0, j)),
                  pl.BlockSpec((k, tn), lambda j, i: (0, j)),
                  pl.BlockSpec((CONV_K, tc), lambda j, i: (0, j)),
                  pl.BlockSpec((1, tc), lambda j, i: (0, j)),
                  pl.BlockSpec((1, tc), lambda j, i: (0, j))],
        out_specs=[pl.BlockSpec((tm, tc), lambda j, i: (i, j)),
                   pl.BlockSpec((tm, tn), lambda j, i: (i, j))],
        out_shape=[jax.ShapeDtypeStruct((m, N_PC), F32), jax.ShapeDtypeStruct((m, N_PN), F32)],
        scratch_shapes=[pltpu.VMEM((tm + SUBLANES, tc), F32)],
        compiler_params=_cparams("parallel", "arbitrary"),
    )(a, wc, wn, cw, cb, cm)


def _ssd_kernel(xs_ref, bs_ref, cs_ref, z_ref, sm_ref, alog_ref, dtb_ref, dsk_ref, nw_ref, ex_ref,
                o_ref, st_ref, *, L, nseq):
    @pl.when(pl.program_id(1) == 0)
    def _():
        st_ref[...] = jnp.zeros(st_ref.shape, F32)

    hd = SSM_HEAD_DIM
    gw = SSM_INNER // SSM_GROUPS
    heads_per_group = SSM_HEADS // SSM_GROUPS
    ex = ex_ref[...]
    row = lax.broadcasted_iota(jnp.int32, (L, L), 0)
    col = lax.broadcasted_iota(jnp.int32, (L, L), 1)
    causal = row >= col
    lane = lax.broadcasted_iota(jnp.int32, (L, LANES), 1)
    first_half = lane < hd
    for s in range(nseq):
        dt = _softplus(sm_ref[s] + dtb_ref[...])
        la = -jnp.exp(alog_ref[...]) * dt
        g_cum = _dot_w3(_tri_incl(L), la)
        g_cum_t = g_cum.T
        dt_e = _dot_x3(dt, ex)
        g_e = _dot_x3(g_cum, ex)
        g_last = g_e[L - 1:L, :]
        e_g = jnp.exp(g_e)
        e_dec = jnp.exp(g_last - g_e)
        e_last = jnp.exp(g_last)

        xs = xs_ref[s]
        xdt = xs * dt_e
        xdec = xdt * e_dec
        y_parts = []
        for g in range(SSM_GROUPS):
            gsl = slice(g * gw, (g + 1) * gw)
            b_g = bs_ref[s, :, g * SSM_STATE:(g + 1) * SSM_STATE]
            c_g = cs_ref[s, :, g * SSM_STATE:(g + 1) * SSM_STATE]
            cb = _dot_nt(c_g, b_g)
            st_g = st_ref[s, :, gsl]
            y_off = _dot(c_g, st_g) * e_g[:, gsl]
            st_ref[s, :, gsl] = st_g * e_last[:, gsl] + _dot_tn(b_g, xdec[:, gsl])
            for p in range(heads_per_group // 2):
                h0 = g * heads_per_group + 2 * p
                base = h0 * hd
                sc = []
                for h in (h0, h0 + 1):
                    diff = g_cum[:, h:h + 1] - g_cum_t[h:h + 1, :]
                    sc.append(cb * jnp.exp(jnp.where(causal, diff, -jnp.inf)))
                yd = _dot(jnp.concatenate(sc, axis=0), xdt[:, base:base + LANES])
                y_parts.append(jnp.where(first_half, yd[:L], yd[L:])
                               + y_off[:, base - g * gw:base - g * gw + LANES])
        y = jnp.concatenate(y_parts, axis=1) + dsk_ref[...] * xs
        y = y * _silu(z_ref[s])
        outs = []
        for g in range(SSM_GROUPS):
            seg = y[:, g * gw:(g + 1) * gw]
            outs.append(seg * lax.rsqrt(jnp.mean(seg * seg, axis=-1, keepdims=True) + NORM_EPS))
        o_ref[s] = (jnp.concatenate(outs, axis=1) * nw_ref[...]).astype(o_ref.dtype)


def _pad_lanes(v, offset=0, width=LANES):
    out = jnp.zeros((1, width), F32)
    return lax.dynamic_update_slice(out, v.reshape(1, -1).astype(F32), (0, offset))


def _ssd_branch(pc, pn, a_log, dt_bias, d_skip, norm_w, bsz, seq):
    L = SSD_CHUNK
    nseq = SEQ_PER_STEP if bsz % SEQ_PER_STEP == 0 else 1
    pc3 = pc.reshape(bsz, seq, N_PC)
    pn3 = pn.reshape(bsz, seq, N_PN)
    ex = np.zeros((LANES, SSM_INNER), np.float32)
    for h in range(SSM_HEADS):
        ex[h, h * SSM_HEAD_DIM:(h + 1) * SSM_HEAD_DIM] = 1.0
    vec = lambda w: pl.BlockSpec((1, w), lambda b, c: (0, 0))
    tok = lambda width, col: pl.BlockSpec((nseq, L, width), lambda b, c: (b, c, col // width))
    out = pl.pallas_call(
        functools.partial(_ssd_kernel, L=L, nseq=nseq),
        grid=(bsz // nseq, seq // L),
        in_specs=[tok(SSM_INNER, PC_XS), tok(256, PC_BS), tok(256, PC_CS), tok(SSM_INNER, PN_ZSSM),
                  tok(LANES, PC_SMALL),
                  vec(LANES), vec(LANES), vec(SSM_INNER), vec(SSM_INNER),
                  pl.BlockSpec((LANES, SSM_INNER), lambda b, c: (0, 0))],
        out_specs=pl.BlockSpec((nseq, L, SSM_INNER), lambda b, c: (b, c, 0)),
        out_shape=jax.ShapeDtypeStruct((bsz, seq, SSM_INNER), BF16),
        scratch_shapes=[pltpu.VMEM((nseq, SSM_STATE, SSM_INNER), F32)],
        compiler_params=_cparams("parallel", "arbitrary"),
    )(pc3, pc3, pc3, pn3, pc3,
      _pad_lanes(a_log, SMALL_DT), _pad_lanes(dt_bias, SMALL_DT),
      jnp.repeat(d_skip.astype(F32), SSM_HEAD_DIM).reshape(1, SSM_INNER),
      norm_w.reshape(1, SSM_INNER).astype(F32), jnp.asarray(ex, BF16))
    return out.reshape(bsz * seq, SSM_INNER)


def _head_stack(x):
    half = x.shape[1] // 2
    lane = lax.broadcasted_iota(jnp.int32, x.shape, 1)
    return jnp.concatenate([jnp.where(lane < half, x, 0.0), jnp.where(lane >= half, x, 0.0)], axis=0)


def _dplr_pairs(units, *, L):
    nu = len(units)
    P = 2 * L
    vc = units[0]["v"].shape[1]
    row = lax.broadcasted_iota(jnp.int32, (L, P), 0)
    tcol = lax.broadcasted_iota(jnp.int32, (L, P), 1)
    first = tcol < L
    tcol = jnp.where(first, tcol, tcol - L)
    strict, lower = tcol < row, tcol <= row

    sc = [_dot_nt(jnp.concatenate([u["a_sc"], u["r_sc"]], axis=0),
                  jnp.concatenate([_head_stack(u["b_sc"]), _head_stack(u["k_sc"])], axis=0))
          for u in units]
    v_st = [_head_stack(u["v"]) for u in units]

    def read(u):
        rows = jnp.concatenate([u["a_sr"], u["r_sr"]], axis=0)
        st = u["s_prev"]
        if len(st) == 1:
            return _dot_nt(rows, st[0])
        kh = rows.shape[1] // 2
        return jnp.concatenate([_dot_nt(rows[:, :kh], st[0]), _dot_nt(rows[:, kh:], st[1])], axis=1)

    reads = [read(u) for u in units]
    n_mat, a_rb, aks = [], [], []
    for i, u in enumerate(units):
        sb, sk = sc[i][:, :P], sc[i][:, P:]
        if u.get("dm_a") is None:
            n_mat.append(jnp.where(strict, sb[:L], 0.0))
            a_rb.append(jnp.where(lower, sb[L:], 0.0))
            aks.append(jnp.concatenate([jnp.where(strict, sk[:L], 0.0), jnp.where(lower, sk[L:], 0.0)], axis=0))
        else:
            n_mat.append(sb[:L] * u["dm_a"])
            a_rb.append(sb[L:] * u["dm_r"])
            aks.append(jnp.concatenate([sk[:L] * u["dm_a"], sk[L:] * u["dm_r"]], axis=0))
    av = [_dot(aks[i], v_st[i]) for i in range(nu)]
    x = [reads[i][:L] + av[i][:L] for i in range(nu)]
    y_part = [reads[i][L:] + av[i][L:] for i in range(nu)]

    steps = int(math.log2(L))
    for s in range(steps):
        x_st = [_head_stack(x[i]) for i in range(nu)]
        if s + 1 < steps:
            t = [_dot(n_mat[i], jnp.concatenate(
                    [x_st[i], jnp.concatenate([jnp.where(first, n_mat[i], 0.0),
                                               jnp.where(first, 0.0, n_mat[i])], axis=0)], axis=1))
                 for i in range(nu)]
            x = [x[i] + t[i][:, :vc] for i in range(nu)]
            n_mat = [t[i][:, vc:] for i in range(nu)]
        else:
            x = [x[i] + _dot(n_mat[i], x_st[i]) for i in range(nu)]
    ys = [y_part[i] + _dot(a_rb[i], _head_stack(x[i])) for i in range(nu)]

    s_new = []
    for i, u in enumerate(units):
        xv = jnp.concatenate([x[i], u["v"]], axis=0)
        bk = jnp.concatenate([u["b_dec"], u["k_dec"]], axis=0)
        st, pl_ = u["s_prev"], u["p_last"]
        if len(st) == 1:
            upd = _dot_tn(xv, bk)
            ri = lax.broadcasted_iota(jnp.int32, upd.shape, 0) // (upd.shape[0] // 2)
            ci = lax.broadcasted_iota(jnp.int32, upd.shape, 1) // (upd.shape[1] // 2)
            s_new.append([st[0] * pl_[0] + jnp.where(ri == ci, upd, 0.0)])
        else:
            vh, kh = xv.shape[1] // 2, bk.shape[1] // 2
            s_new.append([st[0] * pl_[0] + _dot_tn(xv[:, :vh], bk[:, :kh]),
                          st[1] * pl_[1] + _dot_tn(xv[:, vh:], bk[:, kh:])])
    return ys, s_new


def _gdn_kernel(q_ref, k_ref, v_ref, z_ref, sm_ref, alog_ref, dtb_ref, nw_ref, o_ref, st_ref, *, L, nseq):
    @pl.when(pl.program_id(1) == 0)
    def _():
        st_ref[...] = jnp.zeros(st_ref.shape, F32)

    hd = GDN_HEAD_DIM
    row = lax.broadcasted_iota(jnp.int32, (L, 2 * L), 0)
    tcol = lax.broadcasted_iota(jnp.int32, (L, 2 * L), 1)
    first_t = tcol < L
    tcol = jnp.where(first_t, tcol, tcol - L)
    first_c = lax.broadcasted_iota(jnp.int32, (L, 2 * hd), 1) < hd
    nw = nw_ref[...]
    scale = hd ** -0.5
    units = []
    for s in range(nseq):
        sm = sm_ref[s]
        g = -jnp.exp(alog_ref[...]) * _softplus(sm + dtb_ref[...])
        beta = _sigmoid(sm)
        g_cum = _dot_w3(_tri_incl(L), g)
        g_cum_t = jnp.concatenate([g_cum] * (LANES // L), axis=0).T
        for p in range(GDN_HEADS // 2):
            sl = slice(2 * p * hd, (2 * p + 2) * hd)
            q = q_ref[s, :, sl]
            k = k_ref[s, :, sl]

            def per_head(t):
                return jnp.where(first_c, jnp.sum(t[:, :hd], axis=-1, keepdims=True),
                                 jnp.sum(t[:, hd:], axis=-1, keepdims=True))

            def col(arr, base):
                h0 = base + 2 * p
                return arr[:, h0:h0 + 1], arr[:, h0 + 1:h0 + 2]

            qn = q * (lax.rsqrt(per_head(q * q) + L2_EPS) * scale)
            kn = k * lax.rsqrt(per_head(k * k) + L2_EPS)
            gc0, gc1 = col(g_cum, SMALL_A)
            gs0, gs1 = col(g, SMALL_A)
            bt0, bt1 = col(beta, SMALL_B)
            ch = lambda a0, a1: jnp.where(first_c, a0, a1)
            tm_ = lambda a0, a1: jnp.where(first_t, a0, a1)
            gc, gs, bt = ch(gc0, gc1), ch(gs0, gs1), ch(bt0, bt1)
            g_prev = gc - gs
            h0 = SMALL_A + 2 * p
            gr = tm_(g_cum_t[h0:h0 + 1, :], g_cum_t[h0 + 1:h0 + 2, :])
            ge0, ge1 = g_cum[L - 1:L, h0:h0 + 1], g_cum[L - 1:L, h0 + 1:h0 + 2]
            e_end = jnp.exp(ch(ge0, ge1) - gc)
            b_vec = -(bt * jnp.exp(gs)) * kn
            k_vec = bt * kn
            units.append(dict(
                a_sc=kn, r_sc=qn, b_sc=b_vec, k_sc=k_vec,
                a_sr=kn * jnp.exp(g_prev), r_sr=qn * jnp.exp(gc),
                b_dec=b_vec * e_end, k_dec=k_vec * e_end,
                v=v_ref[s, :, sl], s_prev=[st_ref[s, 2 * p], st_ref[s, 2 * p + 1]],
                p_last=[jnp.exp(ge0), jnp.exp(ge1)],
                dm_a=jnp.exp(jnp.where(tcol < row, tm_(gc0 - gs0, gc1 - gs1) - gr, -jnp.inf)),
                dm_r=jnp.exp(jnp.where(tcol <= row, tm_(gc0, gc1) - gr, -jnp.inf))))
    ys, s_new = _dplr_pairs(units, L=L)
    npair = GDN_HEADS // 2
    for s in range(nseq):
        for p in range(npair):
            sl = slice(2 * p * hd, (2 * p + 2) * hd)
            st_ref[s, 2 * p] = s_new[s * npair + p][0]
            st_ref[s, 2 * p + 1] = s_new[s * npair + p][1]
            y = ys[s * npair + p]
            ms = jnp.where(first_c, jnp.mean(y[:, :hd] * y[:, :hd], axis=-1, keepdims=True),
                           jnp.mean(y[:, hd:] * y[:, hd:], axis=-1, keepdims=True))
            yn = y * lax.rsqrt(ms + NORM_EPS) * nw
            o_ref[s, :, sl] = (yn * _silu(z_ref[s, :, sl])).astype(o_ref.dtype)


def _gdn_branch(pc, pn, a_log, dt_bias, norm_w, bsz, seq):
    L = DPLR_CHUNK
    nseq = SEQ_PER_STEP if bsz % SEQ_PER_STEP == 0 else 1
    pc3 = pc.reshape(bsz, seq, N_PC)
    pn3 = pn.reshape(bsz, seq, N_PN)
    vec = lambda w: pl.BlockSpec((1, w), lambda b, c: (0, 0))
    tok = lambda width, col: pl.BlockSpec((nseq, L, width), lambda b, c: (b, c, col // width))
    out = pl.pallas_call(
        functools.partial(_gdn_kernel, L=L, nseq=nseq),
        grid=(bsz // nseq, seq // L),
        in_specs=[tok(GDN_INNER, PC_Q), tok(GDN_INNER, PC_K), tok(GDN_INNER, PC_V),
                  tok(GDN_INNER, PN_ZGDN), tok(LANES, PC_SMALL),
                  vec(LANES), vec(LANES), vec(2 * GDN_HEAD_DIM)],
        out_specs=pl.BlockSpec((nseq, L, GDN_INNER), lambda b, c: (b, c, 0)),
        out_shape=jax.ShapeDtypeStruct((bsz, seq, GDN_INNER), BF16),
        scratch_shapes=[pltpu.VMEM((nseq, GDN_HEADS, LANES, LANES), F32)],
        compiler_params=_cparams("parallel", "arbitrary"),
    )(pc3, pc3, pc3, pn3, pc3,
      _pad_lanes(a_log, SMALL_A), _pad_lanes(dt_bias, SMALL_A),
      jnp.tile(norm_w.reshape(1, GDN_HEAD_DIM).astype(F32), (1, 2)))
    return out.reshape(bsz * seq, GDN_INNER)


def _rwkv_kernel(rkv_ref, lora_ref, mu_ref, mul_ref, w0_ref, w2_ref, a0_ref, a2_ref, g2_ref,
                 kk_ref, ka_ref, rk_ref, lnw_ref, lnb_ref, o_ref, sh_ref, shl_ref, st_ref, *, L, nseq):
    @pl.when(pl.program_id(1) == 0)
    def _():
        st_ref[...] = jnp.zeros(st_ref.shape, F32)
        for s in range(nseq):
            sh_ref[s, 0:SUBLANES, :] = jnp.zeros((SUBLANES, sh_ref.shape[2]), F32)
            shl_ref[s, 0:SUBLANES, :] = jnp.zeros((SUBLANES, shl_ref.shape[2]), F32)

    n = RWKV_INNER
    npair = RWKV_HEADS // 2
    ones_bd = _half_ones()
    inv_hd = 1.0 / RWKV_HEAD_DIM

    def head_sums(t):
        st = jnp.concatenate([t[:, p * LANES:(p + 1) * LANES] for p in range(npair)], axis=0)
        sm = _dot(st, ones_bd)
        return jnp.concatenate([sm[p * L:(p + 1) * L] for p in range(npair)], axis=1)

    units, tails = [], []
    for s in range(nseq):
        u = rkv_ref[s]
        ul = lora_ref[s]
        sh_ref[s, SUBLANES:SUBLANES + L, :] = u
        shl_ref[s, SUBLANES:SUBLANES + L, :] = ul
        u = u + (sh_ref[s, pl.ds(SUBLANES - 1, L), :] - u) * mu_ref[...]
        ul = ul + (shl_ref[s, pl.ds(SUBLANES - 1, L), :] - ul) * mul_ref[...]
        sh_ref[s, 0:SUBLANES, :] = sh_ref[s, L:L + SUBLANES, :]
        shl_ref[s, 0:SUBLANES, :] = shl_ref[s, L:L + SUBLANES, :]

        r, k, v = u[:, 0:n], u[:, n:2 * n], u[:, 2 * n:3 * n]
        lo = ul[:, 0:LANES]
        dg = ul[:, LANES:2 * LANES]
        w_log = -_softplus(-(w0_ref[...] + _dot_hl(jnp.tanh(lo), w2_ref[...]))) - 0.5
        lw = -jnp.exp(w_log)
        a_ic = _sigmoid(a0_ref[...] + _dot_hl(lo, a2_ref[...]))
        gate = _dot(_sigmoid(dg), g2_ref[...])
        g_cum = _dot_w3(_tri_incl(L), lw)
        kk_raw = k * kk_ref[...]
        k_mod = k * (1.0 + (a_ic - 1.0) * ka_ref[...])
        kk_all = kk_raw * lax.rsqrt(head_sums(kk_raw * kk_raw) + L2_EPS)
        tails.append((head_sums(r * k_mod * rk_ref[...]) * v, gate))
        for p in range(npair):
            sl = slice(p * LANES, (p + 1) * LANES)
            kk = kk_all[:, sl]
            gc = g_cum[:, sl]
            g_end = gc[L - 1:L, :]
            e_neg = jnp.exp(-gc)
            e_end = jnp.exp(g_end - gc)
            b_vec = kk * a_ic[:, sl]
            k_vec = k_mod[:, sl]
            a_t = -kk * jnp.exp(gc - lw[:, sl])
            r_t = r[:, sl] * jnp.exp(gc)
            units.append(dict(
                a_sc=a_t, r_sc=r_t, b_sc=b_vec * e_neg, k_sc=k_vec * e_neg,
                a_sr=a_t, r_sr=r_t, b_dec=b_vec * e_end, k_dec=k_vec * e_end,
                v=v[:, sl], s_prev=[st_ref[s, p]], p_last=[jnp.exp(g_end)]))
    ys, s_new = _dplr_pairs(units, L=L)
    for s in range(nseq):
        for p in range(npair):
            st_ref[s, p] = s_new[s * npair + p][0]
        y = jnp.concatenate(ys[s * npair:(s + 1) * npair], axis=1)
        yc = y - head_sums(y) * inv_hd
        var = head_sums(yc * yc) * inv_hd
        yn = yc * lax.rsqrt(var + RWKV_GN_EPS) * lnw_ref[...] + lnb_ref[...]
        bonus, gate = tails[s]
        o_ref[s] = ((yn + bonus) * gate).astype(o_ref.dtype)


def _rwkv_branch(pc, pn, mu, w0, w2, a0, a2, g2, k_k, k_a, r_k, ln_w, ln_b, bsz, seq):
    L = DPLR_CHUNK
    n = RWKV_INNER
    nseq = SEQ_PER_STEP if bsz % SEQ_PER_STEP == 0 else 1
    pc3 = pc.reshape(bsz, seq, N_PC)
    pn3 = pn.reshape(bsz, seq, N_PN)
    vec = lambda w: pl.BlockSpec((1, w), lambda b, c: (0, 0))
    mat = lambda: pl.BlockSpec((LANES, n), lambda b, c: (0, 0))
    tok = lambda width, col: pl.BlockSpec((nseq, L, width), lambda b, c: (b, c, col // width))
    w2p = jnp.concatenate([w2, jnp.zeros((LANES - RWKV_DECAY_LORA, n), F32)], axis=0)
    a2p = jnp.concatenate([jnp.zeros((RWKV_DECAY_LORA, n), F32), a2], axis=0)
    r1 = lambda t: t.reshape(1, -1).astype(F32)
    out = pl.pallas_call(
        functools.partial(_rwkv_kernel, L=L, nseq=nseq),
        grid=(bsz // nseq, seq // L),
        in_specs=[tok(3 * n, PN_RKV), tok(256, PC_LORA),
                  vec(3 * n), vec(256), vec(n), mat(), vec(n), mat(), mat(),
                  vec(n), vec(n), vec(n), vec(n), vec(n)],
        out_specs=pl.BlockSpec((nseq, L, n), lambda b, c: (b, c, 0)),
        out_shape=jax.ShapeDtypeStruct((bsz, seq, n), BF16),
        scratch_shapes=[pltpu.VMEM((nseq, L + SUBLANES, 3 * n), F32),
                        pltpu.VMEM((nseq, L + SUBLANES, 256), F32),
                        pltpu.VMEM((nseq, RWKV_HEADS // 2, LANES, LANES), F32)],
        compiler_params=_cparams("parallel", "arbitrary"),
    )(pn3, pc3, r1(mu[:3 * n]), r1(mu[3 * n:]), r1(w0), w2p, r1(a0), a2p, g2.astype(F32),
      r1(k_k), r1(k_a), r1(r_k), r1(ln_w), r1(ln_b))
    return out.reshape(bsz * seq, n)


def _merge_kernel(x_ref, ys_ref, yg_ref, yr_ref, gate_ref, ps_ref, pg_ref, pr_ref, wo_ref, o_ref):
    d = D_MODEL
    m = _sigmoid(gate_ref[:, 0:d]) * jnp.dot(ys_ref[...], ps_ref[...], preferred_element_type=F32)
    m = m + _sigmoid(gate_ref[:, d:2 * d]) * jnp.dot(yg_ref[...], pg_ref[...], preferred_element_type=F32)
    m = m + _sigmoid(gate_ref[:, 2 * d:3 * d]) * jnp.dot(yr_ref[...], pr_ref[...], preferred_element_type=F32)
    o_ref[...] = x_ref[...] + jnp.dot(m.astype(BF16), wo_ref[...], preferred_element_type=F32)


def _merge(x, y_ssm, y_gdn, y_rwkv, pn, p_ssm, p_gdn, p_rwkv, w_out, tm=512):
    t, d = x.shape
    tok = lambda: pl.BlockSpec((tm, d), lambda i: (i, 0))
    wgt = lambda: pl.BlockSpec((d, d), lambda i: (0, 0))
    return pl.pallas_call(
        _merge_kernel,
        grid=(t // tm,),
        in_specs=[tok(), tok(), tok(), tok(),
                  pl.BlockSpec((tm, 3 * d), lambda i: (i, PN_GATE // (3 * d))),
                  wgt(), wgt(), wgt(), wgt()],
        out_specs=tok(),
        out_shape=jax.ShapeDtypeStruct((t, d), F32),
        compiler_params=_cparams("parallel"),
    )(x, y_ssm, y_gdn, y_rwkv, pn, p_ssm, p_gdn, p_rwkv, w_out)


def _ffn_kernel(x_ref, nw_ref, wg_ref, wu_ref, wd_ref, o_ref, h_ref, acc_ref):
    f = pl.program_id(1)

    @pl.when(f == 0)
    def _():
        x = x_ref[...]
        ms = jnp.mean(x * x, axis=-1, keepdims=True)
        h_ref[...] = (x * lax.rsqrt(ms + NORM_EPS) * nw_ref[...]).astype(BF16)
        acc_ref[...] = x

    h = h_ref[...]
    act = _silu(jnp.dot(h, wg_ref[...], preferred_element_type=F32)) * jnp.dot(
        h, wu_ref[...], preferred_element_type=F32)
    acc_ref[...] += jnp.dot(act.astype(BF16), wd_ref[...], preferred_element_type=F32)

    @pl.when(f == pl.num_programs(1) - 1)
    def _():
        o_ref[...] = acc_ref[...]


def _ffn_dense(x, norm_w, w_gate, w_up, w_down, tm=512, tf=1408):
    t, d = x.shape
    nf = w_gate.shape[1] // tf
    return pl.pallas_call(
        _ffn_kernel,
        grid=(t // tm, nf),
        in_specs=[pl.BlockSpec((tm, d), lambda i, f: (i, 0)),
                  pl.BlockSpec((1, d), lambda i, f: (0, 0)),
                  pl.BlockSpec((d, tf), lambda i, f: (0, f)),
                  pl.BlockSpec((d, tf), lambda i, f: (0, f)),
                  pl.BlockSpec((tf, d), lambda i, f: (f, 0))],
        out_specs=pl.BlockSpec((tm, d), lambda i, f: (i, 0)),
        out_shape=jax.ShapeDtypeStruct((t, d), F32),
        scratch_shapes=[pltpu.VMEM((tm, d), BF16), pltpu.VMEM((tm, d), F32)],
        compiler_params=_cparams("parallel", "arbitrary"),
    )(x, norm_w.reshape(1, d).astype(F32), w_gate, w_up, w_down)


MOE_TM = 1024
MOE_BLK = 128
MOE_FIRST_ROWS = 288


def _route_kernel(x_ref, nw_ref, rt_ref, h_ref, cmb_ref, cnt_ref):
    x = x_ref[...]
    lane = lax.broadcasted_iota(jnp.int32, cmb_ref.shape, 1)
    ms = jnp.mean(x * x, axis=-1, keepdims=True)
    hf = x * lax.rsqrt(ms + NORM_EPS) * nw_ref[...]
    h_ref[...] = hf.astype(BF16)
    logits = jnp.where(lane < N_EXPERTS, _dot_hl(hf, rt_ref[...]), -jnp.inf)
    m1 = jnp.max(logits, axis=-1, keepdims=True)
    i1 = jnp.min(jnp.where(logits == m1, lane, LANES), axis=-1, keepdims=True)
    rest = jnp.where(lane == i1, -jnp.inf, logits)
    m2 = jnp.max(rest, axis=-1, keepdims=True)
    i2 = jnp.min(jnp.where(rest == m2, lane, LANES), axis=-1, keepdims=True)
    e2 = jnp.exp(m2 - m1)
    p1 = 1.0 / (1.0 + e2)
    cmb = jnp.where(lane == i1, p1, 0.0) + jnp.where(lane == i2, e2 * p1, 0.0)
    cmb_ref[...] = cmb
    cnt_ref[0] = jnp.sum(jnp.where(cmb > 0.0, 1.0, 0.0), axis=0, keepdims=True).astype(jnp.int32)


def _moe_route(x, norm_w, router, tm):
    t, d = x.shape
    ne = router.shape[1]
    rt = jnp.concatenate([router.astype(F32), jnp.zeros((d, LANES - ne), F32)], axis=1)
    return pl.pallas_call(
        _route_kernel,
        grid=(t // tm,),
        in_specs=[pl.BlockSpec((tm, d), lambda i: (i, 0)),
                  pl.BlockSpec((1, d), lambda i: (0, 0)),
                  pl.BlockSpec((d, LANES), lambda i: (0, 0))],
        out_specs=[pl.BlockSpec((tm, d), lambda i: (i, 0)),
                   pl.BlockSpec((tm, LANES), lambda i: (i, 0)),
                   pl.BlockSpec((1, 1, LANES), lambda i: (i, 0, 0))],
        out_shape=[jax.ShapeDtypeStruct((t, d), BF16),
                   jax.ShapeDtypeStruct((t, LANES), F32),
                   jax.ShapeDtypeStruct((t // tm, 1, LANES), jnp.int32)],
        compiler_params=_cparams("parallel"),
    )(x, norm_w.reshape(1, d).astype(F32), rt)


def _moe_kernel(cnt_ref, x_ref, h_ref, cmb_ref, wg_ref, wu_ref, wd_ref, fw_ref, o_ref,
                rank_ref, xc_ref, yc_ref, oh_ref, *, tm, final_norm):
    i = pl.program_id(0)
    e = pl.program_id(1)
    f = pl.program_id(2)
    last_f = f == pl.num_programs(2) - 1
    lane = lax.broadcasted_iota(jnp.int32, (tm, LANES), 1)

    @pl.when((e == 0) & (f == 0))
    def _():
        o_ref[...] = jnp.zeros(o_ref.shape, F32)
        r = lax.broadcasted_iota(jnp.int32, (tm, tm), 0)
        c = lax.broadcasted_iota(jnp.int32, (tm, tm), 1)
        before = jnp.where(c < r, 1.0, 0.0).astype(BF16)
        sel = jnp.where(cmb_ref[...] > 0.0, 1.0, 0.0).astype(BF16)
        rank_ref[...] = jnp.dot(before, sel, preferred_element_type=F32)

    cnt = cnt_ref[i, e]

    def weight():
        return jnp.sum(jnp.where(lane == e, cmb_ref[...], 0.0), axis=-1, keepdims=True)

    def one_hot(off, rows):
        r_e = jnp.sum(jnp.where(lane == e, rank_ref[...], 0.0), axis=-1, keepdims=True)
        key = jnp.where(weight() > 0.0, r_e, -1.0) - jnp.asarray(off, F32)
        slot = lax.broadcasted_iota(jnp.int32, (tm, rows), 1).astype(F32)
        return jnp.where(key == slot, 1.0, 0.0).astype(BF16)

    def process(off, rows, cache):
        rs = pl.ds(off, rows)

        @pl.when(f == 0)
        def _():
            onehot = one_hot(off, rows)
            if cache:
                oh_ref[...] = onehot
            xc_ref[rs, :] = _dot_tn(onehot, h_ref[...]).astype(BF16)
            yc_ref[rs, :] = jnp.zeros((rows, yc_ref.shape[1]), F32)

        xc = xc_ref[rs, :]
        act = _silu(jnp.dot(xc, wg_ref[0], preferred_element_type=F32)) * jnp.dot(
            xc, wu_ref[0], preferred_element_type=F32)
        yc_ref[rs, :] += jnp.dot(act.astype(BF16), wd_ref[0], preferred_element_type=F32)

        @pl.when(last_f)
        def _():
            onehot = oh_ref[...] if cache else one_hot(off, rows)
            o_ref[...] += weight() * jnp.dot(onehot, yc_ref[rs, :].astype(BF16), preferred_element_type=F32)

    @pl.when(cnt > 0)
    def _():
        process(0, MOE_FIRST_ROWS, True)

    def extra(b, carry):
        process(pl.multiple_of(MOE_FIRST_ROWS + b * MOE_BLK, 32), MOE_BLK, False)
        return carry

    lax.fori_loop(0, (jnp.maximum(cnt - MOE_FIRST_ROWS, 0) + MOE_BLK - 1) // MOE_BLK, extra, 0)

    @pl.when((e == pl.num_programs(1) - 1) & last_f)
    def _():
        y = x_ref[...] + o_ref[...]
        if final_norm:
            y = y * lax.rsqrt(jnp.mean(y * y, axis=-1, keepdims=True) + NORM_EPS) * fw_ref[...]
        o_ref[...] = y


def _ffn_moe(x, norm_w, router, w_gate, w_up, w_down, final_w, tm=MOE_TM, tf=1408):
    t, d = x.shape
    ne, _, ff = w_gate.shape
    nf = ff // tf
    h, cmb, cnt = _moe_route(x, norm_w, router, tm)
    final_norm = final_w is not None
    fw = (final_w if final_norm else jnp.ones((d,), F32)).reshape(1, d).astype(F32)
    cap = MOE_FIRST_ROWS + -(-max(tm - MOE_FIRST_ROWS, 0) // MOE_BLK) * MOE_BLK
    grid_spec = pltpu.PrefetchScalarGridSpec(
        num_scalar_prefetch=1,
        grid=(t // tm, ne, nf),
        in_specs=[pl.BlockSpec((tm, d), lambda i, e, f, c: (i, 0)),
                  pl.BlockSpec((tm, d), lambda i, e, f, c: (i, 0)),
                  pl.BlockSpec((tm, LANES), lambda i, e, f, c: (i, 0)),
                  pl.BlockSpec((1, d, tf), lambda i, e, f, c: (e, 0, f)),
                  pl.BlockSpec((1, d, tf), lambda i, e, f, c: (e, 0, f)),
                  pl.BlockSpec((1, tf, d), lambda i, e, f, c: (e, f, 0)),
                  pl.BlockSpec((1, d), lambda i, e, f, c: (0, 0))],
        out_specs=pl.BlockSpec((tm, d), lambda i, e, f, c: (i, 0)),
        scratch_shapes=[pltpu.VMEM((tm, LANES), F32), pltpu.VMEM((cap, d), BF16), pltpu.VMEM((cap, d), F32),
                        pltpu.VMEM((tm, MOE_FIRST_ROWS), BF16)])
    return pl.pallas_call(
        functools.partial(_moe_kernel, tm=tm, final_norm=final_norm),
        grid_spec=grid_spec,
        out_shape=jax.ShapeDtypeStruct((t, d), F32),
        compiler_params=_cparams("parallel", "arbitrary", "arbitrary"),
    )(cnt.reshape(t // tm, LANES), x, h, cmb, w_gate, w_up, w_down, fw)


def _reorder_w_in(w):
    d = w.shape[0]
    wc = [w[:, 0:1024], w[:, 1536:4608], w[:, 1024:1536], w[:, 9760:10016], w[:, 5632:5648],
          w[:, 6672:6688], jnp.zeros((d, N_PC - PC_SMALL - 32), w.dtype)]
    wn = [w[:, 6688:9760], w[:, 10016:13088], w[:, 4608:5632], w[:, 5648:6672]]
    return jnp.concatenate(wc, axis=1).astype(BF16), jnp.concatenate(wn, axis=1).astype(BF16)


def _token_mixer_layer(x, i, bsz, seq, p):
    h = _rmsnorm(x, p["attn_norm_w"][i], BF16)
    wc, wn = _reorder_w_in(p["w_in"][i])
    pc, pn = _inproj(h, wc, wn, p["conv_w"][i], p["conv_b"][i], seq, tm=1024)
    y_ssm = _ssd_branch(pc, pn, p["ssm_a_log"][i], p["ssm_dt_bias"][i], p["ssm_d"][i],
                        p["ssm_norm_w"][i], bsz, seq)
    y_gdn = _gdn_branch(pc, pn, p["gdn_a_log"][i], p["gdn_dt_bias"][i], p["gdn_norm_w"][i],
                        bsz, seq)
    y_rwkv = _rwkv_branch(pc, pn, p["rwkv_mu"][i], p["rwkv_w0"][i], p["rwkv_w2"][i], p["rwkv_a0"][i],
                          p["rwkv_a2"][i], p["rwkv_g2"][i], p["rwkv_k_k"][i], p["rwkv_k_a"][i],
                          p["rwkv_r_k"][i], p["rwkv_ln_w"][i], p["rwkv_ln_b"][i], bsz, seq)
    return _merge(x, y_ssm, y_gdn, y_rwkv, pn, p["proj_ssm"][i].astype(BF16),
                  p["proj_gdn"][i].astype(BF16), p["proj_rwkv"][i].astype(BF16),
                  p["w_out"][i].astype(BF16))


def kernel(x, attn_norm_w, w_in, conv_w, conv_b, ssm_a_log, ssm_dt_bias, ssm_d, ssm_norm_w, gdn_a_log, gdn_dt_bias, gdn_norm_w, rwkv_mu, rwkv_w0, rwkv_w2, rwkv_a0, rwkv_a2, rwkv_g2, rwkv_k_k, rwkv_k_a, rwkv_r_k, rwkv_ln_w, rwkv_ln_b, proj_ssm, proj_gdn, proj_rwkv, w_out, ffn_norm_w, dense_w_gate, dense_w_up, dense_w_down, moe_router, moe_w_gate, moe_w_up, moe_w_down, final_norm_w):
    p = dict(attn_norm_w=attn_norm_w, w_in=w_in, conv_w=conv_w, conv_b=conv_b, ssm_a_log=ssm_a_log,
             ssm_dt_bias=ssm_dt_bias, ssm_d=ssm_d, ssm_norm_w=ssm_norm_w, gdn_a_log=gdn_a_log,
             gdn_dt_bias=gdn_dt_bias, gdn_norm_w=gdn_norm_w, rwkv_mu=rwkv_mu, rwkv_w0=rwkv_w0,
             rwkv_w2=rwkv_w2, rwkv_a0=rwkv_a0, rwkv_a2=rwkv_a2, rwkv_g2=rwkv_g2, rwkv_k_k=rwkv_k_k,
             rwkv_k_a=rwkv_k_a, rwkv_r_k=rwkv_r_k, rwkv_ln_w=rwkv_ln_w, rwkv_ln_b=rwkv_ln_b,
             proj_ssm=proj_ssm, proj_gdn=proj_gdn, proj_rwkv=proj_rwkv, w_out=w_out)
    bsz, seq, d = x.shape
    depth = attn_norm_w.shape[0]
    xt = x.reshape(bsz * seq, d)
    for i in range(depth):
        xt = _token_mixer_layer(xt, i, bsz, seq, p)
        j = i // 2
        last = i == depth - 1
        if i % 2 == 0:
            xt = _ffn_dense(xt, ffn_norm_w[i], dense_w_gate[j].astype(BF16), dense_w_up[j].astype(BF16),
                            dense_w_down[j].astype(BF16))
            if last:
                xt = _rmsnorm(xt, final_norm_w, F32)
        else:
            xt = _ffn_moe(xt, ffn_norm_w[i], moe_router[j], moe_w_gate[j].astype(BF16),
                          moe_w_up[j].astype(BF16), moe_w_down[j].astype(BF16),
                          final_norm_w if last else None)
    return xt.reshape(bsz, seq, d)
```

```python
import functools
import math

import numpy as np
import jax
import jax.numpy as jnp
from jax import lax
from jax.experimental import pallas as pl
from jax.experimental.pallas import tpu as pltpu

F32 = jnp.float32
BF16 = jnp.bfloat16

LANES = 128
SUBLANES = 8
VMEM_LIMIT_BYTES = 56 * 1024 * 1024

D_MODEL = 1024
SSM_HEADS, SSM_HEAD_DIM, SSM_GROUPS, SSM_STATE = 16, 64, 2, 128
SSM_INNER = SSM_HEADS * SSM_HEAD_DIM
GDN_HEADS, GDN_HEAD_DIM = 8, 128
GDN_INNER = GDN_HEADS * GDN_HEAD_DIM
RWKV_HEADS, RWKV_HEAD_DIM = 16, 64
RWKV_INNER = RWKV_HEADS * RWKV_HEAD_DIM
RWKV_DECAY_LORA, RWKV_ICLR_LORA, RWKV_GATE_LORA = 64, 64, 128
RWKV_GN_EPS = 64e-5
CONV_K = 4
FFN_DIM = 2816
N_EXPERTS = 8
NORM_EPS = 1e-6
L2_EPS = 1e-6

PC_XS, PC_Q, PC_K, PC_V = 0, 1024, 2048, 3072
PC_BS, PC_CS = 4096, 4352
PC_CONV_END = 4608
PC_LORA = 4608
PC_SMALL = 4864
N_PC = 5120
PN_RKV = 0
PN_GATE = 3072
PN_ZSSM, PN_ZGDN = 6144, 7168
N_PN = 8192
SMALL_DT, SMALL_A, SMALL_B = 0, 16, 24

SSD_CHUNK = 128
DPLR_CHUNK = 64
SEQ_PER_STEP = 4
INPROJ_ROW_CHUNK = 512


def _cparams(*sem):
    return pltpu.CompilerParams(dimension_semantics=sem, vmem_limit_bytes=VMEM_LIMIT_BYTES)


def _dot(a, b):
    return jnp.dot(a.astype(BF16), b.astype(BF16), preferred_element_type=F32)


def _dot_nt(a, b):
    return lax.dot_general(a.astype(BF16), b.astype(BF16), (((1,), (1,)), ((), ())),
                           preferred_element_type=F32)


def _dot_tn(a, b):
    return lax.dot_general(a.astype(BF16), b.astype(BF16), (((0,), (0,)), ((), ())),
                           preferred_element_type=F32)


def _split3(x):
    hi = x.astype(BF16)
    r1 = x - hi.astype(F32)
    mid = r1.astype(BF16)
    lo = (r1 - mid.astype(F32)).astype(BF16)
    return hi, mid, lo


def _dot_x3(x, w):
    hi, mid, lo = _split3(x)
    return _dot(hi, w) + _dot(mid, w) + _dot(lo, w)


def _dot_w3(w, x):
    hi, mid, lo = _split3(x)
    return _dot(w, hi) + _dot(w, mid) + _dot(w, lo)


def _dot_hl(a, b):
    ah = a.astype(BF16)
    al = (a - ah.astype(F32)).astype(BF16)
    bh = b.astype(BF16)
    bl = (b - bh.astype(F32)).astype(BF16)
    return _dot(ah, bh) + _dot(ah, bl) + _dot(al, bh)


def _sigmoid(x):
    return 0.5 * jnp.tanh(0.5 * x) + 0.5


def _silu(x):
    return x * _sigmoid(x)


def _softplus(x):
    return jnp.maximum(x, 0.0) + jnp.log(1.0 + jnp.exp(-jnp.abs(x)))


def _tri_incl(n):
    r = lax.broadcasted_iota(jnp.int32, (n, n), 0)
    c = lax.broadcasted_iota(jnp.int32, (n, n), 1)
    return jnp.where(r >= c, 1.0, 0.0).astype(BF16)


def _half_ones():
    r = lax.broadcasted_iota(jnp.int32, (LANES, LANES), 0)
    c = lax.broadcasted_iota(jnp.int32, (LANES, LANES), 1)
    return jnp.where((r // 64) == (c // 64), 1.0, 0.0).astype(BF16)


def _rmsnorm_kernel(x_ref, w_ref, o_ref):
    x = x_ref[...]
    ms = jnp.mean(x * x, axis=-1, keepdims=True)
    o_ref[...] = (x * lax.rsqrt(ms + NORM_EPS) * w_ref[...]).astype(o_ref.dtype)


def _rmsnorm(x, w, out_dtype, tm=512):
    t, d = x.shape
    return pl.pallas_call(
        _rmsnorm_kernel,
        grid=(t // tm,),
        in_specs=[pl.BlockSpec((tm, d), lambda i: (i, 0)),
                  pl.BlockSpec((1, d), lambda i: (0, 0))],
        out_specs=pl.BlockSpec((tm, d), lambda i: (i, 0)),
        out_shape=jax.ShapeDtypeStruct((t, d), out_dtype),
        compiler_params=_cparams("parallel"),
    )(x, w.reshape(1, d))


def _inproj_kernel(a_ref, wc_ref, wn_ref, cw_ref, cb_ref, cm_ref, oc_ref, on_ref, sh_ref, *,
                   tm, tiles_per_seq):
    i = pl.program_id(1)
    tc = oc_ref.shape[1]

    @pl.when(i % tiles_per_seq == 0)
    def _():
        sh_ref[0:SUBLANES, :] = jnp.zeros((SUBLANES, tc), F32)

    rc = min(tm, INPROJ_ROW_CHUNK)
    starts = list(range(0, tm, rc))
    acc = jnp.dot(a_ref[0:rc, :], wc_ref[...], preferred_element_type=F32)
    for n, r0 in enumerate(starts):
        sh_ref[SUBLANES + r0:SUBLANES + r0 + rc, :] = acc
        on_ref[r0:r0 + rc, :] = jnp.dot(a_ref[r0:r0 + rc, :], wn_ref[...], preferred_element_type=F32)
        if n + 1 < len(starts):
            acc_next = jnp.dot(a_ref[r0 + rc:r0 + 2 * rc, :], wc_ref[...], preferred_element_type=F32)
        c = acc * cw_ref[CONV_K - 1:CONV_K, :] + cb_ref[...]
        for k in range(1, CONV_K):
            c = c + sh_ref[pl.ds(SUBLANES + r0 - k, rc), :] * cw_ref[CONV_K - 1 - k:CONV_K - k, :]
        oc_ref[r0:r0 + rc, :] = jnp.where(cm_ref[...] > 0.0, _silu(c), acc)
        if n + 1 < len(starts):
            acc = acc_next
    sh_ref[0:SUBLANES, :] = sh_ref[tm:tm + SUBLANES, :]


def _conv_cols(c):
    r = c.shape[0]
    return jnp.concatenate([c[:, 0:1024], c[:, 1536:4608], c[:, 1024:1536],
                            jnp.zeros((r, N_PC - PC_CONV_END), F32)], axis=1).astype(F32)


def _inproj(a, wc, wn, conv_w, conv_b, seq, tm, nsteps=8):
    m, k = a.shape
    tc, tn = N_PC // nsteps, N_PN // nsteps
    cw = _conv_cols(conv_w)
    cb = _conv_cols(conv_b.reshape(1, -1))
    cm = _conv_cols(jnp.ones((1, conv_w.shape[1]), F32))
    return pl.pallas_call(
        functools.partial(_inproj_kernel, tm=tm, tiles_per_seq=seq // tm),
        grid=(nsteps, m // tm),
        in_specs=[pl.BlockSpec((tm, k), lambda j, i: (i, 0)),
                  pl.BlockSpec((k, tc), lambda j, i: (0, j)),
                  pl.BlockSpec((k, tn), lambda j, i: (0, j)),
                  pl.BlockSpec((CONV_K, tc), lambda j, i: (0, j)),
                  pl.BlockSpec((1, tc), lambda j, i: (0, j)),
                  pl.BlockSpec((1, tc), lambda j, i: (0, j))],
        out_specs=[pl.BlockSpec((tm, tc), lambda j, i: (i, j)),
                   pl.BlockSpec((tm, tn), lambda j, i: (i, j))],
        out_shape=[jax.ShapeDtypeStruct((m, N_PC), F32), jax.ShapeDtypeStruct((m, N_PN), F32)],
        scratch_shapes=[pltpu.VMEM((tm + SUBLANES, tc), F32)],
        compiler_params=_cparams("parallel", "arbitrary"),
    )(a, wc, wn, cw, cb, cm)


def _ssd_kernel(xs_ref, bs_ref, cs_ref, z_ref, sm_ref, alog_ref, dtb_ref, dsk_ref, nw_ref, ex_ref,
                o_ref, st_ref, *, L, nseq):
    @pl.when(pl.program_id(1) == 0)
    def _():
        st_ref[...] = jnp.zeros(st_ref.shape, F32)

    hd = SSM_HEAD_DIM
    gw = SSM_INNER // SSM_GROUPS
    heads_per_group = SSM_HEADS // SSM_GROUPS
    ex = ex_ref[...]
    row = lax.broadcasted_iota(jnp.int32, (L, L), 0)
    col = lax.broadcasted_iota(jnp.int32, (L, L), 1)
    causal = row >= col
    lane = lax.broadcasted_iota(jnp.int32, (L, LANES), 1)
    first_half = lane < hd
    for s in range(nseq):
        dt = _softplus(sm_ref[s] + dtb_ref[...])
        la = -jnp.exp(alog_ref[...]) * dt
        g_cum = _dot_w3(_tri_incl(L), la)
        g_cum_t = g_cum.T
        dt_e = _dot_x3(dt, ex)
        g_e = _dot_x3(g_cum, ex)
        g_last = g_e[L - 1:L, :]
        e_g = jnp.exp(g_e)
        e_dec = jnp.exp(g_last - g_e)
        e_last = jnp.exp(g_last)

        xs = xs_ref[s]
        xdt = xs * dt_e
        xdec = xdt * e_dec
        y_parts = []
        for g in range(SSM_GROUPS):
            gsl = slice(g * gw, (g + 1) * gw)
            b_g = bs_ref[s, :, g * SSM_STATE:(g + 1) * SSM_STATE]
            c_g = cs_ref[s, :, g * SSM_STATE:(g + 1) * SSM_STATE]
            cb = _dot_nt(c_g, b_g)
            st_g = st_ref[s, :, gsl]
            y_off = _dot(c_g, st_g) * e_g[:, gsl]
            st_ref[s, :, gsl] = st_g * e_last[:, gsl] + _dot_tn(b_g, xdec[:, gsl])
            for p in range(heads_per_group // 2):
                h0 = g * heads_per_group + 2 * p
                base = h0 * hd
                sc = []
                for h in (h0, h0 + 1):
                    diff = g_cum[:, h:h + 1] - g_cum_t[h:h + 1, :]
                    sc.append(cb * jnp.exp(jnp.where(causal, diff, -jnp.inf)))
                yd = _dot(jnp.concatenate(sc, axis=0), xdt[:, base:base + LANES])
                y_parts.append(jnp.where(first_half, yd[:L], yd[L:])
                               + y_off[:, base - g * gw:base - g * gw + LANES])
        y = jnp.concatenate(y_parts, axis=1) + dsk_ref[...] * xs
        y = y * _silu(z_ref[s])
        outs = []
        for g in range(SSM_GROUPS):
            seg = y[:, g * gw:(g + 1) * gw]
            outs.append(seg * lax.rsqrt(jnp.mean(seg * seg, axis=-1, keepdims=True) + NORM_EPS))
        o_ref[s] = (jnp.concatenate(outs, axis=1) * nw_ref[...]).astype(o_ref.dtype)


def _pad_lanes(v, offset=0, width=LANES):
    out = jnp.zeros((1, width), F32)
    return lax.dynamic_update_slice(out, v.reshape(1, -1).astype(F32), (0, offset))


def _ssd_branch(pc, pn, a_log, dt_bias, d_skip, norm_w, bsz, seq):
    L = SSD_CHUNK
    nseq = SEQ_PER_STEP if bsz % SEQ_PER_STEP == 0 else 1
    pc3 = pc.reshape(bsz, seq, N_PC)
    pn3 = pn.reshape(bsz, seq, N_PN)
    ex = np.zeros((LANES, SSM_INNER), np.float32)
    for h in range(SSM_HEADS):
        ex[h, h * SSM_HEAD_DIM:(h + 1) * SSM_HEAD_DIM] = 1.0
    vec = lambda w: pl.BlockSpec((1, w), lambda b, c: (0, 0))
    tok = lambda width, col: pl.BlockSpec((nseq, L, width), lambda b, c: (b, c, col // width))
    out = pl.pallas_call(
        functools.partial(_ssd_kernel, L=L, nseq=nseq),
        grid=(bsz // nseq, seq // L),
        in_specs=[tok(SSM_INNER, PC_XS), tok(256, PC_BS), tok(256, PC_CS), tok(SSM_INNER, PN_ZSSM),
                  tok(LANES, PC_SMALL),
                  vec(LANES), vec(LANES), vec(SSM_INNER), vec(SSM_INNER),
                  pl.BlockSpec((LANES, SSM_INNER), lambda b, c: (0, 0))],
        out_specs=pl.BlockSpec((nseq, L, SSM_INNER), lambda b, c: (b, c, 0)),
        out_shape=jax.ShapeDtypeStruct((bsz, seq, SSM_INNER), BF16),
        scratch_shapes=[pltpu.VMEM((nseq, SSM_STATE, SSM_INNER), F32)],
        compiler_params=_cparams("parallel", "arbitrary"),
    )(pc3, pc3, pc3, pn3, pc3,
      _pad_lanes(a_log, SMALL_DT), _pad_lanes(dt_bias, SMALL_DT),
      jnp.repeat(d_skip.astype(F32), SSM_HEAD_DIM).reshape(1, SSM_INNER),
      norm_w.reshape(1, SSM_INNER).astype(F32), jnp.asarray(ex, BF16))
    return out.reshape(bsz * seq, SSM_INNER)


def _head_stack(x):
    half = x.shape[1] // 2
    lane = lax.broadcasted_iota(jnp.int32, x.shape, 1)
    return jnp.concatenate([jnp.where(lane < half, x, 0.0), jnp.where(lane >= half, x, 0.0)], axis=0)


def _dplr_pairs(units, *, L):
    nu = len(units)
    P = 2 * L
    vc = units[0]["v"].shape[1]
    row = lax.broadcasted_iota(jnp.int32, (L, P), 0)
    tcol = lax.broadcasted_iota(jnp.int32, (L, P), 1)
    first = tcol < L
    tcol = jnp.where(first, tcol, tcol - L)
    strict, lower = tcol < row, tcol <= row

    sc = [_dot_nt(jnp.concatenate([u["a_sc"], u["r_sc"]], axis=0),
                  jnp.concatenate([_head_stack(u["b_sc"]), _head_stack(u["k_sc"])], axis=0))
          for u in units]
    v_st = [_head_stack(u["v"]) for u in units]

    def read(u):
        rows = jnp.concatenate([u["a_sr"], u["r_sr"]], axis=0)
        st = u["s_prev"]
        if len(st) == 1:
            return _dot_nt(rows, st[0])
        kh = rows.shape[1] // 2
        return jnp.concatenate([_dot_nt(rows[:, :kh], st[0]), _dot_nt(rows[:, kh:], st[1])], axis=1)

    reads = [read(u) for u in units]
    n_mat, a_rb, aks = [], [], []
    for i, u in enumerate(units):
        sb, sk = sc[i][:, :P], sc[i][:, P:]
        if u.get("dm_a") is None:
            n_mat.append(jnp.where(strict, sb[:L], 0.0))
            a_rb.append(jnp.where(lower, sb[L:], 0.0))
            aks.append(jnp.concatenate([jnp.where(strict, sk[:L], 0.0), jnp.where(lower, sk[L:], 0.0)], axis=0))
        else:
            n_mat.append(sb[:L] * u["dm_a"])
            a_rb.append(sb[L:] * u["dm_r"])
            aks.append(jnp.concatenate([sk[:L] * u["dm_a"], sk[L:] * u["dm_r"]], axis=0))
    av = [_dot(aks[i], v_st[i]) for i in range(nu)]
    x = [reads[i][:L] + av[i][:L] for i in range(nu)]
    y_part = [reads[i][L:] + av[i][L:] for i in range(nu)]

    steps = int(math.log2(L))
    for s in range(steps):
        x_st = [_head_stack(x[i]) for i in range(nu)]
        if s + 1 < steps:
            t = [_dot(n_mat[i], jnp.concatenate(
                    [x_st[i], jnp.concatenate([jnp.where(first, n_mat[i], 0.0),
                                               jnp.where(first, 0.0, n_mat[i])], axis=0)], axis=1))
                 for i in range(nu)]
            x = [x[i] + t[i][:, :vc] for i in range(nu)]
            n_mat = [t[i][:, vc:] for i in range(nu)]
        else:
            x = [x[i] + _dot(n_mat[i], x_st[i]) for i in range(nu)]
    ys = [y_part[i] + _dot(a_rb[i], _head_stack(x[i])) for i in range(nu)]

    s_new = []
    for i, u in enumerate(units):
        xv = jnp.concatenate([x[i], u["v"]], axis=0)
        bk = jnp.concatenate([u["b_dec"], u["k_dec"]], axis=0)
        st, pl_ = u["s_prev"], u["p_last"]
        if len(st) == 1:
            upd = _dot_tn(xv, bk)
            ri = lax.broadcasted_iota(jnp.int32, upd.shape, 0) // (upd.shape[0] // 2)
            ci = lax.broadcasted_iota(jnp.int32, upd.shape, 1) // (upd.shape[1] // 2)
            s_new.append([st[0] * pl_[0] + jnp.where(ri == ci, upd, 0.0)])
        else:
            vh, kh = xv.shape[1] // 2, bk.shape[1] // 2
            s_new.append([st[0] * pl_[0] + _dot_tn(xv[:, :vh], bk[:, :kh]),
                          st[1] * pl_[1] + _dot_tn(xv[:, vh:], bk[:, kh:])])
    return ys, s_new


def _gdn_kernel(q_ref, k_ref, v_ref, z_ref, sm_ref, alog_ref, dtb_ref, nw_ref, o_ref, st_ref, *, L, nseq):
    @pl.when(pl.program_id(1) == 0)
    def _():
        st_ref[...] = jnp.zeros(st_ref.shape, F32)

    hd = GDN_HEAD_DIM
    row = lax.broadcasted_iota(jnp.int32, (L, 2 * L), 0)
    tcol = lax.broadcasted_iota(jnp.int32, (L, 2 * L), 1)
    first_t = tcol < L
    tcol = jnp.where(first_t, tcol, tcol - L)
    first_c = lax.broadcasted_iota(jnp.int32, (L, 2 * hd), 1) < hd
    nw = nw_ref[...]
    scale = hd ** -0.5
    units = []
    for s in range(nseq):
        sm = sm_ref[s]
        g = -jnp.exp(alog_ref[...]) * _softplus(sm + dtb_ref[...])
        beta = _sigmoid(sm)
        g_cum = _dot_w3(_tri_incl(L), g)
        g_cum_t = jnp.concatenate([g_cum] * (LANES // L), axis=0).T
        for p in range(GDN_HEADS // 2):
            sl = slice(2 * p * hd, (2 * p + 2) * hd)
            q = q_ref[s, :, sl]
            k = k_ref[s, :, sl]

            def per_head(t):
                return jnp.where(first_c, jnp.sum(t[:, :hd], axis=-1, keepdims=True),
                                 jnp.sum(t[:, hd:], axis=-1, keepdims=True))

            def col(arr, base):
                h0 = base + 2 * p
                return arr[:, h0:h0 + 1], arr[:, h0 + 1:h0 + 2]

            qn = q * (lax.rsqrt(per_head(q * q) + L2_EPS) * scale)
            kn = k * lax.rsqrt(per_head(k * k) + L2_EPS)
            gc0, gc1 = col(g_cum, SMALL_A)
            gs0, gs1 = col(g, SMALL_A)
            bt0, bt1 = col(beta, SMALL_B)
            ch = lambda a0, a1: jnp.where(first_c, a0, a1)
            tm_ = lambda a0, a1: jnp.where(first_t, a0, a1)
            gc, gs, bt = ch(gc0, gc1), ch(gs0, gs1), ch(bt0, bt1)
            g_prev = gc - gs
            h0 = SMALL_A + 2 * p
            gr = tm_(g_cum_t[h0:h0 + 1, :], g_cum_t[h0 + 1:h0 + 2, :])
            ge0, ge1 = g_cum[L - 1:L, h0:h0 + 1], g_cum[L - 1:L, h0 + 1:h0 + 2]
            e_end = jnp.exp(ch(ge0, ge1) - gc)
            b_vec = -(bt * jnp.exp(gs)) * kn
            k_vec = bt * kn
            units.append(dict(
                a_sc=kn, r_sc=qn, b_sc=b_vec, k_sc=k_vec,
                a_sr=kn * jnp.exp(g_prev), r_sr=qn * jnp.exp(gc),
                b_dec=b_vec * e_end, k_dec=k_vec * e_end,
                v=v_ref[s, :, sl], s_prev=[st_ref[s, 2 * p], st_ref[s, 2 * p + 1]],
                p_last=[jnp.exp(ge0), jnp.exp(ge1)],
                dm_a=jnp.exp(jnp.where(tcol < row, tm_(gc0 - gs0, gc1 - gs1) - gr, -jnp.inf)),
                dm_r=jnp.exp(jnp.where(tcol <= row, tm_(gc0, gc1) - gr, -jnp.inf))))
    ys, s_new = _dplr_pairs(units, L=L)
    npair = GDN_HEADS // 2
    for s in range(nseq):
        for p in range(npair):
            sl = slice(2 * p * hd, (2 * p + 2) * hd)
            st_ref[s, 2 * p] = s_new[s * npair + p][0]
            st_ref[s, 2 * p + 1] = s_new[s * npair + p][1]
            y = ys[s * npair + p]
            ms = jnp.where(first_c, jnp.mean(y[:, :hd] * y[:, :hd], axis=-1, keepdims=True),
                           jnp.mean(y[:, hd:] * y[:, hd:], axis=-1, keepdims=True))
            yn = y * lax.rsqrt(ms + NORM_EPS) * nw
            o_ref[s, :, sl] = (yn * _silu(z_ref[s, :, sl])).astype(o_ref.dtype)


def _gdn_branch(pc, pn, a_log, dt_bias, norm_w, bsz, seq):
    L = DPLR_CHUNK
    nseq = SEQ_PER_STEP if bsz % SEQ_PER_STEP == 0 else 1
    pc3 = pc.reshape(bsz, seq, N_PC)
    pn3 = pn.reshape(bsz, seq, N_PN)
    vec = lambda w: pl.BlockSpec((1, w), lambda b, c: (0, 0))
    tok = lambda width, col: pl.BlockSpec((nseq, L, width), lambda b, c: (b, c, col // width))
    out = pl.pallas_call(
        functools.partial(_gdn_kernel, L=L, nseq=nseq),
        grid=(bsz // nseq, seq // L),
        in_specs=[tok(GDN_INNER, PC_Q), tok(GDN_INNER, PC_K), tok(GDN_INNER, PC_V),
                  tok(GDN_INNER, PN_ZGDN), tok(LANES, PC_SMALL),
                  vec(LANES), vec(LANES), vec(2 * GDN_HEAD_DIM)],
        out_specs=pl.BlockSpec((nseq, L, GDN_INNER), lambda b, c: (b, c, 0)),
        out_shape=jax.ShapeDtypeStruct((bsz, seq, GDN_INNER), BF16),
        scratch_shapes=[pltpu.VMEM((nseq, GDN_HEADS, LANES, LANES), F32)],
        compiler_params=_cparams("parallel", "arbitrary"),
    )(pc3, pc3, pc3, pn3, pc3,
      _pad_lanes(a_log, SMALL_A), _pad_lanes(dt_bias, SMALL_A),
      jnp.tile(norm_w.reshape(1, GDN_HEAD_DIM).astype(F32), (1, 2)))
    return out.reshape(bsz * seq, GDN_INNER)


def _rwkv_kernel(rkv_ref, lora_ref, mu_ref, mul_ref, w0_ref, w2_ref, a0_ref, a2_ref, g2_ref,
                 kk_ref, ka_ref, rk_ref, lnw_ref, lnb_ref, o_ref, sh_ref, shl_ref, st_ref, *, L, nseq):
    @pl.when(pl.program_id(1) == 0)
    def _():
        st_ref[...] = jnp.zeros(st_ref.shape, F32)
        for s in range(nseq):
            sh_ref[s, 0:SUBLANES, :] = jnp.zeros((SUBLANES, sh_ref.shape[2]), F32)
            shl_ref[s, 0:SUBLANES, :] = jnp.zeros((SUBLANES, shl_ref.shape[2]), F32)

    n = RWKV_INNER
    npair = RWKV_HEADS // 2
    ones_bd = _half_ones()
    inv_hd = 1.0 / RWKV_HEAD_DIM

    def head_sums(t):
        st = jnp.concatenate([t[:, p * LANES:(p + 1) * LANES] for p in range(npair)], axis=0)
        sm = _dot(st, ones_bd)
        return jnp.concatenate([sm[p * L:(p + 1) * L] for p in range(npair)], axis=1)

    units, tails = [], []
    for s in range(nseq):
        u = rkv_ref[s]
        ul = lora_ref[s]
        sh_ref[s, SUBLANES:SUBLANES + L, :] = u
        shl_ref[s, SUBLANES:SUBLANES + L, :] = ul
        u = u + (sh_ref[s, pl.ds(SUBLANES - 1, L), :] - u) * mu_ref[...]
        ul = ul + (shl_ref[s, pl.ds(SUBLANES - 1, L), :] - ul) * mul_ref[...]
        sh_ref[s, 0:SUBLANES, :] = sh_ref[s, L:L + SUBLANES, :]
        shl_ref[s, 0:SUBLANES, :] = shl_ref[s, L:L + SUBLANES, :]

        r, k, v = u[:, 0:n], u[:, n:2 * n], u[:, 2 * n:3 * n]
        lo = ul[:, 0:LANES]
        dg = ul[:, LANES:2 * LANES]
        w_log = -_softplus(-(w0_ref[...] + _dot_hl(jnp.tanh(lo), w2_ref[...]))) - 0.5
        lw = -jnp.exp(w_log)
        a_ic = _sigmoid(a0_ref[...] + _dot_hl(lo, a2_ref[...]))
        gate = _dot(_sigmoid(dg), g2_ref[...])
        g_cum = _dot_w3(_tri_incl(L), lw)
        kk_raw = k * kk_ref[...]
        k_mod = k * (1.0 + (a_ic - 1.0) * ka_ref[...])
        kk_all = kk_raw * lax.rsqrt(head_sums(kk_raw * kk_raw) + L2_EPS)
        tails.append((head_sums(r * k_mod * rk_ref[...]) * v, gate))
        for p in range(npair):
            sl = slice(p * LANES, (p + 1) * LANES)
            kk = kk_all[:, sl]
            gc = g_cum[:, sl]
            g_end = gc[L - 1:L, :]
            e_neg = jnp.exp(-gc)
            e_end = jnp.exp(g_end - gc)
            b_vec = kk * a_ic[:, sl]
            k_vec = k_mod[:, sl]
            a_t = -kk * jnp.exp(gc - lw[:, sl])
            r_t = r[:, sl] * jnp.exp(gc)
            units.append(dict(
                a_sc=a_t, r_sc=r_t, b_sc=b_vec * e_neg, k_sc=k_vec * e_neg,
                a_sr=a_t, r_sr=r_t, b_dec=b_vec * e_end, k_dec=k_vec * e_end,
                v=v[:, sl], s_prev=[st_ref[s, p]], p_last=[jnp.exp(g_end)]))
    ys, s_new = _dplr_pairs(units, L=L)
    for s in range(nseq):
        for p in range(npair):
            st_ref[s, p] = s_new[s * npair + p][0]
        y = jnp.concatenate(ys[s * npair:(s + 1) * npair], axis=1)
        yc = y - head_sums(y) * inv_hd
        var = head_sums(yc * yc) * inv_hd
        yn = yc * lax.rsqrt(var + RWKV_GN_EPS) * lnw_ref[...] + lnb_ref[...]
        bonus, gate = tails[s]
        o_ref[s] = ((yn + bonus) * gate).astype(o_ref.dtype)


def _rwkv_branch(pc, pn, mu, w0, w2, a0, a2, g2, k_k, k_a, r_k, ln_w, ln_b, bsz, seq):
    L = DPLR_CHUNK
    n = RWKV_INNER
    nseq = SEQ_PER_STEP if bsz % SEQ_PER_STEP == 0 else 1
    pc3 = pc.reshape(bsz, seq, N_PC)
    pn3 = pn.reshape(bsz, seq, N_PN)
    vec = lambda w: pl.BlockSpec((1, w), lambda b, c: (0, 0))
    mat = lambda: pl.BlockSpec((LANES, n), lambda b, c: (0, 0))
    tok = lambda width, col: pl.BlockSpec((nseq, L, width), lambda b, c: (b, c, col // width))
    w2p = jnp.concatenate([w2, jnp.zeros((LANES - RWKV_DECAY_LORA, n), F32)], axis=0)
    a2p = jnp.concatenate([jnp.zeros((RWKV_DECAY_LORA, n), F32), a2], axis=0)
    r1 = lambda t: t.reshape(1, -1).astype(F32)
    out = pl.pallas_call(
        functools.partial(_rwkv_kernel, L=L, nseq=nseq),
        grid=(bsz // nseq, seq // L),
        in_specs=[tok(3 * n, PN_RKV), tok(256, PC_LORA),
                  vec(3 * n), vec(256), vec(n), mat(), vec(n), mat(), mat(),
                  vec(n), vec(n), vec(n), vec(n), vec(n)],
        out_specs=pl.BlockSpec((nseq, L, n), lambda b, c: (b, c, 0)),
        out_shape=jax.ShapeDtypeStruct((bsz, seq, n), BF16),
        scratch_shapes=[pltpu.VMEM((nseq, L + SUBLANES, 3 * n), F32),
                        pltpu.VMEM((nseq, L + SUBLANES, 256), F32),
                        pltpu.VMEM((nseq, RWKV_HEADS // 2, LANES, LANES), F32)],
        compiler_params=_cparams("parallel", "arbitrary"),
    )(pn3, pc3, r1(mu[:3 * n]), r1(mu[3 * n:]), r1(w0), w2p, r1(a0), a2p, g2.astype(F32),
      r1(k_k), r1(k_a), r1(r_k), r1(ln_w), r1(ln_b))
    return out.reshape(bsz * seq, n)


def _merge_kernel(x_ref, ys_ref, yg_ref, yr_ref, gate_ref, ps_ref, pg_ref, pr_ref, wo_ref, o_ref):
    d = D_MODEL
    m = _sigmoid(gate_ref[:, 0:d]) * jnp.dot(ys_ref[...], ps_ref[...], preferred_element_type=F32)
    m = m + _sigmoid(gate_ref[:, d:2 * d]) * jnp.dot(yg_ref[...], pg_ref[...], preferred_element_type=F32)
    m = m + _sigmoid(gate_ref[:, 2 * d:3 * d]) * jnp.dot(yr_ref[...], pr_ref[...], preferred_element_type=F32)
    o_ref[...] = x_ref[...] + jnp.dot(m.astype(BF16), wo_ref[...], preferred_element_type=F32)


def _merge(x, y_ssm, y_gdn, y_rwkv, pn, p_ssm, p_gdn, p_rwkv, w_out, tm=512):
    t, d = x.shape
    tok = lambda: pl.BlockSpec((tm, d), lambda i: (i, 0))
    wgt = lambda: pl.BlockSpec((d, d), lambda i: (0, 0))
    return pl.pallas_call(
        _merge_kernel,
        grid=(t // tm,),
        in_specs=[tok(), tok(), tok(), tok(),
                  pl.BlockSpec((tm, 3 * d), lambda i: (i, PN_GATE // (3 * d))),
                  wgt(), wgt(), wgt(), wgt()],
        out_specs=tok(),
        out_shape=jax.ShapeDtypeStruct((t, d), F32),
        compiler_params=_cparams("parallel"),
    )(x, y_ssm, y_gdn, y_rwkv, pn, p_ssm, p_gdn, p_rwkv, w_out)


def _ffn_kernel(x_ref, nw_ref, wg_ref, wu_ref, wd_ref, o_ref, h_ref, acc_ref):
    f = pl.program_id(1)

    @pl.when(f == 0)
    def _():
        x = x_ref[...]
        ms = jnp.mean(x * x, axis=-1, keepdims=True)
        h_ref[...] = (x * lax.rsqrt(ms + NORM_EPS) * nw_ref[...]).astype(BF16)
        acc_ref[...] = x

    h = h_ref[...]
    act = _silu(jnp.dot(h, wg_ref[...], preferred_element_type=F32)) * jnp.dot(
        h, wu_ref[...], preferred_element_type=F32)
    acc_ref[...] += jnp.dot(act.astype(BF16), wd_ref[...], preferred_element_type=F32)

    @pl.when(f == pl.num_programs(1) - 1)
    def _():
        o_ref[...] = acc_ref[...]


def _ffn_dense(x, norm_w, w_gate, w_up, w_down, tm=512, tf=1408):
    t, d = x.shape
    nf = w_gate.shape[1] // tf
    return pl.pallas_call(
        _ffn_kernel,
        grid=(t // tm, nf),
        in_specs=[pl.BlockSpec((tm, d), lambda i, f: (i, 0)),
                  pl.BlockSpec((1, d), lambda i, f: (0, 0)),
                  pl.BlockSpec((d, tf), lambda i, f: (0, f)),
                  pl.BlockSpec((d, tf), lambda i, f: (0, f)),
                  pl.BlockSpec((tf, d), lambda i, f: (f, 0))],
        out_specs=pl.BlockSpec((tm, d), lambda i, f: (i, 0)),
        out_shape=jax.ShapeDtypeStruct((t, d), F32),
        scratch_shapes=[pltpu.VMEM((tm, d), BF16), pltpu.VMEM((tm, d), F32)],
        compiler_params=_cparams("parallel", "arbitrary"),
    )(x, norm_w.reshape(1, d).astype(F32), w_gate, w_up, w_down)


MOE_TM = 1024
MOE_BLK = 128
MOE_FIRST_ROWS = 288


def _route_kernel(x_ref, nw_ref, rt_ref, h_ref, cmb_ref, cnt_ref):
    x = x_ref[...]
    lane = lax.broadcasted_iota(jnp.int32, cmb_ref.shape, 1)
    ms = jnp.mean(x * x, axis=-1, keepdims=True)
    hf = x * lax.rsqrt(ms + NORM_EPS) * nw_ref[...]
    h_ref[...] = hf.astype(BF16)
    logits = jnp.where(lane < N_EXPERTS, _dot_hl(hf, rt_ref[...]), -jnp.inf)
    m1 = jnp.max(logits, axis=-1, keepdims=True)
    i1 = jnp.min(jnp.where(logits == m1, lane, LANES), axis=-1, keepdims=True)
    rest = jnp.where(lane == i1, -jnp.inf, logits)
    m2 = jnp.max(rest, axis=-1, keepdims=True)
    i2 = jnp.min(jnp.where(rest == m2, lane, LANES), axis=-1, keepdims=True)
    e2 = jnp.exp(m2 - m1)
    p1 = 1.0 / (1.0 + e2)
    cmb = jnp.where(lane == i1, p1, 0.0) + jnp.where(lane == i2, e2 * p1, 0.0)
    cmb_ref[...] = cmb
    cnt_ref[0] = jnp.sum(jnp.where(cmb > 0.0, 1.0, 0.0), axis=0, keepdims=True).astype(jnp.int32)


def _moe_route(x, norm_w, router, tm):
    t, d = x.shape
    ne = router.shape[1]
    rt = jnp.concatenate([router.astype(F32), jnp.zeros((d, LANES - ne), F32)], axis=1)
    return pl.pallas_call(
        _route_kernel,
        grid=(t // tm,),
        in_specs=[pl.BlockSpec((tm, d), lambda i: (i, 0)),
                  pl.BlockSpec((1, d), lambda i: (0, 0)),
                  pl.BlockSpec((d, LANES), lambda i: (0, 0))],
        out_specs=[pl.BlockSpec((tm, d), lambda i: (i, 0)),
                   pl.BlockSpec((tm, LANES), lambda i: (i, 0)),
                   pl.BlockSpec((1, 1, LANES), lambda i: (i, 0, 0))],
        out_shape=[jax.ShapeDtypeStruct((t, d), BF16),
                   jax.ShapeDtypeStruct((t, LANES), F32),
                   jax.ShapeDtypeStruct((t // tm, 1, LANES), jnp.int32)],
        compiler_params=_cparams("parallel"),
    )(x, norm_w.reshape(1, d).astype(F32), rt)


def _moe_kernel(cnt_ref, x_ref, h_ref, cmb_ref, wg_ref, wu_ref, wd_ref, fw_ref, o_ref,
                rank_ref, xc_ref, yc_ref, oh_ref, *, tm, final_norm):
    i = pl.program_id(0)
    e = pl.program_id(1)
    f = pl.program_id(2)
    last_f = f == pl.num_programs(2) - 1
    lane = lax.broadcasted_iota(jnp.int32, (tm, LANES), 1)

    @pl.when((e == 0) & (f == 0))
    def _():
        o_ref[...] = jnp.zeros(o_ref.shape, F32)
        r = lax.broadcasted_iota(jnp.int32, (tm, tm), 0)
        c = lax.broadcasted_iota(jnp.int32, (tm, tm), 1)
        before = jnp.where(c < r, 1.0, 0.0).astype(BF16)
        sel = jnp.where(cmb_ref[...] > 0.0, 1.0, 0.0).astype(BF16)
        rank_ref[...] = jnp.dot(before, sel, preferred_element_type=F32)

    cnt = cnt_ref[i, e]

    def weight():
        return jnp.sum(jnp.where(lane == e, cmb_ref[...], 0.0), axis=-1, keepdims=True)

    def one_hot(off, rows):
        r_e = jnp.sum(jnp.where(lane == e, rank_ref[...], 0.0), axis=-1, keepdims=True)
        key = jnp.where(weight() > 0.0, r_e, -1.0) - jnp.asarray(off, F32)
        slot = lax.broadcasted_iota(jnp.int32, (tm, rows), 1).astype(F32)
        return jnp.where(key == slot, 1.0, 0.0).astype(BF16)

    def process(off, rows, cache):
        rs = pl.ds(off, rows)

        @pl.when(f == 0)
        def _():
            onehot = one_hot(off, rows)
            if cache:
                oh_ref[...] = onehot
            xc_ref[rs, :] = _dot_tn(onehot, h_ref[...]).astype(BF16)
            yc_ref[rs, :] = jnp.zeros((rows, yc_ref.shape[1]), F32)

        xc = xc_ref[rs, :]
        act = _silu(jnp.dot(xc, wg_ref[0], preferred_element_type=F32)) * jnp.dot(
            xc, wu_ref[0], preferred_element_type=F32)
        yc_ref[rs, :] += jnp.dot(act.astype(BF16), wd_ref[0], preferred_element_type=F32)

        @pl.when(last_f)
        def _():
            onehot = oh_ref[...] if cache else one_hot(off, rows)
            o_ref[...] += weight() * jnp.dot(onehot, yc_ref[rs, :].astype(BF16), preferred_element_type=F32)

    @pl.when(cnt > 0)
    def _():
        process(0, MOE_FIRST_ROWS, True)

    def extra(b, carry):
        process(pl.multiple_of(MOE_FIRST_ROWS + b * MOE_BLK, 32), MOE_BLK, False)
        return carry

    lax.fori_loop(0, (jnp.maximum(cnt - MOE_FIRST_ROWS, 0) + MOE_BLK - 1) // MOE_BLK, extra, 0)

    @pl.when((e == pl.num_programs(1) - 1) & last_f)
    def _():
        y = x_ref[...] + o_ref[...]
        if final_norm:
            y = y * lax.rsqrt(jnp.mean(y * y, axis=-1, keepdims=True) + NORM_EPS) * fw_ref[...]
        o_ref[...] = y


def _ffn_moe(x, norm_w, router, w_gate, w_up, w_down, final_w, tm=MOE_TM, tf=1408):
    t, d = x.shape
    ne, _, ff = w_gate.shape
    nf = ff // tf
    h, cmb, cnt = _moe_route(x, norm_w, router, tm)
    final_norm = final_w is not None
    fw = (final_w if final_norm else jnp.ones((d,), F32)).reshape(1, d).astype(F32)
    cap = MOE_FIRST_ROWS + -(-max(tm - MOE_FIRST_ROWS, 0) // MOE_BLK) * MOE_BLK
    grid_spec = pltpu.PrefetchScalarGridSpec(
        num_scalar_prefetch=1,
        grid=(t // tm, ne, nf),
        in_specs=[pl.BlockSpec((tm, d), lambda i, e, f, c: (i, 0)),
                  pl.BlockSpec((tm, d), lambda i, e, f, c: (i, 0)),
                  pl.BlockSpec((tm, LANES), lambda i, e, f, c: (i, 0)),
                  pl.BlockSpec((1, d, tf), lambda i, e, f, c: (e, 0, f)),
                  pl.BlockSpec((1, d, tf), lambda i, e, f, c: (e, 0, f)),
                  pl.BlockSpec((1, tf, d), lambda i, e, f, c: (e, f, 0)),
                  pl.BlockSpec((1, d), lambda i, e, f, c: (0, 0))],
        out_specs=pl.BlockSpec((tm, d), lambda i, e, f, c: (i, 0)),
        scratch_shapes=[pltpu.VMEM((tm, LANES), F32), pltpu.VMEM((cap, d), BF16), pltpu.VMEM((cap, d), F32),
                        pltpu.VMEM((tm, MOE_FIRST_ROWS), BF16)])
    return pl.pallas_call(
        functools.partial(_moe_kernel, tm=tm, final_norm=final_norm),
        grid_spec=grid_spec,
        out_shape=jax.ShapeDtypeStruct((t, d), F32),
        compiler_params=_cparams("parallel", "arbitrary", "arbitrary"),
    )(cnt.reshape(t // tm, LANES), x, h, cmb, w_gate, w_up, w_down, fw)


def _reorder_w_in(w):
    d = w.shape[0]
    wc = [w[:, 0:1024], w[:, 1536:4608], w[:, 1024:1536], w[:, 9760:10016], w[:, 5632:5648],
          w[:, 6672:6688], jnp.zeros((d, N_PC - PC_SMALL - 32), w.dtype)]
    wn = [w[:, 6688:9760], w[:, 10016:13088], w[:, 4608:5632], w[:, 5648:6672]]
    return jnp.concatenate(wc, axis=1), jnp.concatenate(wn, axis=1)


def _token_mixer_layer(x, i, bsz, seq, p):
    h = _rmsnorm(x, p["attn_norm_w"][i], BF16)
    wc, wn = _reorder_w_in(p["w_in"][i])
    pc, pn = _inproj(h, wc, wn, p["conv_w"][i], p["conv_b"][i], seq, tm=1024)
    y_ssm = _ssd_branch(pc, pn, p["ssm_a_log"][i], p["ssm_dt_bias"][i], p["ssm_d"][i],
                        p["ssm_norm_w"][i], bsz, seq)
    y_gdn = _gdn_branch(pc, pn, p["gdn_a_log"][i], p["gdn_dt_bias"][i], p["gdn_norm_w"][i],
                        bsz, seq)
    y_rwkv = _rwkv_branch(pc, pn, p["rwkv_mu"][i], p["rwkv_w0"][i], p["rwkv_w2"][i], p["rwkv_a0"][i],
                          p["rwkv_a2"][i], p["rwkv_g2"][i], p["rwkv_k_k"][i], p["rwkv_k_a"][i],
                          p["rwkv_r_k"][i], p["rwkv_ln_w"][i], p["rwkv_ln_b"][i], bsz, seq)
    return _merge(x, y_ssm, y_gdn, y_rwkv, pn, p["proj_ssm"][i], p["proj_gdn"][i], p["proj_rwkv"][i],
                  p["w_out"][i])


def kernel(x, attn_norm_w, w_in, conv_w, conv_b, ssm_a_log, ssm_dt_bias, ssm_d, ssm_norm_w, gdn_a_log, gdn_dt_bias, gdn_norm_w, rwkv_mu, rwkv_w0, rwkv_w2, rwkv_a0, rwkv_a2, rwkv_g2, rwkv_k_k, rwkv_k_a, rwkv_r_k, rwkv_ln_w, rwkv_ln_b, proj_ssm, proj_gdn, proj_rwkv, w_out, ffn_norm_w, dense_w_gate, dense_w_up, dense_w_down, moe_router, moe_w_gate, moe_w_up, moe_w_down, final_norm_w):
    p = dict(attn_norm_w=attn_norm_w, w_in=w_in, conv_w=conv_w, conv_b=conv_b, ssm_a_log=ssm_a_log,
             ssm_dt_bias=ssm_dt_bias, ssm_d=ssm_d, ssm_norm_w=ssm_norm_w, gdn_a_log=gdn_a_log,
             gdn_dt_bias=gdn_dt_bias, gdn_norm_w=gdn_norm_w, rwkv_mu=rwkv_mu, rwkv_w0=rwkv_w0,
             rwkv_w2=rwkv_w2, rwkv_a0=rwkv_a0, rwkv_a2=rwkv_a2, rwkv_g2=rwkv_g2, rwkv_k_k=rwkv_k_k,
             rwkv_k_a=rwkv_k_a, rwkv_r_k=rwkv_r_k, rwkv_ln_w=rwkv_ln_w, rwkv_ln_b=rwkv_ln_b,
             proj_ssm=proj_ssm, proj_gdn=proj_gdn, proj_rwkv=proj_rwkv, w_out=w_out)
    for name in ("w_in", "proj_ssm", "proj_gdn", "proj_rwkv", "w_out"):
        p[name] = p[name].astype(BF16)
    dense_w_gate, dense_w_up, dense_w_down = (t.astype(BF16) for t in (dense_w_gate, dense_w_up, dense_w_down))
    moe_w_gate, moe_w_up, moe_w_down = (t.astype(BF16) for t in (moe_w_gate, moe_w_up, moe_w_down))
    bsz, seq, d = x.shape
    depth = attn_norm_w.shape[0]
    xt = x.reshape(bsz * seq, d)
    for i in range(depth):
        xt = _token_mixer_layer(xt, i, bsz, seq, p)
        j = i // 2
        last = i == depth - 1
        if i % 2 == 0:
            xt = _ffn_dense(xt, ffn_norm_w[i], dense_w_gate[j], dense_w_up[j], dense_w_down[j])
            if last:
                xt = _rmsnorm(xt, final_norm_w, F32)
        else:
            xt = _ffn_moe(xt, ffn_norm_w[i], moe_router[j], moe_w_gate[j], moe_w_up[j], moe_w_down[j],
                          final_norm_w if last else None)
    return xt.reshape(bsz, seq, d)
```

```python
import functools
import math

import numpy as np
import jax
import jax.numpy as jnp
from jax import lax
from jax.experimental import pallas as pl
from jax.experimental.pallas import tpu as pltpu

F32 = jnp.float32
BF16 = jnp.bfloat16

LANES = 128
SUBLANES = 8
VMEM_LIMIT_BYTES = 56 * 1024 * 1024

D_MODEL = 1024
SSM_HEADS, SSM_HEAD_DIM, SSM_GROUPS, SSM_STATE = 16, 64, 2, 128
SSM_INNER = SSM_HEADS * SSM_HEAD_DIM
GDN_HEADS, GDN_HEAD_DIM = 8, 128
GDN_INNER = GDN_HEADS * GDN_HEAD_DIM
RWKV_HEADS, RWKV_HEAD_DIM = 16, 64
RWKV_INNER = RWKV_HEADS * RWKV_HEAD_DIM
RWKV_DECAY_LORA, RWKV_ICLR_LORA, RWKV_GATE_LORA = 64, 64, 128
RWKV_GN_EPS = 64e-5
CONV_K = 4
FFN_DIM = 2816
N_EXPERTS = 8
NORM_EPS = 1e-6
L2_EPS = 1e-6

PC_XS, PC_Q, PC_K, PC_V = 0, 1024, 2048, 3072
PC_BS, PC_CS = 4096, 4352
PC_CONV_END = 4608
PC_LORA = 4608
PC_SMALL = 4864
N_PC = 5120
PN_RKV = 0
PN_GATE = 3072
PN_ZSSM, PN_ZGDN = 6144, 7168
N_PN = 8192
SMALL_DT, SMALL_A, SMALL_B = 0, 16, 24

SSD_CHUNK = 128
DPLR_CHUNK = 64
SEQ_PER_STEP = 4
INPROJ_ROW_CHUNK = 512


def _cparams(*sem):
    return pltpu.CompilerParams(dimension_semantics=sem, vmem_limit_bytes=VMEM_LIMIT_BYTES)


def _dot(a, b):
    return jnp.dot(a.astype(BF16), b.astype(BF16), preferred_element_type=F32)


def _dot_nt(a, b):
    return lax.dot_general(a.astype(BF16), b.astype(BF16), (((1,), (1,)), ((), ())),
                           preferred_element_type=F32)


def _dot_tn(a, b):
    return lax.dot_general(a.astype(BF16), b.astype(BF16), (((0,), (0,)), ((), ())),
                           preferred_element_type=F32)


def _split3(x):
    hi = x.astype(BF16)
    r1 = x - hi.astype(F32)
    mid = r1.astype(BF16)
    lo = (r1 - mid.astype(F32)).astype(BF16)
    return hi, mid, lo


def _dot_x3(x, w):
    hi, mid, lo = _split3(x)
    return _dot(hi, w) + _dot(mid, w) + _dot(lo, w)


def _dot_w3(w, x):
    hi, mid, lo = _split3(x)
    return _dot(w, hi) + _dot(w, mid) + _dot(w, lo)


def _dot_hl(a, b):
    ah = a.astype(BF16)
    al = (a - ah.astype(F32)).astype(BF16)
    bh = b.astype(BF16)
    bl = (b - bh.astype(F32)).astype(BF16)
    return _dot(ah, bh) + _dot(ah, bl) + _dot(al, bh)


def _sigmoid(x):
    return 0.5 * jnp.tanh(0.5 * x) + 0.5


def _silu(x):
    return x * _sigmoid(x)


def _softplus(x):
    return jnp.maximum(x, 0.0) + jnp.log(1.0 + jnp.exp(-jnp.abs(x)))


def _tri_incl(n):
    r = lax.broadcasted_iota(jnp.int32, (n, n), 0)
    c = lax.broadcasted_iota(jnp.int32, (n, n), 1)
    return jnp.where(r >= c, 1.0, 0.0).astype(BF16)


def _half_ones():
    r = lax.broadcasted_iota(jnp.int32, (LANES, LANES), 0)
    c = lax.broadcasted_iota(jnp.int32, (LANES, LANES), 1)
    return jnp.where((r // 64) == (c // 64), 1.0, 0.0).astype(BF16)


def _rmsnorm_kernel(x_ref, w_ref, o_ref):
    x = x_ref[...]
    ms = jnp.mean(x * x, axis=-1, keepdims=True)
    o_ref[...] = (x * lax.rsqrt(ms + NORM_EPS) * w_ref[...]).astype(o_ref.dtype)


def _rmsnorm(x, w, out_dtype, tm=512):
    t, d = x.shape
    return pl.pallas_call(
        _rmsnorm_kernel,
        grid=(t // tm,),
        in_specs=[pl.BlockSpec((tm, d), lambda i: (i, 0)),
                  pl.BlockSpec((1, d), lambda i: (0, 0))],
        out_specs=pl.BlockSpec((tm, d), lambda i: (i, 0)),
        out_shape=jax.ShapeDtypeStruct((t, d), out_dtype),
        compiler_params=_cparams("parallel"),
    )(x, w.reshape(1, d))


def _inproj_kernel(a_ref, wc_ref, wn_ref, cw_ref, cb_ref, cm_ref, oc_ref, on_ref, sh_ref, *,
                   tm, tiles_per_seq):
    i = pl.program_id(1)
    tc = oc_ref.shape[1]

    @pl.when(i % tiles_per_seq == 0)
    def _():
        sh_ref[0:SUBLANES, :] = jnp.zeros((SUBLANES, tc), F32)

    rc = min(tm, INPROJ_ROW_CHUNK)
    starts = list(range(0, tm, rc))
    acc = jnp.dot(a_ref[0:rc, :], wc_ref[...], preferred_element_type=F32)
    for n, r0 in enumerate(starts):
        sh_ref[SUBLANES + r0:SUBLANES + r0 + rc, :] = acc
        on_ref[r0:r0 + rc, :] = jnp.dot(a_ref[r0:r0 + rc, :], wn_ref[...], preferred_element_type=F32)
        if n + 1 < len(starts):
            acc_next = jnp.dot(a_ref[r0 + rc:r0 + 2 * rc, :], wc_ref[...], preferred_element_type=F32)
        c = acc * cw_ref[CONV_K - 1:CONV_K, :] + cb_ref[...]
        for k in range(1, CONV_K):
            c = c + sh_ref[pl.ds(SUBLANES + r0 - k, rc), :] * cw_ref[CONV_K - 1 - k:CONV_K - k, :]
        oc_ref[r0:r0 + rc, :] = jnp.where(cm_ref[...] > 0.0, _silu(c), acc)
        if n + 1 < len(starts):
            acc = acc_next
    sh_ref[0:SUBLANES, :] = sh_ref[tm:tm + SUBLANES, :]


def _conv_cols(c):
    r = c.shape[0]
    return jnp.concatenate([c[:, 0:1024], c[:, 1536:4608], c[:, 1024:1536],
                            jnp.zeros((r, N_PC - PC_CONV_END), F32)], axis=1).astype(F32)


def _inproj(a, wc, wn, conv_w, conv_b, seq, tm, nsteps=8):
    m, k = a.shape
    tc, tn = N_PC // nsteps, N_PN // nsteps
    cw = _conv_cols(conv_w)
    cb = _conv_cols(conv_b.reshape(1, -1))
    cm = _conv_cols(jnp.ones((1, conv_w.shape[1]), F32))
    return pl.pallas_call(
        functools.partial(_inproj_kernel, tm=tm, tiles_per_seq=seq // tm),
        grid=(nsteps, m // tm),
        in_specs=[pl.BlockSpec((tm, k), lambda j, i: (i, 0)),
                  pl.BlockSpec((k, tc), lambda j, i: (0, j)),
                  pl.BlockSpec((k, tn), lambda j, i: (0, j)),
                  pl.BlockSpec((CONV_K, tc), lambda j, i: (0, j)),
                  pl.BlockSpec((1, tc), lambda j, i: (0, j)),
                  pl.BlockSpec((1, tc), lambda j, i: (0, j))],
        out_specs=[pl.BlockSpec((tm, tc), lambda j, i: (i, j)),
                   pl.BlockSpec((tm, tn), lambda j, i: (i, j))],
        out_shape=[jax.ShapeDtypeStruct((m, N_PC), F32), jax.ShapeDtypeStruct((m, N_PN), F32)],
        scratch_shapes=[pltpu.VMEM((tm + SUBLANES, tc), F32)],
        compiler_params=_cparams("parallel", "arbitrary"),
    )(a, wc, wn, cw, cb, cm)


def _ssd_kernel(xs_ref, bs_ref, cs_ref, z_ref, sm_ref, alog_ref, dtb_ref, dsk_ref, nw_ref, ex_ref,
                o_ref, st_ref, *, L, nseq):
    @pl.when(pl.program_id(1) == 0)
    def _():
        st_ref[...] = jnp.zeros(st_ref.shape, F32)

    hd = SSM_HEAD_DIM
    gw = SSM_INNER // SSM_GROUPS
    heads_per_group = SSM_HEADS // SSM_GROUPS
    ex = ex_ref[...]
    row = lax.broadcasted_iota(jnp.int32, (L, L), 0)
    col = lax.broadcasted_iota(jnp.int32, (L, L), 1)
    causal = row >= col
    lane = lax.broadcasted_iota(jnp.int32, (L, LANES), 1)
    first_half = lane < hd
    for s in range(nseq):
        dt = _softplus(sm_ref[s] + dtb_ref[...])
        la = -jnp.exp(alog_ref[...]) * dt
        g_cum = _dot_w3(_tri_incl(L), la)
        g_cum_t = g_cum.T
        dt_e = _dot_x3(dt, ex)
        g_e = _dot_x3(g_cum, ex)
        g_last = g_e[L - 1:L, :]
        e_g = jnp.exp(g_e)
        e_dec = jnp.exp(g_last - g_e)
        e_last = jnp.exp(g_last)

        xs = xs_ref[s]
        xdt = xs * dt_e
        xdec = xdt * e_dec
        y_parts = []
        for g in range(SSM_GROUPS):
            gsl = slice(g * gw, (g + 1) * gw)
            b_g = bs_ref[s, :, g * SSM_STATE:(g + 1) * SSM_STATE]
            c_g = cs_ref[s, :, g * SSM_STATE:(g + 1) * SSM_STATE]
            cb = _dot_nt(c_g, b_g)
            st_g = st_ref[s, :, gsl]
            y_off = _dot(c_g, st_g) * e_g[:, gsl]
            st_ref[s, :, gsl] = st_g * e_last[:, gsl] + _dot_tn(b_g, xdec[:, gsl])
            for p in range(heads_per_group // 2):
                h0 = g * heads_per_group + 2 * p
                base = h0 * hd
                sc = []
                for h in (h0, h0 + 1):
                    diff = g_cum[:, h:h + 1] - g_cum_t[h:h + 1, :]
                    sc.append(cb * jnp.exp(jnp.where(causal, diff, -jnp.inf)))
                yd = _dot(jnp.concatenate(sc, axis=0), xdt[:, base:base + LANES])
                y_parts.append(jnp.where(first_half, yd[:L], yd[L:])
                               + y_off[:, base - g * gw:base - g * gw + LANES])
        y = jnp.concatenate(y_parts, axis=1) + dsk_ref[...] * xs
        y = y * _silu(z_ref[s])
        outs = []
        for g in range(SSM_GROUPS):
            seg = y[:, g * gw:(g + 1) * gw]
            outs.append(seg * lax.rsqrt(jnp.mean(seg * seg, axis=-1, keepdims=True) + NORM_EPS))
        o_ref[s] = (jnp.concatenate(outs, axis=1) * nw_ref[...]).astype(o_ref.dtype)


def _pad_lanes(v, offset=0, width=LANES):
    out = jnp.zeros((1, width), F32)
    return lax.dynamic_update_slice(out, v.reshape(1, -1).astype(F32), (0, offset))


def _ssd_branch(pc, pn, a_log, dt_bias, d_skip, norm_w, bsz, seq):
    L = SSD_CHUNK
    nseq = SEQ_PER_STEP if bsz % SEQ_PER_STEP == 0 else 1
    pc3 = pc.reshape(bsz, seq, N_PC)
    pn3 = pn.reshape(bsz, seq, N_PN)
    ex = np.zeros((LANES, SSM_INNER), np.float32)
    for h in range(SSM_HEADS):
        ex[h, h * SSM_HEAD_DIM:(h + 1) * SSM_HEAD_DIM] = 1.0
    vec = lambda w: pl.BlockSpec((1, w), lambda b, c: (0, 0))
    tok = lambda width, col: pl.BlockSpec((nseq, L, width), lambda b, c: (b, c, col // width))
    out = pl.pallas_call(
        functools.partial(_ssd_kernel, L=L, nseq=nseq),
        grid=(bsz // nseq, seq // L),
        in_specs=[tok(SSM_INNER, PC_XS), tok(256, PC_BS), tok(256, PC_CS), tok(SSM_INNER, PN_ZSSM),
                  tok(LANES, PC_SMALL),
                  vec(LANES), vec(LANES), vec(SSM_INNER), vec(SSM_INNER),
                  pl.BlockSpec((LANES, SSM_INNER), lambda b, c: (0, 0))],
        out_specs=pl.BlockSpec((nseq, L, SSM_INNER), lambda b, c: (b, c, 0)),
        out_shape=jax.ShapeDtypeStruct((bsz, seq, SSM_INNER), BF16),
        scratch_shapes=[pltpu.VMEM((nseq, SSM_STATE, SSM_INNER), F32)],
        compiler_params=_cparams("parallel", "arbitrary"),
    )(pc3, pc3, pc3, pn3, pc3,
      _pad_lanes(a_log, SMALL_DT), _pad_lanes(dt_bias, SMALL_DT),
      jnp.repeat(d_skip.astype(F32), SSM_HEAD_DIM).reshape(1, SSM_INNER),
      norm_w.reshape(1, SSM_INNER).astype(F32), jnp.asarray(ex, BF16))
    return out.reshape(bsz * seq, SSM_INNER)


def _head_stack(x):
    half = x.shape[1] // 2
    lane = lax.broadcasted_iota(jnp.int32, x.shape, 1)
    return jnp.concatenate([jnp.where(lane < half, x, 0.0), jnp.where(lane >= half, x, 0.0)], axis=0)


def _dplr_pairs(units, *, L):
    nu = len(units)
    P = 2 * L
    vc = units[0]["v"].shape[1]
    row = lax.broadcasted_iota(jnp.int32, (L, P), 0)
    tcol = lax.broadcasted_iota(jnp.int32, (L, P), 1)
    first = tcol < L
    tcol = jnp.where(first, tcol, tcol - L)
    strict, lower = tcol < row, tcol <= row

    sc = [_dot_nt(jnp.concatenate([u["a_sc"], u["r_sc"]], axis=0),
                  jnp.concatenate([_head_stack(u["b_sc"]), _head_stack(u["k_sc"])], axis=0))
          for u in units]
    v_st = [_head_stack(u["v"]) for u in units]

    def read(u):
        rows = jnp.concatenate([u["a_sr"], u["r_sr"]], axis=0)
        st = u["s_prev"]
        if len(st) == 1:
            return _dot_nt(rows, st[0])
        kh = rows.shape[1] // 2
        return jnp.concatenate([_dot_nt(rows[:, :kh], st[0]), _dot_nt(rows[:, kh:], st[1])], axis=1)

    reads = [read(u) for u in units]
    n_mat, a_rb, aks = [], [], []
    for i, u in enumerate(units):
        sb, sk = sc[i][:, :P], sc[i][:, P:]
        if u.get("dm_a") is None:
            n_mat.append(jnp.where(strict, sb[:L], 0.0))
            a_rb.append(jnp.where(lower, sb[L:], 0.0))
            aks.append(jnp.concatenate([jnp.where(strict, sk[:L], 0.0), jnp.where(lower, sk[L:], 0.0)], axis=0))
        else:
            n_mat.append(sb[:L] * u["dm_a"])
            a_rb.append(sb[L:] * u["dm_r"])
            aks.append(jnp.concatenate([sk[:L] * u["dm_a"], sk[L:] * u["dm_r"]], axis=0))
    av = [_dot(aks[i], v_st[i]) for i in range(nu)]
    x = [reads[i][:L] + av[i][:L] for i in range(nu)]
    y_part = [reads[i][L:] + av[i][L:] for i in range(nu)]

    steps = int(math.log2(L))
    for s in range(steps):
        x_st = [_head_stack(x[i]) for i in range(nu)]
        if s + 1 < steps:
            t = [_dot(n_mat[i], jnp.concatenate(
                    [x_st[i], jnp.concatenate([jnp.where(first, n_mat[i], 0.0),
                                               jnp.where(first, 0.0, n_mat[i])], axis=0)], axis=1))
                 for i in range(nu)]
            x = [x[i] + t[i][:, :vc] for i in range(nu)]
            n_mat = [t[i][:, vc:] for i in range(nu)]
        else:
            x = [x[i] + _dot(n_mat[i], x_st[i]) for i in range(nu)]
    ys = [y_part[i] + _dot(a_rb[i], _head_stack(x[i])) for i in range(nu)]

    s_new = []
    for i, u in enumerate(units):
        xv = jnp.concatenate([x[i], u["v"]], axis=0)
        bk = jnp.concatenate([u["b_dec"], u["k_dec"]], axis=0)
        st, pl_ = u["s_prev"], u["p_last"]
        if len(st) == 1:
            upd = _dot_tn(xv, bk)
            ri = lax.broadcasted_iota(jnp.int32, upd.shape, 0) // (upd.shape[0] // 2)
            ci = lax.broadcasted_iota(jnp.int32, upd.shape, 1) // (upd.shape[1] // 2)
            s_new.append([st[0] * pl_[0] + jnp.where(ri == ci, upd, 0.0)])
        else:
            vh, kh = xv.shape[1] // 2, bk.shape[1] // 2
            s_new.append([st[0] * pl_[0] + _dot_tn(xv[:, :vh], bk[:, :kh]),
                          st[1] * pl_[1] + _dot_tn(xv[:, vh:], bk[:, kh:])])
    return ys, s_new


def _gdn_kernel(q_ref, k_ref, v_ref, z_ref, sm_ref, alog_ref, dtb_ref, nw_ref, o_ref, st_ref, *, L, nseq):
    @pl.when(pl.program_id(1) == 0)
    def _():
        st_ref[...] = jnp.zeros(st_ref.shape, F32)

    hd = GDN_HEAD_DIM
    row = lax.broadcasted_iota(jnp.int32, (L, 2 * L), 0)
    tcol = lax.broadcasted_iota(jnp.int32, (L, 2 * L), 1)
    first_t = tcol < L
    tcol = jnp.where(first_t, tcol, tcol - L)
    first_c = lax.broadcasted_iota(jnp.int32, (L, 2 * hd), 1) < hd
    nw = nw_ref[...]
    scale = hd ** -0.5
    units = []
    for s in range(nseq):
        sm = sm_ref[s]
        g = -jnp.exp(alog_ref[...]) * _softplus(sm + dtb_ref[...])
        beta = _sigmoid(sm)
        g_cum = _dot_w3(_tri_incl(L), g)
        g_cum_t = jnp.concatenate([g_cum] * (LANES // L), axis=0).T
        for p in range(GDN_HEADS // 2):
            sl = slice(2 * p * hd, (2 * p + 2) * hd)
            q = q_ref[s, :, sl]
            k = k_ref[s, :, sl]

            def per_head(t):
                return jnp.where(first_c, jnp.sum(t[:, :hd], axis=-1, keepdims=True),
                                 jnp.sum(t[:, hd:], axis=-1, keepdims=True))

            def col(arr, base):
                h0 = base + 2 * p
                return arr[:, h0:h0 + 1], arr[:, h0 + 1:h0 + 2]

            qn = q * (lax.rsqrt(per_head(q * q) + L2_EPS) * scale)
            kn = k * lax.rsqrt(per_head(k * k) + L2_EPS)
            gc0, gc1 = col(g_cum, SMALL_A)
            gs0, gs1 = col(g, SMALL_A)
            bt0, bt1 = col(beta, SMALL_B)
            ch = lambda a0, a1: jnp.where(first_c, a0, a1)
            tm_ = lambda a0, a1: jnp.where(first_t, a0, a1)
            gc, gs, bt = ch(gc0, gc1), ch(gs0, gs1), ch(bt0, bt1)
            g_prev = gc - gs
            h0 = SMALL_A + 2 * p
            gr = tm_(g_cum_t[h0:h0 + 1, :], g_cum_t[h0 + 1:h0 + 2, :])
            ge0, ge1 = g_cum[L - 1:L, h0:h0 + 1], g_cum[L - 1:L, h0 + 1:h0 + 2]
            e_end = jnp.exp(ch(ge0, ge1) - gc)
            b_vec = -(bt * jnp.exp(gs)) * kn
            k_vec = bt * kn
            units.append(dict(
                a_sc=kn, r_sc=qn, b_sc=b_vec, k_sc=k_vec,
                a_sr=kn * jnp.exp(g_prev), r_sr=qn * jnp.exp(gc),
                b_dec=b_vec * e_end, k_dec=k_vec * e_end,
                v=v_ref[s, :, sl], s_prev=[st_ref[s, 2 * p], st_ref[s, 2 * p + 1]],
                p_last=[jnp.exp(ge0), jnp.exp(ge1)],
                dm_a=jnp.exp(jnp.where(tcol < row, tm_(gc0 - gs0, gc1 - gs1) - gr, -jnp.inf)),
                dm_r=jnp.exp(jnp.where(tcol <= row, tm_(gc0, gc1) - gr, -jnp.inf))))
    ys, s_new = _dplr_pairs(units, L=L)
    npair = GDN_HEADS // 2
    for s in range(nseq):
        for p in range(npair):
            sl = slice(2 * p * hd, (2 * p + 2) * hd)
            st_ref[s, 2 * p] = s_new[s * npair + p][0]
            st_ref[s, 2 * p + 1] = s_new[s * npair + p][1]
            y = ys[s * npair + p]
            ms = jnp.where(first_c, jnp.mean(y[:, :hd] * y[:, :hd], axis=-1, keepdims=True),
                           jnp.mean(y[:, hd:] * y[:, hd:], axis=-1, keepdims=True))
            yn = y * lax.rsqrt(ms + NORM_EPS) * nw
            o_ref[s, :, sl] = (yn * _silu(z_ref[s, :, sl])).astype(o_ref.dtype)


def _gdn_branch(pc, pn, a_log, dt_bias, norm_w, bsz, seq):
    L = DPLR_CHUNK
    nseq = SEQ_PER_STEP if bsz % SEQ_PER_STEP == 0 else 1
    pc3 = pc.reshape(bsz, seq, N_PC)
    pn3 = pn.reshape(bsz, seq, N_PN)
    vec = lambda w: pl.BlockSpec((1, w), lambda b, c: (0, 0))
    tok = lambda width, col: pl.BlockSpec((nseq, L, width), lambda b, c: (b, c, col // width))
    out = pl.pallas_call(
        functools.partial(_gdn_kernel, L=L, nseq=nseq),
        grid=(bsz // nseq, seq // L),
        in_specs=[tok(GDN_INNER, PC_Q), tok(GDN_INNER, PC_K), tok(GDN_INNER, PC_V),
                  tok(GDN_INNER, PN_ZGDN), tok(LANES, PC_SMALL),
                  vec(LANES), vec(LANES), vec(2 * GDN_HEAD_DIM)],
        out_specs=pl.BlockSpec((nseq, L, GDN_INNER), lambda b, c: (b, c, 0)),
        out_shape=jax.ShapeDtypeStruct((bsz, seq, GDN_INNER), BF16),
        scratch_shapes=[pltpu.VMEM((nseq, GDN_HEADS, LANES, LANES), F32)],
        compiler_params=_cparams("parallel", "arbitrary"),
    )(pc3, pc3, pc3, pn3, pc3,
      _pad_lanes(a_log, SMALL_A), _pad_lanes(dt_bias, SMALL_A),
      jnp.tile(norm_w.reshape(1, GDN_HEAD_DIM).astype(F32), (1, 2)))
    return out.reshape(bsz * seq, GDN_INNER)


def _rwkv_kernel(rkv_ref, lora_ref, mu_ref, mul_ref, w0_ref, w2_ref, a0_ref, a2_ref, g2_ref,
                 kk_ref, ka_ref, rk_ref, lnw_ref, lnb_ref, o_ref, sh_ref, shl_ref, st_ref, *, L, nseq):
    @pl.when(pl.program_id(1) == 0)
    def _():
        st_ref[...] = jnp.zeros(st_ref.shape, F32)
        for s in range(nseq):
            sh_ref[s, 0:SUBLANES, :] = jnp.zeros((SUBLANES, sh_ref.shape[2]), F32)
            shl_ref[s, 0:SUBLANES, :] = jnp.zeros((SUBLANES, shl_ref.shape[2]), F32)

    n = RWKV_INNER
    npair = RWKV_HEADS // 2
    ones_bd = _half_ones()
    inv_hd = 1.0 / RWKV_HEAD_DIM

    def head_sums(t):
        st = jnp.concatenate([t[:, p * LANES:(p + 1) * LANES] for p in range(npair)], axis=0)
        sm = _dot(st, ones_bd)
        return jnp.concatenate([sm[p * L:(p + 1) * L] for p in range(npair)], axis=1)

    units, tails = [], []
    for s in range(nseq):
        u = rkv_ref[s]
        ul = lora_ref[s]
        sh_ref[s, SUBLANES:SUBLANES + L, :] = u
        shl_ref[s, SUBLANES:SUBLANES + L, :] = ul
        u = u + (sh_ref[s, pl.ds(SUBLANES - 1, L), :] - u) * mu_ref[...]
        ul = ul + (shl_ref[s, pl.ds(SUBLANES - 1, L), :] - ul) * mul_ref[...]
        sh_ref[s, 0:SUBLANES, :] = sh_ref[s, L:L + SUBLANES, :]
        shl_ref[s, 0:SUBLANES, :] = shl_ref[s, L:L + SUBLANES, :]

        r, k, v = u[:, 0:n], u[:, n:2 * n], u[:, 2 * n:3 * n]
        lo = ul[:, 0:LANES]
        dg = ul[:, LANES:2 * LANES]
        w_log = -_softplus(-(w0_ref[...] + _dot_hl(jnp.tanh(lo), w2_ref[...]))) - 0.5
        lw = -jnp.exp(w_log)
        a_ic = _sigmoid(a0_ref[...] + _dot_hl(lo, a2_ref[...]))
        gate = _dot(_sigmoid(dg), g2_ref[...])
        g_cum = _dot_w3(_tri_incl(L), lw)
        kk_raw = k * kk_ref[...]
        k_mod = k * (1.0 + (a_ic - 1.0) * ka_ref[...])
        kk_all = kk_raw * lax.rsqrt(head_sums(kk_raw * kk_raw) + L2_EPS)
        tails.append((head_sums(r * k_mod * rk_ref[...]) * v, gate))
        for p in range(npair):
            sl = slice(p * LANES, (p + 1) * LANES)
            kk = kk_all[:, sl]
            gc = g_cum[:, sl]
            g_end = gc[L - 1:L, :]
            e_neg = jnp.exp(-gc)
            e_end = jnp.exp(g_end - gc)
            b_vec = kk * a_ic[:, sl]
            k_vec = k_mod[:, sl]
            a_t = -kk * jnp.exp(gc - lw[:, sl])
            r_t = r[:, sl] * jnp.exp(gc)
            units.append(dict(
                a_sc=a_t, r_sc=r_t, b_sc=b_vec * e_neg, k_sc=k_vec * e_neg,
                a_sr=a_t, r_sr=r_t, b_dec=b_vec * e_end, k_dec=k_vec * e_end,
                v=v[:, sl], s_prev=[st_ref[s, p]], p_last=[jnp.exp(g_end)]))
    ys, s_new = _dplr_pairs(units, L=L)
    for s in range(nseq):
        for p in range(npair):
            st_ref[s, p] = s_new[s * npair + p][0]
        y = jnp.concatenate(ys[s * npair:(s + 1) * npair], axis=1)
        yc = y - head_sums(y) * inv_hd
        var = head_sums(yc * yc) * inv_hd
        yn = yc * lax.rsqrt(var + RWKV_GN_EPS) * lnw_ref[...] + lnb_ref[...]
        bonus, gate = tails[s]
        o_ref[s] = ((yn + bonus) * gate).astype(o_ref.dtype)


def _rwkv_branch(pc, pn, mu, w0, w2, a0, a2, g2, k_k, k_a, r_k, ln_w, ln_b, bsz, seq):
    L = DPLR_CHUNK
    n = RWKV_INNER
    nseq = SEQ_PER_STEP if bsz % SEQ_PER_STEP == 0 else 1
    pc3 = pc.reshape(bsz, seq, N_PC)
    pn3 = pn.reshape(bsz, seq, N_PN)
    vec = lambda w: pl.BlockSpec((1, w), lambda b, c: (0, 0))
    mat = lambda: pl.BlockSpec((LANES, n), lambda b, c: (0, 0))
    tok = lambda width, col: pl.BlockSpec((nseq, L, width), lambda b, c: (b, c, col // width))
    w2p = jnp.concatenate([w2, jnp.zeros((LANES - RWKV_DECAY_LORA, n), F32)], axis=0)
    a2p = jnp.concatenate([jnp.zeros((RWKV_DECAY_LORA, n), F32), a2], axis=0)
    r1 = lambda t: t.reshape(1, -1).astype(F32)
    out = pl.pallas_call(
        functools.partial(_rwkv_kernel, L=L, nseq=nseq),
        grid=(bsz // nseq, seq // L),
        in_specs=[tok(3 * n, PN_RKV), tok(256, PC_LORA),
                  vec(3 * n), vec(256), vec(n), mat(), vec(n), mat(), mat(),
                  vec(n), vec(n), vec(n), vec(n), vec(n)],
        out_specs=pl.BlockSpec((nseq, L, n), lambda b, c: (b, c, 0)),
        out_shape=jax.ShapeDtypeStruct((bsz, seq, n), BF16),
        scratch_shapes=[pltpu.VMEM((nseq, L + SUBLANES, 3 * n), F32),
                        pltpu.VMEM((nseq, L + SUBLANES, 256), F32),
                        pltpu.VMEM((nseq, RWKV_HEADS // 2, LANES, LANES), F32)],
        compiler_params=_cparams("parallel", "arbitrary"),
    )(pn3, pc3, r1(mu[:3 * n]), r1(mu[3 * n:]), r1(w0), w2p, r1(a0), a2p, g2.astype(F32),
      r1(k_k), r1(k_a), r1(r_k), r1(ln_w), r1(ln_b))
    return out.reshape(bsz * seq, n)


def _merge_kernel(x_ref, ys_ref, yg_ref, yr_ref, gate_ref, ps_ref, pg_ref, pr_ref, wo_ref, o_ref):
    d = D_MODEL
    m = _sigmoid(gate_ref[:, 0:d]) * jnp.dot(ys_ref[...], ps_ref[...], preferred_element_type=F32)
    m = m + _sigmoid(gate_ref[:, d:2 * d]) * jnp.dot(yg_ref[...], pg_ref[...], preferred_element_type=F32)
    m = m + _sigmoid(gate_ref[:, 2 * d:3 * d]) * jnp.dot(yr_ref[...], pr_ref[...], preferred_element_type=F32)
    o_ref[...] = x_ref[...] + jnp.dot(m.astype(BF16), wo_ref[...], preferred_element_type=F32)


def _merge(x, y_ssm, y_gdn, y_rwkv, pn, p_ssm, p_gdn, p_rwkv, w_out, tm=512):
    t, d = x.shape
    tok = lambda: pl.BlockSpec((tm, d), lambda i: (i, 0))
    wgt = lambda: pl.BlockSpec((d, d), lambda i: (0, 0))
    return pl.pallas_call(
        _merge_kernel,
        grid=(t // tm,),
        in_specs=[tok(), tok(), tok(), tok(),
                  pl.BlockSpec((tm, 3 * d), lambda i: (i, PN_GATE // (3 * d))),
                  wgt(), wgt(), wgt(), wgt()],
        out_specs=tok(),
        out_shape=jax.ShapeDtypeStruct((t, d), F32),
        compiler_params=_cparams("parallel"),
    )(x, y_ssm, y_gdn, y_rwkv, pn, p_ssm, p_gdn, p_rwkv, w_out)


def _ffn_kernel(x_ref, nw_ref, wg_ref, wu_ref, wd_ref, o_ref, h_ref, acc_ref):
    f = pl.program_id(1)

    @pl.when(f == 0)
    def _():
        x = x_ref[...]
        ms = jnp.mean(x * x, axis=-1, keepdims=True)
        h_ref[...] = (x * lax.rsqrt(ms + NORM_EPS) * nw_ref[...]).astype(BF16)
        acc_ref[...] = x

    h = h_ref[...]
    act = _silu(jnp.dot(h, wg_ref[...], preferred_element_type=F32)) * jnp.dot(
        h, wu_ref[...], preferred_element_type=F32)
    acc_ref[...] += jnp.dot(act.astype(BF16), wd_ref[...], preferred_element_type=F32)

    @pl.when(f == pl.num_programs(1) - 1)
    def _():
        o_ref[...] = acc_ref[...]


def _ffn_dense(x, norm_w, w_gate, w_up, w_down, tm=512, tf=1408):
    t, d = x.shape
    nf = w_gate.shape[1] // tf
    return pl.pallas_call(
        _ffn_kernel,
        grid=(t // tm, nf),
        in_specs=[pl.BlockSpec((tm, d), lambda i, f: (i, 0)),
                  pl.BlockSpec((1, d), lambda i, f: (0, 0)),
                  pl.BlockSpec((d, tf), lambda i, f: (0, f)),
                  pl.BlockSpec((d, tf), lambda i, f: (0, f)),
                  pl.BlockSpec((tf, d), lambda i, f: (f, 0))],
        out_specs=pl.BlockSpec((tm, d), lambda i, f: (i, 0)),
        out_shape=jax.ShapeDtypeStruct((t, d), F32),
        scratch_shapes=[pltpu.VMEM((tm, d), BF16), pltpu.VMEM((tm, d), F32)],
        compiler_params=_cparams("parallel", "arbitrary"),
    )(x, norm_w.reshape(1, d).astype(F32), w_gate, w_up, w_down)


MOE_TM = 1024
MOE_BLK = 128
MOE_FIRST_ROWS = 288


def _route_kernel(x_ref, nw_ref, rt_ref, h_ref, cmb_ref, cnt_ref):
    x = x_ref[...]
    lane = lax.broadcasted_iota(jnp.int32, cmb_ref.shape, 1)
    ms = jnp.mean(x * x, axis=-1, keepdims=True)
    hf = x * lax.rsqrt(ms + NORM_EPS) * nw_ref[...]
    h_ref[...] = hf.astype(BF16)
    logits = jnp.where(lane < N_EXPERTS, _dot_hl(hf, rt_ref[...]), -jnp.inf)
    m1 = jnp.max(logits, axis=-1, keepdims=True)
    i1 = jnp.min(jnp.where(logits == m1, lane, LANES), axis=-1, keepdims=True)
    rest = jnp.where(lane == i1, -jnp.inf, logits)
    m2 = jnp.max(rest, axis=-1, keepdims=True)
    i2 = jnp.min(jnp.where(rest == m2, lane, LANES), axis=-1, keepdims=True)
    e2 = jnp.exp(m2 - m1)
    p1 = 1.0 / (1.0 + e2)
    cmb = jnp.where(lane == i1, p1, 0.0) + jnp.where(lane == i2, e2 * p1, 0.0)
    cmb_ref[...] = cmb
    cnt_ref[0] = jnp.sum(jnp.where(cmb > 0.0, 1.0, 0.0), axis=0, keepdims=True).astype(jnp.int32)


def _moe_route(x, norm_w, router, tm):
    t, d = x.shape
    ne = router.shape[1]
    rt = jnp.concatenate([router.astype(F32), jnp.zeros((d, LANES - ne), F32)], axis=1)
    return pl.pallas_call(
        _route_kernel,
        grid=(t // tm,),
        in_specs=[pl.BlockSpec((tm, d), lambda i: (i, 0)),
                  pl.BlockSpec((1, d), lambda i: (0, 0)),
                  pl.BlockSpec((d, LANES), lambda i: (0, 0))],
        out_specs=[pl.BlockSpec((tm, d), lambda i: (i, 0)),
                   pl.BlockSpec((tm, LANES), lambda i: (i, 0)),
                   pl.BlockSpec((1, 1, LANES), lambda i: (i, 0, 0))],
        out_shape=[jax.ShapeDtypeStruct((t, d), BF16),
                   jax.ShapeDtypeStruct((t, LANES), F32),
                   jax.ShapeDtypeStruct((t // tm, 1, LANES), jnp.int32)],
        compiler_params=_cparams("parallel"),
    )(x, norm_w.reshape(1, d).astype(F32), rt)


def _moe_kernel(cnt_ref, x_ref, h_ref, cmb_ref, wg_ref, wu_ref, wd_ref, fw_ref, o_ref,
                rank_ref, xc_ref, yc_ref, oh_ref, *, tm, final_norm):
    i = pl.program_id(0)
    e = pl.program_id(1)
    f = pl.program_id(2)
    last_f = f == pl.num_programs(2) - 1
    lane = lax.broadcasted_iota(jnp.int32, (tm, LANES), 1)

    @pl.when((e == 0) & (f == 0))
    def _():
        o_ref[...] = jnp.zeros(o_ref.shape, F32)
        r = lax.broadcasted_iota(jnp.int32, (tm, tm), 0)
        c = lax.broadcasted_iota(jnp.int32, (tm, tm), 1)
        before = jnp.where(c < r, 1.0, 0.0).astype(BF16)
        sel = jnp.where(cmb_ref[...] > 0.0, 1.0, 0.0).astype(BF16)
        rank_ref[...] = jnp.dot(before, sel, preferred_element_type=F32)

    cnt = cnt_ref[i, e]

    def weight():
        return jnp.sum(jnp.where(lane == e, cmb_ref[...], 0.0), axis=-1, keepdims=True)

    def one_hot(off, rows):
        r_e = jnp.sum(jnp.where(lane == e, rank_ref[...], 0.0), axis=-1, keepdims=True)
        key = jnp.where(weight() > 0.0, r_e, -1.0) - jnp.asarray(off, F32)
        slot = lax.broadcasted_iota(jnp.int32, (tm, rows), 1).astype(F32)
        return jnp.where(key == slot, 1.0, 0.0).astype(BF16)

    def process(off, rows, cache):
        rs = pl.ds(off, rows)

        @pl.when(f == 0)
        def _():
            onehot = one_hot(off, rows)
            if cache:
                oh_ref[...] = onehot
            xc_ref[rs, :] = _dot_tn(onehot, h_ref[...]).astype(BF16)
            yc_ref[rs, :] = jnp.zeros((rows, yc_ref.shape[1]), F32)

        xc = xc_ref[rs, :]
        act = _silu(jnp.dot(xc, wg_ref[0], preferred_element_type=F32)) * jnp.dot(
            xc, wu_ref[0], preferred_element_type=F32)
        yc_ref[rs, :] += jnp.dot(act.astype(BF16), wd_ref[0], preferred_element_type=F32)

        @pl.when(last_f)
        def _():
            onehot = oh_ref[...] if cache else one_hot(off, rows)
            o_ref[...] += weight() * jnp.dot(onehot, yc_ref[rs, :].astype(BF16), preferred_element_type=F32)

    @pl.when(cnt > 0)
    def _():
        process(0, MOE_FIRST_ROWS, True)

    def extra(b, carry):
        process(pl.multiple_of(MOE_FIRST_ROWS + b * MOE_BLK, 32), MOE_BLK, False)
        return carry

    lax.fori_loop(0, (jnp.maximum(cnt - MOE_FIRST_ROWS, 0) + MOE_BLK - 1) // MOE_BLK, extra, 0)

    @pl.when((e == pl.num_programs(1) - 1) & last_f)
    def _():
        y = x_ref[...] + o_ref[...]
        if final_norm:
            y = y * lax.rsqrt(jnp.mean(y * y, axis=-1, keepdims=True) + NORM_EPS) * fw_ref[...]
        o_ref[...] = y


def _ffn_moe(x, norm_w, router, w_gate, w_up, w_down, final_w, tm=MOE_TM, tf=1408):
    t, d = x.shape
    ne, _, ff = w_gate.shape
    nf = ff // tf
    h, cmb, cnt = _moe_route(x, norm_w, router, tm)
    final_norm = final_w is not None
    fw = (final_w if final_norm else jnp.ones((d,), F32)).reshape(1, d).astype(F32)
    cap = MOE_FIRST_ROWS + -(-max(tm - MOE_FIRST_ROWS, 0) // MOE_BLK) * MOE_BLK
    grid_spec = pltpu.PrefetchScalarGridSpec(
        num_scalar_prefetch=1,
        grid=(t // tm, ne, nf),
        in_specs=[pl.BlockSpec((tm, d), lambda i, e, f, c: (i, 0)),
                  pl.BlockSpec((tm, d), lambda i, e, f, c: (i, 0)),
                  pl.BlockSpec((tm, LANES), lambda i, e, f, c: (i, 0)),
                  pl.BlockSpec((1, d, tf), lambda i, e, f, c: (e, 0, f)),
                  pl.BlockSpec((1, d, tf), lambda i, e, f, c: (e, 0, f)),
                  pl.BlockSpec((1, tf, d), lambda i, e, f, c: (e, f, 0)),
                  pl.BlockSpec((1, d), lambda i, e, f, c: (0, 0))],
        out_specs=pl.BlockSpec((tm, d), lambda i, e, f, c: (i, 0)),
        scratch_shapes=[pltpu.VMEM((tm, LANES), F32), pltpu.VMEM((cap, d), BF16), pltpu.VMEM((cap, d), F32),
                        pltpu.VMEM((tm, MOE_FIRST_ROWS), BF16)])
    return pl.pallas_call(
        functools.partial(_moe_kernel, tm=tm, final_norm=final_norm),
        grid_spec=grid_spec,
        out_shape=jax.ShapeDtypeStruct((t, d), F32),
        compiler_params=_cparams("parallel", "arbitrary", "arbitrary"),
    )(cnt.reshape(t // tm, LANES), x, h, cmb, w_gate, w_up, w_down, fw)


def _reorder_kernel(w_ref, wc_ref, wn_ref):
    w = w_ref[...]
    wc = [w[:, 0:1024], w[:, 1536:4608], w[:, 1024:1536], w[:, 9760:10016], w[:, 5632:5648],
          w[:, 6672:6688], jnp.zeros((w.shape[0], N_PC - PC_SMALL - 32), w.dtype)]
    wn = [w[:, 6688:9760], w[:, 10016:13088], w[:, 4608:5632], w[:, 5648:6672]]
    wc_ref[...] = jnp.concatenate(wc, axis=1).astype(BF16)
    wn_ref[...] = jnp.concatenate(wn, axis=1).astype(BF16)


def _reorder_w_in(w, tk=128):
    d, n = w.shape
    return pl.pallas_call(
        _reorder_kernel,
        grid=(d // tk,),
        in_specs=[pl.BlockSpec((tk, n), lambda i: (i, 0))],
        out_specs=[pl.BlockSpec((tk, N_PC), lambda i: (i, 0)), pl.BlockSpec((tk, N_PN), lambda i: (i, 0))],
        out_shape=[jax.ShapeDtypeStruct((d, N_PC), BF16), jax.ShapeDtypeStruct((d, N_PN), BF16)],
        compiler_params=_cparams("parallel"),
    )(w)


def _token_mixer_layer(x, i, bsz, seq, p):
    h = _rmsnorm(x, p["attn_norm_w"][i], BF16)
    wc, wn = _reorder_w_in(p["w_in"][i])
    pc, pn = _inproj(h, wc, wn, p["conv_w"][i], p["conv_b"][i], seq, tm=1024)
    y_ssm = _ssd_branch(pc, pn, p["ssm_a_log"][i], p["ssm_dt_bias"][i], p["ssm_d"][i],
                        p["ssm_norm_w"][i], bsz, seq)
    y_gdn = _gdn_branch(pc, pn, p["gdn_a_log"][i], p["gdn_dt_bias"][i], p["gdn_norm_w"][i],
                        bsz, seq)
    y_rwkv = _rwkv_branch(pc, pn, p["rwkv_mu"][i], p["rwkv_w0"][i], p["rwkv_w2"][i], p["rwkv_a0"][i],
                          p["rwkv_a2"][i], p["rwkv_g2"][i], p["rwkv_k_k"][i], p["rwkv_k_a"][i],
                          p["rwkv_r_k"][i], p["rwkv_ln_w"][i], p["rwkv_ln_b"][i], bsz, seq)
    return _merge(x, y_ssm, y_gdn, y_rwkv, pn, p["proj_ssm"][i], p["proj_gdn"][i], p["proj_rwkv"][i],
                  p["w_out"][i])


def kernel(x, attn_norm_w, w_in, conv_w, conv_b, ssm_a_log, ssm_dt_bias, ssm_d, ssm_norm_w, gdn_a_log, gdn_dt_bias, gdn_norm_w, rwkv_mu, rwkv_w0, rwkv_w2, rwkv_a0, rwkv_a2, rwkv_g2, rwkv_k_k, rwkv_k_a, rwkv_r_k, rwkv_ln_w, rwkv_ln_b, proj_ssm, proj_gdn, proj_rwkv, w_out, ffn_norm_w, dense_w_gate, dense_w_up, dense_w_down, moe_router, moe_w_gate, moe_w_up, moe_w_down, final_norm_w):
    p = dict(attn_norm_w=attn_norm_w, w_in=w_in, conv_w=conv_w, conv_b=conv_b, ssm_a_log=ssm_a_log,
             ssm_dt_bias=ssm_dt_bias, ssm_d=ssm_d, ssm_norm_w=ssm_norm_w, gdn_a_log=gdn_a_log,
             gdn_dt_bias=gdn_dt_bias, gdn_norm_w=gdn_norm_w, rwkv_mu=rwkv_mu, rwkv_w0=rwkv_w0,
             rwkv_w2=rwkv_w2, rwkv_a0=rwkv_a0, rwkv_a2=rwkv_a2, rwkv_g2=rwkv_g2, rwkv_k_k=rwkv_k_k,
             rwkv_k_a=rwkv_k_a, rwkv_r_k=rwkv_r_k, rwkv_ln_w=rwkv_ln_w, rwkv_ln_b=rwkv_ln_b,
             proj_ssm=proj_ssm, proj_gdn=proj_gdn, proj_rwkv=proj_rwkv, w_out=w_out)
    for name in ("proj_ssm", "proj_gdn", "proj_rwkv", "w_out"):
        p[name] = p[name].astype(BF16)
    dense_w_gate, dense_w_up, dense_w_down = (t.astype(BF16) for t in (dense_w_gate, dense_w_up, dense_w_down))
    moe_w_gate, moe_w_up, moe_w_down = (t.astype(BF16) for t in (moe_w_gate, moe_w_up, moe_w_down))
    bsz, seq, d = x.shape
    depth = attn_norm_w.shape[0]
    xt = x.reshape(bsz * seq, d)
    for i in range(depth):
        xt = _token_mixer_layer(xt, i, bsz, seq, p)
        j = i // 2
        last = i == depth - 1
        if i % 2 == 0:
            xt = _ffn_dense(xt, ffn_norm_w[i], dense_w_gate[j], dense_w_up[j], dense_w_down[j])
            if last:
                xt = _rmsnorm(xt, final_norm_w, F32)
        else:
            xt = _ffn_moe(xt, ffn_norm_w[i], moe_router[j], moe_w_gate[j], moe_w_up[j], moe_w_down[j],
                          final_norm_w if last else None)
    return xt.reshape(bsz, seq, d)
```

```python
import functools
import math

import numpy as np
import jax
import jax.numpy as jnp
from jax import lax
from jax.experimental import pallas as pl
from jax.experimental.pallas import tpu as pltpu

F32 = jnp.float32
BF16 = jnp.bfloat16

LANES = 128
SUBLANES = 8
VMEM_LIMIT_BYTES = 56 * 1024 * 1024

D_MODEL = 1024
SSM_HEADS, SSM_HEAD_DIM, SSM_GROUPS, SSM_STATE = 16, 64, 2, 128
SSM_INNER = SSM_HEADS * SSM_HEAD_DIM
GDN_HEADS, GDN_HEAD_DIM = 8, 128
GDN_INNER = GDN_HEADS * GDN_HEAD_DIM
RWKV_HEADS, RWKV_HEAD_DIM = 16, 64
RWKV_INNER = RWKV_HEADS * RWKV_HEAD_DIM
RWKV_DECAY_LORA, RWKV_ICLR_LORA, RWKV_GATE_LORA = 64, 64, 128
RWKV_GN_EPS = 64e-5
CONV_K = 4
FFN_DIM = 2816
N_EXPERTS = 8
NORM_EPS = 1e-6
L2_EPS = 1e-6

PC_XS, PC_Q, PC_K, PC_V = 0, 1024, 2048, 3072
PC_BS, PC_CS = 4096, 4352
PC_CONV_END = 4608
PC_LORA = 4608
PC_SMALL = 4864
N_PC = 5120
PN_RKV = 0
PN_GATE = 3072
PN_ZSSM, PN_ZGDN = 6144, 7168
N_PN = 8192
SMALL_DT, SMALL_A, SMALL_B = 0, 16, 24

SSD_CHUNK = 128
DPLR_CHUNK = 64
SEQ_PER_STEP = 4
INPROJ_ROW_CHUNK = 512


def _cparams(*sem):
    return pltpu.CompilerParams(dimension_semantics=sem, vmem_limit_bytes=VMEM_LIMIT_BYTES)


def _dot(a, b):
    return jnp.dot(a.astype(BF16), b.astype(BF16), preferred_element_type=F32)


def _dot_nt(a, b):
    return lax.dot_general(a.astype(BF16), b.astype(BF16), (((1,), (1,)), ((), ())),
                           preferred_element_type=F32)


def _dot_tn(a, b):
    return lax.dot_general(a.astype(BF16), b.astype(BF16), (((0,), (0,)), ((), ())),
                           preferred_element_type=F32)


def _split3(x):
    hi = x.astype(BF16)
    r1 = x - hi.astype(F32)
    mid = r1.astype(BF16)
    lo = (r1 - mid.astype(F32)).astype(BF16)
    return hi, mid, lo


def _dot_x3(x, w):
    hi, mid, lo = _split3(x)
    return _dot(hi, w) + _dot(mid, w) + _dot(lo, w)


def _dot_w3(w, x):
    hi, mid, lo = _split3(x)
    return _dot(w, hi) + _dot(w, mid) + _dot(w, lo)


def _dot_hl(a, b):
    ah = a.astype(BF16)
    al = (a - ah.astype(F32)).astype(BF16)
    bh = b.astype(BF16)
    bl = (b - bh.astype(F32)).astype(BF16)
    return _dot(ah, bh) + _dot(ah, bl) + _dot(al, bh)


def _sigmoid(x):
    return 0.5 * jnp.tanh(0.5 * x) + 0.5


def _silu(x):
    return x * _sigmoid(x)


def _softplus(x):
    return jnp.maximum(x, 0.0) + jnp.log(1.0 + jnp.exp(-jnp.abs(x)))


def _tri_incl(n):
    r = lax.broadcasted_iota(jnp.int32, (n, n), 0)
    c = lax.broadcasted_iota(jnp.int32, (n, n), 1)
    return jnp.where(r >= c, 1.0, 0.0).astype(BF16)


def _half_ones():
    r = lax.broadcasted_iota(jnp.int32, (LANES, LANES), 0)
    c = lax.broadcasted_iota(jnp.int32, (LANES, LANES), 1)
    return jnp.where((r // 64) == (c // 64), 1.0, 0.0).astype(BF16)


def _rmsnorm_kernel(x_ref, w_ref, o_ref):
    x = x_ref[...]
    ms = jnp.mean(x * x, axis=-1, keepdims=True)
    o_ref[...] = (x * lax.rsqrt(ms + NORM_EPS) * w_ref[...]).astype(o_ref.dtype)


def _rmsnorm(x, w, out_dtype, tm=512):
    t, d = x.shape
    return pl.pallas_call(
        _rmsnorm_kernel,
        grid=(t // tm,),
        in_specs=[pl.BlockSpec((tm, d), lambda i: (i, 0)),
                  pl.BlockSpec((1, d), lambda i: (0, 0))],
        out_specs=pl.BlockSpec((tm, d), lambda i: (i, 0)),
        out_shape=jax.ShapeDtypeStruct((t, d), out_dtype),
        compiler_params=_cparams("parallel"),
    )(x, w.reshape(1, d))


def _inproj_kernel(a_ref, wc_ref, wn_ref, cw_ref, cb_ref, cm_ref, oc_ref, on_ref, sh_ref, *,
                   tm, tiles_per_seq):
    i = pl.program_id(1)
    tc = oc_ref.shape[1]

    @pl.when(i % tiles_per_seq == 0)
    def _():
        sh_ref[0:SUBLANES, :] = jnp.zeros((SUBLANES, tc), F32)

    rc = min(tm, INPROJ_ROW_CHUNK)
    starts = list(range(0, tm, rc))
    acc = jnp.dot(a_ref[0:rc, :], wc_ref[...], preferred_element_type=F32)
    for n, r0 in enumerate(starts):
        sh_ref[SUBLANES + r0:SUBLANES + r0 + rc, :] = acc
        on_ref[r0:r0 + rc, :] = jnp.dot(a_ref[r0:r0 + rc, :], wn_ref[...], preferred_element_type=F32)
        if n + 1 < len(starts):
            acc_next = jnp.dot(a_ref[r0 + rc:r0 + 2 * rc, :], wc_ref[...], preferred_element_type=F32)
        c = acc * cw_ref[CONV_K - 1:CONV_K, :] + cb_ref[...]
        for k in range(1, CONV_K):
            c = c + sh_ref[pl.ds(SUBLANES + r0 - k, rc), :] * cw_ref[CONV_K - 1 - k:CONV_K - k, :]
        oc_ref[r0:r0 + rc, :] = jnp.where(cm_ref[...] > 0.0, _silu(c), acc)
        if n + 1 < len(starts):
            acc = acc_next
    sh_ref[0:SUBLANES, :] = sh_ref[tm:tm + SUBLANES, :]


def _conv_cols(c):
    r = c.shape[0]
    return jnp.concatenate([c[:, 0:1024], c[:, 1536:4608], c[:, 1024:1536],
                            jnp.zeros((r, N_PC - PC_CONV_END), F32)], axis=1).astype(F32)


def _inproj(a, wc, wn, conv_w, conv_b, seq, tm, nsteps=8):
    m, k = a.shape
    tc, tn = N_PC // nsteps, N_PN // nsteps
    cw = _conv_cols(conv_w)
    cb = _conv_cols(conv_b.reshape(1, -1))
    cm = _conv_cols(jnp.ones((1, conv_w.shape[1]), F32))
    return pl.pallas_call(
        functools.partial(_inproj_kernel, tm=tm, tiles_per_seq=seq // tm),
        grid=(nsteps, m // tm),
        in_specs=[pl.BlockSpec((tm, k), lambda j, i: (i, 0)),
                  pl.BlockSpec((k, tc), lambda j, i: (0, j)),
                  pl.BlockSpec((k, tn), lambda j, i: (0, j)),
                  pl.BlockSpec((CONV_K, tc), lambda j, i: (0, j)),
                  pl.BlockSpec((1, tc), lambda j, i: (0, j)),
                  pl.BlockSpec((1, tc), lambda j, i: (0, j))],
        out_specs=[pl.BlockSpec((tm, tc), lambda j, i: (i, j)),
                   pl.BlockSpec((tm, tn), lambda j, i: (i, j))],
        out_shape=[jax.ShapeDtypeStruct((m, N_PC), F32), jax.ShapeDtypeStruct((m, N_PN), F32)],
        scratch_shapes=[pltpu.VMEM((tm + SUBLANES, tc), F32)],
        compiler_params=_cparams("parallel", "arbitrary"),
    )(a, wc, wn, cw, cb, cm)


def _ssd_kernel(xs_ref, bs_ref, cs_ref, z_ref, sm_ref, alog_ref, dtb_ref, dsk_ref, nw_ref, ex_ref,
                o_ref, st_ref, *, L, nseq):
    @pl.when(pl.program_id(1) == 0)
    def _():
        st_ref[...] = jnp.zeros(st_ref.shape, F32)

    hd = SSM_HEAD_DIM
    gw = SSM_INNER // SSM_GROUPS
    heads_per_group = SSM_HEADS // SSM_GROUPS
    ex = ex_ref[...]
    row = lax.broadcasted_iota(jnp.int32, (L, L), 0)
    col = lax.broadcasted_iota(jnp.int32, (L, L), 1)
    causal = row >= col
    lane = lax.broadcasted_iota(jnp.int32, (L, LANES), 1)
    first_half = lane < hd
    for s in range(nseq):
        dt = _softplus(sm_ref[s] + dtb_ref[...])
        la = -jnp.exp(alog_ref[...]) * dt
        g_cum = _dot_w3(_tri_incl(L), la)
        g_cum_t = g_cum.T
        dt_e = _dot_x3(dt, ex)
        g_e = _dot_x3(g_cum, ex)
        g_last = g_e[L - 1:L, :]
        e_g = jnp.exp(g_e)
        e_dec = jnp.exp(g_last - g_e)
        e_last = jnp.exp(g_last)

        xs = xs_ref[s]
        xdt = xs * dt_e
        xdec = xdt * e_dec
        y_parts = []
        for g in range(SSM_GROUPS):
            gsl = slice(g * gw, (g + 1) * gw)
            b_g = bs_ref[s, :, g * SSM_STATE:(g + 1) * SSM_STATE]
            c_g = cs_ref[s, :, g * SSM_STATE:(g + 1) * SSM_STATE]
            cb = _dot_nt(c_g, b_g)
            st_g = st_ref[s, :, gsl]
            y_off = _dot(c_g, st_g) * e_g[:, gsl]
            st_ref[s, :, gsl] = st_g * e_last[:, gsl] + _dot_tn(b_g, xdec[:, gsl])
            for p in range(heads_per_group // 2):
                h0 = g * heads_per_group + 2 * p
                base = h0 * hd
                sc = []
                for h in (h0, h0 + 1):
                    diff = g_cum[:, h:h + 1] - g_cum_t[h:h + 1, :]
                    sc.append(cb * jnp.exp(jnp.where(causal, diff, -jnp.inf)))
                yd = _dot(jnp.concatenate(sc, axis=0), xdt[:, base:base + LANES])
                y_parts.append(jnp.where(first_half, yd[:L], yd[L:])
                               + y_off[:, base - g * gw:base - g * gw + LANES])
        y = jnp.concatenate(y_parts, axis=1) + dsk_ref[...] * xs
        y = y * _silu(z_ref[s])
        outs = []
        for g in range(SSM_GROUPS):
            seg = y[:, g * gw:(g + 1) * gw]
            outs.append(seg * lax.rsqrt(jnp.mean(seg * seg, axis=-1, keepdims=True) + NORM_EPS))
        o_ref[s] = (jnp.concatenate(outs, axis=1) * nw_ref[...]).astype(o_ref.dtype)


def _pad_lanes(v, offset=0, width=LANES):
    out = jnp.zeros((1, width), F32)
    return lax.dynamic_update_slice(out, v.reshape(1, -1).astype(F32), (0, offset))


def _ssd_branch(pc, pn, a_log, dt_bias, d_skip, norm_w, bsz, seq):
    L = SSD_CHUNK
    nseq = SEQ_PER_STEP if bsz % SEQ_PER_STEP == 0 else 1
    pc3 = pc.reshape(bsz, seq, N_PC)
    pn3 = pn.reshape(bsz, seq, N_PN)
    ex = np.zeros((LANES, SSM_INNER), np.float32)
    for h in range(SSM_HEADS):
        ex[h, h * SSM_HEAD_DIM:(h + 1) * SSM_HEAD_DIM] = 1.0
    vec = lambda w: pl.BlockSpec((1, w), lambda b, c: (0, 0))
    tok = lambda width, col: pl.BlockSpec((nseq, L, width), lambda b, c: (b, c, col // width))
    out = pl.pallas_call(
        functools.partial(_ssd_kernel, L=L, nseq=nseq),
        grid=(bsz // nseq, seq // L),
        in_specs=[tok(SSM_INNER, PC_XS), tok(256, PC_BS), tok(256, PC_CS), tok(SSM_INNER, PN_ZSSM),
                  tok(LANES, PC_SMALL),
                  vec(LANES), vec(LANES), vec(SSM_INNER), vec(SSM_INNER),
                  pl.BlockSpec((LANES, SSM_INNER), lambda b, c: (0, 0))],
        out_specs=pl.BlockSpec((nseq, L, SSM_INNER), lambda b, c: (b, c, 0)),
        out_shape=jax.ShapeDtypeStruct((bsz, seq, SSM_INNER), BF16),
        scratch_shapes=[pltpu.VMEM((nseq, SSM_STATE, SSM_INNER), F32)],
        compiler_params=_cparams("parallel", "arbitrary"),
    )(pc3, pc3, pc3, pn3, pc3,
      _pad_lanes(a_log, SMALL_DT), _pad_lanes(dt_bias, SMALL_DT),
      jnp.repeat(d_skip.astype(F32), SSM_HEAD_DIM).reshape(1, SSM_INNER),
      norm_w.reshape(1, SSM_INNER).astype(F32), jnp.asarray(ex, BF16))
    return out.reshape(bsz * seq, SSM_INNER)


def _head_stack(x):
    half = x.shape[1] // 2
    lane = lax.broadcasted_iota(jnp.int32, x.shape, 1)
    return jnp.concatenate([jnp.where(lane < half, x, 0.0), jnp.where(lane >= half, x, 0.0)], axis=0)


def _dplr_pairs(units, *, L):
    nu = len(units)
    P = 2 * L
    vc = units[0]["v"].shape[1]
    row = lax.broadcasted_iota(jnp.int32, (L, P), 0)
    tcol = lax.broadcasted_iota(jnp.int32, (L, P), 1)
    first = tcol < L
    tcol = jnp.where(first, tcol, tcol - L)
    strict, lower = tcol < row, tcol <= row

    sc = [_dot_nt(jnp.concatenate([u["a_sc"], u["r_sc"]], axis=0),
                  jnp.concatenate([_head_stack(u["b_sc"]), _head_stack(u["k_sc"])], axis=0))
          for u in units]
    v_st = [_head_stack(u["v"]) for u in units]

    def read(u):
        rows = jnp.concatenate([u["a_sr"], u["r_sr"]], axis=0)
        st = u["s_prev"]
        if len(st) == 1:
            return _dot_nt(rows, st[0])
        kh = rows.shape[1] // 2
        return jnp.concatenate([_dot_nt(rows[:, :kh], st[0]), _dot_nt(rows[:, kh:], st[1])], axis=1)

    reads = [read(u) for u in units]
    n_mat, a_rb, aks = [], [], []
    for i, u in enumerate(units):
        sb, sk = sc[i][:, :P], sc[i][:, P:]
        if u.get("dm_a") is None:
            n_mat.append(jnp.where(strict, sb[:L], 0.0))
            a_rb.append(jnp.where(lower, sb[L:], 0.0))
            aks.append(jnp.concatenate([jnp.where(strict, sk[:L], 0.0), jnp.where(lower, sk[L:], 0.0)], axis=0))
        else:
            n_mat.append(sb[:L] * u["dm_a"])
            a_rb.append(sb[L:] * u["dm_r"])
            aks.append(jnp.concatenate([sk[:L] * u["dm_a"], sk[L:] * u["dm_r"]], axis=0))
    av = [_dot(aks[i], v_st[i]) for i in range(nu)]
    x = [reads[i][:L] + av[i][:L] for i in range(nu)]
    y_part = [reads[i][L:] + av[i][L:] for i in range(nu)]

    steps = int(math.log2(L))
    for s in range(steps):
        x_st = [_head_stack(x[i]) for i in range(nu)]
        if s + 1 < steps:
            t = [_dot(n_mat[i], jnp.concatenate(
                    [x_st[i], jnp.concatenate([jnp.where(first, n_mat[i], 0.0),
                                               jnp.where(first, 0.0, n_mat[i])], axis=0)], axis=1))
                 for i in range(nu)]
            x = [x[i] + t[i][:, :vc] for i in range(nu)]
            n_mat = [t[i][:, vc:] for i in range(nu)]
        else:
            x = [x[i] + _dot(n_mat[i], x_st[i]) for i in range(nu)]
    ys = [y_part[i] + _dot(a_rb[i], _head_stack(x[i])) for i in range(nu)]

    s_new = []
    for i, u in enumerate(units):
        xv = jnp.concatenate([x[i], u["v"]], axis=0)
        bk = jnp.concatenate([u["b_dec"], u["k_dec"]], axis=0)
        st, pl_ = u["s_prev"], u["p_last"]
        if len(st) == 1:
            upd = _dot_tn(xv, bk)
            ri = lax.broadcasted_iota(jnp.int32, upd.shape, 0) // (upd.shape[0] // 2)
            ci = lax.broadcasted_iota(jnp.int32, upd.shape, 1) // (upd.shape[1] // 2)
            s_new.append([st[0] * pl_[0] + jnp.where(ri == ci, upd, 0.0)])
        else:
            vh, kh = xv.shape[1] // 2, bk.shape[1] // 2
            s_new.append([st[0] * pl_[0] + _dot_tn(xv[:, :vh], bk[:, :kh]),
                          st[1] * pl_[1] + _dot_tn(xv[:, vh:], bk[:, kh:])])
    return ys, s_new


def _gdn_kernel(q_ref, k_ref, v_ref, z_ref, sm_ref, alog_ref, dtb_ref, nw_ref, o_ref, st_ref, *, L, nseq):
    @pl.when(pl.program_id(1) == 0)
    def _():
        st_ref[...] = jnp.zeros(st_ref.shape, F32)

    hd = GDN_HEAD_DIM
    row = lax.broadcasted_iota(jnp.int32, (L, 2 * L), 0)
    tcol = lax.broadcasted_iota(jnp.int32, (L, 2 * L), 1)
    first_t = tcol < L
    tcol = jnp.where(first_t, tcol, tcol - L)
    first_c = lax.broadcasted_iota(jnp.int32, (L, 2 * hd), 1) < hd
    nw = nw_ref[...]
    scale = hd ** -0.5
    units = []
    for s in range(nseq):
        sm = sm_ref[s]
        g = -jnp.exp(alog_ref[...]) * _softplus(sm + dtb_ref[...])
        beta = _sigmoid(sm)
        g_cum = _dot_w3(_tri_incl(L), g)
        g_cum_t = jnp.concatenate([g_cum] * (LANES // L), axis=0).T
        for p in range(GDN_HEADS // 2):
            sl = slice(2 * p * hd, (2 * p + 2) * hd)
            q = q_ref[s, :, sl]
            k = k_ref[s, :, sl]

            def per_head(t):
                return jnp.where(first_c, jnp.sum(t[:, :hd], axis=-1, keepdims=True),
                                 jnp.sum(t[:, hd:], axis=-1, keepdims=True))

            def col(arr, base):
                h0 = base + 2 * p
                return arr[:, h0:h0 + 1], arr[:, h0 + 1:h0 + 2]

            qn = q * (lax.rsqrt(per_head(q * q) + L2_EPS) * scale)
            kn = k * lax.rsqrt(per_head(k * k) + L2_EPS)
            gc0, gc1 = col(g_cum, SMALL_A)
            gs0, gs1 = col(g, SMALL_A)
            bt0, bt1 = col(beta, SMALL_B)
            ch = lambda a0, a1: jnp.where(first_c, a0, a1)
            tm_ = lambda a0, a1: jnp.where(first_t, a0, a1)
            gc, gs, bt = ch(gc0, gc1), ch(gs0, gs1), ch(bt0, bt1)
            g_prev = gc - gs
            h0 = SMALL_A + 2 * p
            gr = tm_(g_cum_t[h0:h0 + 1, :], g_cum_t[h0 + 1:h0 + 2, :])
            ge0, ge1 = g_cum[L - 1:L, h0:h0 + 1], g_cum[L - 1:L, h0 + 1:h0 + 2]
            e_end = jnp.exp(ch(ge0, ge1) - gc)
            b_vec = -(bt * jnp.exp(gs)) * kn
            k_vec = bt * kn
            units.append(dict(
                a_sc=kn, r_sc=qn, b_sc=b_vec, k_sc=k_vec,
                a_sr=kn * jnp.exp(g_prev), r_sr=qn * jnp.exp(gc),
                b_dec=b_vec * e_end, k_dec=k_vec * e_end,
                v=v_ref[s, :, sl], s_prev=[st_ref[s, 2 * p], st_ref[s, 2 * p + 1]],
                p_last=[jnp.exp(ge0), jnp.exp(ge1)],
                dm_a=jnp.exp(jnp.where(tcol < row, tm_(gc0 - gs0, gc1 - gs1) - gr, -jnp.inf)),
                dm_r=jnp.exp(jnp.where(tcol <= row, tm_(gc0, gc1) - gr, -jnp.inf))))
    ys, s_new = _dplr_pairs(units, L=L)
    npair = GDN_HEADS // 2
    for s in range(nseq):
        for p in range(npair):
            sl = slice(2 * p * hd, (2 * p + 2) * hd)
            st_ref[s, 2 * p] = s_new[s * npair + p][0]
            st_ref[s, 2 * p + 1] = s_new[s * npair + p][1]
            y = ys[s * npair + p]
            ms = jnp.where(first_c, jnp.mean(y[:, :hd] * y[:, :hd], axis=-1, keepdims=True),
                           jnp.mean(y[:, hd:] * y[:, hd:], axis=-1, keepdims=True))
            yn = y * lax.rsqrt(ms + NORM_EPS) * nw
            o_ref[s, :, sl] = (yn * _silu(z_ref[s, :, sl])).astype(o_ref.dtype)


def _gdn_branch(pc, pn, a_log, dt_bias, norm_w, bsz, seq):
    L = DPLR_CHUNK
    nseq = SEQ_PER_STEP if bsz % SEQ_PER_STEP == 0 else 1
    pc3 = pc.reshape(bsz, seq, N_PC)
    pn3 = pn.reshape(bsz, seq, N_PN)
    vec = lambda w: pl.BlockSpec((1, w), lambda b, c: (0, 0))
    tok = lambda width, col: pl.BlockSpec((nseq, L, width), lambda b, c: (b, c, col // width))
    out = pl.pallas_call(
        functools.partial(_gdn_kernel, L=L, nseq=nseq),
        grid=(bsz // nseq, seq // L),
        in_specs=[tok(GDN_INNER, PC_Q), tok(GDN_INNER, PC_K), tok(GDN_INNER, PC_V),
                  tok(GDN_INNER, PN_ZGDN), tok(LANES, PC_SMALL),
                  vec(LANES), vec(LANES), vec(2 * GDN_HEAD_DIM)],
        out_specs=pl.BlockSpec((nseq, L, GDN_INNER), lambda b, c: (b, c, 0)),
        out_shape=jax.ShapeDtypeStruct((bsz, seq, GDN_INNER), BF16),
        scratch_shapes=[pltpu.VMEM((nseq, GDN_HEADS, LANES, LANES), F32)],
        compiler_params=_cparams("parallel", "arbitrary"),
    )(pc3, pc3, pc3, pn3, pc3,
      _pad_lanes(a_log, SMALL_A), _pad_lanes(dt_bias, SMALL_A),
      jnp.tile(norm_w.reshape(1, GDN_HEAD_DIM).astype(F32), (1, 2)))
    return out.reshape(bsz * seq, GDN_INNER)


def _rwkv_kernel(rkv_ref, lora_ref, mu_ref, mul_ref, w0_ref, w2_ref, a0_ref, a2_ref, g2_ref,
                 kk_ref, ka_ref, rk_ref, lnw_ref, lnb_ref, o_ref, sh_ref, shl_ref, st_ref, *, L, nseq):
    @pl.when(pl.program_id(1) == 0)
    def _():
        st_ref[...] = jnp.zeros(st_ref.shape, F32)
        for s in range(nseq):
            sh_ref[s, 0:SUBLANES, :] = jnp.zeros((SUBLANES, sh_ref.shape[2]), F32)
            shl_ref[s, 0:SUBLANES, :] = jnp.zeros((SUBLANES, shl_ref.shape[2]), F32)

    n = RWKV_INNER
    npair = RWKV_HEADS // 2
    ones_bd = _half_ones()
    inv_hd = 1.0 / RWKV_HEAD_DIM

    def head_sums(t):
        st = jnp.concatenate([t[:, p * LANES:(p + 1) * LANES] for p in range(npair)], axis=0)
        sm = _dot(st, ones_bd)
        return jnp.concatenate([sm[p * L:(p + 1) * L] for p in range(npair)], axis=1)

    units, tails = [], []
    for s in range(nseq):
        u = rkv_ref[s]
        ul = lora_ref[s]
        sh_ref[s, SUBLANES:SUBLANES + L, :] = u
        shl_ref[s, SUBLANES:SUBLANES + L, :] = ul
        u = u + (sh_ref[s, pl.ds(SUBLANES - 1, L), :] - u) * mu_ref[...]
        ul = ul + (shl_ref[s, pl.ds(SUBLANES - 1, L), :] - ul) * mul_ref[...]
        sh_ref[s, 0:SUBLANES, :] = sh_ref[s, L:L + SUBLANES, :]
        shl_ref[s, 0:SUBLANES, :] = shl_ref[s, L:L + SUBLANES, :]

        r, k, v = u[:, 0:n], u[:, n:2 * n], u[:, 2 * n:3 * n]
        lo = ul[:, 0:LANES]
        dg = ul[:, LANES:2 * LANES]
        w_log = -_softplus(-(w0_ref[...] + _dot_hl(jnp.tanh(lo), w2_ref[...]))) - 0.5
        lw = -jnp.exp(w_log)
        a_ic = _sigmoid(a0_ref[...] + _dot_hl(lo, a2_ref[...]))
        gate = _dot(_sigmoid(dg), g2_ref[...])
        g_cum = _dot_w3(_tri_incl(L), lw)
        kk_raw = k * kk_ref[...]
        k_mod = k * (1.0 + (a_ic - 1.0) * ka_ref[...])
        kk_all = kk_raw * lax.rsqrt(head_sums(kk_raw * kk_raw) + L2_EPS)
        tails.append((head_sums(r * k_mod * rk_ref[...]) * v, gate))
        for p in range(npair):
            sl = slice(p * LANES, (p + 1) * LANES)
            kk = kk_all[:, sl]
            gc = g_cum[:, sl]
            g_end = gc[L - 1:L, :]
            e_neg = jnp.exp(-gc)
            e_end = jnp.exp(g_end - gc)
            b_vec = kk * a_ic[:, sl]
            k_vec = k_mod[:, sl]
            a_t = -kk * jnp.exp(gc - lw[:, sl])
            r_t = r[:, sl] * jnp.exp(gc)
            units.append(dict(
                a_sc=a_t, r_sc=r_t, b_sc=b_vec * e_neg, k_sc=k_vec * e_neg,
                a_sr=a_t, r_sr=r_t, b_dec=b_vec * e_end, k_dec=k_vec * e_end,
                v=v[:, sl], s_prev=[st_ref[s, p]], p_last=[jnp.exp(g_end)]))
    ys, s_new = _dplr_pairs(units, L=L)
    for s in range(nseq):
        for p in range(npair):
            st_ref[s, p] = s_new[s * npair + p][0]
        y = jnp.concatenate(ys[s * npair:(s + 1) * npair], axis=1)
        yc = y - head_sums(y) * inv_hd
        var = head_sums(yc * yc) * inv_hd
        yn = yc * lax.rsqrt(var + RWKV_GN_EPS) * lnw_ref[...] + lnb_ref[...]
        bonus, gate = tails[s]
        o_ref[s] = ((yn + bonus) * gate).astype(o_ref.dtype)


def _rwkv_branch(pc, pn, mu, w0, w2, a0, a2, g2, k_k, k_a, r_k, ln_w, ln_b, bsz, seq):
    L = DPLR_CHUNK
    n = RWKV_INNER
    nseq = SEQ_PER_STEP if bsz % SEQ_PER_STEP == 0 else 1
    pc3 = pc.reshape(bsz, seq, N_PC)
    pn3 = pn.reshape(bsz, seq, N_PN)
    vec = lambda w: pl.BlockSpec((1, w), lambda b, c: (0, 0))
    mat = lambda: pl.BlockSpec((LANES, n), lambda b, c: (0, 0))
    tok = lambda width, col: pl.BlockSpec((nseq, L, width), lambda b, c: (b, c, col // width))
    w2p = jnp.concatenate([w2, jnp.zeros((LANES - RWKV_DECAY_LORA, n), F32)], axis=0)
    a2p = jnp.concatenate([jnp.zeros((RWKV_DECAY_LORA, n), F32), a2], axis=0)
    r1 = lambda t: t.reshape(1, -1).astype(F32)
    out = pl.pallas_call(
        functools.partial(_rwkv_kernel, L=L, nseq=nseq),
        grid=(bsz // nseq, seq // L),
        in_specs=[tok(3 * n, PN_RKV), tok(256, PC_LORA),
                  vec(3 * n), vec(256), vec(n), mat(), vec(n), mat(), mat(),
                  vec(n), vec(n), vec(n), vec(n), vec(n)],
        out_specs=pl.BlockSpec((nseq, L, n), lambda b, c: (b, c, 0)),
        out_shape=jax.ShapeDtypeStruct((bsz, seq, n), BF16),
        scratch_shapes=[pltpu.VMEM((nseq, L + SUBLANES, 3 * n), F32),
                        pltpu.VMEM((nseq, L + SUBLANES, 256), F32),
                        pltpu.VMEM((nseq, RWKV_HEADS // 2, LANES, LANES), F32)],
        compiler_params=_cparams("parallel", "arbitrary"),
    )(pn3, pc3, r1(mu[:3 * n]), r1(mu[3 * n:]), r1(w0), w2p, r1(a0), a2p, g2.astype(F32),
      r1(k_k), r1(k_a), r1(r_k), r1(ln_w), r1(ln_b))
    return out.reshape(bsz * seq, n)


def _merge_kernel(x_ref, ys_ref, yg_ref, yr_ref, gate_ref, ps_ref, pg_ref, pr_ref, wo_ref, o_ref):
    d = D_MODEL
    m = _sigmoid(gate_ref[:, 0:d]) * jnp.dot(ys_ref[...], ps_ref[...], preferred_element_type=F32)
    m = m + _sigmoid(gate_ref[:, d:2 * d]) * jnp.dot(yg_ref[...], pg_ref[...], preferred_element_type=F32)
    m = m + _sigmoid(gate_ref[:, 2 * d:3 * d]) * jnp.dot(yr_ref[...], pr_ref[...], preferred_element_type=F32)
    o_ref[...] = x_ref[...] + jnp.dot(m.astype(BF16), wo_ref[...], preferred_element_type=F32)


def _merge(x, y_ssm, y_gdn, y_rwkv, pn, p_ssm, p_gdn, p_rwkv, w_out, tm=512):
    t, d = x.shape
    tok = lambda: pl.BlockSpec((tm, d), lambda i: (i, 0))
    wgt = lambda: pl.BlockSpec((d, d), lambda i: (0, 0))
    return pl.pallas_call(
        _merge_kernel,
        grid=(t // tm,),
        in_specs=[tok(), tok(), tok(), tok(),
                  pl.BlockSpec((tm, 3 * d), lambda i: (i, PN_GATE // (3 * d))),
                  wgt(), wgt(), wgt(), wgt()],
        out_specs=tok(),
        out_shape=jax.ShapeDtypeStruct((t, d), F32),
        compiler_params=_cparams("parallel"),
    )(x, y_ssm, y_gdn, y_rwkv, pn, p_ssm, p_gdn, p_rwkv, w_out)


def _ffn_kernel(x_ref, nw_ref, wg_ref, wu_ref, wd_ref, o_ref, h_ref, acc_ref):
    f = pl.program_id(1)

    @pl.when(f == 0)
    def _():
        x = x_ref[...]
        ms = jnp.mean(x * x, axis=-1, keepdims=True)
        h_ref[...] = (x * lax.rsqrt(ms + NORM_EPS) * nw_ref[...]).astype(BF16)
        acc_ref[...] = x

    h = h_ref[...]
    act = _silu(jnp.dot(h, wg_ref[...], preferred_element_type=F32)) * jnp.dot(
        h, wu_ref[...], preferred_element_type=F32)
    acc_ref[...] += jnp.dot(act.astype(BF16), wd_ref[...], preferred_element_type=F32)

    @pl.when(f == pl.num_programs(1) - 1)
    def _():
        o_ref[...] = acc_ref[...]


def _ffn_dense(x, norm_w, w_gate, w_up, w_down, tm=512, tf=1408):
    t, d = x.shape
    nf = w_gate.shape[1] // tf
    return pl.pallas_call(
        _ffn_kernel,
        grid=(t // tm, nf),
        in_specs=[pl.BlockSpec((tm, d), lambda i, f: (i, 0)),
                  pl.BlockSpec((1, d), lambda i, f: (0, 0)),
                  pl.BlockSpec((d, tf), lambda i, f: (0, f)),
                  pl.BlockSpec((d, tf), lambda i, f: (0, f)),
                  pl.BlockSpec((tf, d), lambda i, f: (f, 0))],
        out_specs=pl.BlockSpec((tm, d), lambda i, f: (i, 0)),
        out_shape=jax.ShapeDtypeStruct((t, d), F32),
        scratch_shapes=[pltpu.VMEM((tm, d), BF16), pltpu.VMEM((tm, d), F32)],
        compiler_params=_cparams("parallel", "arbitrary"),
    )(x, norm_w.reshape(1, d).astype(F32), w_gate, w_up, w_down)


MOE_TM = 1024
MOE_BLK = 128
MOE_FIRST_ROWS = 288


def _route_kernel(x_ref, nw_ref, rt_ref, h_ref, cmb_ref, cnt_ref):
    x = x_ref[...]
    lane = lax.broadcasted_iota(jnp.int32, cmb_ref.shape, 1)
    ms = jnp.mean(x * x, axis=-1, keepdims=True)
    hf = x * lax.rsqrt(ms + NORM_EPS) * nw_ref[...]
    h_ref[...] = hf.astype(BF16)
    logits = jnp.where(lane < N_EXPERTS, _dot_hl(hf, rt_ref[...]), -jnp.inf)
    m1 = jnp.max(logits, axis=-1, keepdims=True)
    i1 = jnp.min(jnp.where(logits == m1, lane, LANES), axis=-1, keepdims=True)
    rest = jnp.where(lane == i1, -jnp.inf, logits)
    m2 = jnp.max(rest, axis=-1, keepdims=True)
    i2 = jnp.min(jnp.where(rest == m2, lane, LANES), axis=-1, keepdims=True)
    e2 = jnp.exp(m2 - m1)
    p1 = 1.0 / (1.0 + e2)
    cmb = jnp.where(lane == i1, p1, 0.0) + jnp.where(lane == i2, e2 * p1, 0.0)
    cmb_ref[...] = cmb
    cnt_ref[0] = jnp.sum(jnp.where(cmb > 0.0, 1.0, 0.0), axis=0, keepdims=True).astype(jnp.int32)


def _moe_route(x, norm_w, router, tm):
    t, d = x.shape
    ne = router.shape[1]
    rt = jnp.concatenate([router.astype(F32), jnp.zeros((d, LANES - ne), F32)], axis=1)
    return pl.pallas_call(
        _route_kernel,
        grid=(t // tm,),
        in_specs=[pl.BlockSpec((tm, d), lambda i: (i, 0)),
                  pl.BlockSpec((1, d), lambda i: (0, 0)),
                  pl.BlockSpec((d, LANES), lambda i: (0, 0))],
        out_specs=[pl.BlockSpec((tm, d), lambda i: (i, 0)),
                   pl.BlockSpec((tm, LANES), lambda i: (i, 0)),
                   pl.BlockSpec((1, 1, LANES), lambda i: (i, 0, 0))],
        out_shape=[jax.ShapeDtypeStruct((t, d), BF16),
                   jax.ShapeDtypeStruct((t, LANES), F32),
                   jax.ShapeDtypeStruct((t // tm, 1, LANES), jnp.int32)],
        compiler_params=_cparams("parallel"),
    )(x, norm_w.reshape(1, d).astype(F32), rt)


def _moe_kernel(cnt_ref, x_ref, h_ref, cmb_ref, wg_ref, wu_ref, wd_ref, fw_ref, o_ref,
                rank_ref, xc_ref, yc_ref, oh_ref, *, tm, final_norm):
    i = pl.program_id(0)
    e = pl.program_id(1)
    f = pl.program_id(2)
    last_f = f == pl.num_programs(2) - 1
    lane = lax.broadcasted_iota(jnp.int32, (tm, LANES), 1)

    @pl.when((e == 0) & (f == 0))
    def _():
        o_ref[...] = jnp.zeros(o_ref.shape, F32)
        r = lax.broadcasted_iota(jnp.int32, (tm, tm), 0)
        c = lax.broadcasted_iota(jnp.int32, (tm, tm), 1)
        before = jnp.where(c < r, 1.0, 0.0).astype(BF16)
        sel = jnp.where(cmb_ref[...] > 0.0, 1.0, 0.0).astype(BF16)
        rank_ref[...] = jnp.dot(before, sel, preferred_element_type=F32)

    cnt = cnt_ref[i, e]

    def weight():
        return jnp.sum(jnp.where(lane == e, cmb_ref[...], 0.0), axis=-1, keepdims=True)

    def one_hot(off, rows):
        r_e = jnp.sum(jnp.where(lane == e, rank_ref[...], 0.0), axis=-1, keepdims=True)
        key = jnp.where(weight() > 0.0, r_e, -1.0) - jnp.asarray(off, F32)
        slot = lax.broadcasted_iota(jnp.int32, (tm, rows), 1).astype(F32)
        return jnp.where(key == slot, 1.0, 0.0).astype(BF16)

    def process(off, rows, cache):
        rs = pl.ds(off, rows)

        @pl.when(f == 0)
        def _():
            onehot = one_hot(off, rows)
            if cache:
                oh_ref[...] = onehot
            xc_ref[rs, :] = _dot_tn(onehot, h_ref[...]).astype(BF16)
            yc_ref[rs, :] = jnp.zeros((rows, yc_ref.shape[1]), F32)

        xc = xc_ref[rs, :]
        act = _silu(jnp.dot(xc, wg_ref[0], preferred_element_type=F32)) * jnp.dot(
            xc, wu_ref[0], preferred_element_type=F32)
        yc_ref[rs, :] += jnp.dot(act.astype(BF16), wd_ref[0], preferred_element_type=F32)

        @pl.when(last_f)
        def _():
            onehot = oh_ref[...] if cache else one_hot(off, rows)
            o_ref[...] += weight() * jnp.dot(onehot, yc_ref[rs, :].astype(BF16), preferred_element_type=F32)

    @pl.when(cnt > 0)
    def _():
        process(0, MOE_FIRST_ROWS, True)

    def extra(b, carry):
        process(pl.multiple_of(MOE_FIRST_ROWS + b * MOE_BLK, 32), MOE_BLK, False)
        return carry

    lax.fori_loop(0, (jnp.maximum(cnt - MOE_FIRST_ROWS, 0) + MOE_BLK - 1) // MOE_BLK, extra, 0)

    @pl.when((e == pl.num_programs(1) - 1) & last_f)
    def _():
        y = x_ref[...] + o_ref[...]
        if final_norm:
            y = y * lax.rsqrt(jnp.mean(y * y, axis=-1, keepdims=True) + NORM_EPS) * fw_ref[...]
        o_ref[...] = y


def _ffn_moe(x, norm_w, router, w_gate, w_up, w_down, final_w, tm=MOE_TM, tf=1408):
    t, d = x.shape
    ne, _, ff = w_gate.shape
    nf = ff // tf
    h, cmb, cnt = _moe_route(x, norm_w, router, tm)
    final_norm = final_w is not None
    fw = (final_w if final_norm else jnp.ones((d,), F32)).reshape(1, d).astype(F32)
    cap = MOE_FIRST_ROWS + -(-max(tm - MOE_FIRST_ROWS, 0) // MOE_BLK) * MOE_BLK
    grid_spec = pltpu.PrefetchScalarGridSpec(
        num_scalar_prefetch=1,
        grid=(t // tm, ne, nf),
        in_specs=[pl.BlockSpec((tm, d), lambda i, e, f, c: (i, 0)),
                  pl.BlockSpec((tm, d), lambda i, e, f, c: (i, 0)),
                  pl.BlockSpec((tm, LANES), lambda i, e, f, c: (i, 0)),
                  pl.BlockSpec((1, d, tf), lambda i, e, f, c: (e, 0, f)),
                  pl.BlockSpec((1, d, tf), lambda i, e, f, c: (e, 0, f)),
                  pl.BlockSpec((1, tf, d), lambda i, e, f, c: (e, f, 0)),
                  pl.BlockSpec((1, d), lambda i, e, f, c: (0, 0))],
        out_specs=pl.BlockSpec((tm, d), lambda i, e, f, c: (i, 0)),
        scratch_shapes=[pltpu.VMEM((tm, LANES), F32), pltpu.VMEM((cap, d), BF16), pltpu.VMEM((cap, d), F32),
                        pltpu.VMEM((tm, MOE_FIRST_ROWS), BF16)])
    return pl.pallas_call(
        functools.partial(_moe_kernel, tm=tm, final_norm=final_norm),
        grid_spec=grid_spec,
        out_shape=jax.ShapeDtypeStruct((t, d), F32),
        compiler_params=_cparams("parallel", "arbitrary", "arbitrary"),
    )(cnt.reshape(t // tm, LANES), x, h, cmb, w_gate, w_up, w_down, fw)


def _reorder_kernel(w_ref, wc_ref, wn_ref):
    w = w_ref[0]
    wc = [w[:, 0:1024], w[:, 1536:4608], w[:, 1024:1536], w[:, 9760:10016], w[:, 5632:5648],
          w[:, 6672:6688], jnp.zeros((w.shape[0], N_PC - PC_SMALL - 32), w.dtype)]
    wn = [w[:, 6688:9760], w[:, 10016:13088], w[:, 4608:5632], w[:, 5648:6672]]
    wc_ref[...] = jnp.concatenate(wc, axis=1).astype(BF16)
    wn_ref[...] = jnp.concatenate(wn, axis=1).astype(BF16)


def _reorder_w_in(w, layer, tk=128):
    _, d, n = w.shape
    return pl.pallas_call(
        _reorder_kernel,
        grid=(d // tk,),
        in_specs=[pl.BlockSpec((1, tk, n), lambda i: (layer, i, 0))],
        out_specs=[pl.BlockSpec((tk, N_PC), lambda i: (i, 0)), pl.BlockSpec((tk, N_PN), lambda i: (i, 0))],
        out_shape=[jax.ShapeDtypeStruct((d, N_PC), BF16), jax.ShapeDtypeStruct((d, N_PN), BF16)],
        compiler_params=_cparams("parallel"),
    )(w)


def _token_mixer_layer(x, i, bsz, seq, p):
    h = _rmsnorm(x, p["attn_norm_w"][i], BF16)
    wc, wn = _reorder_w_in(p["w_in"], i)
    pc, pn = _inproj(h, wc, wn, p["conv_w"][i], p["conv_b"][i], seq, tm=1024)
    y_ssm = _ssd_branch(pc, pn, p["ssm_a_log"][i], p["ssm_dt_bias"][i], p["ssm_d"][i],
                        p["ssm_norm_w"][i], bsz, seq)
    y_gdn = _gdn_branch(pc, pn, p["gdn_a_log"][i], p["gdn_dt_bias"][i], p["gdn_norm_w"][i],
                        bsz, seq)
    y_rwkv = _rwkv_branch(pc, pn, p["rwkv_mu"][i], p["rwkv_w0"][i], p["rwkv_w2"][i], p["rwkv_a0"][i],
                          p["rwkv_a2"][i], p["rwkv_g2"][i], p["rwkv_k_k"][i], p["rwkv_k_a"][i],
                          p["rwkv_r_k"][i], p["rwkv_ln_w"][i], p["rwkv_ln_b"][i], bsz, seq)
    return _merge(x, y_ssm, y_gdn, y_rwkv, pn, p["proj_ssm"][i], p["proj_gdn"][i], p["proj_rwkv"][i],
                  p["w_out"][i])


def kernel(x, attn_norm_w, w_in, conv_w, conv_b, ssm_a_log, ssm_dt_bias, ssm_d, ssm_norm_w, gdn_a_log, gdn_dt_bias, gdn_norm_w, rwkv_mu, rwkv_w0, rwkv_w2, rwkv_a0, rwkv_a2, rwkv_g2, rwkv_k_k, rwkv_k_a, rwkv_r_k, rwkv_ln_w, rwkv_ln_b, proj_ssm, proj_gdn, proj_rwkv, w_out, ffn_norm_w, dense_w_gate, dense_w_up, dense_w_down, moe_router, moe_w_gate, moe_w_up, moe_w_down, final_norm_w):
    p = dict(attn_norm_w=attn_norm_w, w_in=w_in, conv_w=conv_w, conv_b=conv_b, ssm_a_log=ssm_a_log,
             ssm_dt_bias=ssm_dt_bias, ssm_d=ssm_d, ssm_norm_w=ssm_norm_w, gdn_a_log=gdn_a_log,
             gdn_dt_bias=gdn_dt_bias, gdn_norm_w=gdn_norm_w, rwkv_mu=rwkv_mu, rwkv_w0=rwkv_w0,
             rwkv_w2=rwkv_w2, rwkv_a0=rwkv_a0, rwkv_a2=rwkv_a2, rwkv_g2=rwkv_g2, rwkv_k_k=rwkv_k_k,
             rwkv_k_a=rwkv_k_a, rwkv_r_k=rwkv_r_k, rwkv_ln_w=rwkv_ln_w, rwkv_ln_b=rwkv_ln_b,
             proj_ssm=proj_ssm, proj_gdn=proj_gdn, proj_rwkv=proj_rwkv, w_out=w_out)
    for name in ("proj_ssm", "proj_gdn", "proj_rwkv", "w_out"):
        p[name] = p[name].astype(BF16)
    dense_w_gate, dense_w_up, dense_w_down = (t.astype(BF16) for t in (dense_w_gate, dense_w_up, dense_w_down))
    moe_w_gate, moe_w_up, moe_w_down = (t.astype(BF16) for t in (moe_w_gate, moe_w_up, moe_w_down))
    bsz, seq, d = x.shape
    depth = attn_norm_w.shape[0]
    xt = x.reshape(bsz * seq, d)
    for i in range(depth):
        xt = _token_mixer_layer(xt, i, bsz, seq, p)
        j = i // 2
        last = i == depth - 1
        if i % 2 == 0:
            xt = _ffn_dense(xt, ffn_norm_w[i], dense_w_gate[j], dense_w_up[j], dense_w_down[j])
            if last:
                xt = _rmsnorm(xt, final_norm_w, F32)
        else:
            xt = _ffn_moe(xt, ffn_norm_w[i], moe_router[j], moe_w_gate[j], moe_w_up[j], moe_w_down[j],
                          final_norm_w if last else None)
    return xt.reshape(bsz, seq, d)
```

```python
import functools
import math

import numpy as np
import jax
import jax.numpy as jnp
from jax import lax
from jax.experimental import pallas as pl
from jax.experimental.pallas import tpu as pltpu

F32 = jnp.float32
BF16 = jnp.bfloat16

LANES = 128
SUBLANES = 8
VMEM_LIMIT_BYTES = 56 * 1024 * 1024

D_MODEL = 1024
SSM_HEADS, SSM_HEAD_DIM, SSM_GROUPS, SSM_STATE = 16, 64, 2, 128
SSM_INNER = SSM_HEADS * SSM_HEAD_DIM
GDN_HEADS, GDN_HEAD_DIM = 8, 128
GDN_INNER = GDN_HEADS * GDN_HEAD_DIM
RWKV_HEADS, RWKV_HEAD_DIM = 16, 64
RWKV_INNER = RWKV_HEADS * RWKV_HEAD_DIM
RWKV_DECAY_LORA, RWKV_ICLR_LORA, RWKV_GATE_LORA = 64, 64, 128
RWKV_GN_EPS = 64e-5
CONV_K = 4
FFN_DIM = 2816
N_EXPERTS = 8
NORM_EPS = 1e-6
L2_EPS = 1e-6

PC_XS, PC_Q, PC_K, PC_V = 0, 1024, 2048, 3072
PC_BS, PC_CS = 4096, 4352
PC_CONV_END = 4608
PC_LORA = 4608
PC_SMALL = 4864
N_PC = 5120
PN_RKV = 0
PN_GATE = 3072
PN_ZSSM, PN_ZGDN = 6144, 7168
N_PN = 8192
SMALL_DT, SMALL_A, SMALL_B = 0, 16, 24

SSD_CHUNK = 128
DPLR_CHUNK = 64
SEQ_PER_STEP = 4
INPROJ_ROW_CHUNK = 512


def _cparams(*sem):
    return pltpu.CompilerParams(dimension_semantics=sem, vmem_limit_bytes=VMEM_LIMIT_BYTES)


def _dot(a, b):
    return jnp.dot(a.astype(BF16), b.astype(BF16), preferred_element_type=F32)


def _dot_nt(a, b):
    return lax.dot_general(a.astype(BF16), b.astype(BF16), (((1,), (1,)), ((), ())),
                           preferred_element_type=F32)


def _dot_tn(a, b):
    return lax.dot_general(a.astype(BF16), b.astype(BF16), (((0,), (0,)), ((), ())),
                           preferred_element_type=F32)


def _split3(x):
    hi = x.astype(BF16)
    r1 = x - hi.astype(F32)
    mid = r1.astype(BF16)
    lo = (r1 - mid.astype(F32)).astype(BF16)
    return hi, mid, lo


def _dot_x3(x, w):
    hi, mid, lo = _split3(x)
    return _dot(hi, w) + _dot(mid, w) + _dot(lo, w)


def _dot_w3(w, x):
    hi, mid, lo = _split3(x)
    return _dot(w, hi) + _dot(w, mid) + _dot(w, lo)


def _dot_hl(a, b):
    ah = a.astype(BF16)
    al = (a - ah.astype(F32)).astype(BF16)
    bh = b.astype(BF16)
    bl = (b - bh.astype(F32)).astype(BF16)
    return _dot(ah, bh) + _dot(ah, bl) + _dot(al, bh)


def _sigmoid(x):
    return 0.5 * jnp.tanh(0.5 * x) + 0.5


def _silu(x):
    return x * _sigmoid(x)


def _softplus(x):
    return jnp.maximum(x, 0.0) + jnp.log(1.0 + jnp.exp(-jnp.abs(x)))


def _tri_incl(n):
    r = lax.broadcasted_iota(jnp.int32, (n, n), 0)
    c = lax.broadcasted_iota(jnp.int32, (n, n), 1)
    return jnp.where(r >= c, 1.0, 0.0).astype(BF16)


def _half_ones():
    r = lax.broadcasted_iota(jnp.int32, (LANES, LANES), 0)
    c = lax.broadcasted_iota(jnp.int32, (LANES, LANES), 1)
    return jnp.where((r // 64) == (c // 64), 1.0, 0.0).astype(BF16)


def _rmsnorm_kernel(x_ref, w_ref, o_ref):
    x = x_ref[...]
    ms = jnp.mean(x * x, axis=-1, keepdims=True)
    o_ref[...] = (x * lax.rsqrt(ms + NORM_EPS) * w_ref[...]).astype(o_ref.dtype)


def _rmsnorm(x, w, out_dtype, tm=512):
    t, d = x.shape
    return pl.pallas_call(
        _rmsnorm_kernel,
        grid=(t // tm,),
        in_specs=[pl.BlockSpec((tm, d), lambda i: (i, 0)),
                  pl.BlockSpec((1, d), lambda i: (0, 0))],
        out_specs=pl.BlockSpec((tm, d), lambda i: (i, 0)),
        out_shape=jax.ShapeDtypeStruct((t, d), out_dtype),
        compiler_params=_cparams("parallel"),
    )(x, w.reshape(1, d))


def _inproj_kernel(a_ref, wc_ref, wn_ref, cw_ref, cb_ref, cm_ref, oc_ref, on_ref, sh_ref, *,
                   tm, tiles_per_seq):
    i = pl.program_id(1)
    tc = oc_ref.shape[1]

    @pl.when(i % tiles_per_seq == 0)
    def _():
        sh_ref[0:SUBLANES, :] = jnp.zeros((SUBLANES, tc), F32)

    rc = min(tm, INPROJ_ROW_CHUNK)
    starts = list(range(0, tm, rc))
    acc = jnp.dot(a_ref[0:rc, :], wc_ref[...], preferred_element_type=F32)
    for n, r0 in enumerate(starts):
        sh_ref[SUBLANES + r0:SUBLANES + r0 + rc, :] = acc
        on_ref[r0:r0 + rc, :] = jnp.dot(a_ref[r0:r0 + rc, :], wn_ref[...], preferred_element_type=F32)
        if n + 1 < len(starts):
            acc_next = jnp.dot(a_ref[r0 + rc:r0 + 2 * rc, :], wc_ref[...], preferred_element_type=F32)
        c = acc * cw_ref[CONV_K - 1:CONV_K, :] + cb_ref[...]
        for k in range(1, CONV_K):
            c = c + sh_ref[pl.ds(SUBLANES + r0 - k, rc), :] * cw_ref[CONV_K - 1 - k:CONV_K - k, :]
        oc_ref[r0:r0 + rc, :] = jnp.where(cm_ref[...] > 0.0, _silu(c), acc)
        if n + 1 < len(starts):
            acc = acc_next
    sh_ref[0:SUBLANES, :] = sh_ref[tm:tm + SUBLANES, :]


def _conv_cols(c):
    r = c.shape[0]
    return jnp.concatenate([c[:, 0:1024], c[:, 1536:4608], c[:, 1024:1536],
                            jnp.zeros((r, N_PC - PC_CONV_END), F32)], axis=1).astype(F32)


def _inproj(a, wc, wn, conv_w, conv_b, seq, tm, nsteps=8):
    m, k = a.shape
    tc, tn = N_PC // nsteps, N_PN // nsteps
    cw = _conv_cols(conv_w)
    cb = _conv_cols(conv_b.reshape(1, -1))
    cm = _conv_cols(jnp.ones((1, conv_w.shape[1]), F32))
    return pl.pallas_call(
        functools.partial(_inproj_kernel, tm=tm, tiles_per_seq=seq // tm),
        grid=(nsteps, m // tm),
        in_specs=[pl.BlockSpec((tm, k), lambda j, i: (i, 0)),
                  pl.BlockSpec((k, tc), lambda j, i: (0, j)),
                  pl.BlockSpec((k, tn), lambda j, i: (0, j)),
                  pl.BlockSpec((CONV_K, tc), lambda j, i: (0, j)),
                  pl.BlockSpec((1, tc), lambda j, i: (0, j)),
                  pl.BlockSpec((1, tc), lambda j, i: (0, j))],
        out_specs=[pl.BlockSpec((tm, tc), lambda j, i: (i, j)),
                   pl.BlockSpec((tm, tn), lambda j, i: (i, j))],
        out_shape=[jax.ShapeDtypeStruct((m, N_PC), F32), jax.ShapeDtypeStruct((m, N_PN), F32)],
        scratch_shapes=[pltpu.VMEM((tm + SUBLANES, tc), F32)],
        compiler_params=_cparams("parallel", "arbitrary"),
    )(a, wc, wn, cw, cb, cm)


def _ssd_kernel(xs_ref, bs_ref, cs_ref, z_ref, sm_ref, alog_ref, dtb_ref, dsk_ref, nw_ref, ex_ref,
                o_ref, st_ref, *, L, nseq):
    @pl.when(pl.program_id(1) == 0)
    def _():
        st_ref[...] = jnp.zeros(st_ref.shape, F32)

    hd = SSM_HEAD_DIM
    gw = SSM_INNER // SSM_GROUPS
    heads_per_group = SSM_HEADS // SSM_GROUPS
    ex = ex_ref[...]
    row = lax.broadcasted_iota(jnp.int32, (L, L), 0)
    col = lax.broadcasted_iota(jnp.int32, (L, L), 1)
    causal = row >= col
    lane = lax.broadcasted_iota(jnp.int32, (L, LANES), 1)
    first_half = lane < hd
    for s in range(nseq):
        dt = _softplus(sm_ref[s] + dtb_ref[...])
        la = -jnp.exp(alog_ref[...]) * dt
        g_cum = _dot_w3(_tri_incl(L), la)
        g_cum_t = g_cum.T
        dt_e = _dot_x3(dt, ex)
        g_e = _dot_x3(g_cum, ex)
        g_last = g_e[L - 1:L, :]
        e_g = jnp.exp(g_e)
        e_dec = jnp.exp(g_last - g_e)
        e_last = jnp.exp(g_last)

        xs = xs_ref[s]
        xdt = xs * dt_e
        xdec = xdt * e_dec
        y_parts = []
        for g in range(SSM_GROUPS):
            gsl = slice(g * gw, (g + 1) * gw)
            b_g = bs_ref[s, :, g * SSM_STATE:(g + 1) * SSM_STATE]
            c_g = cs_ref[s, :, g * SSM_STATE:(g + 1) * SSM_STATE]
            cb = _dot_nt(c_g, b_g)
            st_g = st_ref[s, :, gsl]
            y_off = _dot(c_g, st_g) * e_g[:, gsl]
            st_ref[s, :, gsl] = st_g * e_last[:, gsl] + _dot_tn(b_g, xdec[:, gsl])
            for p in range(heads_per_group // 2):
                h0 = g * heads_per_group + 2 * p
                base = h0 * hd
                sc = []
                for h in (h0, h0 + 1):
                    diff = g_cum[:, h:h + 1] - g_cum_t[h:h + 1, :]
                    sc.append(cb * jnp.exp(jnp.where(causal, diff, -jnp.inf)))
                yd = _dot(jnp.concatenate(sc, axis=0), xdt[:, base:base + LANES])
                y_parts.append(jnp.where(first_half, yd[:L], yd[L:])
                               + y_off[:, base - g * gw:base - g * gw + LANES])
        y = jnp.concatenate(y_parts, axis=1) + dsk_ref[...] * xs
        y = y * _silu(z_ref[s])
        outs = []
        for g in range(SSM_GROUPS):
            seg = y[:, g * gw:(g + 1) * gw]
            outs.append(seg * lax.rsqrt(jnp.mean(seg * seg, axis=-1, keepdims=True) + NORM_EPS))
        o_ref[s] = (jnp.concatenate(outs, axis=1) * nw_ref[...]).astype(o_ref.dtype)


def _pad_lanes(v, offset=0, width=LANES):
    out = jnp.zeros((1, width), F32)
    return lax.dynamic_update_slice(out, v.reshape(1, -1).astype(F32), (0, offset))


def _ssd_branch(pc, pn, a_log, dt_bias, d_skip, norm_w, bsz, seq):
    L = SSD_CHUNK
    nseq = SEQ_PER_STEP if bsz % SEQ_PER_STEP == 0 else 1
    pc3 = pc.reshape(bsz, seq, N_PC)
    pn3 = pn.reshape(bsz, seq, N_PN)
    ex = np.zeros((LANES, SSM_INNER), np.float32)
    for h in range(SSM_HEADS):
        ex[h, h * SSM_HEAD_DIM:(h + 1) * SSM_HEAD_DIM] = 1.0
    vec = lambda w: pl.BlockSpec((1, w), lambda b, c: (0, 0))
    tok = lambda width, col: pl.BlockSpec((nseq, L, width), lambda b, c: (b, c, col // width))
    out = pl.pallas_call(
        functools.partial(_ssd_kernel, L=L, nseq=nseq),
        grid=(bsz // nseq, seq // L),
        in_specs=[tok(SSM_INNER, PC_XS), tok(256, PC_BS), tok(256, PC_CS), tok(SSM_INNER, PN_ZSSM),
                  tok(LANES, PC_SMALL),
                  vec(LANES), vec(LANES), vec(SSM_INNER), vec(SSM_INNER),
                  pl.BlockSpec((LANES, SSM_INNER), lambda b, c: (0, 0))],
        out_specs=pl.BlockSpec((nseq, L, SSM_INNER), lambda b, c: (b, c, 0)),
        out_shape=jax.ShapeDtypeStruct((bsz, seq, SSM_INNER), BF16),
        scratch_shapes=[pltpu.VMEM((nseq, SSM_STATE, SSM_INNER), F32)],
        compiler_params=_cparams("parallel", "arbitrary"),
    )(pc3, pc3, pc3, pn3, pc3,
      _pad_lanes(a_log, SMALL_DT), _pad_lanes(dt_bias, SMALL_DT),
      jnp.repeat(d_skip.astype(F32), SSM_HEAD_DIM).reshape(1, SSM_INNER),
      norm_w.reshape(1, SSM_INNER).astype(F32), jnp.asarray(ex, BF16))
    return out.reshape(bsz * seq, SSM_INNER)


def _head_stack(x):
    half = x.shape[1] // 2
    lane = lax.broadcasted_iota(jnp.int32, x.shape, 1)
    return jnp.concatenate([jnp.where(lane < half, x, 0.0), jnp.where(lane >= half, x, 0.0)], axis=0)


def _dplr_pairs(units, *, L):
    nu = len(units)
    P = 2 * L
    vc = units[0]["v"].shape[1]
    row = lax.broadcasted_iota(jnp.int32, (L, P), 0)
    tcol = lax.broadcasted_iota(jnp.int32, (L, P), 1)
    first = tcol < L
    tcol = jnp.where(first, tcol, tcol - L)
    strict, lower = tcol < row, tcol <= row

    sc = [_dot_nt(jnp.concatenate([u["a_sc"], u["r_sc"]], axis=0),
                  jnp.concatenate([_head_stack(u["b_sc"]), _head_stack(u["k_sc"])], axis=0))
          for u in units]
    v_st = [_head_stack(u["v"]) for u in units]

    def read(u):
        rows = jnp.concatenate([u["a_sr"], u["r_sr"]], axis=0)
        st = u["s_prev"]
        if len(st) == 1:
            return _dot_nt(rows, st[0])
        kh = rows.shape[1] // 2
        return jnp.concatenate([_dot_nt(rows[:, :kh], st[0]), _dot_nt(rows[:, kh:], st[1])], axis=1)

    reads = [read(u) for u in units]
    n_mat, a_rb, aks = [], [], []
    for i, u in enumerate(units):
        sb, sk = sc[i][:, :P], sc[i][:, P:]
        if u.get("dm_a") is None:
            n_mat.append(jnp.where(strict, sb[:L], 0.0))
            a_rb.append(jnp.where(lower, sb[L:], 0.0))
            aks.append(jnp.concatenate([jnp.where(strict, sk[:L], 0.0), jnp.where(lower, sk[L:], 0.0)], axis=0))
        else:
            n_mat.append(sb[:L] * u["dm_a"])
            a_rb.append(sb[L:] * u["dm_r"])
            aks.append(jnp.concatenate([sk[:L] * u["dm_a"], sk[L:] * u["dm_r"]], axis=0))
    av = [_dot(aks[i], v_st[i]) for i in range(nu)]
    x = [reads[i][:L] + av[i][:L] for i in range(nu)]
    y_part = [reads[i][L:] + av[i][L:] for i in range(nu)]

    steps = int(math.log2(L))
    for s in range(steps):
        x_st = [_head_stack(x[i]) for i in range(nu)]
        if s + 1 < steps:
            t = [_dot(n_mat[i], jnp.concatenate(
                    [x_st[i], jnp.concatenate([jnp.where(first, n_mat[i], 0.0),
                                               jnp.where(first, 0.0, n_mat[i])], axis=0)], axis=1))
                 for i in range(nu)]
            x = [x[i] + t[i][:, :vc] for i in range(nu)]
            n_mat = [t[i][:, vc:] for i in range(nu)]
        else:
            x = [x[i] + _dot(n_mat[i], x_st[i]) for i in range(nu)]
    ys = [y_part[i] + _dot(a_rb[i], _head_stack(x[i])) for i in range(nu)]

    s_new = []
    for i, u in enumerate(units):
        xv = jnp.concatenate([x[i], u["v"]], axis=0)
        bk = jnp.concatenate([u["b_dec"], u["k_dec"]], axis=0)
        st, pl_ = u["s_prev"], u["p_last"]
        if len(st) == 1:
            upd = _dot_tn(xv, bk)
            ri = lax.broadcasted_iota(jnp.int32, upd.shape, 0) // (upd.shape[0] // 2)
            ci = lax.broadcasted_iota(jnp.int32, upd.shape, 1) // (upd.shape[1] // 2)
            s_new.append([st[0] * pl_[0] + jnp.where(ri == ci, upd, 0.0)])
        else:
            vh, kh = xv.shape[1] // 2, bk.shape[1] // 2
            s_new.append([st[0] * pl_[0] + _dot_tn(xv[:, :vh], bk[:, :kh]),
                          st[1] * pl_[1] + _dot_tn(xv[:, vh:], bk[:, kh:])])
    return ys, s_new


def _gdn_kernel(q_ref, k_ref, v_ref, z_ref, sm_ref, alog_ref, dtb_ref, nw_ref, o_ref, st_ref, *, L, nseq):
    @pl.when(pl.program_id(1) == 0)
    def _():
        st_ref[...] = jnp.zeros(st_ref.shape, F32)

    hd = GDN_HEAD_DIM
    row = lax.broadcasted_iota(jnp.int32, (L, 2 * L), 0)
    tcol = lax.broadcasted_iota(jnp.int32, (L, 2 * L), 1)
    first_t = tcol < L
    tcol = jnp.where(first_t, tcol, tcol - L)
    first_c = lax.broadcasted_iota(jnp.int32, (L, 2 * hd), 1) < hd
    nw = nw_ref[...]
    scale = hd ** -0.5
    units = []
    for s in range(nseq):
        sm = sm_ref[s]
        g = -jnp.exp(alog_ref[...]) * _softplus(sm + dtb_ref[...])
        beta = _sigmoid(sm)
        g_cum = _dot_w3(_tri_incl(L), g)
        g_cum_t = jnp.concatenate([g_cum] * (LANES // L), axis=0).T
        for p in range(GDN_HEADS // 2):
            sl = slice(2 * p * hd, (2 * p + 2) * hd)
            q = q_ref[s, :, sl]
            k = k_ref[s, :, sl]

            def per_head(t):
                return jnp.where(first_c, jnp.sum(t[:, :hd], axis=-1, keepdims=True),
                                 jnp.sum(t[:, hd:], axis=-1, keepdims=True))

            def col(arr, base):
                h0 = base + 2 * p
                return arr[:, h0:h0 + 1], arr[:, h0 + 1:h0 + 2]

            qn = q * (lax.rsqrt(per_head(q * q) + L2_EPS) * scale)
            kn = k * lax.rsqrt(per_head(k * k) + L2_EPS)
            gc0, gc1 = col(g_cum, SMALL_A)
            gs0, gs1 = col(g, SMALL_A)
            bt0, bt1 = col(beta, SMALL_B)
            ch = lambda a0, a1: jnp.where(first_c, a0, a1)
            tm_ = lambda a0, a1: jnp.where(first_t, a0, a1)
            gc, gs, bt = ch(gc0, gc1), ch(gs0, gs1), ch(bt0, bt1)
            g_prev = gc - gs
            h0 = SMALL_A + 2 * p
            gr = tm_(g_cum_t[h0:h0 + 1, :], g_cum_t[h0 + 1:h0 + 2, :])
            ge0, ge1 = g_cum[L - 1:L, h0:h0 + 1], g_cum[L - 1:L, h0 + 1:h0 + 2]
            e_end = jnp.exp(ch(ge0, ge1) - gc)
            b_vec = -(bt * jnp.exp(gs)) * kn
            k_vec = bt * kn
            units.append(dict(
                a_sc=kn, r_sc=qn, b_sc=b_vec, k_sc=k_vec,
                a_sr=kn * jnp.exp(g_prev), r_sr=qn * jnp.exp(gc),
                b_dec=b_vec * e_end, k_dec=k_vec * e_end,
                v=v_ref[s, :, sl], s_prev=[st_ref[s, 2 * p], st_ref[s, 2 * p + 1]],
                p_last=[jnp.exp(ge0), jnp.exp(ge1)],
                dm_a=jnp.exp(jnp.where(tcol < row, tm_(gc0 - gs0, gc1 - gs1) - gr, -jnp.inf)),
                dm_r=jnp.exp(jnp.where(tcol <= row, tm_(gc0, gc1) - gr, -jnp.inf))))
    ys, s_new = _dplr_pairs(units, L=L)
    npair = GDN_HEADS // 2
    for s in range(nseq):
        for p in range(npair):
            sl = slice(2 * p * hd, (2 * p + 2) * hd)
            st_ref[s, 2 * p] = s_new[s * npair + p][0]
            st_ref[s, 2 * p + 1] = s_new[s * npair + p][1]
            y = ys[s * npair + p]
            ms = jnp.where(first_c, jnp.mean(y[:, :hd] * y[:, :hd], axis=-1, keepdims=True),
                           jnp.mean(y[:, hd:] * y[:, hd:], axis=-1, keepdims=True))
            yn = y * lax.rsqrt(ms + NORM_EPS) * nw
            o_ref[s, :, sl] = (yn * _silu(z_ref[s, :, sl])).astype(o_ref.dtype)


def _gdn_branch(pc, pn, a_log, dt_bias, norm_w, bsz, seq):
    L = DPLR_CHUNK
    nseq = SEQ_PER_STEP if bsz % SEQ_PER_STEP == 0 else 1
    pc3 = pc.reshape(bsz, seq, N_PC)
    pn3 = pn.reshape(bsz, seq, N_PN)
    vec = lambda w: pl.BlockSpec((1, w), lambda b, c: (0, 0))
    tok = lambda width, col: pl.BlockSpec((nseq, L, width), lambda b, c: (b, c, col // width))
    out = pl.pallas_call(
        functools.partial(_gdn_kernel, L=L, nseq=nseq),
        grid=(bsz // nseq, seq // L),
        in_specs=[tok(GDN_INNER, PC_Q), tok(GDN_INNER, PC_K), tok(GDN_INNER, PC_V),
                  tok(GDN_INNER, PN_ZGDN), tok(LANES, PC_SMALL),
                  vec(LANES), vec(LANES), vec(2 * GDN_HEAD_DIM)],
        out_specs=pl.BlockSpec((nseq, L, GDN_INNER), lambda b, c: (b, c, 0)),
        out_shape=jax.ShapeDtypeStruct((bsz, seq, GDN_INNER), BF16),
        scratch_shapes=[pltpu.VMEM((nseq, GDN_HEADS, LANES, LANES), F32)],
        compiler_params=_cparams("parallel", "arbitrary"),
    )(pc3, pc3, pc3, pn3, pc3,
      _pad_lanes(a_log, SMALL_A), _pad_lanes(dt_bias, SMALL_A),
      jnp.tile(norm_w.reshape(1, GDN_HEAD_DIM).astype(F32), (1, 2)))
    return out.reshape(bsz * seq, GDN_INNER)


def _rwkv_kernel(rkv_ref, lora_ref, mu_ref, mul_ref, w0_ref, w2_ref, a0_ref, a2_ref, g2_ref,
                 kk_ref, ka_ref, rk_ref, lnw_ref, lnb_ref, o_ref, sh_ref, shl_ref, st_ref, *, L, nseq):
    @pl.when(pl.program_id(1) == 0)
    def _():
        st_ref[...] = jnp.zeros(st_ref.shape, F32)
        for s in range(nseq):
            sh_ref[s, 0:SUBLANES, :] = jnp.zeros((SUBLANES, sh_ref.shape[2]), F32)
            shl_ref[s, 0:SUBLANES, :] = jnp.zeros((SUBLANES, shl_ref.shape[2]), F32)

    n = RWKV_INNER
    npair = RWKV_HEADS // 2
    ones_bd = _half_ones()
    inv_hd = 1.0 / RWKV_HEAD_DIM

    def head_sums(t):
        st = jnp.concatenate([t[:, p * LANES:(p + 1) * LANES] for p in range(npair)], axis=0)
        sm = _dot(st, ones_bd)
        return jnp.concatenate([sm[p * L:(p + 1) * L] for p in range(npair)], axis=1)

    units, tails = [], []
    for s in range(nseq):
        u = rkv_ref[s]
        ul = lora_ref[s]
        sh_ref[s, SUBLANES:SUBLANES + L, :] = u
        shl_ref[s, SUBLANES:SUBLANES + L, :] = ul
        u = u + (sh_ref[s, pl.ds(SUBLANES - 1, L), :] - u) * mu_ref[...]
        ul = ul + (shl_ref[s, pl.ds(SUBLANES - 1, L), :] - ul) * mul_ref[...]
        sh_ref[s, 0:SUBLANES, :] = sh_ref[s, L:L + SUBLANES, :]
        shl_ref[s, 0:SUBLANES, :] = shl_ref[s, L:L + SUBLANES, :]

        r, k, v = u[:, 0:n], u[:, n:2 * n], u[:, 2 * n:3 * n]
        lo = ul[:, 0:LANES]
        dg = ul[:, LANES:2 * LANES]
        w_log = -_softplus(-(w0_ref[...] + _dot_hl(jnp.tanh(lo), w2_ref[...]))) - 0.5
        lw = -jnp.exp(w_log)
        a_ic = _sigmoid(a0_ref[...] + _dot_hl(lo, a2_ref[...]))
        gate = _dot(_sigmoid(dg), g2_ref[...])
        g_cum = _dot_w3(_tri_incl(L), lw)
        kk_raw = k * kk_ref[...]
        k_mod = k * (1.0 + (a_ic - 1.0) * ka_ref[...])
        kk_all = kk_raw * lax.rsqrt(head_sums(kk_raw * kk_raw) + L2_EPS)
        tails.append((head_sums(r * k_mod * rk_ref[...]) * v, gate))
        for p in range(npair):
            sl = slice(p * LANES, (p + 1) * LANES)
            kk = kk_all[:, sl]
            gc = g_cum[:, sl]
            g_end = gc[L - 1:L, :]
            e_neg = jnp.exp(-gc)
            e_end = jnp.exp(g_end - gc)
            b_vec = kk * a_ic[:, sl]
            k_vec = k_mod[:, sl]
            a_t = -kk * jnp.exp(gc - lw[:, sl])
            r_t = r[:, sl] * jnp.exp(gc)
            units.append(dict(
                a_sc=a_t, r_sc=r_t, b_sc=b_vec * e_neg, k_sc=k_vec * e_neg,
                a_sr=a_t, r_sr=r_t, b_dec=b_vec * e_end, k_dec=k_vec * e_end,
                v=v[:, sl], s_prev=[st_ref[s, p]], p_last=[jnp.exp(g_end)]))
    ys, s_new = _dplr_pairs(units, L=L)
    for s in range(nseq):
        for p in range(npair):
            st_ref[s, p] = s_new[s * npair + p][0]
        y = jnp.concatenate(ys[s * npair:(s + 1) * npair], axis=1)
        yc = y - head_sums(y) * inv_hd
        var = head_sums(yc * yc) * inv_hd
        yn = yc * lax.rsqrt(var + RWKV_GN_EPS) * lnw_ref[...] + lnb_ref[...]
        bonus, gate = tails[s]
        o_ref[s] = ((yn + bonus) * gate).astype(o_ref.dtype)


def _rwkv_branch(pc, pn, mu, w0, w2, a0, a2, g2, k_k, k_a, r_k, ln_w, ln_b, bsz, seq):
    L = DPLR_CHUNK
    n = RWKV_INNER
    nseq = SEQ_PER_STEP if bsz % SEQ_PER_STEP == 0 else 1
    pc3 = pc.reshape(bsz, seq, N_PC)
    pn3 = pn.reshape(bsz, seq, N_PN)
    vec = lambda w: pl.BlockSpec((1, w), lambda b, c: (0, 0))
    mat = lambda: pl.BlockSpec((LANES, n), lambda b, c: (0, 0))
    tok = lambda width, col: pl.BlockSpec((nseq, L, width), lambda b, c: (b, c, col // width))
    w2p = jnp.concatenate([w2, jnp.zeros((LANES - RWKV_DECAY_LORA, n), F32)], axis=0)
    a2p = jnp.concatenate([jnp.zeros((RWKV_DECAY_LORA, n), F32), a2], axis=0)
    r1 = lambda t: t.reshape(1, -1).astype(F32)
    out = pl.pallas_call(
        functools.partial(_rwkv_kernel, L=L, nseq=nseq),
        grid=(bsz // nseq, seq // L),
        in_specs=[tok(3 * n, PN_RKV), tok(256, PC_LORA),
                  vec(3 * n), vec(256), vec(n), mat(), vec(n), mat(), mat(),
                  vec(n), vec(n), vec(n), vec(n), vec(n)],
        out_specs=pl.BlockSpec((nseq, L, n), lambda b, c: (b, c, 0)),
        out_shape=jax.ShapeDtypeStruct((bsz, seq, n), BF16),
        scratch_shapes=[pltpu.VMEM((nseq, L + SUBLANES, 3 * n), F32),
                        pltpu.VMEM((nseq, L + SUBLANES, 256), F32),
                        pltpu.VMEM((nseq, RWKV_HEADS // 2, LANES, LANES), F32)],
        compiler_params=_cparams("parallel", "arbitrary"),
    )(pn3, pc3, r1(mu[:3 * n]), r1(mu[3 * n:]), r1(w0), w2p, r1(a0), a2p, g2.astype(F32),
      r1(k_k), r1(k_a), r1(r_k), r1(ln_w), r1(ln_b))
    return out.reshape(bsz * seq, n)


def _merge_kernel(x_ref, ys_ref, yg_ref, yr_ref, gate_ref, ps_ref, pg_ref, pr_ref, wo_ref, o_ref):
    d = D_MODEL
    m = _sigmoid(gate_ref[:, 0:d]) * jnp.dot(ys_ref[...], ps_ref[...], preferred_element_type=F32)
    m = m + _sigmoid(gate_ref[:, d:2 * d]) * jnp.dot(yg_ref[...], pg_ref[...], preferred_element_type=F32)
    m = m + _sigmoid(gate_ref[:, 2 * d:3 * d]) * jnp.dot(yr_ref[...], pr_ref[...], preferred_element_type=F32)
    o_ref[...] = x_ref[...] + jnp.dot(m.astype(BF16), wo_ref[...], preferred_element_type=F32)


def _merge(x, y_ssm, y_gdn, y_rwkv, pn, p_ssm, p_gdn, p_rwkv, w_out, tm=512):
    t, d = x.shape
    tok = lambda: pl.BlockSpec((tm, d), lambda i: (i, 0))
    wgt = lambda: pl.BlockSpec((d, d), lambda i: (0, 0))
    return pl.pallas_call(
        _merge_kernel,
        grid=(t // tm,),
        in_specs=[tok(), tok(), tok(), tok(),
                  pl.BlockSpec((tm, 3 * d), lambda i: (i, PN_GATE // (3 * d))),
                  wgt(), wgt(), wgt(), wgt()],
        out_specs=tok(),
        out_shape=jax.ShapeDtypeStruct((t, d), F32),
        compiler_params=_cparams("parallel"),
    )(x, y_ssm, y_gdn, y_rwkv, pn, p_ssm, p_gdn, p_rwkv, w_out)


def _ffn_kernel(x_ref, nw_ref, wg_ref, wu_ref, wd_ref, *rest, emit_next):
    if emit_next:
        nnw_ref, o_ref, hn_ref, h_ref, acc_ref = rest
    else:
        o_ref, h_ref, acc_ref = rest
    f = pl.program_id(1)

    @pl.when(f == 0)
    def _():
        x = x_ref[...]
        ms = jnp.mean(x * x, axis=-1, keepdims=True)
        h_ref[...] = (x * lax.rsqrt(ms + NORM_EPS) * nw_ref[...]).astype(BF16)
        acc_ref[...] = x

    h = h_ref[...]
    act = _silu(jnp.dot(h, wg_ref[...], preferred_element_type=F32)) * jnp.dot(
        h, wu_ref[...], preferred_element_type=F32)
    acc_ref[...] += jnp.dot(act.astype(BF16), wd_ref[...], preferred_element_type=F32)

    @pl.when(f == pl.num_programs(1) - 1)
    def _():
        y = acc_ref[...]
        o_ref[...] = y
        if emit_next:
            ms = jnp.mean(y * y, axis=-1, keepdims=True)
            hn_ref[...] = (y * lax.rsqrt(ms + NORM_EPS) * nnw_ref[...]).astype(BF16)


def _ffn_dense(x, norm_w, w_gate, w_up, w_down, next_norm_w=None, tm=512, tf=1408):
    t, d = x.shape
    nf = w_gate.shape[1] // tf
    emit_next = next_norm_w is not None
    tok = pl.BlockSpec((tm, d), lambda i, f: (i, 0))
    vec = pl.BlockSpec((1, d), lambda i, f: (0, 0))
    in_specs = [tok, vec,
                pl.BlockSpec((d, tf), lambda i, f: (0, f)),
                pl.BlockSpec((d, tf), lambda i, f: (0, f)),
                pl.BlockSpec((tf, d), lambda i, f: (f, 0))]
    args = [x, norm_w.reshape(1, d).astype(F32), w_gate, w_up, w_down]
    out_specs, out_shape = tok, jax.ShapeDtypeStruct((t, d), F32)
    if emit_next:
        in_specs.append(vec)
        args.append(next_norm_w.reshape(1, d).astype(F32))
        out_specs, out_shape = [tok, tok], [out_shape, jax.ShapeDtypeStruct((t, d), BF16)]
    return pl.pallas_call(
        functools.partial(_ffn_kernel, emit_next=emit_next),
        grid=(t // tm, nf),
        in_specs=in_specs,
        out_specs=out_specs,
        out_shape=out_shape,
        scratch_shapes=[pltpu.VMEM((tm, d), BF16), pltpu.VMEM((tm, d), F32)],
        compiler_params=_cparams("parallel", "arbitrary"),
    )(*args)


MOE_TM = 1024
MOE_BLK = 128
MOE_FIRST_ROWS = 288


def _route_kernel(x_ref, nw_ref, rt_ref, h_ref, cmb_ref, cnt_ref):
    x = x_ref[...]
    lane = lax.broadcasted_iota(jnp.int32, cmb_ref.shape, 1)
    ms = jnp.mean(x * x, axis=-1, keepdims=True)
    hf = x * lax.rsqrt(ms + NORM_EPS) * nw_ref[...]
    h_ref[...] = hf.astype(BF16)
    logits = jnp.where(lane < N_EXPERTS, _dot_hl(hf, rt_ref[...]), -jnp.inf)
    m1 = jnp.max(logits, axis=-1, keepdims=True)
    i1 = jnp.min(jnp.where(logits == m1, lane, LANES), axis=-1, keepdims=True)
    rest = jnp.where(lane == i1, -jnp.inf, logits)
    m2 = jnp.max(rest, axis=-1, keepdims=True)
    i2 = jnp.min(jnp.where(rest == m2, lane, LANES), axis=-1, keepdims=True)
    e2 = jnp.exp(m2 - m1)
    p1 = 1.0 / (1.0 + e2)
    cmb = jnp.where(lane == i1, p1, 0.0) + jnp.where(lane == i2, e2 * p1, 0.0)
    cmb_ref[...] = cmb
    cnt_ref[0] = jnp.sum(jnp.where(cmb > 0.0, 1.0, 0.0), axis=0, keepdims=True).astype(jnp.int32)


def _moe_route(x, norm_w, router, tm):
    t, d = x.shape
    ne = router.shape[1]
    rt = jnp.concatenate([router.astype(F32), jnp.zeros((d, LANES - ne), F32)], axis=1)
    return pl.pallas_call(
        _route_kernel,
        grid=(t // tm,),
        in_specs=[pl.BlockSpec((tm, d), lambda i: (i, 0)),
                  pl.BlockSpec((1, d), lambda i: (0, 0)),
                  pl.BlockSpec((d, LANES), lambda i: (0, 0))],
        out_specs=[pl.BlockSpec((tm, d), lambda i: (i, 0)),
                   pl.BlockSpec((tm, LANES), lambda i: (i, 0)),
                   pl.BlockSpec((1, 1, LANES), lambda i: (i, 0, 0))],
        out_shape=[jax.ShapeDtypeStruct((t, d), BF16),
                   jax.ShapeDtypeStruct((t, LANES), F32),
                   jax.ShapeDtypeStruct((t // tm, 1, LANES), jnp.int32)],
        compiler_params=_cparams("parallel"),
    )(x, norm_w.reshape(1, d).astype(F32), rt)


def _moe_kernel(cnt_ref, x_ref, h_ref, cmb_ref, wg_ref, wu_ref, wd_ref, fw_ref, o_ref,
                rank_ref, xc_ref, yc_ref, oh_ref, *, tm, final_norm):
    i = pl.program_id(0)
    e = pl.program_id(1)
    f = pl.program_id(2)
    last_f = f == pl.num_programs(2) - 1
    lane = lax.broadcasted_iota(jnp.int32, (tm, LANES), 1)

    @pl.when((e == 0) & (f == 0))
    def _():
        o_ref[...] = jnp.zeros(o_ref.shape, F32)
        r = lax.broadcasted_iota(jnp.int32, (tm, tm), 0)
        c = lax.broadcasted_iota(jnp.int32, (tm, tm), 1)
        before = jnp.where(c < r, 1.0, 0.0).astype(BF16)
        sel = jnp.where(cmb_ref[...] > 0.0, 1.0, 0.0).astype(BF16)
        rank_ref[...] = jnp.dot(before, sel, preferred_element_type=F32)

    cnt = cnt_ref[i, e]

    def weight():
        return jnp.sum(jnp.where(lane == e, cmb_ref[...], 0.0), axis=-1, keepdims=True)

    def one_hot(off, rows):
        r_e = jnp.sum(jnp.where(lane == e, rank_ref[...], 0.0), axis=-1, keepdims=True)
        key = jnp.where(weight() > 0.0, r_e, -1.0) - jnp.asarray(off, F32)
        slot = lax.broadcasted_iota(jnp.int32, (tm, rows), 1).astype(F32)
        return jnp.where(key == slot, 1.0, 0.0).astype(BF16)

    def process(off, rows, cache):
        rs = pl.ds(off, rows)

        @pl.when(f == 0)
        def _():
            onehot = one_hot(off, rows)
            if cache:
                oh_ref[...] = onehot
            xc_ref[rs, :] = _dot_tn(onehot, h_ref[...]).astype(BF16)
            yc_ref[rs, :] = jnp.zeros((rows, yc_ref.shape[1]), F32)

        xc = xc_ref[rs, :]
        act = _silu(jnp.dot(xc, wg_ref[0], preferred_element_type=F32)) * jnp.dot(
            xc, wu_ref[0], preferred_element_type=F32)
        yc_ref[rs, :] += jnp.dot(act.astype(BF16), wd_ref[0], preferred_element_type=F32)

        @pl.when(last_f)
        def _():
            onehot = oh_ref[...] if cache else one_hot(off, rows)
            o_ref[...] += weight() * jnp.dot(onehot, yc_ref[rs, :].astype(BF16), preferred_element_type=F32)

    @pl.when(cnt > 0)
    def _():
        process(0, MOE_FIRST_ROWS, True)

    def extra(b, carry):
        process(pl.multiple_of(MOE_FIRST_ROWS + b * MOE_BLK, 32), MOE_BLK, False)
        return carry

    lax.fori_loop(0, (jnp.maximum(cnt - MOE_FIRST_ROWS, 0) + MOE_BLK - 1) // MOE_BLK, extra, 0)

    @pl.when((e == pl.num_programs(1) - 1) & last_f)
    def _():
        y = x_ref[...] + o_ref[...]
        if final_norm:
            y = y * lax.rsqrt(jnp.mean(y * y, axis=-1, keepdims=True) + NORM_EPS) * fw_ref[...]
        o_ref[...] = y


def _ffn_moe(x, norm_w, router, w_gate, w_up, w_down, final_w, tm=MOE_TM, tf=1408):
    t, d = x.shape
    ne, _, ff = w_gate.shape
    nf = ff // tf
    h, cmb, cnt = _moe_route(x, norm_w, router, tm)
    final_norm = final_w is not None
    fw = (final_w if final_norm else jnp.ones((d,), F32)).reshape(1, d).astype(F32)
    cap = MOE_FIRST_ROWS + -(-max(tm - MOE_FIRST_ROWS, 0) // MOE_BLK) * MOE_BLK
    grid_spec = pltpu.PrefetchScalarGridSpec(
        num_scalar_prefetch=1,
        grid=(t // tm, ne, nf),
        in_specs=[pl.BlockSpec((tm, d), lambda i, e, f, c: (i, 0)),
                  pl.BlockSpec((tm, d), lambda i, e, f, c: (i, 0)),
                  pl.BlockSpec((tm, LANES), lambda i, e, f, c: (i, 0)),
                  pl.BlockSpec((1, d, tf), lambda i, e, f, c: (e, 0, f)),
                  pl.BlockSpec((1, d, tf), lambda i, e, f, c: (e, 0, f)),
                  pl.BlockSpec((1, tf, d), lambda i, e, f, c: (e, f, 0)),
                  pl.BlockSpec((1, d), lambda i, e, f, c: (0, 0))],
        out_specs=pl.BlockSpec((tm, d), lambda i, e, f, c: (i, 0)),
        scratch_shapes=[pltpu.VMEM((tm, LANES), F32), pltpu.VMEM((cap, d), BF16), pltpu.VMEM((cap, d), F32),
                        pltpu.VMEM((tm, MOE_FIRST_ROWS), BF16)])
    return pl.pallas_call(
        functools.partial(_moe_kernel, tm=tm, final_norm=final_norm),
        grid_spec=grid_spec,
        out_shape=jax.ShapeDtypeStruct((t, d), F32),
        compiler_params=_cparams("parallel", "arbitrary", "arbitrary"),
    )(cnt.reshape(t // tm, LANES), x, h, cmb, w_gate, w_up, w_down, fw)


def _reorder_kernel(w_ref, wc_ref, wn_ref):
    w = w_ref[0]
    wc = [w[:, 0:1024], w[:, 1536:4608], w[:, 1024:1536], w[:, 9760:10016], w[:, 5632:5648],
          w[:, 6672:6688], jnp.zeros((w.shape[0], N_PC - PC_SMALL - 32), w.dtype)]
    wn = [w[:, 6688:9760], w[:, 10016:13088], w[:, 4608:5632], w[:, 5648:6672]]
    wc_ref[...] = jnp.concatenate(wc, axis=1).astype(BF16)
    wn_ref[...] = jnp.concatenate(wn, axis=1).astype(BF16)


def _reorder_w_in(w, layer, tk=128):
    _, d, n = w.shape
    return pl.pallas_call(
        _reorder_kernel,
        grid=(d // tk,),
        in_specs=[pl.BlockSpec((1, tk, n), lambda i: (layer, i, 0))],
        out_specs=[pl.BlockSpec((tk, N_PC), lambda i: (i, 0)), pl.BlockSpec((tk, N_PN), lambda i: (i, 0))],
        out_shape=[jax.ShapeDtypeStruct((d, N_PC), BF16), jax.ShapeDtypeStruct((d, N_PN), BF16)],
        compiler_params=_cparams("parallel"),
    )(w)


def _token_mixer_layer(x, h, i, bsz, seq, p):
    if h is None:
        h = _rmsnorm(x, p["attn_norm_w"][i], BF16)
    wc, wn = _reorder_w_in(p["w_in"], i)
    pc, pn = _inproj(h, wc, wn, p["conv_w"][i], p["conv_b"][i], seq, tm=1024)
    y_ssm = _ssd_branch(pc, pn, p["ssm_a_log"][i], p["ssm_dt_bias"][i], p["ssm_d"][i],
                        p["ssm_norm_w"][i], bsz, seq)
    y_gdn = _gdn_branch(pc, pn, p["gdn_a_log"][i], p["gdn_dt_bias"][i], p["gdn_norm_w"][i],
                        bsz, seq)
    y_rwkv = _rwkv_branch(pc, pn, p["rwkv_mu"][i], p["rwkv_w0"][i], p["rwkv_w2"][i], p["rwkv_a0"][i],
                          p["rwkv_a2"][i], p["rwkv_g2"][i], p["rwkv_k_k"][i], p["rwkv_k_a"][i],
                          p["rwkv_r_k"][i], p["rwkv_ln_w"][i], p["rwkv_ln_b"][i], bsz, seq)
    return _merge(x, y_ssm, y_gdn, y_rwkv, pn, p["proj_ssm"][i], p["proj_gdn"][i], p["proj_rwkv"][i],
                  p["w_out"][i])


def kernel(x, attn_norm_w, w_in, conv_w, conv_b, ssm_a_log, ssm_dt_bias, ssm_d, ssm_norm_w, gdn_a_log, gdn_dt_bias, gdn_norm_w, rwkv_mu, rwkv_w0, rwkv_w2, rwkv_a0, rwkv_a2, rwkv_g2, rwkv_k_k, rwkv_k_a, rwkv_r_k, rwkv_ln_w, rwkv_ln_b, proj_ssm, proj_gdn, proj_rwkv, w_out, ffn_norm_w, dense_w_gate, dense_w_up, dense_w_down, moe_router, moe_w_gate, moe_w_up, moe_w_down, final_norm_w):
    p = dict(attn_norm_w=attn_norm_w, w_in=w_in, conv_w=conv_w, conv_b=conv_b, ssm_a_log=ssm_a_log,
             ssm_dt_bias=ssm_dt_bias, ssm_d=ssm_d, ssm_norm_w=ssm_norm_w, gdn_a_log=gdn_a_log,
             gdn_dt_bias=gdn_dt_bias, gdn_norm_w=gdn_norm_w, rwkv_mu=rwkv_mu, rwkv_w0=rwkv_w0,
             rwkv_w2=rwkv_w2, rwkv_a0=rwkv_a0, rwkv_a2=rwkv_a2, rwkv_g2=rwkv_g2, rwkv_k_k=rwkv_k_k,
             rwkv_k_a=rwkv_k_a, rwkv_r_k=rwkv_r_k, rwkv_ln_w=rwkv_ln_w, rwkv_ln_b=rwkv_ln_b,
             proj_ssm=proj_ssm, proj_gdn=proj_gdn, proj_rwkv=proj_rwkv, w_out=w_out)
    for name in ("proj_ssm", "proj_gdn", "proj_rwkv", "w_out"):
        p[name] = p[name].astype(BF16)
    dense_w_gate, dense_w_up, dense_w_down = (t.astype(BF16) for t in (dense_w_gate, dense_w_up, dense_w_down))
    moe_w_gate, moe_w_up, moe_w_down = (t.astype(BF16) for t in (moe_w_gate, moe_w_up, moe_w_down))
    bsz, seq, d = x.shape
    depth = attn_norm_w.shape[0]
    xt = x.reshape(bsz * seq, d)
    h_next = None
    for i in range(depth):
        xt = _token_mixer_layer(xt, h_next, i, bsz, seq, p)
        h_next = None
        j = i // 2
        last = i == depth - 1
        if i % 2 == 0:
            if last:
                xt = _ffn_dense(xt, ffn_norm_w[i], dense_w_gate[j], dense_w_up[j], dense_w_down[j])
                xt = _rmsnorm(xt, final_norm_w, F32)
            else:
                xt, h_next = _ffn_dense(xt, ffn_norm_w[i], dense_w_gate[j], dense_w_up[j], dense_w_down[j],
                                        next_norm_w=attn_norm_w[i + 1])
        else:
            xt = _ffn_moe(xt, ffn_norm_w[i], moe_router[j], moe_w_gate[j], moe_w_up[j], moe_w_down[j],
                          final_norm_w if last else None)
    return xt.reshape(bsz, seq, d)
```

```python
import functools
import math

import numpy as np
import jax
import jax.numpy as jnp
from jax import lax
from jax.experimental import pallas as pl
from jax.experimental.pallas import tpu as pltpu

F32 = jnp.float32
BF16 = jnp.bfloat16

LANES = 128
SUBLANES = 8
VMEM_LIMIT_BYTES = 56 * 1024 * 1024

D_MODEL = 1024
SSM_HEADS, SSM_HEAD_DIM, SSM_GROUPS, SSM_STATE = 16, 64, 2, 128
SSM_INNER = SSM_HEADS * SSM_HEAD_DIM
GDN_HEADS, GDN_HEAD_DIM = 8, 128
GDN_INNER = GDN_HEADS * GDN_HEAD_DIM
RWKV_HEADS, RWKV_HEAD_DIM = 16, 64
RWKV_INNER = RWKV_HEADS * RWKV_HEAD_DIM
RWKV_DECAY_LORA, RWKV_ICLR_LORA, RWKV_GATE_LORA = 64, 64, 128
RWKV_GN_EPS = 64e-5
CONV_K = 4
FFN_DIM = 2816
N_EXPERTS = 8
NORM_EPS = 1e-6
L2_EPS = 1e-6

PC_XS, PC_Q, PC_K, PC_V = 0, 1024, 2048, 3072
PC_BS, PC_CS = 4096, 4352
PC_CONV_END = 4608
PC_LORA = 4608
PC_SMALL = 4864
N_PC = 5120
PN_RKV = 0
PN_GATE = 3072
PN_ZSSM, PN_ZGDN = 6144, 7168
N_PN = 8192
SMALL_DT, SMALL_A, SMALL_B = 0, 16, 24

SSD_CHUNK = 128
DPLR_CHUNK = 64
SEQ_PER_STEP = 4
INPROJ_ROW_CHUNK = 512


def _cparams(*sem):
    return pltpu.CompilerParams(dimension_semantics=sem, vmem_limit_bytes=VMEM_LIMIT_BYTES)


def _dot(a, b):
    return jnp.dot(a.astype(BF16), b.astype(BF16), preferred_element_type=F32)


def _dot_nt(a, b):
    return lax.dot_general(a.astype(BF16), b.astype(BF16), (((1,), (1,)), ((), ())),
                           preferred_element_type=F32)


def _dot_tn(a, b):
    return lax.dot_general(a.astype(BF16), b.astype(BF16), (((0,), (0,)), ((), ())),
                           preferred_element_type=F32)


def _split3(x):
    hi = x.astype(BF16)
    r1 = x - hi.astype(F32)
    mid = r1.astype(BF16)
    lo = (r1 - mid.astype(F32)).astype(BF16)
    return hi, mid, lo


def _dot_x3(x, w):
    hi, mid, lo = _split3(x)
    return _dot(hi, w) + _dot(mid, w) + _dot(lo, w)


def _dot_w3(w, x):
    hi, mid, lo = _split3(x)
    return _dot(w, hi) + _dot(w, mid) + _dot(w, lo)


def _dot_hl(a, b):
    ah = a.astype(BF16)
    al = (a - ah.astype(F32)).astype(BF16)
    bh = b.astype(BF16)
    bl = (b - bh.astype(F32)).astype(BF16)
    return _dot(ah, bh) + _dot(ah, bl) + _dot(al, bh)


def _sigmoid(x):
    return 0.5 * jnp.tanh(0.5 * x) + 0.5


def _silu(x):
    return x * _sigmoid(x)


def _softplus(x):
    return jnp.maximum(x, 0.0) + jnp.log(1.0 + jnp.exp(-jnp.abs(x)))


def _tri_incl(n):
    r = lax.broadcasted_iota(jnp.int32, (n, n), 0)
    c = lax.broadcasted_iota(jnp.int32, (n, n), 1)
    return jnp.where(r >= c, 1.0, 0.0).astype(BF16)


def _half_ones():
    r = lax.broadcasted_iota(jnp.int32, (LANES, LANES), 0)
    c = lax.broadcasted_iota(jnp.int32, (LANES, LANES), 1)
    return jnp.where((r // 64) == (c // 64), 1.0, 0.0).astype(BF16)


def _rmsnorm_kernel(x_ref, w_ref, o_ref):
    x = x_ref[...]
    ms = jnp.mean(x * x, axis=-1, keepdims=True)
    o_ref[...] = (x * lax.rsqrt(ms + NORM_EPS) * w_ref[...]).astype(o_ref.dtype)


def _rmsnorm(x, w, out_dtype, tm=512):
    t, d = x.shape
    return pl.pallas_call(
        _rmsnorm_kernel,
        grid=(t // tm,),
        in_specs=[pl.BlockSpec((tm, d), lambda i: (i, 0)),
                  pl.BlockSpec((1, d), lambda i: (0, 0))],
        out_specs=pl.BlockSpec((tm, d), lambda i: (i, 0)),
        out_shape=jax.ShapeDtypeStruct((t, d), out_dtype),
        compiler_params=_cparams("parallel"),
    )(x, w.reshape(1, d))


def _inproj_kernel(a_ref, wc_ref, wn_ref, cw_ref, cb_ref, cm_ref, oc_ref, on_ref, sh_ref, *,
                   tm, tiles_per_seq):
    i = pl.program_id(1)
    tc = oc_ref.shape[1]

    @pl.when(i % tiles_per_seq == 0)
    def _():
        sh_ref[0:SUBLANES, :] = jnp.zeros((SUBLANES, tc), F32)

    rc = min(tm, INPROJ_ROW_CHUNK)
    starts = list(range(0, tm, rc))
    acc = jnp.dot(a_ref[0:rc, :], wc_ref[...], preferred_element_type=F32)
    for n, r0 in enumerate(starts):
        sh_ref[SUBLANES + r0:SUBLANES + r0 + rc, :] = acc
        on_ref[r0:r0 + rc, :] = jnp.dot(a_ref[r0:r0 + rc, :], wn_ref[...], preferred_element_type=F32)
        if n + 1 < len(starts):
            acc_next = jnp.dot(a_ref[r0 + rc:r0 + 2 * rc, :], wc_ref[...], preferred_element_type=F32)
        c = acc * cw_ref[CONV_K - 1:CONV_K, :] + cb_ref[...]
        for k in range(1, CONV_K):
            c = c + sh_ref[pl.ds(SUBLANES + r0 - k, rc), :] * cw_ref[CONV_K - 1 - k:CONV_K - k, :]
        oc_ref[r0:r0 + rc, :] = jnp.where(cm_ref[...] > 0.0, _silu(c), acc)
        if n + 1 < len(starts):
            acc = acc_next
    sh_ref[0:SUBLANES, :] = sh_ref[tm:tm + SUBLANES, :]


def _conv_cols(c):
    r = c.shape[0]
    return jnp.concatenate([c[:, 0:1024], c[:, 1536:4608], c[:, 1024:1536],
                            jnp.zeros((r, N_PC - PC_CONV_END), F32)], axis=1).astype(F32)


def _inproj(a, wc, wn, conv_w, conv_b, seq, tm, nsteps=4):
    m, k = a.shape
    tc, tn = N_PC // nsteps, N_PN // nsteps
    cw = _conv_cols(conv_w)
    cb = _conv_cols(conv_b.reshape(1, -1))
    cm = _conv_cols(jnp.ones((1, conv_w.shape[1]), F32))
    return pl.pallas_call(
        functools.partial(_inproj_kernel, tm=tm, tiles_per_seq=seq // tm),
        grid=(nsteps, m // tm),
        in_specs=[pl.BlockSpec((tm, k), lambda j, i: (i, 0)),
                  pl.BlockSpec((k, tc), lambda j, i: (0, j)),
                  pl.BlockSpec((k, tn), lambda j, i: (0, j)),
                  pl.BlockSpec((CONV_K, tc), lambda j, i: (0, j)),
                  pl.BlockSpec((1, tc), lambda j, i: (0, j)),
                  pl.BlockSpec((1, tc), lambda j, i: (0, j))],
        out_specs=[pl.BlockSpec((tm, tc), lambda j, i: (i, j)),
                   pl.BlockSpec((tm, tn), lambda j, i: (i, j))],
        out_shape=[jax.ShapeDtypeStruct((m, N_PC), F32), jax.ShapeDtypeStruct((m, N_PN), F32)],
        scratch_shapes=[pltpu.VMEM((tm + SUBLANES, tc), F32)],
        compiler_params=_cparams("parallel", "arbitrary"),
    )(a, wc, wn, cw, cb, cm)


def _ssd_kernel(xs_ref, bs_ref, cs_ref, z_ref, sm_ref, alog_ref, dtb_ref, dsk_ref, nw_ref, ex_ref,
                o_ref, st_ref, *, L, nseq):
    @pl.when(pl.program_id(1) == 0)
    def _():
        st_ref[...] = jnp.zeros(st_ref.shape, F32)

    hd = SSM_HEAD_DIM
    gw = SSM_INNER // SSM_GROUPS
    heads_per_group = SSM_HEADS // SSM_GROUPS
    ex = ex_ref[...]
    row = lax.broadcasted_iota(jnp.int32, (L, L), 0)
    col = lax.broadcasted_iota(jnp.int32, (L, L), 1)
    causal = row >= col
    lane = lax.broadcasted_iota(jnp.int32, (L, LANES), 1)
    first_half = lane < hd
    for s in range(nseq):
        dt = _softplus(sm_ref[s] + dtb_ref[...])
        la = -jnp.exp(alog_ref[...]) * dt
        g_cum = _dot_w3(_tri_incl(L), la)
        g_cum_t = g_cum.T
        dt_e = _dot_x3(dt, ex)
        g_e = _dot_x3(g_cum, ex)
        g_last = g_e[L - 1:L, :]
        e_g = jnp.exp(g_e)
        e_dec = jnp.exp(g_last - g_e)
        e_last = jnp.exp(g_last)

        xs = xs_ref[s]
        xdt = xs * dt_e
        xdec = xdt * e_dec
        y_parts = []
        for g in range(SSM_GROUPS):
            gsl = slice(g * gw, (g + 1) * gw)
            b_g = bs_ref[s, :, g * SSM_STATE:(g + 1) * SSM_STATE]
            c_g = cs_ref[s, :, g * SSM_STATE:(g + 1) * SSM_STATE]
            cb = _dot_nt(c_g, b_g)
            st_g = st_ref[s, :, gsl]
            y_off = _dot(c_g, st_g) * e_g[:, gsl]
            st_ref[s, :, gsl] = st_g * e_last[:, gsl] + _dot_tn(b_g, xdec[:, gsl])
            for p in range(heads_per_group // 2):
                h0 = g * heads_per_group + 2 * p
                base = h0 * hd
                sc = []
                for h in (h0, h0 + 1):
                    diff = g_cum[:, h:h + 1] - g_cum_t[h:h + 1, :]
                    sc.append(cb * jnp.exp(jnp.where(causal, diff, -jnp.inf)))
                yd = _dot(jnp.concatenate(sc, axis=0), xdt[:, base:base + LANES])
                y_parts.append(jnp.where(first_half, yd[:L], yd[L:])
                               + y_off[:, base - g * gw:base - g * gw + LANES])
        y = jnp.concatenate(y_parts, axis=1) + dsk_ref[...] * xs
        y = y * _silu(z_ref[s])
        outs = []
        for g in range(SSM_GROUPS):
            seg = y[:, g * gw:(g + 1) * gw]
            outs.append(seg * lax.rsqrt(jnp.mean(seg * seg, axis=-1, keepdims=True) + NORM_EPS))
        o_ref[s] = (jnp.concatenate(outs, axis=1) * nw_ref[...]).astype(o_ref.dtype)


def _pad_lanes(v, offset=0, width=LANES):
    out = jnp.zeros((1, width), F32)
    return lax.dynamic_update_slice(out, v.reshape(1, -1).astype(F32), (0, offset))


def _ssd_branch(pc, pn, a_log, dt_bias, d_skip, norm_w, bsz, seq):
    L = SSD_CHUNK
    nseq = SEQ_PER_STEP if bsz % SEQ_PER_STEP == 0 else 1
    pc3 = pc.reshape(bsz, seq, N_PC)
    pn3 = pn.reshape(bsz, seq, N_PN)
    ex = np.zeros((LANES, SSM_INNER), np.float32)
    for h in range(SSM_HEADS):
        ex[h, h * SSM_HEAD_DIM:(h + 1) * SSM_HEAD_DIM] = 1.0
    vec = lambda w: pl.BlockSpec((1, w), lambda b, c: (0, 0))
    tok = lambda width, col: pl.BlockSpec((nseq, L, width), lambda b, c: (b, c, col // width))
    out = pl.pallas_call(
        functools.partial(_ssd_kernel, L=L, nseq=nseq),
        grid=(bsz // nseq, seq // L),
        in_specs=[tok(SSM_INNER, PC_XS), tok(256, PC_BS), tok(256, PC_CS), tok(SSM_INNER, PN_ZSSM),
                  tok(LANES, PC_SMALL),
                  vec(LANES), vec(LANES), vec(SSM_INNER), vec(SSM_INNER),
                  pl.BlockSpec((LANES, SSM_INNER), lambda b, c: (0, 0))],
        out_specs=pl.BlockSpec((nseq, L, SSM_INNER), lambda b, c: (b, c, 0)),
        out_shape=jax.ShapeDtypeStruct((bsz, seq, SSM_INNER), BF16),
        scratch_shapes=[pltpu.VMEM((nseq, SSM_STATE, SSM_INNER), F32)],
        compiler_params=_cparams("parallel", "arbitrary"),
    )(pc3, pc3, pc3, pn3, pc3,
      _pad_lanes(a_log, SMALL_DT), _pad_lanes(dt_bias, SMALL_DT),
      jnp.repeat(d_skip.astype(F32), SSM_HEAD_DIM).reshape(1, SSM_INNER),
      norm_w.reshape(1, SSM_INNER).astype(F32), jnp.asarray(ex, BF16))
    return out.reshape(bsz * seq, SSM_INNER)


def _head_stack(x):
    half = x.shape[1] // 2
    lane = lax.broadcasted_iota(jnp.int32, x.shape, 1)
    return jnp.concatenate([jnp.where(lane < half, x, 0.0), jnp.where(lane >= half, x, 0.0)], axis=0)


def _dplr_pairs(units, *, L):
    nu = len(units)
    P = 2 * L
    vc = units[0]["v"].shape[1]
    row = lax.broadcasted_iota(jnp.int32, (L, P), 0)
    tcol = lax.broadcasted_iota(jnp.int32, (L, P), 1)
    first = tcol < L
    tcol = jnp.where(first, tcol, tcol - L)
    strict, lower = tcol < row, tcol <= row

    sc = [_dot_nt(jnp.concatenate([u["a_sc"], u["r_sc"]], axis=0),
                  jnp.concatenate([_head_stack(u["b_sc"]), _head_stack(u["k_sc"])], axis=0))
          for u in units]
    v_st = [_head_stack(u["v"]) for u in units]

    def read(u):
        rows = jnp.concatenate([u["a_sr"], u["r_sr"]], axis=0)
        st = u["s_prev"]
        if len(st) == 1:
            return _dot_nt(rows, st[0])
        kh = rows.shape[1] // 2
        return jnp.concatenate([_dot_nt(rows[:, :kh], st[0]), _dot_nt(rows[:, kh:], st[1])], axis=1)

    reads = [read(u) for u in units]
    n_mat, a_rb, aks = [], [], []
    for i, u in enumerate(units):
        sb, sk = sc[i][:, :P], sc[i][:, P:]
        if u.get("dm_a") is None:
            n_mat.append(jnp.where(strict, sb[:L], 0.0))
            a_rb.append(jnp.where(lower, sb[L:], 0.0))
            aks.append(jnp.concatenate([jnp.where(strict, sk[:L], 0.0), jnp.where(lower, sk[L:], 0.0)], axis=0))
        else:
            n_mat.append(sb[:L] * u["dm_a"])
            a_rb.append(sb[L:] * u["dm_r"])
            aks.append(jnp.concatenate([sk[:L] * u["dm_a"], sk[L:] * u["dm_r"]], axis=0))
    av = [_dot(aks[i], v_st[i]) for i in range(nu)]
    x = [reads[i][:L] + av[i][:L] for i in range(nu)]
    y_part = [reads[i][L:] + av[i][L:] for i in range(nu)]

    steps = int(math.log2(L))
    for s in range(steps):
        x_st = [_head_stack(x[i]) for i in range(nu)]
        if s + 1 < steps:
            t = [_dot(n_mat[i], jnp.concatenate(
                    [x_st[i], jnp.concatenate([jnp.where(first, n_mat[i], 0.0),
                                               jnp.where(first, 0.0, n_mat[i])], axis=0)], axis=1))
                 for i in range(nu)]
            x = [x[i] + t[i][:, :vc] for i in range(nu)]
            n_mat = [t[i][:, vc:] for i in range(nu)]
        else:
            x = [x[i] + _dot(n_mat[i], x_st[i]) for i in range(nu)]
    ys = [y_part[i] + _dot(a_rb[i], _head_stack(x[i])) for i in range(nu)]

    s_new = []
    for i, u in enumerate(units):
        xv = jnp.concatenate([x[i], u["v"]], axis=0)
        bk = jnp.concatenate([u["b_dec"], u["k_dec"]], axis=0)
        st, pl_ = u["s_prev"], u["p_last"]
        if len(st) == 1:
            upd = _dot_tn(xv, bk)
            ri = lax.broadcasted_iota(jnp.int32, upd.shape, 0) // (upd.shape[0] // 2)
            ci = lax.broadcasted_iota(jnp.int32, upd.shape, 1) // (upd.shape[1] // 2)
            s_new.append([st[0] * pl_[0] + jnp.where(ri == ci, upd, 0.0)])
        else:
            vh, kh = xv.shape[1] // 2, bk.shape[1] // 2
            s_new.append([st[0] * pl_[0] + _dot_tn(xv[:, :vh], bk[:, :kh]),
                          st[1] * pl_[1] + _dot_tn(xv[:, vh:], bk[:, kh:])])
    return ys, s_new


def _gdn_kernel(q_ref, k_ref, v_ref, z_ref, sm_ref, alog_ref, dtb_ref, nw_ref, o_ref, st_ref, *, L, nseq):
    @pl.when(pl.program_id(1) == 0)
    def _():
        st_ref[...] = jnp.zeros(st_ref.shape, F32)

    hd = GDN_HEAD_DIM
    row = lax.broadcasted_iota(jnp.int32, (L, 2 * L), 0)
    tcol = lax.broadcasted_iota(jnp.int32, (L, 2 * L), 1)
    first_t = tcol < L
    tcol = jnp.where(first_t, tcol, tcol - L)
    first_c = lax.broadcasted_iota(jnp.int32, (L, 2 * hd), 1) < hd
    nw = nw_ref[...]
    scale = hd ** -0.5
    units = []
    for s in range(nseq):
        sm = sm_ref[s]
        g = -jnp.exp(alog_ref[...]) * _softplus(sm + dtb_ref[...])
        beta = _sigmoid(sm)
        g_cum = _dot_w3(_tri_incl(L), g)
        g_cum_t = jnp.concatenate([g_cum] * (LANES // L), axis=0).T
        for p in range(GDN_HEADS // 2):
            sl = slice(2 * p * hd, (2 * p + 2) * hd)
            q = q_ref[s, :, sl]
            k = k_ref[s, :, sl]

            def per_head(t):
                return jnp.where(first_c, jnp.sum(t[:, :hd], axis=-1, keepdims=True),
                                 jnp.sum(t[:, hd:], axis=-1, keepdims=True))

            def col(arr, base):
                h0 = base + 2 * p
                return arr[:, h0:h0 + 1], arr[:, h0 + 1:h0 + 2]

            qn = q * (lax.rsqrt(per_head(q * q) + L2_EPS) * scale)
            kn = k * lax.rsqrt(per_head(k * k) + L2_EPS)
            gc0, gc1 = col(g_cum, SMALL_A)
            gs0, gs1 = col(g, SMALL_A)
            bt0, bt1 = col(beta, SMALL_B)
            ch = lambda a0, a1: jnp.where(first_c, a0, a1)
            tm_ = lambda a0, a1: jnp.where(first_t, a0, a1)
            gc, gs, bt = ch(gc0, gc1), ch(gs0, gs1), ch(bt0, bt1)
            g_prev = gc - gs
            h0 = SMALL_A + 2 * p
            gr = tm_(g_cum_t[h0:h0 + 1, :], g_cum_t[h0 + 1:h0 + 2, :])
            ge0, ge1 = g_cum[L - 1:L, h0:h0 + 1], g_cum[L - 1:L, h0 + 1:h0 + 2]
            e_end = jnp.exp(ch(ge0, ge1) - gc)
            b_vec = -(bt * jnp.exp(gs)) * kn
            k_vec = bt * kn
            units.append(dict(
                a_sc=kn, r_sc=qn, b_sc=b_vec, k_sc=k_vec,
                a_sr=kn * jnp.exp(g_prev), r_sr=qn * jnp.exp(gc),
                b_dec=b_vec * e_end, k_dec=k_vec * e_end,
                v=v_ref[s, :, sl], s_prev=[st_ref[s, 2 * p], st_ref[s, 2 * p + 1]],
                p_last=[jnp.exp(ge0), jnp.exp(ge1)],
                dm_a=jnp.exp(jnp.where(tcol < row, tm_(gc0 - gs0, gc1 - gs1) - gr, -jnp.inf)),
                dm_r=jnp.exp(jnp.where(tcol <= row, tm_(gc0, gc1) - gr, -jnp.inf))))
    ys, s_new = _dplr_pairs(units, L=L)
    npair = GDN_HEADS // 2
    for s in range(nseq):
        for p in range(npair):
            sl = slice(2 * p * hd, (2 * p + 2) * hd)
            st_ref[s, 2 * p] = s_new[s * npair + p][0]
            st_ref[s, 2 * p + 1] = s_new[s * npair + p][1]
            y = ys[s * npair + p]
            ms = jnp.where(first_c, jnp.mean(y[:, :hd] * y[:, :hd], axis=-1, keepdims=True),
                           jnp.mean(y[:, hd:] * y[:, hd:], axis=-1, keepdims=True))
            yn = y * lax.rsqrt(ms + NORM_EPS) * nw
            o_ref[s, :, sl] = (yn * _silu(z_ref[s, :, sl])).astype(o_ref.dtype)


def _gdn_branch(pc, pn, a_log, dt_bias, norm_w, bsz, seq):
    L = DPLR_CHUNK
    nseq = SEQ_PER_STEP if bsz % SEQ_PER_STEP == 0 else 1
    pc3 = pc.reshape(bsz, seq, N_PC)
    pn3 = pn.reshape(bsz, seq, N_PN)
    vec = lambda w: pl.BlockSpec((1, w), lambda b, c: (0, 0))
    tok = lambda width, col: pl.BlockSpec((nseq, L, width), lambda b, c: (b, c, col // width))
    out = pl.pallas_call(
        functools.partial(_gdn_kernel, L=L, nseq=nseq),
        grid=(bsz // nseq, seq // L),
        in_specs=[tok(GDN_INNER, PC_Q), tok(GDN_INNER, PC_K), tok(GDN_INNER, PC_V),
                  tok(GDN_INNER, PN_ZGDN), tok(LANES, PC_SMALL),
                  vec(LANES), vec(LANES), vec(2 * GDN_HEAD_DIM)],
        out_specs=pl.BlockSpec((nseq, L, GDN_INNER), lambda b, c: (b, c, 0)),
        out_shape=jax.ShapeDtypeStruct((bsz, seq, GDN_INNER), BF16),
        scratch_shapes=[pltpu.VMEM((nseq, GDN_HEADS, LANES, LANES), F32)],
        compiler_params=_cparams("parallel", "arbitrary"),
    )(pc3, pc3, pc3, pn3, pc3,
      _pad_lanes(a_log, SMALL_A), _pad_lanes(dt_bias, SMALL_A),
      jnp.tile(norm_w.reshape(1, GDN_HEAD_DIM).astype(F32), (1, 2)))
    return out.reshape(bsz * seq, GDN_INNER)


def _rwkv_kernel(rkv_ref, lora_ref, mu_ref, mul_ref, w0_ref, w2_ref, a0_ref, a2_ref, g2_ref,
                 kk_ref, ka_ref, rk_ref, lnw_ref, lnb_ref, o_ref, sh_ref, shl_ref, st_ref, *, L, nseq):
    @pl.when(pl.program_id(1) == 0)
    def _():
        st_ref[...] = jnp.zeros(st_ref.shape, F32)
        for s in range(nseq):
            sh_ref[s, 0:SUBLANES, :] = jnp.zeros((SUBLANES, sh_ref.shape[2]), F32)
            shl_ref[s, 0:SUBLANES, :] = jnp.zeros((SUBLANES, shl_ref.shape[2]), F32)

    n = RWKV_INNER
    npair = RWKV_HEADS // 2
    ones_bd = _half_ones()
    inv_hd = 1.0 / RWKV_HEAD_DIM

    def head_sums(t):
        st = jnp.concatenate([t[:, p * LANES:(p + 1) * LANES] for p in range(npair)], axis=0)
        sm = _dot(st, ones_bd)
        return jnp.concatenate([sm[p * L:(p + 1) * L] for p in range(npair)], axis=1)

    units, tails = [], []
    for s in range(nseq):
        u = rkv_ref[s]
        ul = lora_ref[s]
        sh_ref[s, SUBLANES:SUBLANES + L, :] = u
        shl_ref[s, SUBLANES:SUBLANES + L, :] = ul
        u = u + (sh_ref[s, pl.ds(SUBLANES - 1, L), :] - u) * mu_ref[...]
        ul = ul + (shl_ref[s, pl.ds(SUBLANES - 1, L), :] - ul) * mul_ref[...]
        sh_ref[s, 0:SUBLANES, :] = sh_ref[s, L:L + SUBLANES, :]
        shl_ref[s, 0:SUBLANES, :] = shl_ref[s, L:L + SUBLANES, :]

        r, k, v = u[:, 0:n], u[:, n:2 * n], u[:, 2 * n:3 * n]
        lo = ul[:, 0:LANES]
        dg = ul[:, LANES:2 * LANES]
        w_log = -_softplus(-(w0_ref[...] + _dot_hl(jnp.tanh(lo), w2_ref[...]))) - 0.5
        lw = -jnp.exp(w_log)
        a_ic = _sigmoid(a0_ref[...] + _dot_hl(lo, a2_ref[...]))
        gate = _dot(_sigmoid(dg), g2_ref[...])
        g_cum = _dot_w3(_tri_incl(L), lw)
        kk_raw = k * kk_ref[...]
        k_mod = k * (1.0 + (a_ic - 1.0) * ka_ref[...])
        kk_all = kk_raw * lax.rsqrt(head_sums(kk_raw * kk_raw) + L2_EPS)
        tails.append((head_sums(r * k_mod * rk_ref[...]) * v, gate))
        for p in range(npair):
            sl = slice(p * LANES, (p + 1) * LANES)
            kk = kk_all[:, sl]
            gc = g_cum[:, sl]
            g_end = gc[L - 1:L, :]
            e_neg = jnp.exp(-gc)
            e_end = jnp.exp(g_end - gc)
            b_vec = kk * a_ic[:, sl]
            k_vec = k_mod[:, sl]
            a_t = -kk * jnp.exp(gc - lw[:, sl])
            r_t = r[:, sl] * jnp.exp(gc)
            units.append(dict(
                a_sc=a_t, r_sc=r_t, b_sc=b_vec * e_neg, k_sc=k_vec * e_neg,
                a_sr=a_t, r_sr=r_t, b_dec=b_vec * e_end, k_dec=k_vec * e_end,
                v=v[:, sl], s_prev=[st_ref[s, p]], p_last=[jnp.exp(g_end)]))
    ys, s_new = _dplr_pairs(units, L=L)
    for s in range(nseq):
        for p in range(npair):
            st_ref[s, p] = s_new[s * npair + p][0]
        y = jnp.concatenate(ys[s * npair:(s + 1) * npair], axis=1)
        yc = y - head_sums(y) * inv_hd
        var = head_sums(yc * yc) * inv_hd
        yn = yc * lax.rsqrt(var + RWKV_GN_EPS) * lnw_ref[...] + lnb_ref[...]
        bonus, gate = tails[s]
        o_ref[s] = ((yn + bonus) * gate).astype(o_ref.dtype)


def _rwkv_branch(pc, pn, mu, w0, w2, a0, a2, g2, k_k, k_a, r_k, ln_w, ln_b, bsz, seq):
    L = DPLR_CHUNK
    n = RWKV_INNER
    nseq = SEQ_PER_STEP if bsz % SEQ_PER_STEP == 0 else 1
    pc3 = pc.reshape(bsz, seq, N_PC)
    pn3 = pn.reshape(bsz, seq, N_PN)
    vec = lambda w: pl.BlockSpec((1, w), lambda b, c: (0, 0))
    mat = lambda: pl.BlockSpec((LANES, n), lambda b, c: (0, 0))
    tok = lambda width, col: pl.BlockSpec((nseq, L, width), lambda b, c: (b, c, col // width))
    w2p = jnp.concatenate([w2, jnp.zeros((LANES - RWKV_DECAY_LORA, n), F32)], axis=0)
    a2p = jnp.concatenate([jnp.zeros((RWKV_DECAY_LORA, n), F32), a2], axis=0)
    r1 = lambda t: t.reshape(1, -1).astype(F32)
    out = pl.pallas_call(
        functools.partial(_rwkv_kernel, L=L, nseq=nseq),
        grid=(bsz // nseq, seq // L),
        in_specs=[tok(3 * n, PN_RKV), tok(256, PC_LORA),
                  vec(3 * n), vec(256), vec(n), mat(), vec(n), mat(), mat(),
                  vec(n), vec(n), vec(n), vec(n), vec(n)],
        out_specs=pl.BlockSpec((nseq, L, n), lambda b, c: (b, c, 0)),
        out_shape=jax.ShapeDtypeStruct((bsz, seq, n), BF16),
        scratch_shapes=[pltpu.VMEM((nseq, L + SUBLANES, 3 * n), F32),
                        pltpu.VMEM((nseq, L + SUBLANES, 256), F32),
                        pltpu.VMEM((nseq, RWKV_HEADS // 2, LANES, LANES), F32)],
        compiler_params=_cparams("parallel", "arbitrary"),
    )(pn3, pc3, r1(mu[:3 * n]), r1(mu[3 * n:]), r1(w0), w2p, r1(a0), a2p, g2.astype(F32),
      r1(k_k), r1(k_a), r1(r_k), r1(ln_w), r1(ln_b))
    return out.reshape(bsz * seq, n)


def _merge_kernel(x_ref, ys_ref, yg_ref, yr_ref, gate_ref, ps_ref, pg_ref, pr_ref, wo_ref, o_ref):
    d = D_MODEL
    m = _sigmoid(gate_ref[:, 0:d]) * jnp.dot(ys_ref[...], ps_ref[...], preferred_element_type=F32)
    m = m + _sigmoid(gate_ref[:, d:2 * d]) * jnp.dot(yg_ref[...], pg_ref[...], preferred_element_type=F32)
    m = m + _sigmoid(gate_ref[:, 2 * d:3 * d]) * jnp.dot(yr_ref[...], pr_ref[...], preferred_element_type=F32)
    o_ref[...] = x_ref[...] + jnp.dot(m.astype(BF16), wo_ref[...], preferred_element_type=F32)


def _merge(x, y_ssm, y_gdn, y_rwkv, pn, p_ssm, p_gdn, p_rwkv, w_out, tm=512):
    t, d = x.shape
    tok = lambda: pl.BlockSpec((tm, d), lambda i: (i, 0))
    wgt = lambda: pl.BlockSpec((d, d), lambda i: (0, 0))
    return pl.pallas_call(
        _merge_kernel,
        grid=(t // tm,),
        in_specs=[tok(), tok(), tok(), tok(),
                  pl.BlockSpec((tm, 3 * d), lambda i: (i, PN_GATE // (3 * d))),
                  wgt(), wgt(), wgt(), wgt()],
        out_specs=tok(),
        out_shape=jax.ShapeDtypeStruct((t, d), F32),
        compiler_params=_cparams("parallel"),
    )(x, y_ssm, y_gdn, y_rwkv, pn, p_ssm, p_gdn, p_rwkv, w_out)


def _ffn_kernel(x_ref, nw_ref, wg_ref, wu_ref, wd_ref, *rest, emit_next):
    if emit_next:
        nnw_ref, o_ref, hn_ref, h_ref, acc_ref = rest
    else:
        o_ref, h_ref, acc_ref = rest
    f = pl.program_id(1)

    @pl.when(f == 0)
    def _():
        x = x_ref[...]
        ms = jnp.mean(x * x, axis=-1, keepdims=True)
        h_ref[...] = (x * lax.rsqrt(ms + NORM_EPS) * nw_ref[...]).astype(BF16)
        acc_ref[...] = x

    h = h_ref[...]
    act = _silu(jnp.dot(h, wg_ref[...], preferred_element_type=F32)) * jnp.dot(
        h, wu_ref[...], preferred_element_type=F32)
    acc_ref[...] += jnp.dot(act.astype(BF16), wd_ref[...], preferred_element_type=F32)

    @pl.when(f == pl.num_programs(1) - 1)
    def _():
        y = acc_ref[...]
        o_ref[...] = y
        if emit_next:
            ms = jnp.mean(y * y, axis=-1, keepdims=True)
            hn_ref[...] = (y * lax.rsqrt(ms + NORM_EPS) * nnw_ref[...]).astype(BF16)


def _ffn_dense(x, norm_w, w_gate, w_up, w_down, next_norm_w=None, tm=512, tf=2816):
    t, d = x.shape
    nf = w_gate.shape[1] // tf
    emit_next = next_norm_w is not None
    tok = pl.BlockSpec((tm, d), lambda i, f: (i, 0))
    vec = pl.BlockSpec((1, d), lambda i, f: (0, 0))
    in_specs = [tok, vec,
                pl.BlockSpec((d, tf), lambda i, f: (0, f)),
                pl.BlockSpec((d, tf), lambda i, f: (0, f)),
                pl.BlockSpec((tf, d), lambda i, f: (f, 0))]
    args = [x, norm_w.reshape(1, d).astype(F32), w_gate, w_up, w_down]
    out_specs, out_shape = tok, jax.ShapeDtypeStruct((t, d), F32)
    if emit_next:
        in_specs.append(vec)
        args.append(next_norm_w.reshape(1, d).astype(F32))
        out_specs, out_shape = [tok, tok], [out_shape, jax.ShapeDtypeStruct((t, d), BF16)]
    return pl.pallas_call(
        functools.partial(_ffn_kernel, emit_next=emit_next),
        grid=(t // tm, nf),
        in_specs=in_specs,
        out_specs=out_specs,
        out_shape=out_shape,
        scratch_shapes=[pltpu.VMEM((tm, d), BF16), pltpu.VMEM((tm, d), F32)],
        compiler_params=_cparams("parallel", "arbitrary"),
    )(*args)


MOE_TM = 1024
MOE_BLK = 128
MOE_FIRST_ROWS = 288


def _route_kernel(x_ref, nw_ref, rt_ref, h_ref, cmb_ref, cnt_ref):
    x = x_ref[...]
    lane = lax.broadcasted_iota(jnp.int32, cmb_ref.shape, 1)
    ms = jnp.mean(x * x, axis=-1, keepdims=True)
    hf = x * lax.rsqrt(ms + NORM_EPS) * nw_ref[...]
    h_ref[...] = hf.astype(BF16)
    logits = jnp.where(lane < N_EXPERTS, _dot_hl(hf, rt_ref[...]), -jnp.inf)
    m1 = jnp.max(logits, axis=-1, keepdims=True)
    i1 = jnp.min(jnp.where(logits == m1, lane, LANES), axis=-1, keepdims=True)
    rest = jnp.where(lane == i1, -jnp.inf, logits)
    m2 = jnp.max(rest, axis=-1, keepdims=True)
    i2 = jnp.min(jnp.where(rest == m2, lane, LANES), axis=-1, keepdims=True)
    e2 = jnp.exp(m2 - m1)
    p1 = 1.0 / (1.0 + e2)
    cmb = jnp.where(lane == i1, p1, 0.0) + jnp.where(lane == i2, e2 * p1, 0.0)
    cmb_ref[...] = cmb
    cnt_ref[0] = jnp.sum(jnp.where(cmb > 0.0, 1.0, 0.0), axis=0, keepdims=True).astype(jnp.int32)


def _moe_route(x, norm_w, router, tm):
    t, d = x.shape
    ne = router.shape[1]
    rt = jnp.concatenate([router.astype(F32), jnp.zeros((d, LANES - ne), F32)], axis=1)
    return pl.pallas_call(
        _route_kernel,
        grid=(t // tm,),
        in_specs=[pl.BlockSpec((tm, d), lambda i: (i, 0)),
                  pl.BlockSpec((1, d), lambda i: (0, 0)),
                  pl.BlockSpec((d, LANES), lambda i: (0, 0))],
        out_specs=[pl.BlockSpec((tm, d), lambda i: (i, 0)),
                   pl.BlockSpec((tm, LANES), lambda i: (i, 0)),
                   pl.BlockSpec((1, 1, LANES), lambda i: (i, 0, 0))],
        out_shape=[jax.ShapeDtypeStruct((t, d), BF16),
                   jax.ShapeDtypeStruct((t, LANES), F32),
                   jax.ShapeDtypeStruct((t // tm, 1, LANES), jnp.int32)],
        compiler_params=_cparams("parallel"),
    )(x, norm_w.reshape(1, d).astype(F32), rt)


def _moe_kernel(cnt_ref, x_ref, h_ref, cmb_ref, wg_ref, wu_ref, wd_ref, fw_ref, o_ref,
                rank_ref, xc_ref, yc_ref, oh_ref, *, tm, final_norm):
    i = pl.program_id(0)
    e = pl.program_id(1)
    f = pl.program_id(2)
    last_f = f == pl.num_programs(2) - 1
    lane = lax.broadcasted_iota(jnp.int32, (tm, LANES), 1)

    @pl.when((e == 0) & (f == 0))
    def _():
        o_ref[...] = jnp.zeros(o_ref.shape, F32)
        r = lax.broadcasted_iota(jnp.int32, (tm, tm), 0)
        c = lax.broadcasted_iota(jnp.int32, (tm, tm), 1)
        before = jnp.where(c < r, 1.0, 0.0).astype(BF16)
        sel = jnp.where(cmb_ref[...] > 0.0, 1.0, 0.0).astype(BF16)
        rank_ref[...] = jnp.dot(before, sel, preferred_element_type=F32)

    cnt = cnt_ref[i, e]

    def weight():
        return jnp.sum(jnp.where(lane == e, cmb_ref[...], 0.0), axis=-1, keepdims=True)

    def one_hot(off, rows):
        r_e = jnp.sum(jnp.where(lane == e, rank_ref[...], 0.0), axis=-1, keepdims=True)
        key = jnp.where(weight() > 0.0, r_e, -1.0) - jnp.asarray(off, F32)
        slot = lax.broadcasted_iota(jnp.int32, (tm, rows), 1).astype(F32)
        return jnp.where(key == slot, 1.0, 0.0).astype(BF16)

    def process(off, rows, cache):
        rs = pl.ds(off, rows)

        @pl.when(f == 0)
        def _():
            onehot = one_hot(off, rows)
            if cache:
                oh_ref[...] = onehot
            xc_ref[rs, :] = _dot_tn(onehot, h_ref[...]).astype(BF16)
            yc_ref[rs, :] = jnp.zeros((rows, yc_ref.shape[1]), F32)

        xc = xc_ref[rs, :]
        act = _silu(jnp.dot(xc, wg_ref[0], preferred_element_type=F32)) * jnp.dot(
            xc, wu_ref[0], preferred_element_type=F32)
        yc_ref[rs, :] += jnp.dot(act.astype(BF16), wd_ref[0], preferred_element_type=F32)

        @pl.when(last_f)
        def _():
            onehot = oh_ref[...] if cache else one_hot(off, rows)
            o_ref[...] += weight() * jnp.dot(onehot, yc_ref[rs, :].astype(BF16), preferred_element_type=F32)

    @pl.when(cnt > 0)
    def _():
        process(0, MOE_FIRST_ROWS, True)

    def extra(b, carry):
        process(pl.multiple_of(MOE_FIRST_ROWS + b * MOE_BLK, 32), MOE_BLK, False)
        return carry

    lax.fori_loop(0, (jnp.maximum(cnt - MOE_FIRST_ROWS, 0) + MOE_BLK - 1) // MOE_BLK, extra, 0)

    @pl.when((e == pl.num_programs(1) - 1) & last_f)
    def _():
        y = x_ref[...] + o_ref[...]
        if final_norm:
            y = y * lax.rsqrt(jnp.mean(y * y, axis=-1, keepdims=True) + NORM_EPS) * fw_ref[...]
        o_ref[...] = y


def _ffn_moe(x, norm_w, router, w_gate, w_up, w_down, final_w, tm=MOE_TM, tf=1408):
    t, d = x.shape
    ne, _, ff = w_gate.shape
    nf = ff // tf
    h, cmb, cnt = _moe_route(x, norm_w, router, tm)
    final_norm = final_w is not None
    fw = (final_w if final_norm else jnp.ones((d,), F32)).reshape(1, d).astype(F32)
    cap = MOE_FIRST_ROWS + -(-max(tm - MOE_FIRST_ROWS, 0) // MOE_BLK) * MOE_BLK
    grid_spec = pltpu.PrefetchScalarGridSpec(
        num_scalar_prefetch=1,
        grid=(t // tm, ne, nf),
        in_specs=[pl.BlockSpec((tm, d), lambda i, e, f, c: (i, 0)),
                  pl.BlockSpec((tm, d), lambda i, e, f, c: (i, 0)),
                  pl.BlockSpec((tm, LANES), lambda i, e, f, c: (i, 0)),
                  pl.BlockSpec((1, d, tf), lambda i, e, f, c: (e, 0, f)),
                  pl.BlockSpec((1, d, tf), lambda i, e, f, c: (e, 0, f)),
                  pl.BlockSpec((1, tf, d), lambda i, e, f, c: (e, f, 0)),
                  pl.BlockSpec((1, d), lambda i, e, f, c: (0, 0))],
        out_specs=pl.BlockSpec((tm, d), lambda i, e, f, c: (i, 0)),
        scratch_shapes=[pltpu.VMEM((tm, LANES), F32), pltpu.VMEM((cap, d), BF16), pltpu.VMEM((cap, d), F32),
                        pltpu.VMEM((tm, MOE_FIRST_ROWS), BF16)])
    return pl.pallas_call(
        functools.partial(_moe_kernel, tm=tm, final_norm=final_norm),
        grid_spec=grid_spec,
        out_shape=jax.ShapeDtypeStruct((t, d), F32),
        compiler_params=_cparams("parallel", "arbitrary", "arbitrary"),
    )(cnt.reshape(t // tm, LANES), x, h, cmb, w_gate, w_up, w_down, fw)


def _reorder_kernel(w_ref, wc_ref, wn_ref):
    w = w_ref[0]
    wc = [w[:, 0:1024], w[:, 1536:4608], w[:, 1024:1536], w[:, 9760:10016], w[:, 5632:5648],
          w[:, 6672:6688], jnp.zeros((w.shape[0], N_PC - PC_SMALL - 32), w.dtype)]
    wn = [w[:, 6688:9760], w[:, 10016:13088], w[:, 4608:5632], w[:, 5648:6672]]
    wc_ref[...] = jnp.concatenate(wc, axis=1).astype(BF16)
    wn_ref[...] = jnp.concatenate(wn, axis=1).astype(BF16)


def _reorder_w_in(w, layer, tk=128):
    _, d, n = w.shape
    return pl.pallas_call(
        _reorder_kernel,
        grid=(d // tk,),
        in_specs=[pl.BlockSpec((1, tk, n), lambda i: (layer, i, 0))],
        out_specs=[pl.BlockSpec((tk, N_PC), lambda i: (i, 0)), pl.BlockSpec((tk, N_PN), lambda i: (i, 0))],
        out_shape=[jax.ShapeDtypeStruct((d, N_PC), BF16), jax.ShapeDtypeStruct((d, N_PN), BF16)],
        compiler_params=_cparams("parallel"),
    )(w)


def _token_mixer_layer(x, h, i, bsz, seq, p):
    if h is None:
        h = _rmsnorm(x, p["attn_norm_w"][i], BF16)
    wc, wn = _reorder_w_in(p["w_in"], i)
    pc, pn = _inproj(h, wc, wn, p["conv_w"][i], p["conv_b"][i], seq, tm=1024)
    y_ssm = _ssd_branch(pc, pn, p["ssm_a_log"][i], p["ssm_dt_bias"][i], p["ssm_d"][i],
                        p["ssm_norm_w"][i], bsz, seq)
    y_gdn = _gdn_branch(pc, pn, p["gdn_a_log"][i], p["gdn_dt_bias"][i], p["gdn_norm_w"][i],
                        bsz, seq)
    y_rwkv = _rwkv_branch(pc, pn, p["rwkv_mu"][i], p["rwkv_w0"][i], p["rwkv_w2"][i], p["rwkv_a0"][i],
                          p["rwkv_a2"][i], p["rwkv_g2"][i], p["rwkv_k_k"][i], p["rwkv_k_a"][i],
                          p["rwkv_r_k"][i], p["rwkv_ln_w"][i], p["rwkv_ln_b"][i], bsz, seq)
    return _merge(x, y_ssm, y_gdn, y_rwkv, pn, p["proj_ssm"][i], p["proj_gdn"][i], p["proj_rwkv"][i],
                  p["w_out"][i])


def kernel(x, attn_norm_w, w_in, conv_w, conv_b, ssm_a_log, ssm_dt_bias, ssm_d, ssm_norm_w, gdn_a_log, gdn_dt_bias, gdn_norm_w, rwkv_mu, rwkv_w0, rwkv_w2, rwkv_a0, rwkv_a2, rwkv_g2, rwkv_k_k, rwkv_k_a, rwkv_r_k, rwkv_ln_w, rwkv_ln_b, proj_ssm, proj_gdn, proj_rwkv, w_out, ffn_norm_w, dense_w_gate, dense_w_up, dense_w_down, moe_router, moe_w_gate, moe_w_up, moe_w_down, final_norm_w):
    p = dict(attn_norm_w=attn_norm_w, w_in=w_in, conv_w=conv_w, conv_b=conv_b, ssm_a_log=ssm_a_log,
             ssm_dt_bias=ssm_dt_bias, ssm_d=ssm_d, ssm_norm_w=ssm_norm_w, gdn_a_log=gdn_a_log,
             gdn_dt_bias=gdn_dt_bias, gdn_norm_w=gdn_norm_w, rwkv_mu=rwkv_mu, rwkv_w0=rwkv_w0,
             rwkv_w2=rwkv_w2, rwkv_a0=rwkv_a0, rwkv_a2=rwkv_a2, rwkv_g2=rwkv_g2, rwkv_k_k=rwkv_k_k,
             rwkv_k_a=rwkv_k_a, rwkv_r_k=rwkv_r_k, rwkv_ln_w=rwkv_ln_w, rwkv_ln_b=rwkv_ln_b,
             proj_ssm=proj_ssm, proj_gdn=proj_gdn, proj_rwkv=proj_rwkv, w_out=w_out)
    for name in ("proj_ssm", "proj_gdn", "proj_rwkv", "w_out"):
        p[name] = p[name].astype(BF16)
    dense_w_gate, dense_w_up, dense_w_down = (t.astype(BF16) for t in (dense_w_gate, dense_w_up, dense_w_down))
    moe_w_gate, moe_w_up, moe_w_down = (t.astype(BF16) for t in (moe_w_gate, moe_w_up, moe_w_down))
    bsz, seq, d = x.shape
    depth = attn_norm_w.shape[0]
    xt = x.reshape(bsz * seq, d)
    h_next = None
    for i in range(depth):
        xt = _token_mixer_layer(xt, h_next, i, bsz, seq, p)
        h_next = None
        j = i // 2
        last = i == depth - 1
        if i % 2 == 0:
            if last:
                xt = _ffn_dense(xt, ffn_norm_w[i], dense_w_gate[j], dense_w_up[j], dense_w_down[j])
                xt = _rmsnorm(xt, final_norm_w, F32)
            else:
                xt, h_next = _ffn_dense(xt, ffn_norm_w[i], dense_w_gate[j], dense_w_up[j], dense_w_down[j],
                                        next_norm_w=attn_norm_w[i + 1])
        else:
            xt = _ffn_moe(xt, ffn_norm_w[i], moe_router[j], moe_w_gate[j], moe_w_up[j], moe_w_down[j],
                          final_norm_w if last else None)
    return xt.reshape(bsz, seq, d)
```

```python
import functools
import math

import numpy as np
import jax
import jax.numpy as jnp
from jax import lax
from jax.experimental import pallas as pl
from jax.experimental.pallas import tpu as pltpu

F32 = jnp.float32
BF16 = jnp.bfloat16

LANES = 128
SUBLANES = 8
VMEM_LIMIT_BYTES = 56 * 1024 * 1024

D_MODEL = 1024
SSM_HEADS, SSM_HEAD_DIM, SSM_GROUPS, SSM_STATE = 16, 64, 2, 128
SSM_INNER = SSM_HEADS * SSM_HEAD_DIM
GDN_HEADS, GDN_HEAD_DIM = 8, 128
GDN_INNER = GDN_HEADS * GDN_HEAD_DIM
RWKV_HEADS, RWKV_HEAD_DIM = 16, 64
RWKV_INNER = RWKV_HEADS * RWKV_HEAD_DIM
RWKV_DECAY_LORA, RWKV_ICLR_LORA, RWKV_GATE_LORA = 64, 64, 128
RWKV_GN_EPS = 64e-5
CONV_K = 4
FFN_DIM = 2816
N_EXPERTS = 8
NORM_EPS = 1e-6
L2_EPS = 1e-6

PC_XS, PC_Q, PC_K, PC_V = 0, 1024, 2048, 3072
PC_BS, PC_CS = 4096, 4352
PC_CONV_END = 4608
PC_LORA = 4608
PC_SMALL = 4864
N_PC = 5120
PN_RKV = 0
PN_GATE = 3072
PN_ZSSM, PN_ZGDN = 6144, 7168
N_PN = 8192
SMALL_DT, SMALL_A, SMALL_B = 0, 16, 24

SSD_CHUNK = 128
DPLR_CHUNK = 64
SEQ_PER_STEP = 4
INPROJ_ROW_CHUNK = 512


def _cparams(*sem):
    return pltpu.CompilerParams(dimension_semantics=sem, vmem_limit_bytes=VMEM_LIMIT_BYTES)


def _dot(a, b):
    return jnp.dot(a.astype(BF16), b.astype(BF16), preferred_element_type=F32)


def _dot_nt(a, b):
    return lax.dot_general(a.astype(BF16), b.astype(BF16), (((1,), (1,)), ((), ())),
                           preferred_element_type=F32)


def _dot_tn(a, b):
    return lax.dot_general(a.astype(BF16), b.astype(BF16), (((0,), (0,)), ((), ())),
                           preferred_element_type=F32)


def _split3(x):
    hi = x.astype(BF16)
    r1 = x - hi.astype(F32)
    mid = r1.astype(BF16)
    lo = (r1 - mid.astype(F32)).astype(BF16)
    return hi, mid, lo


def _dot_x3(x, w):
    hi, mid, lo = _split3(x)
    return _dot(hi, w) + _dot(mid, w) + _dot(lo, w)


def _dot_w3(w, x):
    hi, mid, lo = _split3(x)
    return _dot(w, hi) + _dot(w, mid) + _dot(w, lo)


def _dot_hl(a, b):
    ah = a.astype(BF16)
    al = (a - ah.astype(F32)).astype(BF16)
    bh = b.astype(BF16)
    bl = (b - bh.astype(F32)).astype(BF16)
    return _dot(ah, bh) + _dot(ah, bl) + _dot(al, bh)


def _sigmoid(x):
    return 0.5 * jnp.tanh(0.5 * x) + 0.5


def _silu(x):
    return x * _sigmoid(x)


def _softplus(x):
    return jnp.maximum(x, 0.0) + jnp.log(1.0 + jnp.exp(-jnp.abs(x)))


def _tri_incl(n):
    r = lax.broadcasted_iota(jnp.int32, (n, n), 0)
    c = lax.broadcasted_iota(jnp.int32, (n, n), 1)
    return jnp.where(r >= c, 1.0, 0.0).astype(BF16)


def _half_ones():
    r = lax.broadcasted_iota(jnp.int32, (LANES, LANES), 0)
    c = lax.broadcasted_iota(jnp.int32, (LANES, LANES), 1)
    return jnp.where((r // 64) == (c // 64), 1.0, 0.0).astype(BF16)


def _rmsnorm_kernel(x_ref, w_ref, o_ref):
    x = x_ref[...]
    ms = jnp.mean(x * x, axis=-1, keepdims=True)
    o_ref[...] = (x * lax.rsqrt(ms + NORM_EPS) * w_ref[...]).astype(o_ref.dtype)


def _rmsnorm(x, w, out_dtype, tm=512):
    t, d = x.shape
    return pl.pallas_call(
        _rmsnorm_kernel,
        grid=(t // tm,),
        in_specs=[pl.BlockSpec((tm, d), lambda i: (i, 0)),
                  pl.BlockSpec((1, d), lambda i: (0, 0))],
        out_specs=pl.BlockSpec((tm, d), lambda i: (i, 0)),
        out_shape=jax.ShapeDtypeStruct((t, d), out_dtype),
        compiler_params=_cparams("parallel"),
    )(x, w.reshape(1, d))


def _inproj_kernel(a_ref, wc_ref, wn_ref, cw_ref, cb_ref, cm_ref, oc_ref, on_ref, sh_ref, *,
                   tm, tiles_per_seq):
    i = pl.program_id(1)
    tc = oc_ref.shape[1]

    @pl.when(i % tiles_per_seq == 0)
    def _():
        sh_ref[0:SUBLANES, :] = jnp.zeros((SUBLANES, tc), F32)

    rc = min(tm, INPROJ_ROW_CHUNK)
    starts = list(range(0, tm, rc))
    acc = jnp.dot(a_ref[0:rc, :], wc_ref[...], preferred_element_type=F32)
    for n, r0 in enumerate(starts):
        sh_ref[SUBLANES + r0:SUBLANES + r0 + rc, :] = acc
        on_ref[r0:r0 + rc, :] = jnp.dot(a_ref[r0:r0 + rc, :], wn_ref[...], preferred_element_type=F32)
        if n + 1 < len(starts):
            acc_next = jnp.dot(a_ref[r0 + rc:r0 + 2 * rc, :], wc_ref[...], preferred_element_type=F32)
        c = acc * cw_ref[CONV_K - 1:CONV_K, :] + cb_ref[...]
        for k in range(1, CONV_K):
            c = c + sh_ref[pl.ds(SUBLANES + r0 - k, rc), :] * cw_ref[CONV_K - 1 - k:CONV_K - k, :]
        oc_ref[r0:r0 + rc, :] = jnp.where(cm_ref[...] > 0.0, _silu(c), acc)
        if n + 1 < len(starts):
            acc = acc_next
    sh_ref[0:SUBLANES, :] = sh_ref[tm:tm + SUBLANES, :]


def _conv_cols(c):
    r = c.shape[0]
    return jnp.concatenate([c[:, 0:1024], c[:, 1536:4608], c[:, 1024:1536],
                            jnp.zeros((r, N_PC - PC_CONV_END), F32)], axis=1).astype(F32)


def _inproj(a, wc, wn, conv_w, conv_b, seq, tm, nsteps=4):
    m, k = a.shape
    tc, tn = N_PC // nsteps, N_PN // nsteps
    cw = _conv_cols(conv_w)
    cb = _conv_cols(conv_b.reshape(1, -1))
    cm = _conv_cols(jnp.ones((1, conv_w.shape[1]), F32))
    return pl.pallas_call(
        functools.partial(_inproj_kernel, tm=tm, tiles_per_seq=seq // tm),
        grid=(nsteps, m // tm),
        in_specs=[pl.BlockSpec((tm, k), lambda j, i: (i, 0)),
                  pl.BlockSpec((k, tc), lambda j, i: (0, j)),
                  pl.BlockSpec((k, tn), lambda j, i: (0, j)),
                  pl.BlockSpec((CONV_K, tc), lambda j, i: (0, j)),
                  pl.BlockSpec((1, tc), lambda j, i: (0, j)),
                  pl.BlockSpec((1, tc), lambda j, i: (0, j))],
        out_specs=[pl.BlockSpec((tm, tc), lambda j, i: (i, j)),
                   pl.BlockSpec((tm, tn), lambda j, i: (i, j))],
        out_shape=[jax.ShapeDtypeStruct((m, N_PC), F32), jax.ShapeDtypeStruct((m, N_PN), F32)],
        scratch_shapes=[pltpu.VMEM((tm + SUBLANES, tc), F32)],
        compiler_params=_cparams("parallel", "arbitrary"),
    )(a, wc, wn, cw, cb, cm)


def _ssd_kernel(xs_ref, bs_ref, cs_ref, z_ref, sm_ref, alog_ref, dtb_ref, dsk_ref, nw_ref, ex_ref,
                o_ref, st_ref, *, L, nseq):
    @pl.when(pl.program_id(1) == 0)
    def _():
        st_ref[...] = jnp.zeros(st_ref.shape, F32)

    hd = SSM_HEAD_DIM
    gw = SSM_INNER // SSM_GROUPS
    heads_per_group = SSM_HEADS // SSM_GROUPS
    ex = ex_ref[...]
    row = lax.broadcasted_iota(jnp.int32, (L, L), 0)
    col = lax.broadcasted_iota(jnp.int32, (L, L), 1)
    causal = row >= col
    lane = lax.broadcasted_iota(jnp.int32, (L, LANES), 1)
    first_half = lane < hd
    for s in range(nseq):
        dt = _softplus(sm_ref[s] + dtb_ref[...])
        la = -jnp.exp(alog_ref[...]) * dt
        g_cum = _dot_w3(_tri_incl(L), la)
        g_cum_t = g_cum.T
        dt_e = _dot_x3(dt, ex)
        g_e = _dot_x3(g_cum, ex)
        g_last = g_e[L - 1:L, :]
        e_g = jnp.exp(g_e)
        e_dec = jnp.exp(g_last - g_e)
        e_last = jnp.exp(g_last)

        xs = xs_ref[s]
        xdt = xs * dt_e
        xdec = xdt * e_dec
        y_parts = []
        for g in range(SSM_GROUPS):
            gsl = slice(g * gw, (g + 1) * gw)
            b_g = bs_ref[s, :, g * SSM_STATE:(g + 1) * SSM_STATE]
            c_g = cs_ref[s, :, g * SSM_STATE:(g + 1) * SSM_STATE]
            cb = _dot_nt(c_g, b_g)
            st_g = st_ref[s, :, gsl]
            y_off = _dot(c_g, st_g) * e_g[:, gsl]
            st_ref[s, :, gsl] = st_g * e_last[:, gsl] + _dot_tn(b_g, xdec[:, gsl])
            for p in range(heads_per_group // 2):
                h0 = g * heads_per_group + 2 * p
                base = h0 * hd
                sc = []
                for h in (h0, h0 + 1):
                    diff = g_cum[:, h:h + 1] - g_cum_t[h:h + 1, :]
                    sc.append(cb * jnp.exp(jnp.where(causal, diff, -jnp.inf)))
                yd = _dot(jnp.concatenate(sc, axis=0), xdt[:, base:base + LANES])
                y_parts.append(jnp.where(first_half, yd[:L], yd[L:])
                               + y_off[:, base - g * gw:base - g * gw + LANES])
        y = jnp.concatenate(y_parts, axis=1) + dsk_ref[...] * xs
        y = y * _silu(z_ref[s])
        outs = []
        for g in range(SSM_GROUPS):
            seg = y[:, g * gw:(g + 1) * gw]
            outs.append(seg * lax.rsqrt(jnp.mean(seg * seg, axis=-1, keepdims=True) + NORM_EPS))
        o_ref[s] = (jnp.concatenate(outs, axis=1) * nw_ref[...]).astype(o_ref.dtype)


def _pad_lanes(v, offset=0, width=LANES):
    out = jnp.zeros((1, width), F32)
    return lax.dynamic_update_slice(out, v.reshape(1, -1).astype(F32), (0, offset))


def _ssd_branch(pc, pn, a_log, dt_bias, d_skip, norm_w, bsz, seq):
    L = SSD_CHUNK
    nseq = SEQ_PER_STEP if bsz % SEQ_PER_STEP == 0 else 1
    pc3 = pc.reshape(bsz, seq, N_PC)
    pn3 = pn.reshape(bsz, seq, N_PN)
    ex = np.zeros((LANES, SSM_INNER), np.float32)
    for h in range(SSM_HEADS):
        ex[h, h * SSM_HEAD_DIM:(h + 1) * SSM_HEAD_DIM] = 1.0
    vec = lambda w: pl.BlockSpec((1, w), lambda b, c: (0, 0))
    tok = lambda width, col: pl.BlockSpec((nseq, L, width), lambda b, c: (b, c, col // width))
    out = pl.pallas_call(
        functools.partial(_ssd_kernel, L=L, nseq=nseq),
        grid=(bsz // nseq, seq // L),
        in_specs=[tok(SSM_INNER, PC_XS), tok(256, PC_BS), tok(256, PC_CS), tok(SSM_INNER, PN_ZSSM),
                  tok(LANES, PC_SMALL),
                  vec(LANES), vec(LANES), vec(SSM_INNER), vec(SSM_INNER),
                  pl.BlockSpec((LANES, SSM_INNER), lambda b, c: (0, 0))],
        out_specs=pl.BlockSpec((nseq, L, SSM_INNER), lambda b, c: (b, c, 0)),
        out_shape=jax.ShapeDtypeStruct((bsz, seq, SSM_INNER), BF16),
        scratch_shapes=[pltpu.VMEM((nseq, SSM_STATE, SSM_INNER), F32)],
        compiler_params=_cparams("parallel", "arbitrary"),
    )(pc3, pc3, pc3, pn3, pc3,
      _pad_lanes(a_log, SMALL_DT), _pad_lanes(dt_bias, SMALL_DT),
      jnp.repeat(d_skip.astype(F32), SSM_HEAD_DIM).reshape(1, SSM_INNER),
      norm_w.reshape(1, SSM_INNER).astype(F32), jnp.asarray(ex, BF16))
    return out.reshape(bsz * seq, SSM_INNER)


def _head_stack(x):
    half = x.shape[1] // 2
    lane = lax.broadcasted_iota(jnp.int32, x.shape, 1)
    return jnp.concatenate([jnp.where(lane < half, x, 0.0), jnp.where(lane >= half, x, 0.0)], axis=0)


def _dplr_pairs(units, *, L):
    nu = len(units)
    P = 2 * L
    vc = units[0]["v"].shape[1]
    row = lax.broadcasted_iota(jnp.int32, (L, P), 0)
    tcol = lax.broadcasted_iota(jnp.int32, (L, P), 1)
    first = tcol < L
    tcol = jnp.where(first, tcol, tcol - L)
    strict, lower = tcol < row, tcol <= row

    sc = [_dot_nt(jnp.concatenate([u["a_sc"], u["r_sc"]], axis=0),
                  jnp.concatenate([_head_stack(u["b_sc"]), _head_stack(u["k_sc"])], axis=0))
          for u in units]
    v_st = [_head_stack(u["v"]) for u in units]

    def read(u):
        rows = jnp.concatenate([u["a_sr"], u["r_sr"]], axis=0)
        st = u["s_prev"]
        if len(st) == 1:
            return _dot_nt(rows, st[0])
        kh = rows.shape[1] // 2
        return jnp.concatenate([_dot_nt(rows[:, :kh], st[0]), _dot_nt(rows[:, kh:], st[1])], axis=1)

    reads = [read(u) for u in units]
    n_mat, a_rb, aks = [], [], []
    for i, u in enumerate(units):
        sb, sk = sc[i][:, :P], sc[i][:, P:]
        if u.get("dm_a") is None:
            n_mat.append(jnp.where(strict, sb[:L], 0.0))
            a_rb.append(jnp.where(lower, sb[L:], 0.0))
            aks.append(jnp.concatenate([jnp.where(strict, sk[:L], 0.0), jnp.where(lower, sk[L:], 0.0)], axis=0))
        else:
            n_mat.append(sb[:L] * u["dm_a"])
            a_rb.append(sb[L:] * u["dm_r"])
            aks.append(jnp.concatenate([sk[:L] * u["dm_a"], sk[L:] * u["dm_r"]], axis=0))
    av = [_dot(aks[i], v_st[i]) for i in range(nu)]
    x = [reads[i][:L] + av[i][:L] for i in range(nu)]
    y_part = [reads[i][L:] + av[i][L:] for i in range(nu)]

    steps = int(math.log2(L))
    for s in range(steps):
        x_st = [_head_stack(x[i]) for i in range(nu)]
        if s + 1 < steps:
            t = [_dot(n_mat[i], jnp.concatenate(
                    [x_st[i], jnp.concatenate([jnp.where(first, n_mat[i], 0.0),
                                               jnp.where(first, 0.0, n_mat[i])], axis=0)], axis=1))
                 for i in range(nu)]
            x = [x[i] + t[i][:, :vc] for i in range(nu)]
            n_mat = [t[i][:, vc:] for i in range(nu)]
        else:
            x = [x[i] + _dot(n_mat[i], x_st[i]) for i in range(nu)]
    ys = [y_part[i] + _dot(a_rb[i], _head_stack(x[i])) for i in range(nu)]

    s_new = []
    for i, u in enumerate(units):
        xv = jnp.concatenate([x[i], u["v"]], axis=0)
        bk = jnp.concatenate([u["b_dec"], u["k_dec"]], axis=0)
        st, pl_ = u["s_prev"], u["p_last"]
        if len(st) == 1:
            upd = _dot_tn(xv, bk)
            ri = lax.broadcasted_iota(jnp.int32, upd.shape, 0) // (upd.shape[0] // 2)
            ci = lax.broadcasted_iota(jnp.int32, upd.shape, 1) // (upd.shape[1] // 2)
            s_new.append([st[0] * pl_[0] + jnp.where(ri == ci, upd, 0.0)])
        else:
            vh, kh = xv.shape[1] // 2, bk.shape[1] // 2
            s_new.append([st[0] * pl_[0] + _dot_tn(xv[:, :vh], bk[:, :kh]),
                          st[1] * pl_[1] + _dot_tn(xv[:, vh:], bk[:, kh:])])
    return ys, s_new


def _gdn_kernel(q_ref, k_ref, v_ref, z_ref, sm_ref, alog_ref, dtb_ref, nw_ref, o_ref, st_ref, *, L, nseq):
    @pl.when(pl.program_id(1) == 0)
    def _():
        st_ref[...] = jnp.zeros(st_ref.shape, F32)

    hd = GDN_HEAD_DIM
    row = lax.broadcasted_iota(jnp.int32, (L, 2 * L), 0)
    tcol = lax.broadcasted_iota(jnp.int32, (L, 2 * L), 1)
    first_t = tcol < L
    tcol = jnp.where(first_t, tcol, tcol - L)
    first_c = lax.broadcasted_iota(jnp.int32, (L, 2 * hd), 1) < hd
    nw = nw_ref[...]
    scale = hd ** -0.5
    units = []
    for s in range(nseq):
        sm = sm_ref[s]
        g = -jnp.exp(alog_ref[...]) * _softplus(sm + dtb_ref[...])
        beta = _sigmoid(sm)
        g_cum = _dot_w3(_tri_incl(L), g)
        g_cum_t = jnp.concatenate([g_cum] * (LANES // L), axis=0).T
        for p in range(GDN_HEADS // 2):
            sl = slice(2 * p * hd, (2 * p + 2) * hd)
            q = q_ref[s, :, sl]
            k = k_ref[s, :, sl]

            def per_head(t):
                return jnp.where(first_c, jnp.sum(t[:, :hd], axis=-1, keepdims=True),
                                 jnp.sum(t[:, hd:], axis=-1, keepdims=True))

            def col(arr, base):
                h0 = base + 2 * p
                return arr[:, h0:h0 + 1], arr[:, h0 + 1:h0 + 2]

            qn = q * (lax.rsqrt(per_head(q * q) + L2_EPS) * scale)
            kn = k * lax.rsqrt(per_head(k * k) + L2_EPS)
            gc0, gc1 = col(g_cum, SMALL_A)
            gs0, gs1 = col(g, SMALL_A)
            bt0, bt1 = col(beta, SMALL_B)
            ch = lambda a0, a1: jnp.where(first_c, a0, a1)
            tm_ = lambda a0, a1: jnp.where(first_t, a0, a1)
            gc, gs, bt = ch(gc0, gc1), ch(gs0, gs1), ch(bt0, bt1)
            g_prev = gc - gs
            h0 = SMALL_A + 2 * p
            gr = tm_(g_cum_t[h0:h0 + 1, :], g_cum_t[h0 + 1:h0 + 2, :])
            ge0, ge1 = g_cum[L - 1:L, h0:h0 + 1], g_cum[L - 1:L, h0 + 1:h0 + 2]
            e_end = jnp.exp(ch(ge0, ge1) - gc)
            b_vec = -(bt * jnp.exp(gs)) * kn
            k_vec = bt * kn
            units.append(dict(
                a_sc=kn, r_sc=qn, b_sc=b_vec, k_sc=k_vec,
                a_sr=kn * jnp.exp(g_prev), r_sr=qn * jnp.exp(gc),
                b_dec=b_vec * e_end, k_dec=k_vec * e_end,
                v=v_ref[s, :, sl], s_prev=[st_ref[s, 2 * p], st_ref[s, 2 * p + 1]],
                p_last=[jnp.exp(ge0), jnp.exp(ge1)],
                dm_a=jnp.exp(jnp.where(tcol < row, tm_(gc0 - gs0, gc1 - gs1) - gr, -jnp.inf)),
                dm_r=jnp.exp(jnp.where(tcol <= row, tm_(gc0, gc1) - gr, -jnp.inf))))
    ys, s_new = _dplr_pairs(units, L=L)
    npair = GDN_HEADS // 2
    for s in range(nseq):
        for p in range(npair):
            sl = slice(2 * p * hd, (2 * p + 2) * hd)
            st_ref[s, 2 * p] = s_new[s * npair + p][0]
            st_ref[s, 2 * p + 1] = s_new[s * npair + p][1]
            y = ys[s * npair + p]
            ms = jnp.where(first_c, jnp.mean(y[:, :hd] * y[:, :hd], axis=-1, keepdims=True),
                           jnp.mean(y[:, hd:] * y[:, hd:], axis=-1, keepdims=True))
            yn = y * lax.rsqrt(ms + NORM_EPS) * nw
            o_ref[s, :, sl] = (yn * _silu(z_ref[s, :, sl])).astype(o_ref.dtype)


def _gdn_branch(pc, pn, a_log, dt_bias, norm_w, bsz, seq):
    L = DPLR_CHUNK
    nseq = SEQ_PER_STEP if bsz % SEQ_PER_STEP == 0 else 1
    pc3 = pc.reshape(bsz, seq, N_PC)
    pn3 = pn.reshape(bsz, seq, N_PN)
    vec = lambda w: pl.BlockSpec((1, w), lambda b, c: (0, 0))
    tok = lambda width, col: pl.BlockSpec((nseq, L, width), lambda b, c: (b, c, col // width))
    out = pl.pallas_call(
        functools.partial(_gdn_kernel, L=L, nseq=nseq),
        grid=(bsz // nseq, seq // L),
        in_specs=[tok(GDN_INNER, PC_Q), tok(GDN_INNER, PC_K), tok(GDN_INNER, PC_V),
                  tok(GDN_INNER, PN_ZGDN), tok(LANES, PC_SMALL),
                  vec(LANES), vec(LANES), vec(2 * GDN_HEAD_DIM)],
        out_specs=pl.BlockSpec((nseq, L, GDN_INNER), lambda b, c: (b, c, 0)),
        out_shape=jax.ShapeDtypeStruct((bsz, seq, GDN_INNER), BF16),
        scratch_shapes=[pltpu.VMEM((nseq, GDN_HEADS, LANES, LANES), F32)],
        compiler_params=_cparams("parallel", "arbitrary"),
    )(pc3, pc3, pc3, pn3, pc3,
      _pad_lanes(a_log, SMALL_A), _pad_lanes(dt_bias, SMALL_A),
      jnp.tile(norm_w.reshape(1, GDN_HEAD_DIM).astype(F32), (1, 2)))
    return out.reshape(bsz * seq, GDN_INNER)


def _rwkv_kernel(rkv_ref, lora_ref, mu_ref, mul_ref, w0_ref, w2_ref, a0_ref, a2_ref, g2_ref,
                 kk_ref, ka_ref, rk_ref, lnw_ref, lnb_ref, o_ref, sh_ref, shl_ref, st_ref, *, L, nseq):
    @pl.when(pl.program_id(1) == 0)
    def _():
        st_ref[...] = jnp.zeros(st_ref.shape, F32)
        for s in range(nseq):
            sh_ref[s, 0:SUBLANES, :] = jnp.zeros((SUBLANES, sh_ref.shape[2]), F32)
            shl_ref[s, 0:SUBLANES, :] = jnp.zeros((SUBLANES, shl_ref.shape[2]), F32)

    n = RWKV_INNER
    npair = RWKV_HEADS // 2
    ones_bd = _half_ones()
    inv_hd = 1.0 / RWKV_HEAD_DIM

    def head_sums(t):
        st = jnp.concatenate([t[:, p * LANES:(p + 1) * LANES] for p in range(npair)], axis=0)
        sm = _dot(st, ones_bd)
        return jnp.concatenate([sm[p * L:(p + 1) * L] for p in range(npair)], axis=1)

    units, tails = [], []
    for s in range(nseq):
        u = rkv_ref[s]
        ul = lora_ref[s]
        sh_ref[s, SUBLANES:SUBLANES + L, :] = u
        shl_ref[s, SUBLANES:SUBLANES + L, :] = ul
        u = u + (sh_ref[s, pl.ds(SUBLANES - 1, L), :] - u) * mu_ref[...]
        ul = ul + (shl_ref[s, pl.ds(SUBLANES - 1, L), :] - ul) * mul_ref[...]
        sh_ref[s, 0:SUBLANES, :] = sh_ref[s, L:L + SUBLANES, :]
        shl_ref[s, 0:SUBLANES, :] = shl_ref[s, L:L + SUBLANES, :]

        r, k, v = u[:, 0:n], u[:, n:2 * n], u[:, 2 * n:3 * n]
        lo = ul[:, 0:LANES]
        dg = ul[:, LANES:2 * LANES]
        w_log = -_softplus(-(w0_ref[...] + _dot_hl(jnp.tanh(lo), w2_ref[...]))) - 0.5
        lw = -jnp.exp(w_log)
        a_ic = _sigmoid(a0_ref[...] + _dot_hl(lo, a2_ref[...]))
        gate = _dot(_sigmoid(dg), g2_ref[...])
        g_cum = _dot_w3(_tri_incl(L), lw)
        kk_raw = k * kk_ref[...]
        k_mod = k * (1.0 + (a_ic - 1.0) * ka_ref[...])
        kk_all = kk_raw * lax.rsqrt(head_sums(kk_raw * kk_raw) + L2_EPS)
        tails.append((head_sums(r * k_mod * rk_ref[...]) * v, gate))
        for p in range(npair):
            sl = slice(p * LANES, (p + 1) * LANES)
            kk = kk_all[:, sl]
            gc = g_cum[:, sl]
            g_end = gc[L - 1:L, :]
            e_neg = jnp.exp(-gc)
            e_end = jnp.exp(g_end - gc)
            b_vec = kk * a_ic[:, sl]
            k_vec = k_mod[:, sl]
            a_t = -kk * jnp.exp(gc - lw[:, sl])
            r_t = r[:, sl] * jnp.exp(gc)
            units.append(dict(
                a_sc=a_t, r_sc=r_t, b_sc=b_vec * e_neg, k_sc=k_vec * e_neg,
                a_sr=a_t, r_sr=r_t, b_dec=b_vec * e_end, k_dec=k_vec * e_end,
                v=v[:, sl], s_prev=[st_ref[s, p]], p_last=[jnp.exp(g_end)]))
    ys, s_new = _dplr_pairs(units, L=L)
    for s in range(nseq):
        for p in range(npair):
            st_ref[s, p] = s_new[s * npair + p][0]
        y = jnp.concatenate(ys[s * npair:(s + 1) * npair], axis=1)
        yc = y - head_sums(y) * inv_hd
        var = head_sums(yc * yc) * inv_hd
        yn = yc * lax.rsqrt(var + RWKV_GN_EPS) * lnw_ref[...] + lnb_ref[...]
        bonus, gate = tails[s]
        o_ref[s] = ((yn + bonus) * gate).astype(o_ref.dtype)


def _rwkv_branch(pc, pn, mu, w0, w2, a0, a2, g2, k_k, k_a, r_k, ln_w, ln_b, bsz, seq):
    L = DPLR_CHUNK
    n = RWKV_INNER
    nseq = SEQ_PER_STEP if bsz % SEQ_PER_STEP == 0 else 1
    pc3 = pc.reshape(bsz, seq, N_PC)
    pn3 = pn.reshape(bsz, seq, N_PN)
    vec = lambda w: pl.BlockSpec((1, w), lambda b, c: (0, 0))
    mat = lambda: pl.BlockSpec((LANES, n), lambda b, c: (0, 0))
    tok = lambda width, col: pl.BlockSpec((nseq, L, width), lambda b, c: (b, c, col // width))
    w2p = jnp.concatenate([w2, jnp.zeros((LANES - RWKV_DECAY_LORA, n), F32)], axis=0)
    a2p = jnp.concatenate([jnp.zeros((RWKV_DECAY_LORA, n), F32), a2], axis=0)
    r1 = lambda t: t.reshape(1, -1).astype(F32)
    out = pl.pallas_call(
        functools.partial(_rwkv_kernel, L=L, nseq=nseq),
        grid=(bsz // nseq, seq // L),
        in_specs=[tok(3 * n, PN_RKV), tok(256, PC_LORA),
                  vec(3 * n), vec(256), vec(n), mat(), vec(n), mat(), mat(),
                  vec(n), vec(n), vec(n), vec(n), vec(n)],
        out_specs=pl.BlockSpec((nseq, L, n), lambda b, c: (b, c, 0)),
        out_shape=jax.ShapeDtypeStruct((bsz, seq, n), BF16),
        scratch_shapes=[pltpu.VMEM((nseq, L + SUBLANES, 3 * n), F32),
                        pltpu.VMEM((nseq, L + SUBLANES, 256), F32),
                        pltpu.VMEM((nseq, RWKV_HEADS // 2, LANES, LANES), F32)],
        compiler_params=_cparams("parallel", "arbitrary"),
    )(pn3, pc3, r1(mu[:3 * n]), r1(mu[3 * n:]), r1(w0), w2p, r1(a0), a2p, g2.astype(F32),
      r1(k_k), r1(k_a), r1(r_k), r1(ln_w), r1(ln_b))
    return out.reshape(bsz * seq, n)


def _merge_kernel(x_ref, ys_ref, yg_ref, yr_ref, gate_ref, ps_ref, pg_ref, pr_ref, wo_ref, o_ref):
    d = D_MODEL
    m = _sigmoid(gate_ref[:, 0:d]) * jnp.dot(ys_ref[...], ps_ref[...], preferred_element_type=F32)
    m = m + _sigmoid(gate_ref[:, d:2 * d]) * jnp.dot(yg_ref[...], pg_ref[...], preferred_element_type=F32)
    m = m + _sigmoid(gate_ref[:, 2 * d:3 * d]) * jnp.dot(yr_ref[...], pr_ref[...], preferred_element_type=F32)
    o_ref[...] = x_ref[...] + jnp.dot(m.astype(BF16), wo_ref[...], preferred_element_type=F32)


def _merge(x, y_ssm, y_gdn, y_rwkv, pn, p_ssm, p_gdn, p_rwkv, w_out, tm=512):
    t, d = x.shape
    tok = lambda: pl.BlockSpec((tm, d), lambda i: (i, 0))
    wgt = lambda: pl.BlockSpec((d, d), lambda i: (0, 0))
    return pl.pallas_call(
        _merge_kernel,
        grid=(t // tm,),
        in_specs=[tok(), tok(), tok(), tok(),
                  pl.BlockSpec((tm, 3 * d), lambda i: (i, PN_GATE // (3 * d))),
                  wgt(), wgt(), wgt(), wgt()],
        out_specs=tok(),
        out_shape=jax.ShapeDtypeStruct((t, d), F32),
        compiler_params=_cparams("parallel"),
    )(x, y_ssm, y_gdn, y_rwkv, pn, p_ssm, p_gdn, p_rwkv, w_out)


def _ffn_kernel(x_ref, nw_ref, wg_ref, wu_ref, wd_ref, *rest, emit_next):
    if emit_next:
        nnw_ref, o_ref, hn_ref, h_ref, acc_ref = rest
    else:
        o_ref, h_ref, acc_ref = rest
    f = pl.program_id(1)

    @pl.when(f == 0)
    def _():
        x = x_ref[...]
        ms = jnp.mean(x * x, axis=-1, keepdims=True)
        h_ref[...] = (x * lax.rsqrt(ms + NORM_EPS) * nw_ref[...]).astype(BF16)
        acc_ref[...] = x

    h = h_ref[...]
    act = _silu(jnp.dot(h, wg_ref[...], preferred_element_type=F32)) * jnp.dot(
        h, wu_ref[...], preferred_element_type=F32)
    acc_ref[...] += jnp.dot(act.astype(BF16), wd_ref[...], preferred_element_type=F32)

    @pl.when(f == pl.num_programs(1) - 1)
    def _():
        y = acc_ref[...]
        o_ref[...] = y
        if emit_next:
            ms = jnp.mean(y * y, axis=-1, keepdims=True)
            hn_ref[...] = (y * lax.rsqrt(ms + NORM_EPS) * nnw_ref[...]).astype(BF16)


def _ffn_dense(x, norm_w, w_gate, w_up, w_down, next_norm_w=None, tm=512, tf=2816):
    t, d = x.shape
    nf = w_gate.shape[1] // tf
    emit_next = next_norm_w is not None
    tok = pl.BlockSpec((tm, d), lambda i, f: (i, 0))
    vec = pl.BlockSpec((1, d), lambda i, f: (0, 0))
    in_specs = [tok, vec,
                pl.BlockSpec((d, tf), lambda i, f: (0, f)),
                pl.BlockSpec((d, tf), lambda i, f: (0, f)),
                pl.BlockSpec((tf, d), lambda i, f: (f, 0))]
    args = [x, norm_w.reshape(1, d).astype(F32), w_gate, w_up, w_down]
    out_specs, out_shape = tok, jax.ShapeDtypeStruct((t, d), F32)
    if emit_next:
        in_specs.append(vec)
        args.append(next_norm_w.reshape(1, d).astype(F32))
        out_specs, out_shape = [tok, tok], [out_shape, jax.ShapeDtypeStruct((t, d), BF16)]
    return pl.pallas_call(
        functools.partial(_ffn_kernel, emit_next=emit_next),
        grid=(t // tm, nf),
        in_specs=in_specs,
        out_specs=out_specs,
        out_shape=out_shape,
        scratch_shapes=[pltpu.VMEM((tm, d), BF16), pltpu.VMEM((tm, d), F32)],
        compiler_params=_cparams("parallel", "arbitrary"),
    )(*args)


MOE_TM = 1024
MOE_BLK = 128
MOE_FIRST_ROWS = 288


def _route_kernel(x_ref, nw_ref, rt_ref, h_ref, cmb_ref, cnt_ref):
    x = x_ref[...]
    lane = lax.broadcasted_iota(jnp.int32, cmb_ref.shape, 1)
    ms = jnp.mean(x * x, axis=-1, keepdims=True)
    hf = x * lax.rsqrt(ms + NORM_EPS) * nw_ref[...]
    h_ref[...] = hf.astype(BF16)
    logits = jnp.where(lane < N_EXPERTS, _dot_hl(hf, rt_ref[...]), -jnp.inf)
    m1 = jnp.max(logits, axis=-1, keepdims=True)
    i1 = jnp.min(jnp.where(logits == m1, lane, LANES), axis=-1, keepdims=True)
    rest = jnp.where(lane == i1, -jnp.inf, logits)
    m2 = jnp.max(rest, axis=-1, keepdims=True)
    i2 = jnp.min(jnp.where(rest == m2, lane, LANES), axis=-1, keepdims=True)
    e2 = jnp.exp(m2 - m1)
    p1 = 1.0 / (1.0 + e2)
    cmb = jnp.where(lane == i1, p1, 0.0) + jnp.where(lane == i2, e2 * p1, 0.0)
    cmb_ref[...] = cmb
    cnt_ref[0] = jnp.sum(jnp.where(cmb > 0.0, 1.0, 0.0), axis=0, keepdims=True).astype(jnp.int32)


def _moe_route(x, norm_w, router, tm):
    t, d = x.shape
    ne = router.shape[1]
    rt = jnp.concatenate([router.astype(F32), jnp.zeros((d, LANES - ne), F32)], axis=1)
    return pl.pallas_call(
        _route_kernel,
        grid=(t // tm,),
        in_specs=[pl.BlockSpec((tm, d), lambda i: (i, 0)),
                  pl.BlockSpec((1, d), lambda i: (0, 0)),
                  pl.BlockSpec((d, LANES), lambda i: (0, 0))],
        out_specs=[pl.BlockSpec((tm, d), lambda i: (i, 0)),
                   pl.BlockSpec((tm, LANES), lambda i: (i, 0)),
                   pl.BlockSpec((1, 1, LANES), lambda i: (i, 0, 0))],
        out_shape=[jax.ShapeDtypeStruct((t, d), BF16),
                   jax.ShapeDtypeStruct((t, LANES), F32),
                   jax.ShapeDtypeStruct((t // tm, 1, LANES), jnp.int32)],
        compiler_params=_cparams("parallel"),
    )(x, norm_w.reshape(1, d).astype(F32), rt)


def _moe_kernel(cnt_ref, x_ref, h_ref, cmb_ref, wg_ref, wu_ref, wd_ref, fw_ref, o_ref,
                rank_ref, xc_ref, yc_ref, oh_ref, *, tm, final_norm):
    i = pl.program_id(0)
    e = pl.program_id(1)
    f = pl.program_id(2)
    last_f = f == pl.num_programs(2) - 1
    lane = lax.broadcasted_iota(jnp.int32, (tm, LANES), 1)

    @pl.when((e == 0) & (f == 0))
    def _():
        o_ref[...] = jnp.zeros(o_ref.shape, F32)
        r = lax.broadcasted_iota(jnp.int32, (tm, tm), 0)
        c = lax.broadcasted_iota(jnp.int32, (tm, tm), 1)
        before = jnp.where(c < r, 1.0, 0.0).astype(BF16)
        sel = jnp.where(cmb_ref[...] > 0.0, 1.0, 0.0).astype(BF16)
        rank_ref[...] = jnp.dot(before, sel, preferred_element_type=F32)

    cnt = cnt_ref[i, e]

    def weight():
        return jnp.sum(jnp.where(lane == e, cmb_ref[...], 0.0), axis=-1, keepdims=True)

    def one_hot(off, rows):
        r_e = jnp.sum(jnp.where(lane == e, rank_ref[...], 0.0), axis=-1, keepdims=True)
        key = jnp.where(weight() > 0.0, r_e, -1.0) - jnp.asarray(off, F32)
        slot = lax.broadcasted_iota(jnp.int32, (tm, rows), 1).astype(F32)
        return jnp.where(key == slot, 1.0, 0.0).astype(BF16)

    def process(off, rows, cache):
        rs = pl.ds(off, rows)

        @pl.when(f == 0)
        def _():
            onehot = one_hot(off, rows)
            if cache:
                oh_ref[...] = onehot
            xc_ref[rs, :] = _dot_tn(onehot, h_ref[...]).astype(BF16)
            yc_ref[rs, :] = jnp.zeros((rows, yc_ref.shape[1]), F32)

        xc = xc_ref[rs, :]
        act = _silu(jnp.dot(xc, wg_ref[0], preferred_element_type=F32)) * jnp.dot(
            xc, wu_ref[0], preferred_element_type=F32)
        yc_ref[rs, :] += jnp.dot(act.astype(BF16), wd_ref[0], preferred_element_type=F32)

        @pl.when(last_f)
        def _():
            onehot = oh_ref[...] if cache else one_hot(off, rows)
            o_ref[...] += weight() * jnp.dot(onehot, yc_ref[rs, :].astype(BF16), preferred_element_type=F32)

    @pl.when(cnt > 0)
    def _():
        process(0, MOE_FIRST_ROWS, True)

    def extra(b, carry):
        process(pl.multiple_of(MOE_FIRST_ROWS + b * MOE_BLK, 32), MOE_BLK, False)
        return carry

    lax.fori_loop(0, (jnp.maximum(cnt - MOE_FIRST_ROWS, 0) + MOE_BLK - 1) // MOE_BLK, extra, 0)

    @pl.when((e == pl.num_programs(1) - 1) & last_f)
    def _():
        y = x_ref[...] + o_ref[...]
        if final_norm:
            y = y * lax.rsqrt(jnp.mean(y * y, axis=-1, keepdims=True) + NORM_EPS) * fw_ref[...]
        o_ref[...] = y


def _ffn_moe(x, norm_w, router, w_gate, w_up, w_down, final_w, tm=MOE_TM, tf=1408):
    t, d = x.shape
    ne, _, ff = w_gate.shape
    nf = ff // tf
    h, cmb, cnt = _moe_route(x, norm_w, router, tm)
    final_norm = final_w is not None
    fw = (final_w if final_norm else jnp.ones((d,), F32)).reshape(1, d).astype(F32)
    cap = MOE_FIRST_ROWS + -(-max(tm - MOE_FIRST_ROWS, 0) // MOE_BLK) * MOE_BLK
    grid_spec = pltpu.PrefetchScalarGridSpec(
        num_scalar_prefetch=1,
        grid=(t // tm, ne, nf),
        in_specs=[pl.BlockSpec((tm, d), lambda i, e, f, c: (i, 0)),
                  pl.BlockSpec((tm, d), lambda i, e, f, c: (i, 0)),
                  pl.BlockSpec((tm, LANES), lambda i, e, f, c: (i, 0)),
                  pl.BlockSpec((1, d, tf), lambda i, e, f, c: (e, 0, f)),
                  pl.BlockSpec((1, d, tf), lambda i, e, f, c: (e, 0, f)),
                  pl.BlockSpec((1, tf, d), lambda i, e, f, c: (e, f, 0)),
                  pl.BlockSpec((1, d), lambda i, e, f, c: (0, 0))],
        out_specs=pl.BlockSpec((tm, d), lambda i, e, f, c: (i, 0)),
        scratch_shapes=[pltpu.VMEM((tm, LANES), F32), pltpu.VMEM((cap, d), BF16), pltpu.VMEM((cap, d), F32),
                        pltpu.VMEM((tm, MOE_FIRST_ROWS), BF16)])
    return pl.pallas_call(
        functools.partial(_moe_kernel, tm=tm, final_norm=final_norm),
        grid_spec=grid_spec,
        out_shape=jax.ShapeDtypeStruct((t, d), F32),
        compiler_params=_cparams("parallel", "arbitrary", "arbitrary"),
    )(cnt.reshape(t // tm, LANES), x, h, cmb, w_gate, w_up, w_down, fw)


def _reorder_kernel(w_ref, wc_ref, wn_ref):
    w = w_ref[0]
    wc = [w[:, 0:1024], w[:, 1536:4608], w[:, 1024:1536], w[:, 9760:10016], w[:, 5632:5648],
          w[:, 6672:6688], jnp.zeros((w.shape[0], N_PC - PC_SMALL - 32), w.dtype)]
    wn = [w[:, 6688:9760], w[:, 10016:13088], w[:, 4608:5632], w[:, 5648:6672]]
    wc_ref[...] = jnp.concatenate(wc, axis=1).astype(BF16)
    wn_ref[...] = jnp.concatenate(wn, axis=1).astype(BF16)


def _reorder_w_in(w, layer, tk=128):
    _, d, n = w.shape
    return pl.pallas_call(
        _reorder_kernel,
        grid=(d // tk,),
        in_specs=[pl.BlockSpec((1, tk, n), lambda i: (layer, i, 0))],
        out_specs=[pl.BlockSpec((tk, N_PC), lambda i: (i, 0)), pl.BlockSpec((tk, N_PN), lambda i: (i, 0))],
        out_shape=[jax.ShapeDtypeStruct((d, N_PC), BF16), jax.ShapeDtypeStruct((d, N_PN), BF16)],
        compiler_params=_cparams("parallel"),
    )(w)


def _token_mixer_layer(x, h, i, bsz, seq, p):
    if h is None:
        h = _rmsnorm(x, p["attn_norm_w"][i], BF16)
    wc, wn = _reorder_w_in(p["w_in"], i)
    pc, pn = _inproj(h, wc, wn, p["conv_w"][i], p["conv_b"][i], seq, tm=1024)
    y_ssm = _ssd_branch(pc, pn, p["ssm_a_log"][i], p["ssm_dt_bias"][i], p["ssm_d"][i],
                        p["ssm_norm_w"][i], bsz, seq)
    y_gdn = _gdn_branch(pc, pn, p["gdn_a_log"][i], p["gdn_dt_bias"][i], p["gdn_norm_w"][i],
                        bsz, seq)
    y_rwkv = _rwkv_branch(pc, pn, p["rwkv_mu"][i], p["rwkv_w0"][i], p["rwkv_w2"][i], p["rwkv_a0"][i],
                          p["rwkv_a2"][i], p["rwkv_g2"][i], p["rwkv_k_k"][i], p["rwkv_k_a"][i],
                          p["rwkv_r_k"][i], p["rwkv_ln_w"][i], p["rwkv_ln_b"][i], bsz, seq)
    return _merge(x, y_ssm, y_gdn, y_rwkv, pn, p["proj_ssm"][i], p["proj_gdn"][i], p["proj_rwkv"][i],
                  p["w_out"][i])


def kernel(x, attn_norm_w, w_in, conv_w, conv_b, ssm_a_log, ssm_dt_bias, ssm_d, ssm_norm_w, gdn_a_log, gdn_dt_bias, gdn_norm_w, rwkv_mu, rwkv_w0, rwkv_w2, rwkv_a0, rwkv_a2, rwkv_g2, rwkv_k_k, rwkv_k_a, rwkv_r_k, rwkv_ln_w, rwkv_ln_b, proj_ssm, proj_gdn, proj_rwkv, w_out, ffn_norm_w, dense_w_gate, dense_w_up, dense_w_down, moe_router, moe_w_gate, moe_w_up, moe_w_down, final_norm_w):
    p = dict(attn_norm_w=attn_norm_w, w_in=w_in, conv_w=conv_w, conv_b=conv_b, ssm_a_log=ssm_a_log,
             ssm_dt_bias=ssm_dt_bias, ssm_d=ssm_d, ssm_norm_w=ssm_norm_w, gdn_a_log=gdn_a_log,
             gdn_dt_bias=gdn_dt_bias, gdn_norm_w=gdn_norm_w, rwkv_mu=rwkv_mu, rwkv_w0=rwkv_w0,
             rwkv_w2=rwkv_w2, rwkv_a0=rwkv_a0, rwkv_a2=rwkv_a2, rwkv_g2=rwkv_g2, rwkv_k_k=rwkv_k_k,
             rwkv_k_a=rwkv_k_a, rwkv_r_k=rwkv_r_k, rwkv_ln_w=rwkv_ln_w, rwkv_ln_b=rwkv_ln_b,
             proj_ssm=proj_ssm, proj_gdn=proj_gdn, proj_rwkv=proj_rwkv, w_out=w_out)
    for name in ("w_in", "proj_ssm", "proj_gdn", "proj_rwkv", "w_out"):
        p[name] = p[name].astype(BF16)
    dense_w_gate, dense_w_up, dense_w_down = (t.astype(BF16) for t in (dense_w_gate, dense_w_up, dense_w_down))
    moe_w_gate, moe_w_up, moe_w_down = (t.astype(BF16) for t in (moe_w_gate, moe_w_up, moe_w_down))
    bsz, seq, d = x.shape
    depth = attn_norm_w.shape[0]
    xt = x.reshape(bsz * seq, d)
    h_next = None
    for i in range(depth):
        xt = _token_mixer_layer(xt, h_next, i, bsz, seq, p)
        h_next = None
        j = i // 2
        last = i == depth - 1
        if i % 2 == 0:
            if last:
                xt = _ffn_dense(xt, ffn_norm_w[i], dense_w_gate[j], dense_w_up[j], dense_w_down[j])
                xt = _rmsnorm(xt, final_norm_w, F32)
            else:
                xt, h_next = _ffn_dense(xt, ffn_norm_w[i], dense_w_gate[j], dense_w_up[j], dense_w_down[j],
                                        next_norm_w=attn_norm_w[i + 1])
        else:
            xt = _ffn_moe(xt, ffn_norm_w[i], moe_router[j], moe_w_gate[j], moe_w_up[j], moe_w_down[j],
                          final_norm_w if last else None)
    return xt.reshape(bsz, seq, d)
```

```python
import functools
import math

import numpy as np
import jax
import jax.numpy as jnp
from jax import lax
from jax.experimental import pallas as pl
from jax.experimental.pallas import tpu as pltpu

F32 = jnp.float32
BF16 = jnp.bfloat16

LANES = 128
SUBLANES = 8
VMEM_LIMIT_BYTES = 56 * 1024 * 1024

D_MODEL = 1024
SSM_HEADS, SSM_HEAD_DIM, SSM_GROUPS, SSM_STATE = 16, 64, 2, 128
SSM_INNER = SSM_HEADS * SSM_HEAD_DIM
GDN_HEADS, GDN_HEAD_DIM = 8, 128
GDN_INNER = GDN_HEADS * GDN_HEAD_DIM
RWKV_HEADS, RWKV_HEAD_DIM = 16, 64
RWKV_INNER = RWKV_HEADS * RWKV_HEAD_DIM
RWKV_DECAY_LORA, RWKV_ICLR_LORA, RWKV_GATE_LORA = 64, 64, 128
RWKV_GN_EPS = 64e-5
CONV_K = 4
FFN_DIM = 2816
N_EXPERTS = 8
NORM_EPS = 1e-6
L2_EPS = 1e-6

PC_XS, PC_Q, PC_K, PC_V = 0, 1024, 2048, 3072
PC_BS, PC_CS = 4096, 4352
PC_CONV_END = 4608
PC_LORA = 4608
PC_SMALL = 4864
N_PC = 5120
PN_RKV = 0
PN_GATE = 3072
PN_ZSSM, PN_ZGDN = 6144, 7168
N_PN = 8192
SMALL_DT, SMALL_A, SMALL_B = 0, 16, 24

SSD_CHUNK = 128
DPLR_CHUNK = 64
SEQ_PER_STEP = 4
INPROJ_ROW_CHUNK = 512


def _cparams(*sem):
    return pltpu.CompilerParams(dimension_semantics=sem, vmem_limit_bytes=VMEM_LIMIT_BYTES)


def _dot(a, b):
    return jnp.dot(a.astype(BF16), b.astype(BF16), preferred_element_type=F32)


def _dot_nt(a, b):
    return lax.dot_general(a.astype(BF16), b.astype(BF16), (((1,), (1,)), ((), ())),
                           preferred_element_type=F32)


def _dot_tn(a, b):
    return lax.dot_general(a.astype(BF16), b.astype(BF16), (((0,), (0,)), ((), ())),
                           preferred_element_type=F32)


def _split3(x):
    hi = x.astype(BF16)
    r1 = x - hi.astype(F32)
    mid = r1.astype(BF16)
    lo = (r1 - mid.astype(F32)).astype(BF16)
    return hi, mid, lo


def _dot_x3(x, w):
    hi, mid, lo = _split3(x)
    return _dot(hi, w) + _dot(mid, w) + _dot(lo, w)


def _dot_w3(w, x):
    hi, mid, lo = _split3(x)
    return _dot(w, hi) + _dot(w, mid) + _dot(w, lo)


def _dot_hl(a, b):
    ah = a.astype(BF16)
    al = (a - ah.astype(F32)).astype(BF16)
    bh = b.astype(BF16)
    bl = (b - bh.astype(F32)).astype(BF16)
    return _dot(ah, bh) + _dot(ah, bl) + _dot(al, bh)


def _sigmoid(x):
    return 0.5 * jnp.tanh(0.5 * x) + 0.5


def _silu(x):
    return x * _sigmoid(x)


def _softplus(x):
    return jnp.maximum(x, 0.0) + jnp.log(1.0 + jnp.exp(-jnp.abs(x)))


def _tri_incl(n):
    r = lax.broadcasted_iota(jnp.int32, (n, n), 0)
    c = lax.broadcasted_iota(jnp.int32, (n, n), 1)
    return jnp.where(r >= c, 1.0, 0.0).astype(BF16)


def _half_ones():
    r = lax.broadcasted_iota(jnp.int32, (LANES, LANES), 0)
    c = lax.broadcasted_iota(jnp.int32, (LANES, LANES), 1)
    return jnp.where((r // 64) == (c // 64), 1.0, 0.0).astype(BF16)


def _rmsnorm_kernel(x_ref, w_ref, o_ref):
    x = x_ref[...]
    ms = jnp.mean(x * x, axis=-1, keepdims=True)
    o_ref[...] = (x * lax.rsqrt(ms + NORM_EPS) * w_ref[...]).astype(o_ref.dtype)


def _rmsnorm(x, w, out_dtype, tm=512):
    t, d = x.shape
    return pl.pallas_call(
        _rmsnorm_kernel,
        grid=(t // tm,),
        in_specs=[pl.BlockSpec((tm, d), lambda i: (i, 0)),
                  pl.BlockSpec((1, d), lambda i: (0, 0))],
        out_specs=pl.BlockSpec((tm, d), lambda i: (i, 0)),
        out_shape=jax.ShapeDtypeStruct((t, d), out_dtype),
        compiler_params=_cparams("parallel"),
    )(x, w.reshape(1, d))


def _inproj_kernel(a_ref, wc_ref, wn_ref, cw_ref, cb_ref, cm_ref, oc_ref, on_ref, sh_ref, *,
                   tm, tiles_per_seq):
    i = pl.program_id(1)
    tc = oc_ref.shape[1]

    @pl.when(i % tiles_per_seq == 0)
    def _():
        sh_ref[0:SUBLANES, :] = jnp.zeros((SUBLANES, tc), F32)

    rc = min(tm, INPROJ_ROW_CHUNK)
    starts = list(range(0, tm, rc))
    acc = jnp.dot(a_ref[0:rc, :], wc_ref[...], preferred_element_type=F32)
    for n, r0 in enumerate(starts):
        sh_ref[SUBLANES + r0:SUBLANES + r0 + rc, :] = acc
        on_ref[r0:r0 + rc, :] = jnp.dot(a_ref[r0:r0 + rc, :], wn_ref[...], preferred_element_type=F32)
        if n + 1 < len(starts):
            acc_next = jnp.dot(a_ref[r0 + rc:r0 + 2 * rc, :], wc_ref[...], preferred_element_type=F32)
        c = acc * cw_ref[CONV_K - 1:CONV_K, :] + cb_ref[...]
        for k in range(1, CONV_K):
            c = c + sh_ref[pl.ds(SUBLANES + r0 - k, rc), :] * cw_ref[CONV_K - 1 - k:CONV_K - k, :]
        oc_ref[r0:r0 + rc, :] = jnp.where(cm_ref[...] > 0.0, _silu(c), acc)
        if n + 1 < len(starts):
            acc = acc_next
    sh_ref[0:SUBLANES, :] = sh_ref[tm:tm + SUBLANES, :]


def _conv_cols(c):
    r = c.shape[0]
    return jnp.concatenate([c[:, 0:1024], c[:, 1536:4608], c[:, 1024:1536],
                            jnp.zeros((r, N_PC - PC_CONV_END), F32)], axis=1).astype(F32)


def _inproj(a, wc, wn, conv_w, conv_b, seq, tm, nsteps=4):
    m, k = a.shape
    tc, tn = N_PC // nsteps, N_PN // nsteps
    cw = _conv_cols(conv_w)
    cb = _conv_cols(conv_b.reshape(1, -1))
    cm = _conv_cols(jnp.ones((1, conv_w.shape[1]), F32))
    return pl.pallas_call(
        functools.partial(_inproj_kernel, tm=tm, tiles_per_seq=seq // tm),
        grid=(nsteps, m // tm),
        in_specs=[pl.BlockSpec((tm, k), lambda j, i: (i, 0)),
                  pl.BlockSpec((k, tc), lambda j, i: (0, j)),
                  pl.BlockSpec((k, tn), lambda j, i: (0, j)),
                  pl.BlockSpec((CONV_K, tc), lambda j, i: (0, j)),
                  pl.BlockSpec((1, tc), lambda j, i: (0, j)),
                  pl.BlockSpec((1, tc), lambda j, i: (0, j))],
        out_specs=[pl.BlockSpec((tm, tc), lambda j, i: (i, j)),
                   pl.BlockSpec((tm, tn), lambda j, i: (i, j))],
        out_shape=[jax.ShapeDtypeStruct((m, N_PC), F32), jax.ShapeDtypeStruct((m, N_PN), F32)],
        scratch_shapes=[pltpu.VMEM((tm + SUBLANES, tc), F32)],
        compiler_params=_cparams("parallel", "arbitrary"),
    )(a, wc, wn, cw, cb, cm)


def _ssd_kernel(xs_ref, bs_ref, cs_ref, z_ref, sm_ref, alog_ref, dtb_ref, dsk_ref, nw_ref, ex_ref,
                o_ref, st_ref, *, L, nseq):
    @pl.when(pl.program_id(1) == 0)
    def _():
        st_ref[...] = jnp.zeros(st_ref.shape, F32)

    hd = SSM_HEAD_DIM
    gw = SSM_INNER // SSM_GROUPS
    heads_per_group = SSM_HEADS // SSM_GROUPS
    ex = ex_ref[...]
    row = lax.broadcasted_iota(jnp.int32, (L, L), 0)
    col = lax.broadcasted_iota(jnp.int32, (L, L), 1)
    causal = row >= col
    lane = lax.broadcasted_iota(jnp.int32, (L, LANES), 1)
    first_half = lane < hd
    for s in range(nseq):
        dt = _softplus(sm_ref[s] + dtb_ref[...])
        la = -jnp.exp(alog_ref[...]) * dt
        g_cum = _dot_w3(_tri_incl(L), la)
        g_cum_t = g_cum.T
        dt_e = _dot_x3(dt, ex)
        g_e = _dot_x3(g_cum, ex)
        g_last = g_e[L - 1:L, :]
        e_g = jnp.exp(g_e)
        e_dec = jnp.exp(g_last - g_e)
        e_last = jnp.exp(g_last)

        xs = xs_ref[s]
        xdt = xs * dt_e
        xdec = xdt * e_dec
        y_parts = []
        for g in range(SSM_GROUPS):
            gsl = slice(g * gw, (g + 1) * gw)
            b_g = bs_ref[s, :, g * SSM_STATE:(g + 1) * SSM_STATE]
            c_g = cs_ref[s, :, g * SSM_STATE:(g + 1) * SSM_STATE]
            cb = _dot_nt(c_g, b_g)
            st_g = st_ref[s, :, gsl]
            y_off = _dot(c_g, st_g) * e_g[:, gsl]
            st_ref[s, :, gsl] = st_g * e_last[:, gsl] + _dot_tn(b_g, xdec[:, gsl])
            for p in range(heads_per_group // 2):
                h0 = g * heads_per_group + 2 * p
                base = h0 * hd
                sc = []
                for h in (h0, h0 + 1):
                    diff = g_cum[:, h:h + 1] - g_cum_t[h:h + 1, :]
                    sc.append(cb * jnp.exp(jnp.where(causal, diff, -jnp.inf)))
                yd = _dot(jnp.concatenate(sc, axis=0), xdt[:, base:base + LANES])
                y_parts.append(jnp.where(first_half, yd[:L], yd[L:])
                               + y_off[:, base - g * gw:base - g * gw + LANES])
        y = jnp.concatenate(y_parts, axis=1) + dsk_ref[...] * xs
        y = y * _silu(z_ref[s])
        outs = []
        for g in range(SSM_GROUPS):
            seg = y[:, g * gw:(g + 1) * gw]
            outs.append(seg * lax.rsqrt(jnp.mean(seg * seg, axis=-1, keepdims=True) + NORM_EPS))
        o_ref[s] = (jnp.concatenate(outs, axis=1) * nw_ref[...]).astype(o_ref.dtype)


def _pad_lanes(v, offset=0, width=LANES):
    out = jnp.zeros((1, width), F32)
    return lax.dynamic_update_slice(out, v.reshape(1, -1).astype(F32), (0, offset))


def _ssd_branch(pc, pn, a_log, dt_bias, d_skip, norm_w, bsz, seq):
    L = SSD_CHUNK
    nseq = SEQ_PER_STEP if bsz % SEQ_PER_STEP == 0 else 1
    pc3 = pc.reshape(bsz, seq, N_PC)
    pn3 = pn.reshape(bsz, seq, N_PN)
    ex = np.zeros((LANES, SSM_INNER), np.float32)
    for h in range(SSM_HEADS):
        ex[h, h * SSM_HEAD_DIM:(h + 1) * SSM_HEAD_DIM] = 1.0
    vec = lambda w: pl.BlockSpec((1, w), lambda b, c: (0, 0))
    tok = lambda width, col: pl.BlockSpec((nseq, L, width), lambda b, c: (b, c, col // width))
    out = pl.pallas_call(
        functools.partial(_ssd_kernel, L=L, nseq=nseq),
        grid=(bsz // nseq, seq // L),
        in_specs=[tok(SSM_INNER, PC_XS), tok(256, PC_BS), tok(256, PC_CS), tok(SSM_INNER, PN_ZSSM),
                  tok(LANES, PC_SMALL),
                  vec(LANES), vec(LANES), vec(SSM_INNER), vec(SSM_INNER),
                  pl.BlockSpec((LANES, SSM_INNER), lambda b, c: (0, 0))],
        out_specs=pl.BlockSpec((nseq, L, SSM_INNER), lambda b, c: (b, c, 0)),
        out_shape=jax.ShapeDtypeStruct((bsz, seq, SSM_INNER), BF16),
        scratch_shapes=[pltpu.VMEM((nseq, SSM_STATE, SSM_INNER), F32)],
        compiler_params=_cparams("parallel", "arbitrary"),
    )(pc3, pc3, pc3, pn3, pc3,
      _pad_lanes(a_log, SMALL_DT), _pad_lanes(dt_bias, SMALL_DT),
      jnp.repeat(d_skip.astype(F32), SSM_HEAD_DIM).reshape(1, SSM_INNER),
      norm_w.reshape(1, SSM_INNER).astype(F32), jnp.asarray(ex, BF16))
    return out.reshape(bsz * seq, SSM_INNER)


def _head_stack(x):
    half = x.shape[1] // 2
    lane = lax.broadcasted_iota(jnp.int32, x.shape, 1)
    return jnp.concatenate([jnp.where(lane < half, x, 0.0), jnp.where(lane >= half, x, 0.0)], axis=0)


def _dplr_pairs(units, *, L):
    nu = len(units)
    P = 2 * L
    vc = units[0]["v"].shape[1]
    row = lax.broadcasted_iota(jnp.int32, (L, P), 0)
    tcol = lax.broadcasted_iota(jnp.int32, (L, P), 1)
    first = tcol < L
    tcol = jnp.where(first, tcol, tcol - L)
    strict, lower = tcol < row, tcol <= row

    sc = [_dot_nt(jnp.concatenate([u["a_sc"], u["r_sc"]], axis=0),
                  jnp.concatenate([_head_stack(u["b_sc"]), _head_stack(u["k_sc"])], axis=0))
          for u in units]
    v_st = [_head_stack(u["v"]) for u in units]

    def read(u):
        rows = jnp.concatenate([u["a_sr"], u["r_sr"]], axis=0)
        st = u["s_prev"]
        if len(st) == 1:
            return _dot_nt(rows, st[0])
        kh = rows.shape[1] // 2
        return jnp.concatenate([_dot_nt(rows[:, :kh], st[0]), _dot_nt(rows[:, kh:], st[1])], axis=1)

    reads = [read(u) for u in units]
    n_mat, a_rb, aks = [], [], []
    for i, u in enumerate(units):
        sb, sk = sc[i][:, :P], sc[i][:, P:]
        if u.get("dm_a") is None:
            n_mat.append(jnp.where(strict, sb[:L], 0.0))
            a_rb.append(jnp.where(lower, sb[L:], 0.0))
            aks.append(jnp.concatenate([jnp.where(strict, sk[:L], 0.0), jnp.where(lower, sk[L:], 0.0)], axis=0))
        else:
            n_mat.append(sb[:L] * u["dm_a"])
            a_rb.append(sb[L:] * u["dm_r"])
            aks.append(jnp.concatenate([sk[:L] * u["dm_a"], sk[L:] * u["dm_r"]], axis=0))
    av = [_dot(aks[i], v_st[i]) for i in range(nu)]
    x = [reads[i][:L] + av[i][:L] for i in range(nu)]
    y_part = [reads[i][L:] + av[i][L:] for i in range(nu)]

    steps = int(math.log2(L))
    for s in range(steps):
        x_st = [_head_stack(x[i]) for i in range(nu)]
        if s + 1 < steps:
            t = [_dot(n_mat[i], jnp.concatenate(
                    [x_st[i], jnp.concatenate([jnp.where(first, n_mat[i], 0.0),
                                               jnp.where(first, 0.0, n_mat[i])], axis=0)], axis=1))
                 for i in range(nu)]
            x = [x[i] + t[i][:, :vc] for i in range(nu)]
            n_mat = [t[i][:, vc:] for i in range(nu)]
        else:
            x = [x[i] + _dot(n_mat[i], x_st[i]) for i in range(nu)]
    ys = [y_part[i] + _dot(a_rb[i], _head_stack(x[i])) for i in range(nu)]

    s_new = []
    for i, u in enumerate(units):
        xv = jnp.concatenate([x[i], u["v"]], axis=0)
        bk = jnp.concatenate([u["b_dec"], u["k_dec"]], axis=0)
        st, pl_ = u["s_prev"], u["p_last"]
        if len(st) == 1:
            upd = _dot_tn(xv, bk)
            ri = lax.broadcasted_iota(jnp.int32, upd.shape, 0) // (upd.shape[0] // 2)
            ci = lax.broadcasted_iota(jnp.int32, upd.shape, 1) // (upd.shape[1] // 2)
            s_new.append([st[0] * pl_[0] + jnp.where(ri == ci, upd, 0.0)])
        else:
            vh, kh = xv.shape[1] // 2, bk.shape[1] // 2
            s_new.append([st[0] * pl_[0] + _dot_tn(xv[:, :vh], bk[:, :kh]),
                          st[1] * pl_[1] + _dot_tn(xv[:, vh:], bk[:, kh:])])
    return ys, s_new


def _gdn_kernel(q_ref, k_ref, v_ref, z_ref, sm_ref, alog_ref, dtb_ref, nw_ref, o_ref, st_ref, *, L, nseq):
    @pl.when(pl.program_id(1) == 0)
    def _():
        st_ref[...] = jnp.zeros(st_ref.shape, F32)

    hd = GDN_HEAD_DIM
    row = lax.broadcasted_iota(jnp.int32, (L, 2 * L), 0)
    tcol = lax.broadcasted_iota(jnp.int32, (L, 2 * L), 1)
    first_t = tcol < L
    tcol = jnp.where(first_t, tcol, tcol - L)
    first_c = lax.broadcasted_iota(jnp.int32, (L, 2 * hd), 1) < hd
    nw = nw_ref[...]
    scale = hd ** -0.5
    units = []
    for s in range(nseq):
        sm = sm_ref[s]
        g = -jnp.exp(alog_ref[...]) * _softplus(sm + dtb_ref[...])
        beta = _sigmoid(sm)
        g_cum = _dot_w3(_tri_incl(L), g)
        g_cum_t = jnp.concatenate([g_cum] * (LANES // L), axis=0).T
        for p in range(GDN_HEADS // 2):
            sl = slice(2 * p * hd, (2 * p + 2) * hd)
            q = q_ref[s, :, sl]
            k = k_ref[s, :, sl]

            def per_head(t):
                return jnp.where(first_c, jnp.sum(t[:, :hd], axis=-1, keepdims=True),
                                 jnp.sum(t[:, hd:], axis=-1, keepdims=True))

            def col(arr, base):
                h0 = base + 2 * p
                return arr[:, h0:h0 + 1], arr[:, h0 + 1:h0 + 2]

            qn = q * (lax.rsqrt(per_head(q * q) + L2_EPS) * scale)
            kn = k * lax.rsqrt(per_head(k * k) + L2_EPS)
            gc0, gc1 = col(g_cum, SMALL_A)
            gs0, gs1 = col(g, SMALL_A)
            bt0, bt1 = col(beta, SMALL_B)
            ch = lambda a0, a1: jnp.where(first_c, a0, a1)
            tm_ = lambda a0, a1: jnp.where(first_t, a0, a1)
            gc, gs, bt = ch(gc0, gc1), ch(gs0, gs1), ch(bt0, bt1)
            g_prev = gc - gs
            h0 = SMALL_A + 2 * p
            gr = tm_(g_cum_t[h0:h0 + 1, :], g_cum_t[h0 + 1:h0 + 2, :])
            ge0, ge1 = g_cum[L - 1:L, h0:h0 + 1], g_cum[L - 1:L, h0 + 1:h0 + 2]
            e_end = jnp.exp(ch(ge0, ge1) - gc)
            b_vec = -(bt * jnp.exp(gs)) * kn
            k_vec = bt * kn
            units.append(dict(
                a_sc=kn, r_sc=qn, b_sc=b_vec, k_sc=k_vec,
                a_sr=kn * jnp.exp(g_prev), r_sr=qn * jnp.exp(gc),
                b_dec=b_vec * e_end, k_dec=k_vec * e_end,
                v=v_ref[s, :, sl], s_prev=[st_ref[s, 2 * p], st_ref[s, 2 * p + 1]],
                p_last=[jnp.exp(ge0), jnp.exp(ge1)],
                dm_a=jnp.exp(jnp.where(tcol < row, tm_(gc0 - gs0, gc1 - gs1) - gr, -jnp.inf)),
                dm_r=jnp.exp(jnp.where(tcol <= row, tm_(gc0, gc1) - gr, -jnp.inf))))
    ys, s_new = _dplr_pairs(units, L=L)
    npair = GDN_HEADS // 2
    for s in range(nseq):
        for p in range(npair):
            sl = slice(2 * p * hd, (2 * p + 2) * hd)
            st_ref[s, 2 * p] = s_new[s * npair + p][0]
            st_ref[s, 2 * p + 1] = s_new[s * npair + p][1]
            y = ys[s * npair + p]
            ms = jnp.where(first_c, jnp.mean(y[:, :hd] * y[:, :hd], axis=-1, keepdims=True),
                           jnp.mean(y[:, hd:] * y[:, hd:], axis=-1, keepdims=True))
            yn = y * lax.rsqrt(ms + NORM_EPS) * nw
            o_ref[s, :, sl] = (yn * _silu(z_ref[s, :, sl])).astype(o_ref.dtype)


def _gdn_branch(pc, pn, a_log, dt_bias, norm_w, bsz, seq):
    L = DPLR_CHUNK
    nseq = SEQ_PER_STEP if bsz % SEQ_PER_STEP == 0 else 1
    pc3 = pc.reshape(bsz, seq, N_PC)
    pn3 = pn.reshape(bsz, seq, N_PN)
    vec = lambda w: pl.BlockSpec((1, w), lambda b, c: (0, 0))
    tok = lambda width, col: pl.BlockSpec((nseq, L, width), lambda b, c: (b, c, col // width))
    out = pl.pallas_call(
        functools.partial(_gdn_kernel, L=L, nseq=nseq),
        grid=(bsz // nseq, seq // L),
        in_specs=[tok(GDN_INNER, PC_Q), tok(GDN_INNER, PC_K), tok(GDN_INNER, PC_V),
                  tok(GDN_INNER, PN_ZGDN), tok(LANES, PC_SMALL),
                  vec(LANES), vec(LANES), vec(2 * GDN_HEAD_DIM)],
        out_specs=pl.BlockSpec((nseq, L, GDN_INNER), lambda b, c: (b, c, 0)),
        out_shape=jax.ShapeDtypeStruct((bsz, seq, GDN_INNER), BF16),
        scratch_shapes=[pltpu.VMEM((nseq, GDN_HEADS, LANES, LANES), F32)],
        compiler_params=_cparams("parallel", "arbitrary"),
    )(pc3, pc3, pc3, pn3, pc3,
      _pad_lanes(a_log, SMALL_A), _pad_lanes(dt_bias, SMALL_A),
      jnp.tile(norm_w.reshape(1, GDN_HEAD_DIM).astype(F32), (1, 2)))
    return out.reshape(bsz * seq, GDN_INNER)


def _rwkv_kernel(rkv_ref, lora_ref, mu_ref, mul_ref, w0_ref, w2_ref, a0_ref, a2_ref, g2_ref,
                 kk_ref, ka_ref, rk_ref, lnw_ref, lnb_ref, o_ref, sh_ref, shl_ref, st_ref, *, L, nseq):
    @pl.when(pl.program_id(1) == 0)
    def _():
        st_ref[...] = jnp.zeros(st_ref.shape, F32)
        for s in range(nseq):
            sh_ref[s, 0:SUBLANES, :] = jnp.zeros((SUBLANES, sh_ref.shape[2]), F32)
            shl_ref[s, 0:SUBLANES, :] = jnp.zeros((SUBLANES, shl_ref.shape[2]), F32)

    n = RWKV_INNER
    npair = RWKV_HEADS // 2
    ones_bd = _half_ones()
    inv_hd = 1.0 / RWKV_HEAD_DIM

    def head_sums(t):
        st = jnp.concatenate([t[:, p * LANES:(p + 1) * LANES] for p in range(npair)], axis=0)
        sm = _dot(st, ones_bd)
        return jnp.concatenate([sm[p * L:(p + 1) * L] for p in range(npair)], axis=1)

    units, tails = [], []
    for s in range(nseq):
        u = rkv_ref[s]
        ul = lora_ref[s]
        sh_ref[s, SUBLANES:SUBLANES + L, :] = u
        shl_ref[s, SUBLANES:SUBLANES + L, :] = ul
        u = u + (sh_ref[s, pl.ds(SUBLANES - 1, L), :] - u) * mu_ref[...]
        ul = ul + (shl_ref[s, pl.ds(SUBLANES - 1, L), :] - ul) * mul_ref[...]
        sh_ref[s, 0:SUBLANES, :] = sh_ref[s, L:L + SUBLANES, :]
        shl_ref[s, 0:SUBLANES, :] = shl_ref[s, L:L + SUBLANES, :]

        r, k, v = u[:, 0:n], u[:, n:2 * n], u[:, 2 * n:3 * n]
        lo = ul[:, 0:LANES]
        dg = ul[:, LANES:2 * LANES]
        w_log = -_softplus(-(w0_ref[...] + _dot_hl(jnp.tanh(lo), w2_ref[...]))) - 0.5
        lw = -jnp.exp(w_log)
        a_ic = _sigmoid(a0_ref[...] + _dot_hl(lo, a2_ref[...]))
        gate = _dot(_sigmoid(dg), g2_ref[...])
        g_cum = _dot_w3(_tri_incl(L), lw)
        kk_raw = k * kk_ref[...]
        k_mod = k * (1.0 + (a_ic - 1.0) * ka_ref[...])
        kk_all = kk_raw * lax.rsqrt(head_sums(kk_raw * kk_raw) + L2_EPS)
        tails.append((head_sums(r * k_mod * rk_ref[...]) * v, gate))
        for p in range(npair):
            sl = slice(p * LANES, (p + 1) * LANES)
            kk = kk_all[:, sl]
            gc = g_cum[:, sl]
            g_end = gc[L - 1:L, :]
            e_neg = jnp.exp(-gc)
            e_end = jnp.exp(g_end - gc)
            b_vec = kk * a_ic[:, sl]
            k_vec = k_mod[:, sl]
            a_t = -kk * jnp.exp(gc - lw[:, sl])
            r_t = r[:, sl] * jnp.exp(gc)
            units.append(dict(
                a_sc=a_t, r_sc=r_t, b_sc=b_vec * e_neg, k_sc=k_vec * e_neg,
                a_sr=a_t, r_sr=r_t, b_dec=b_vec * e_end, k_dec=k_vec * e_end,
                v=v[:, sl], s_prev=[st_ref[s, p]], p_last=[jnp.exp(g_end)]))
    ys, s_new = _dplr_pairs(units, L=L)
    for s in range(nseq):
        for p in range(npair):
            st_ref[s, p] = s_new[s * npair + p][0]
        y = jnp.concatenate(ys[s * npair:(s + 1) * npair], axis=1)
        yc = y - head_sums(y) * inv_hd
        var = head_sums(yc * yc) * inv_hd
        yn = yc * lax.rsqrt(var + RWKV_GN_EPS) * lnw_ref[...] + lnb_ref[...]
        bonus, gate = tails[s]
        o_ref[s] = ((yn + bonus) * gate).astype(o_ref.dtype)


def _rwkv_branch(pc, pn, mu, w0, w2, a0, a2, g2, k_k, k_a, r_k, ln_w, ln_b, bsz, seq):
    L = DPLR_CHUNK
    n = RWKV_INNER
    nseq = SEQ_PER_STEP if bsz % SEQ_PER_STEP == 0 else 1
    pc3 = pc.reshape(bsz, seq, N_PC)
    pn3 = pn.reshape(bsz, seq, N_PN)
    vec = lambda w: pl.BlockSpec((1, w), lambda b, c: (0, 0))
    mat = lambda: pl.BlockSpec((LANES, n), lambda b, c: (0, 0))
    tok = lambda width, col: pl.BlockSpec((nseq, L, width), lambda b, c: (b, c, col // width))
    w2p = jnp.concatenate([w2, jnp.zeros((LANES - RWKV_DECAY_LORA, n), F32)], axis=0)
    a2p = jnp.concatenate([jnp.zeros((RWKV_DECAY_LORA, n), F32), a2], axis=0)
    r1 = lambda t: t.reshape(1, -1).astype(F32)
    out = pl.pallas_call(
        functools.partial(_rwkv_kernel, L=L, nseq=nseq),
        grid=(bsz // nseq, seq // L),
        in_specs=[tok(3 * n, PN_RKV), tok(256, PC_LORA),
                  vec(3 * n), vec(256), vec(n), mat(), vec(n), mat(), mat(),
                  vec(n), vec(n), vec(n), vec(n), vec(n)],
        out_specs=pl.BlockSpec((nseq, L, n), lambda b, c: (b, c, 0)),
        out_shape=jax.ShapeDtypeStruct((bsz, seq, n), BF16),
        scratch_shapes=[pltpu.VMEM((nseq, L + SUBLANES, 3 * n), F32),
                        pltpu.VMEM((nseq, L + SUBLANES, 256), F32),
                        pltpu.VMEM((nseq, RWKV_HEADS // 2, LANES, LANES), F32)],
        compiler_params=_cparams("parallel", "arbitrary"),
    )(pn3, pc3, r1(mu[:3 * n]), r1(mu[3 * n:]), r1(w0), w2p, r1(a0), a2p, g2.astype(F32),
      r1(k_k), r1(k_a), r1(r_k), r1(ln_w), r1(ln_b))
    return out.reshape(bsz * seq, n)


def _merge_kernel(x_ref, ys_ref, yg_ref, yr_ref, gate_ref, ps_ref, pg_ref, pr_ref, wo_ref, o_ref):
    d = D_MODEL
    m = _sigmoid(gate_ref[:, 0:d]) * jnp.dot(ys_ref[...], ps_ref[...], preferred_element_type=F32)
    m = m + _sigmoid(gate_ref[:, d:2 * d]) * jnp.dot(yg_ref[...], pg_ref[...], preferred_element_type=F32)
    m = m + _sigmoid(gate_ref[:, 2 * d:3 * d]) * jnp.dot(yr_ref[...], pr_ref[...], preferred_element_type=F32)
    o_ref[...] = x_ref[...] + jnp.dot(m.astype(BF16), wo_ref[...], preferred_element_type=F32)


def _merge(x, y_ssm, y_gdn, y_rwkv, pn, p_ssm, p_gdn, p_rwkv, w_out, tm=512):
    t, d = x.shape
    tok = lambda: pl.BlockSpec((tm, d), lambda i: (i, 0))
    wgt = lambda: pl.BlockSpec((d, d), lambda i: (0, 0))
    return pl.pallas_call(
        _merge_kernel,
        grid=(t // tm,),
        in_specs=[tok(), tok(), tok(), tok(),
                  pl.BlockSpec((tm, 3 * d), lambda i: (i, PN_GATE // (3 * d))),
                  wgt(), wgt(), wgt(), wgt()],
        out_specs=tok(),
        out_shape=jax.ShapeDtypeStruct((t, d), F32),
        compiler_params=_cparams("parallel"),
    )(x, y_ssm, y_gdn, y_rwkv, pn, p_ssm, p_gdn, p_rwkv, w_out)


def _ffn_kernel(x_ref, nw_ref, wg_ref, wu_ref, wd_ref, *rest, emit_next):
    if emit_next:
        nnw_ref, o_ref, hn_ref, h_ref, acc_ref = rest
    else:
        o_ref, h_ref, acc_ref = rest
    f = pl.program_id(1)

    @pl.when(f == 0)
    def _():
        x = x_ref[...]
        ms = jnp.mean(x * x, axis=-1, keepdims=True)
        h_ref[...] = (x * lax.rsqrt(ms + NORM_EPS) * nw_ref[...]).astype(BF16)
        acc_ref[...] = x

    h = h_ref[...]
    act = _silu(jnp.dot(h, wg_ref[...], preferred_element_type=F32)) * jnp.dot(
        h, wu_ref[...], preferred_element_type=F32)
    acc_ref[...] += jnp.dot(act.astype(BF16), wd_ref[...], preferred_element_type=F32)

    @pl.when(f == pl.num_programs(1) - 1)
    def _():
        y = acc_ref[...]
        o_ref[...] = y
        if emit_next:
            ms = jnp.mean(y * y, axis=-1, keepdims=True)
            hn_ref[...] = (y * lax.rsqrt(ms + NORM_EPS) * nnw_ref[...]).astype(BF16)


def _ffn_dense(x, norm_w, w_gate, w_up, w_down, next_norm_w=None, tm=512, tf=2816):
    t, d = x.shape
    nf = w_gate.shape[1] // tf
    emit_next = next_norm_w is not None
    tok = pl.BlockSpec((tm, d), lambda i, f: (i, 0))
    vec = pl.BlockSpec((1, d), lambda i, f: (0, 0))
    in_specs = [tok, vec,
                pl.BlockSpec((d, tf), lambda i, f: (0, f)),
                pl.BlockSpec((d, tf), lambda i, f: (0, f)),
                pl.BlockSpec((tf, d), lambda i, f: (f, 0))]
    args = [x, norm_w.reshape(1, d).astype(F32), w_gate, w_up, w_down]
    out_specs, out_shape = tok, jax.ShapeDtypeStruct((t, d), F32)
    if emit_next:
        in_specs.append(vec)
        args.append(next_norm_w.reshape(1, d).astype(F32))
        out_specs, out_shape = [tok, tok], [out_shape, jax.ShapeDtypeStruct((t, d), BF16)]
    return pl.pallas_call(
        functools.partial(_ffn_kernel, emit_next=emit_next),
        grid=(t // tm, nf),
        in_specs=in_specs,
        out_specs=out_specs,
        out_shape=out_shape,
        scratch_shapes=[pltpu.VMEM((tm, d), BF16), pltpu.VMEM((tm, d), F32)],
        compiler_params=_cparams("parallel", "arbitrary"),
    )(*args)


MOE_TM = 1024
MOE_BLK = 128
MOE_FIRST_ROWS = 288


def _route_kernel(x_ref, nw_ref, rt_ref, h_ref, cmb_ref, cnt_ref):
    x = x_ref[...]
    lane = lax.broadcasted_iota(jnp.int32, cmb_ref.shape, 1)
    ms = jnp.mean(x * x, axis=-1, keepdims=True)
    hf = x * lax.rsqrt(ms + NORM_EPS) * nw_ref[...]
    h_ref[...] = hf.astype(BF16)
    logits = jnp.where(lane < N_EXPERTS, _dot_hl(hf, rt_ref[...]), -jnp.inf)
    m1 = jnp.max(logits, axis=-1, keepdims=True)
    i1 = jnp.min(jnp.where(logits == m1, lane, LANES), axis=-1, keepdims=True)
    rest = jnp.where(lane == i1, -jnp.inf, logits)
    m2 = jnp.max(rest, axis=-1, keepdims=True)
    i2 = jnp.min(jnp.where(rest == m2, lane, LANES), axis=-1, keepdims=True)
    e2 = jnp.exp(m2 - m1)
    p1 = 1.0 / (1.0 + e2)
    cmb = jnp.where(lane == i1, p1, 0.0) + jnp.where(lane == i2, e2 * p1, 0.0)
    cmb_ref[...] = cmb
    cnt_ref[0] = jnp.sum(jnp.where(cmb > 0.0, 1.0, 0.0), axis=0, keepdims=True).astype(jnp.int32)


def _moe_route(x, norm_w, router, tm):
    t, d = x.shape
    ne = router.shape[1]
    rt = jnp.concatenate([router.astype(F32), jnp.zeros((d, LANES - ne), F32)], axis=1)
    return pl.pallas_call(
        _route_kernel,
        grid=(t // tm,),
        in_specs=[pl.BlockSpec((tm, d), lambda i: (i, 0)),
                  pl.BlockSpec((1, d), lambda i: (0, 0)),
                  pl.BlockSpec((d, LANES), lambda i: (0, 0))],
        out_specs=[pl.BlockSpec((tm, d), lambda i: (i, 0)),
                   pl.BlockSpec((tm, LANES), lambda i: (i, 0)),
                   pl.BlockSpec((1, 1, LANES), lambda i: (i, 0, 0))],
        out_shape=[jax.ShapeDtypeStruct((t, d), BF16),
                   jax.ShapeDtypeStruct((t, LANES), F32),
                   jax.ShapeDtypeStruct((t // tm, 1, LANES), jnp.int32)],
        compiler_params=_cparams("parallel"),
    )(x, norm_w.reshape(1, d).astype(F32), rt)


def _moe_kernel(cnt_ref, x_ref, h_ref, cmb_ref, wg_ref, wu_ref, wd_ref, fw_ref, o_ref,
                rank_ref, xc_ref, yc_ref, oh_ref, *, tm, final_norm):
    i = pl.program_id(0)
    e = pl.program_id(1)
    f = pl.program_id(2)
    last_f = f == pl.num_programs(2) - 1
    lane = lax.broadcasted_iota(jnp.int32, (tm, LANES), 1)

    @pl.when((e == 0) & (f == 0))
    def _():
        o_ref[...] = jnp.zeros(o_ref.shape, F32)
        r = lax.broadcasted_iota(jnp.int32, (tm, tm), 0)
        c = lax.broadcasted_iota(jnp.int32, (tm, tm), 1)
        before = jnp.where(c < r, 1.0, 0.0).astype(BF16)
        sel = jnp.where(cmb_ref[...] > 0.0, 1.0, 0.0).astype(BF16)
        rank_ref[...] = jnp.dot(before, sel, preferred_element_type=F32)

    cnt = cnt_ref[i, e]

    def weight():
        return jnp.sum(jnp.where(lane == e, cmb_ref[...], 0.0), axis=-1, keepdims=True)

    def one_hot(off, rows):
        r_e = jnp.sum(jnp.where(lane == e, rank_ref[...], 0.0), axis=-1, keepdims=True)
        key = jnp.where(weight() > 0.0, r_e, -1.0) - jnp.asarray(off, F32)
        slot = lax.broadcasted_iota(jnp.int32, (tm, rows), 1).astype(F32)
        return jnp.where(key == slot, 1.0, 0.0).astype(BF16)

    def process(off, rows, cache):
        rs = pl.ds(off, rows)

        @pl.when(f == 0)
        def _():
            onehot = one_hot(off, rows)
            if cache:
                oh_ref[...] = onehot
            xc_ref[rs, :] = _dot_tn(onehot, h_ref[...]).astype(BF16)
            yc_ref[rs, :] = jnp.zeros((rows, yc_ref.shape[1]), F32)

        xc = xc_ref[rs, :]
        act = _silu(jnp.dot(xc, wg_ref[0, 0], preferred_element_type=F32)) * jnp.dot(
            xc, wu_ref[0, 0], preferred_element_type=F32)
        yc_ref[rs, :] += jnp.dot(act.astype(BF16), wd_ref[0], preferred_element_type=F32)

        @pl.when(last_f)
        def _():
            onehot = oh_ref[...] if cache else one_hot(off, rows)
            o_ref[...] += weight() * jnp.dot(onehot, yc_ref[rs, :].astype(BF16), preferred_element_type=F32)

    @pl.when(cnt > 0)
    def _():
        process(0, MOE_FIRST_ROWS, True)

    def extra(b, carry):
        process(pl.multiple_of(MOE_FIRST_ROWS + b * MOE_BLK, 32), MOE_BLK, False)
        return carry

    lax.fori_loop(0, (jnp.maximum(cnt - MOE_FIRST_ROWS, 0) + MOE_BLK - 1) // MOE_BLK, extra, 0)

    @pl.when((e == pl.num_programs(1) - 1) & last_f)
    def _():
        y = x_ref[...] + o_ref[...]
        if final_norm:
            y = y * lax.rsqrt(jnp.mean(y * y, axis=-1, keepdims=True) + NORM_EPS) * fw_ref[...]
        o_ref[...] = y


def _ffn_moe(x, norm_w, router, w_gate, w_up, w_down, final_w, tm=MOE_TM, tf=1408):
    t, d = x.shape
    ne, _, ff = w_gate.shape
    nf = ff // tf
    col_blocks = lambda w: w.reshape(ne, d, nf, tf).transpose(0, 2, 1, 3)
    h, cmb, cnt = _moe_route(x, norm_w, router, tm)
    final_norm = final_w is not None
    fw = (final_w if final_norm else jnp.ones((d,), F32)).reshape(1, d).astype(F32)
    cap = MOE_FIRST_ROWS + -(-max(tm - MOE_FIRST_ROWS, 0) // MOE_BLK) * MOE_BLK
    grid_spec = pltpu.PrefetchScalarGridSpec(
        num_scalar_prefetch=1,
        grid=(t // tm, ne, nf),
        in_specs=[pl.BlockSpec((tm, d), lambda i, e, f, c: (i, 0)),
                  pl.BlockSpec((tm, d), lambda i, e, f, c: (i, 0)),
                  pl.BlockSpec((tm, LANES), lambda i, e, f, c: (i, 0)),
                  pl.BlockSpec((1, 1, d, tf), lambda i, e, f, c: (e, f, 0, 0)),
                  pl.BlockSpec((1, 1, d, tf), lambda i, e, f, c: (e, f, 0, 0)),
                  pl.BlockSpec((1, tf, d), lambda i, e, f, c: (e, f, 0)),
                  pl.BlockSpec((1, d), lambda i, e, f, c: (0, 0))],
        out_specs=pl.BlockSpec((tm, d), lambda i, e, f, c: (i, 0)),
        scratch_shapes=[pltpu.VMEM((tm, LANES), F32), pltpu.VMEM((cap, d), BF16), pltpu.VMEM((cap, d), F32),
                        pltpu.VMEM((tm, MOE_FIRST_ROWS), BF16)])
    return pl.pallas_call(
        functools.partial(_moe_kernel, tm=tm, final_norm=final_norm),
        grid_spec=grid_spec,
        out_shape=jax.ShapeDtypeStruct((t, d), F32),
        compiler_params=_cparams("parallel", "arbitrary", "arbitrary"),
    )(cnt.reshape(t // tm, LANES), x, h, cmb, col_blocks(w_gate), col_blocks(w_up), w_down, fw)


def _reorder_kernel(w_ref, wc_ref, wn_ref):
    w = w_ref[0]
    wc = [w[:, 0:1024], w[:, 1536:4608], w[:, 1024:1536], w[:, 9760:10016], w[:, 5632:5648],
          w[:, 6672:6688], jnp.zeros((w.shape[0], N_PC - PC_SMALL - 32), w.dtype)]
    wn = [w[:, 6688:9760], w[:, 10016:13088], w[:, 4608:5632], w[:, 5648:6672]]
    wc_ref[...] = jnp.concatenate(wc, axis=1).astype(BF16)
    wn_ref[...] = jnp.concatenate(wn, axis=1).astype(BF16)


def _reorder_w_in(w, layer, tk=128):
    _, d, n = w.shape
    return pl.pallas_call(
        _reorder_kernel,
        grid=(d // tk,),
        in_specs=[pl.BlockSpec((1, tk, n), lambda i: (layer, i, 0))],
        out_specs=[pl.BlockSpec((tk, N_PC), lambda i: (i, 0)), pl.BlockSpec((tk, N_PN), lambda i: (i, 0))],
        out_shape=[jax.ShapeDtypeStruct((d, N_PC), BF16), jax.ShapeDtypeStruct((d, N_PN), BF16)],
        compiler_params=_cparams("parallel"),
    )(w)


def _token_mixer_layer(x, h, i, bsz, seq, p):
    if h is None:
        h = _rmsnorm(x, p["attn_norm_w"][i], BF16)
    wc, wn = _reorder_w_in(p["w_in"], i)
    pc, pn = _inproj(h, wc, wn, p["conv_w"][i], p["conv_b"][i], seq, tm=1024)
    y_ssm = _ssd_branch(pc, pn, p["ssm_a_log"][i], p["ssm_dt_bias"][i], p["ssm_d"][i],
                        p["ssm_norm_w"][i], bsz, seq)
    y_gdn = _gdn_branch(pc, pn, p["gdn_a_log"][i], p["gdn_dt_bias"][i], p["gdn_norm_w"][i],
                        bsz, seq)
    y_rwkv = _rwkv_branch(pc, pn, p["rwkv_mu"][i], p["rwkv_w0"][i], p["rwkv_w2"][i], p["rwkv_a0"][i],
                          p["rwkv_a2"][i], p["rwkv_g2"][i], p["rwkv_k_k"][i], p["rwkv_k_a"][i],
                          p["rwkv_r_k"][i], p["rwkv_ln_w"][i], p["rwkv_ln_b"][i], bsz, seq)
    return _merge(x, y_ssm, y_gdn, y_rwkv, pn, p["proj_ssm"][i], p["proj_gdn"][i], p["proj_rwkv"][i],
                  p["w_out"][i])


def kernel(x, attn_norm_w, w_in, conv_w, conv_b, ssm_a_log, ssm_dt_bias, ssm_d, ssm_norm_w, gdn_a_log, gdn_dt_bias, gdn_norm_w, rwkv_mu, rwkv_w0, rwkv_w2, rwkv_a0, rwkv_a2, rwkv_g2, rwkv_k_k, rwkv_k_a, rwkv_r_k, rwkv_ln_w, rwkv_ln_b, proj_ssm, proj_gdn, proj_rwkv, w_out, ffn_norm_w, dense_w_gate, dense_w_up, dense_w_down, moe_router, moe_w_gate, moe_w_up, moe_w_down, final_norm_w):
    p = dict(attn_norm_w=attn_norm_w, w_in=w_in, conv_w=conv_w, conv_b=conv_b, ssm_a_log=ssm_a_log,
             ssm_dt_bias=ssm_dt_bias, ssm_d=ssm_d, ssm_norm_w=ssm_norm_w, gdn_a_log=gdn_a_log,
             gdn_dt_bias=gdn_dt_bias, gdn_norm_w=gdn_norm_w, rwkv_mu=rwkv_mu, rwkv_w0=rwkv_w0,
             rwkv_w2=rwkv_w2, rwkv_a0=rwkv_a0, rwkv_a2=rwkv_a2, rwkv_g2=rwkv_g2, rwkv_k_k=rwkv_k_k,
             rwkv_k_a=rwkv_k_a, rwkv_r_k=rwkv_r_k, rwkv_ln_w=rwkv_ln_w, rwkv_ln_b=rwkv_ln_b,
             proj_ssm=proj_ssm, proj_gdn=proj_gdn, proj_rwkv=proj_rwkv, w_out=w_out)
    for name in ("w_in", "proj_ssm", "proj_gdn", "proj_rwkv", "w_out"):
        p[name] = p[name].astype(BF16)
    dense_w_gate, dense_w_up, dense_w_down = (t.astype(BF16) for t in (dense_w_gate, dense_w_up, dense_w_down))
    moe_w_gate, moe_w_up, moe_w_down = (t.astype(BF16) for t in (moe_w_gate, moe_w_up, moe_w_down))
    bsz, seq, d = x.shape
    depth = attn_norm_w.shape[0]
    xt = x.reshape(bsz * seq, d)
    h_next = None
    for i in range(depth):
        xt = _token_mixer_layer(xt, h_next, i, bsz, seq, p)
        h_next = None
        j = i // 2
        last = i == depth - 1
        if i % 2 == 0:
            if last:
                xt = _ffn_dense(xt, ffn_norm_w[i], dense_w_gate[j], dense_w_up[j], dense_w_down[j])
                xt = _rmsnorm(xt, final_norm_w, F32)
            else:
                xt, h_next = _ffn_dense(xt, ffn_norm_w[i], dense_w_gate[j], dense_w_up[j], dense_w_down[j],
                                        next_norm_w=attn_norm_w[i + 1])
        else:
            xt = _ffn_moe(xt, ffn_norm_w[i], moe_router[j], moe_w_gate[j], moe_w_up[j], moe_w_down[j],
                          final_norm_w if last else None)
    return xt.reshape(bsz, seq, d)
```

```python
import functools
import math

import numpy as np
import jax
import jax.numpy as jnp
from jax import lax
from jax.experimental import pallas as pl
from jax.experimental.pallas import tpu as pltpu

F32 = jnp.float32
BF16 = jnp.bfloat16

LANES = 128
SUBLANES = 8
VMEM_LIMIT_BYTES = 56 * 1024 * 1024

D_MODEL = 1024
SSM_HEADS, SSM_HEAD_DIM, SSM_GROUPS, SSM_STATE = 16, 64, 2, 128
SSM_INNER = SSM_HEADS * SSM_HEAD_DIM
GDN_HEADS, GDN_HEAD_DIM = 8, 128
GDN_INNER = GDN_HEADS * GDN_HEAD_DIM
RWKV_HEADS, RWKV_HEAD_DIM = 16, 64
RWKV_INNER = RWKV_HEADS * RWKV_HEAD_DIM
RWKV_DECAY_LORA, RWKV_ICLR_LORA, RWKV_GATE_LORA = 64, 64, 128
RWKV_GN_EPS = 64e-5
CONV_K = 4
FFN_DIM = 2816
N_EXPERTS = 8
NORM_EPS = 1e-6
L2_EPS = 1e-6

PC_XS, PC_Q, PC_K, PC_V = 0, 1024, 2048, 3072
PC_BS, PC_CS = 4096, 4352
PC_CONV_END = 4608
PC_LORA = 4608
PC_SMALL = 4864
N_PC = 5120
PN_RKV = 0
PN_GATE = 3072
PN_ZSSM, PN_ZGDN = 6144, 7168
N_PN = 8192
SMALL_DT, SMALL_A, SMALL_B = 0, 16, 24

SSD_CHUNK = 128
DPLR_CHUNK = 64
SEQ_PER_STEP = 4
INPROJ_ROW_CHUNK = 512


def _cparams(*sem):
    return pltpu.CompilerParams(dimension_semantics=sem, vmem_limit_bytes=VMEM_LIMIT_BYTES)


def _dot(a, b):
    return jnp.dot(a.astype(BF16), b.astype(BF16), preferred_element_type=F32)


def _dot_nt(a, b):
    return lax.dot_general(a.astype(BF16), b.astype(BF16), (((1,), (1,)), ((), ())),
                           preferred_element_type=F32)


def _dot_tn(a, b):
    return lax.dot_general(a.astype(BF16), b.astype(BF16), (((0,), (0,)), ((), ())),
                           preferred_element_type=F32)


def _split3(x):
    hi = x.astype(BF16)
    r1 = x - hi.astype(F32)
    mid = r1.astype(BF16)
    lo = (r1 - mid.astype(F32)).astype(BF16)
    return hi, mid, lo


def _dot_x3(x, w):
    hi, mid, lo = _split3(x)
    return _dot(hi, w) + _dot(mid, w) + _dot(lo, w)


def _dot_w3(w, x):
    hi, mid, lo = _split3(x)
    return _dot(w, hi) + _dot(w, mid) + _dot(w, lo)


def _dot_hl(a, b):
    ah = a.astype(BF16)
    al = (a - ah.astype(F32)).astype(BF16)
    bh = b.astype(BF16)
    bl = (b - bh.astype(F32)).astype(BF16)
    return _dot(ah, bh) + _dot(ah, bl) + _dot(al, bh)


def _dot_a2(a, b):
    ah = a.astype(BF16)
    al = (a - ah.astype(F32)).astype(BF16)
    return _dot(ah, b) + _dot(al, b)


def _sigmoid(x):
    return 0.5 * jnp.tanh(0.5 * x) + 0.5


def _silu(x):
    return x * _sigmoid(x)


def _softplus(x):
    return jnp.maximum(x, 0.0) + jnp.log(1.0 + jnp.exp(-jnp.abs(x)))


def _tri_incl(n):
    r = lax.broadcasted_iota(jnp.int32, (n, n), 0)
    c = lax.broadcasted_iota(jnp.int32, (n, n), 1)
    return jnp.where(r >= c, 1.0, 0.0).astype(BF16)


def _half_ones():
    r = lax.broadcasted_iota(jnp.int32, (LANES, LANES), 0)
    c = lax.broadcasted_iota(jnp.int32, (LANES, LANES), 1)
    return jnp.where((r // 64) == (c // 64), 1.0, 0.0).astype(BF16)


def _rmsnorm_kernel(x_ref, w_ref, o_ref):
    x = x_ref[...]
    ms = jnp.mean(x * x, axis=-1, keepdims=True)
    o_ref[...] = (x * lax.rsqrt(ms + NORM_EPS) * w_ref[...]).astype(o_ref.dtype)


def _rmsnorm(x, w, out_dtype, tm=512):
    t, d = x.shape
    return pl.pallas_call(
        _rmsnorm_kernel,
        grid=(t // tm,),
        in_specs=[pl.BlockSpec((tm, d), lambda i: (i, 0)),
                  pl.BlockSpec((1, d), lambda i: (0, 0))],
        out_specs=pl.BlockSpec((tm, d), lambda i: (i, 0)),
        out_shape=jax.ShapeDtypeStruct((t, d), out_dtype),
        compiler_params=_cparams("parallel"),
    )(x, w.reshape(1, d))


def _inproj_kernel(a_ref, wc_ref, wn_ref, cw_ref, cb_ref, cm_ref, oc_ref, on_ref, sh_ref, *,
                   tm, tiles_per_seq):
    i = pl.program_id(1)
    tc = oc_ref.shape[1]

    @pl.when(i % tiles_per_seq == 0)
    def _():
        sh_ref[0:SUBLANES, :] = jnp.zeros((SUBLANES, tc), F32)

    rc = min(tm, INPROJ_ROW_CHUNK)
    starts = list(range(0, tm, rc))
    acc = jnp.dot(a_ref[0:rc, :], wc_ref[...], preferred_element_type=F32)
    for n, r0 in enumerate(starts):
        sh_ref[SUBLANES + r0:SUBLANES + r0 + rc, :] = acc
        on_ref[r0:r0 + rc, :] = jnp.dot(a_ref[r0:r0 + rc, :], wn_ref[...], preferred_element_type=F32)
        if n + 1 < len(starts):
            acc_next = jnp.dot(a_ref[r0 + rc:r0 + 2 * rc, :], wc_ref[...], preferred_element_type=F32)
        c = acc * cw_ref[CONV_K - 1:CONV_K, :] + cb_ref[...]
        for k in range(1, CONV_K):
            c = c + sh_ref[pl.ds(SUBLANES + r0 - k, rc), :] * cw_ref[CONV_K - 1 - k:CONV_K - k, :]
        oc_ref[r0:r0 + rc, :] = jnp.where(cm_ref[...] > 0.0, _silu(c), acc)
        if n + 1 < len(starts):
            acc = acc_next
    sh_ref[0:SUBLANES, :] = sh_ref[tm:tm + SUBLANES, :]


def _conv_cols(c):
    r = c.shape[0]
    return jnp.concatenate([c[:, 0:1024], c[:, 1536:4608], c[:, 1024:1536],
                            jnp.zeros((r, N_PC - PC_CONV_END), F32)], axis=1).astype(F32)


def _inproj(a, wc, wn, conv_w, conv_b, seq, tm, nsteps=4):
    m, k = a.shape
    tc, tn = N_PC // nsteps, N_PN // nsteps
    cw = _conv_cols(conv_w)
    cb = _conv_cols(conv_b.reshape(1, -1))
    cm = _conv_cols(jnp.ones((1, conv_w.shape[1]), F32))
    return pl.pallas_call(
        functools.partial(_inproj_kernel, tm=tm, tiles_per_seq=seq // tm),
        grid=(nsteps, m // tm),
        in_specs=[pl.BlockSpec((tm, k), lambda j, i: (i, 0)),
                  pl.BlockSpec((k, tc), lambda j, i: (0, j)),
                  pl.BlockSpec((k, tn), lambda j, i: (0, j)),
                  pl.BlockSpec((CONV_K, tc), lambda j, i: (0, j)),
                  pl.BlockSpec((1, tc), lambda j, i: (0, j)),
                  pl.BlockSpec((1, tc), lambda j, i: (0, j))],
        out_specs=[pl.BlockSpec((tm, tc), lambda j, i: (i, j)),
                   pl.BlockSpec((tm, tn), lambda j, i: (i, j))],
        out_shape=[jax.ShapeDtypeStruct((m, N_PC), F32), jax.ShapeDtypeStruct((m, N_PN), F32)],
        scratch_shapes=[pltpu.VMEM((tm + SUBLANES, tc), F32)],
        compiler_params=_cparams("parallel", "arbitrary"),
    )(a, wc, wn, cw, cb, cm)


def _ssd_kernel(xs_ref, bs_ref, cs_ref, z_ref, sm_ref, alog_ref, dtb_ref, dsk_ref, nw_ref, ex_ref,
                o_ref, st_ref, *, L, nseq):
    @pl.when(pl.program_id(1) == 0)
    def _():
        st_ref[...] = jnp.zeros(st_ref.shape, F32)

    hd = SSM_HEAD_DIM
    gw = SSM_INNER // SSM_GROUPS
    heads_per_group = SSM_HEADS // SSM_GROUPS
    ex = ex_ref[...]
    row = lax.broadcasted_iota(jnp.int32, (L, L), 0)
    col = lax.broadcasted_iota(jnp.int32, (L, L), 1)
    causal = row >= col
    lane = lax.broadcasted_iota(jnp.int32, (L, LANES), 1)
    first_half = lane < hd
    for s in range(nseq):
        dt = _softplus(sm_ref[s] + dtb_ref[...])
        la = -jnp.exp(alog_ref[...]) * dt
        g_cum = _dot_w3(_tri_incl(L), la)
        g_cum_t = g_cum.T
        dt_e = _dot_a2(dt, ex)
        g_e = _dot_x3(g_cum, ex)
        g_last = g_e[L - 1:L, :]
        e_g = jnp.exp(g_e)
        e_dec = jnp.exp(g_last - g_e)
        e_last = jnp.exp(g_last)

        xs = xs_ref[s]
        xdt = xs * dt_e
        xdec = xdt * e_dec
        y_parts = []
        for g in range(SSM_GROUPS):
            gsl = slice(g * gw, (g + 1) * gw)
            b_g = bs_ref[s, :, g * SSM_STATE:(g + 1) * SSM_STATE]
            c_g = cs_ref[s, :, g * SSM_STATE:(g + 1) * SSM_STATE]
            cb = _dot_nt(c_g, b_g)
            st_g = st_ref[s, :, gsl]
            y_off = _dot(c_g, st_g) * e_g[:, gsl]
            st_ref[s, :, gsl] = st_g * e_last[:, gsl] + _dot_tn(b_g, xdec[:, gsl])
            for p in range(heads_per_group // 2):
                h0 = g * heads_per_group + 2 * p
                base = h0 * hd
                sc = []
                for h in (h0, h0 + 1):
                    diff = g_cum[:, h:h + 1] - g_cum_t[h:h + 1, :]
                    sc.append(cb * jnp.exp(jnp.where(causal, diff, -jnp.inf)))
                yd = _dot(jnp.concatenate(sc, axis=0), xdt[:, base:base + LANES])
                y_parts.append(jnp.where(first_half, yd[:L], yd[L:])
                               + y_off[:, base - g * gw:base - g * gw + LANES])
        y = jnp.concatenate(y_parts, axis=1) + dsk_ref[...] * xs
        y = y * _silu(z_ref[s])
        outs = []
        for g in range(SSM_GROUPS):
            seg = y[:, g * gw:(g + 1) * gw]
            outs.append(seg * lax.rsqrt(jnp.mean(seg * seg, axis=-1, keepdims=True) + NORM_EPS))
        o_ref[s] = (jnp.concatenate(outs, axis=1) * nw_ref[...]).astype(o_ref.dtype)


def _pad_lanes(v, offset=0, width=LANES):
    out = jnp.zeros((1, width), F32)
    return lax.dynamic_update_slice(out, v.reshape(1, -1).astype(F32), (0, offset))


def _ssd_branch(pc, pn, a_log, dt_bias, d_skip, norm_w, bsz, seq):
    L = SSD_CHUNK
    nseq = SEQ_PER_STEP if bsz % SEQ_PER_STEP == 0 else 1
    pc3 = pc.reshape(bsz, seq, N_PC)
    pn3 = pn.reshape(bsz, seq, N_PN)
    ex = np.zeros((LANES, SSM_INNER), np.float32)
    for h in range(SSM_HEADS):
        ex[h, h * SSM_HEAD_DIM:(h + 1) * SSM_HEAD_DIM] = 1.0
    vec = lambda w: pl.BlockSpec((1, w), lambda b, c: (0, 0))
    tok = lambda width, col: pl.BlockSpec((nseq, L, width), lambda b, c: (b, c, col // width))
    out = pl.pallas_call(
        functools.partial(_ssd_kernel, L=L, nseq=nseq),
        grid=(bsz // nseq, seq // L),
        in_specs=[tok(SSM_INNER, PC_XS), tok(256, PC_BS), tok(256, PC_CS), tok(SSM_INNER, PN_ZSSM),
                  tok(LANES, PC_SMALL),
                  vec(LANES), vec(LANES), vec(SSM_INNER), vec(SSM_INNER),
                  pl.BlockSpec((LANES, SSM_INNER), lambda b, c: (0, 0))],
        out_specs=pl.BlockSpec((nseq, L, SSM_INNER), lambda b, c: (b, c, 0)),
        out_shape=jax.ShapeDtypeStruct((bsz, seq, SSM_INNER), BF16),
        scratch_shapes=[pltpu.VMEM((nseq, SSM_STATE, SSM_INNER), F32)],
        compiler_params=_cparams("parallel", "arbitrary"),
    )(pc3, pc3, pc3, pn3, pc3,
      _pad_lanes(a_log, SMALL_DT), _pad_lanes(dt_bias, SMALL_DT),
      jnp.repeat(d_skip.astype(F32), SSM_HEAD_DIM).reshape(1, SSM_INNER),
      norm_w.reshape(1, SSM_INNER).astype(F32), jnp.asarray(ex, BF16))
    return out.reshape(bsz * seq, SSM_INNER)


def _head_stack(x):
    half = x.shape[1] // 2
    lane = lax.broadcasted_iota(jnp.int32, x.shape, 1)
    return jnp.concatenate([jnp.where(lane < half, x, 0.0), jnp.where(lane >= half, x, 0.0)], axis=0)


def _dplr_pairs(units, *, L):
    nu = len(units)
    P = 2 * L
    vc = units[0]["v"].shape[1]
    row = lax.broadcasted_iota(jnp.int32, (L, P), 0)
    tcol = lax.broadcasted_iota(jnp.int32, (L, P), 1)
    first = tcol < L
    tcol = jnp.where(first, tcol, tcol - L)
    strict, lower = tcol < row, tcol <= row

    sc = [_dot_nt(jnp.concatenate([u["a_sc"], u["r_sc"]], axis=0),
                  jnp.concatenate([_head_stack(u["b_sc"]), _head_stack(u["k_sc"])], axis=0))
          for u in units]
    v_st = [_head_stack(u["v"]) for u in units]

    def read(u):
        rows = jnp.concatenate([u["a_sr"], u["r_sr"]], axis=0)
        st = u["s_prev"]
        if len(st) == 1:
            return _dot_nt(rows, st[0])
        kh = rows.shape[1] // 2
        return jnp.concatenate([_dot_nt(rows[:, :kh], st[0]), _dot_nt(rows[:, kh:], st[1])], axis=1)

    reads = [read(u) for u in units]
    n_mat, a_rb, aks = [], [], []
    for i, u in enumerate(units):
        sb, sk = sc[i][:, :P], sc[i][:, P:]
        if u.get("dm_a") is None:
            n_mat.append(jnp.where(strict, sb[:L], 0.0))
            a_rb.append(jnp.where(lower, sb[L:], 0.0))
            aks.append(jnp.concatenate([jnp.where(strict, sk[:L], 0.0), jnp.where(lower, sk[L:], 0.0)], axis=0))
        else:
            n_mat.append(sb[:L] * u["dm_a"])
            a_rb.append(sb[L:] * u["dm_r"])
            aks.append(jnp.concatenate([sk[:L] * u["dm_a"], sk[L:] * u["dm_r"]], axis=0))
    av = [_dot(aks[i], v_st[i]) for i in range(nu)]
    x = [reads[i][:L] + av[i][:L] for i in range(nu)]
    y_part = [reads[i][L:] + av[i][L:] for i in range(nu)]

    steps = int(math.log2(L))
    for s in range(steps):
        x_st = [_head_stack(x[i]) for i in range(nu)]
        if s + 1 < steps:
            t = [_dot(n_mat[i], jnp.concatenate(
                    [x_st[i], jnp.concatenate([jnp.where(first, n_mat[i], 0.0),
                                               jnp.where(first, 0.0, n_mat[i])], axis=0)], axis=1))
                 for i in range(nu)]
            x = [x[i] + t[i][:, :vc] for i in range(nu)]
            n_mat = [t[i][:, vc:] for i in range(nu)]
        else:
            x = [x[i] + _dot(n_mat[i], x_st[i]) for i in range(nu)]
    ys = [y_part[i] + _dot(a_rb[i], _head_stack(x[i])) for i in range(nu)]

    s_new = []
    for i, u in enumerate(units):
        xv = jnp.concatenate([x[i], u["v"]], axis=0)
        bk = jnp.concatenate([u["b_dec"], u["k_dec"]], axis=0)
        st, pl_ = u["s_prev"], u["p_last"]
        if len(st) == 1:
            upd = _dot_tn(xv, bk)
            ri = lax.broadcasted_iota(jnp.int32, upd.shape, 0) // (upd.shape[0] // 2)
            ci = lax.broadcasted_iota(jnp.int32, upd.shape, 1) // (upd.shape[1] // 2)
            s_new.append([st[0] * pl_[0] + jnp.where(ri == ci, upd, 0.0)])
        else:
            vh, kh = xv.shape[1] // 2, bk.shape[1] // 2
            s_new.append([st[0] * pl_[0] + _dot_tn(xv[:, :vh], bk[:, :kh]),
                          st[1] * pl_[1] + _dot_tn(xv[:, vh:], bk[:, kh:])])
    return ys, s_new


def _gdn_kernel(q_ref, k_ref, v_ref, z_ref, sm_ref, alog_ref, dtb_ref, nw_ref, o_ref, st_ref, *, L, nseq):
    @pl.when(pl.program_id(1) == 0)
    def _():
        st_ref[...] = jnp.zeros(st_ref.shape, F32)

    hd = GDN_HEAD_DIM
    row = lax.broadcasted_iota(jnp.int32, (L, 2 * L), 0)
    tcol = lax.broadcasted_iota(jnp.int32, (L, 2 * L), 1)
    first_t = tcol < L
    tcol = jnp.where(first_t, tcol, tcol - L)
    first_c = lax.broadcasted_iota(jnp.int32, (L, 2 * hd), 1) < hd
    nw = nw_ref[...]
    scale = hd ** -0.5
    units = []
    for s in range(nseq):
        sm = sm_ref[s]
        g = -jnp.exp(alog_ref[...]) * _softplus(sm + dtb_ref[...])
        beta = _sigmoid(sm)
        g_cum = _dot_w3(_tri_incl(L), g)
        g_cum_t = jnp.concatenate([g_cum] * (LANES // L), axis=0).T
        for p in range(GDN_HEADS // 2):
            sl = slice(2 * p * hd, (2 * p + 2) * hd)
            q = q_ref[s, :, sl]
            k = k_ref[s, :, sl]

            def per_head(t):
                return jnp.where(first_c, jnp.sum(t[:, :hd], axis=-1, keepdims=True),
                                 jnp.sum(t[:, hd:], axis=-1, keepdims=True))

            def col(arr, base):
                h0 = base + 2 * p
                return arr[:, h0:h0 + 1], arr[:, h0 + 1:h0 + 2]

            qn = q * (lax.rsqrt(per_head(q * q) + L2_EPS) * scale)
            kn = k * lax.rsqrt(per_head(k * k) + L2_EPS)
            gc0, gc1 = col(g_cum, SMALL_A)
            gs0, gs1 = col(g, SMALL_A)
            bt0, bt1 = col(beta, SMALL_B)
            ch = lambda a0, a1: jnp.where(first_c, a0, a1)
            tm_ = lambda a0, a1: jnp.where(first_t, a0, a1)
            gc, gs, bt = ch(gc0, gc1), ch(gs0, gs1), ch(bt0, bt1)
            g_prev = gc - gs
            h0 = SMALL_A + 2 * p
            gr = tm_(g_cum_t[h0:h0 + 1, :], g_cum_t[h0 + 1:h0 + 2, :])
            ge0, ge1 = g_cum[L - 1:L, h0:h0 + 1], g_cum[L - 1:L, h0 + 1:h0 + 2]
            e_end = jnp.exp(ch(ge0, ge1) - gc)
            b_vec = -(bt * jnp.exp(gs)) * kn
            k_vec = bt * kn
            units.append(dict(
                a_sc=kn, r_sc=qn, b_sc=b_vec, k_sc=k_vec,
                a_sr=kn * jnp.exp(g_prev), r_sr=qn * jnp.exp(gc),
                b_dec=b_vec * e_end, k_dec=k_vec * e_end,
                v=v_ref[s, :, sl], s_prev=[st_ref[s, 2 * p], st_ref[s, 2 * p + 1]],
                p_last=[jnp.exp(ge0), jnp.exp(ge1)],
                dm_a=jnp.exp(jnp.where(tcol < row, tm_(gc0 - gs0, gc1 - gs1) - gr, -jnp.inf)),
                dm_r=jnp.exp(jnp.where(tcol <= row, tm_(gc0, gc1) - gr, -jnp.inf))))
    ys, s_new = _dplr_pairs(units, L=L)
    npair = GDN_HEADS // 2
    for s in range(nseq):
        for p in range(npair):
            sl = slice(2 * p * hd, (2 * p + 2) * hd)
            st_ref[s, 2 * p] = s_new[s * npair + p][0]
            st_ref[s, 2 * p + 1] = s_new[s * npair + p][1]
            y = ys[s * npair + p]
            ms = jnp.where(first_c, jnp.mean(y[:, :hd] * y[:, :hd], axis=-1, keepdims=True),
                           jnp.mean(y[:, hd:] * y[:, hd:], axis=-1, keepdims=True))
            yn = y * lax.rsqrt(ms + NORM_EPS) * nw
            o_ref[s, :, sl] = (yn * _silu(z_ref[s, :, sl])).astype(o_ref.dtype)


def _gdn_branch(pc, pn, a_log, dt_bias, norm_w, bsz, seq):
    L = DPLR_CHUNK
    nseq = SEQ_PER_STEP if bsz % SEQ_PER_STEP == 0 else 1
    pc3 = pc.reshape(bsz, seq, N_PC)
    pn3 = pn.reshape(bsz, seq, N_PN)
    vec = lambda w: pl.BlockSpec((1, w), lambda b, c: (0, 0))
    tok = lambda width, col: pl.BlockSpec((nseq, L, width), lambda b, c: (b, c, col // width))
    out = pl.pallas_call(
        functools.partial(_gdn_kernel, L=L, nseq=nseq),
        grid=(bsz // nseq, seq // L),
        in_specs=[tok(GDN_INNER, PC_Q), tok(GDN_INNER, PC_K), tok(GDN_INNER, PC_V),
                  tok(GDN_INNER, PN_ZGDN), tok(LANES, PC_SMALL),
                  vec(LANES), vec(LANES), vec(2 * GDN_HEAD_DIM)],
        out_specs=pl.BlockSpec((nseq, L, GDN_INNER), lambda b, c: (b, c, 0)),
        out_shape=jax.ShapeDtypeStruct((bsz, seq, GDN_INNER), BF16),
        scratch_shapes=[pltpu.VMEM((nseq, GDN_HEADS, LANES, LANES), F32)],
        compiler_params=_cparams("parallel", "arbitrary"),
    )(pc3, pc3, pc3, pn3, pc3,
      _pad_lanes(a_log, SMALL_A), _pad_lanes(dt_bias, SMALL_A),
      jnp.tile(norm_w.reshape(1, GDN_HEAD_DIM).astype(F32), (1, 2)))
    return out.reshape(bsz * seq, GDN_INNER)


def _rwkv_kernel(rkv_ref, lora_ref, mu_ref, mul_ref, w0_ref, w2_ref, a0_ref, a2_ref, g2_ref,
                 kk_ref, ka_ref, rk_ref, lnw_ref, lnb_ref, o_ref, sh_ref, shl_ref, st_ref, *, L, nseq):
    @pl.when(pl.program_id(1) == 0)
    def _():
        st_ref[...] = jnp.zeros(st_ref.shape, F32)
        for s in range(nseq):
            sh_ref[s, 0:SUBLANES, :] = jnp.zeros((SUBLANES, sh_ref.shape[2]), F32)
            shl_ref[s, 0:SUBLANES, :] = jnp.zeros((SUBLANES, shl_ref.shape[2]), F32)

    n = RWKV_INNER
    npair = RWKV_HEADS // 2
    ones_bd = _half_ones()
    inv_hd = 1.0 / RWKV_HEAD_DIM

    def head_sums(t):
        st = jnp.concatenate([t[:, p * LANES:(p + 1) * LANES] for p in range(npair)], axis=0)
        sm = _dot(st, ones_bd)
        return jnp.concatenate([sm[p * L:(p + 1) * L] for p in range(npair)], axis=1)

    units, tails = [], []
    for s in range(nseq):
        u = rkv_ref[s]
        ul = lora_ref[s]
        sh_ref[s, SUBLANES:SUBLANES + L, :] = u
        shl_ref[s, SUBLANES:SUBLANES + L, :] = ul
        u = u + (sh_ref[s, pl.ds(SUBLANES - 1, L), :] - u) * mu_ref[...]
        ul = ul + (shl_ref[s, pl.ds(SUBLANES - 1, L), :] - ul) * mul_ref[...]
        sh_ref[s, 0:SUBLANES, :] = sh_ref[s, L:L + SUBLANES, :]
        shl_ref[s, 0:SUBLANES, :] = shl_ref[s, L:L + SUBLANES, :]

        r, k, v = u[:, 0:n], u[:, n:2 * n], u[:, 2 * n:3 * n]
        lo = ul[:, 0:LANES]
        dg = ul[:, LANES:2 * LANES]
        w_log = -_softplus(-(w0_ref[...] + _dot_a2(jnp.tanh(lo), w2_ref[...]))) - 0.5
        lw = -jnp.exp(w_log)
        a_ic = _sigmoid(a0_ref[...] + _dot_a2(lo, a2_ref[...]))
        gate = _dot(_sigmoid(dg), g2_ref[...])
        g_cum = _dot_w3(_tri_incl(L), lw)
        kk_raw = k * kk_ref[...]
        k_mod = k * (1.0 + (a_ic - 1.0) * ka_ref[...])
        kk_all = kk_raw * lax.rsqrt(head_sums(kk_raw * kk_raw) + L2_EPS)
        tails.append((head_sums(r * k_mod * rk_ref[...]) * v, gate))
        for p in range(npair):
            sl = slice(p * LANES, (p + 1) * LANES)
            kk = kk_all[:, sl]
            gc = g_cum[:, sl]
            g_end = gc[L - 1:L, :]
            e_neg = jnp.exp(-gc)
            e_end = jnp.exp(g_end - gc)
            b_vec = kk * a_ic[:, sl]
            k_vec = k_mod[:, sl]
            a_t = -kk * jnp.exp(gc - lw[:, sl])
            r_t = r[:, sl] * jnp.exp(gc)
            units.append(dict(
                a_sc=a_t, r_sc=r_t, b_sc=b_vec * e_neg, k_sc=k_vec * e_neg,
                a_sr=a_t, r_sr=r_t, b_dec=b_vec * e_end, k_dec=k_vec * e_end,
                v=v[:, sl], s_prev=[st_ref[s, p]], p_last=[jnp.exp(g_end)]))
    ys, s_new = _dplr_pairs(units, L=L)
    for s in range(nseq):
        for p in range(npair):
            st_ref[s, p] = s_new[s * npair + p][0]
        y = jnp.concatenate(ys[s * npair:(s + 1) * npair], axis=1)
        yc = y - head_sums(y) * inv_hd
        var = head_sums(yc * yc) * inv_hd
        yn = yc * lax.rsqrt(var + RWKV_GN_EPS) * lnw_ref[...] + lnb_ref[...]
        bonus, gate = tails[s]
        o_ref[s] = ((yn + bonus) * gate).astype(o_ref.dtype)


def _rwkv_branch(pc, pn, mu, w0, w2, a0, a2, g2, k_k, k_a, r_k, ln_w, ln_b, bsz, seq):
    L = DPLR_CHUNK
    n = RWKV_INNER
    nseq = SEQ_PER_STEP if bsz % SEQ_PER_STEP == 0 else 1
    pc3 = pc.reshape(bsz, seq, N_PC)
    pn3 = pn.reshape(bsz, seq, N_PN)
    vec = lambda w: pl.BlockSpec((1, w), lambda b, c: (0, 0))
    mat = lambda: pl.BlockSpec((LANES, n), lambda b, c: (0, 0))
    tok = lambda width, col: pl.BlockSpec((nseq, L, width), lambda b, c: (b, c, col // width))
    w2p = jnp.concatenate([w2, jnp.zeros((LANES - RWKV_DECAY_LORA, n), F32)], axis=0)
    a2p = jnp.concatenate([jnp.zeros((RWKV_DECAY_LORA, n), F32), a2], axis=0)
    r1 = lambda t: t.reshape(1, -1).astype(F32)
    out = pl.pallas_call(
        functools.partial(_rwkv_kernel, L=L, nseq=nseq),
        grid=(bsz // nseq, seq // L),
        in_specs=[tok(3 * n, PN_RKV), tok(256, PC_LORA),
                  vec(3 * n), vec(256), vec(n), mat(), vec(n), mat(), mat(),
                  vec(n), vec(n), vec(n), vec(n), vec(n)],
        out_specs=pl.BlockSpec((nseq, L, n), lambda b, c: (b, c, 0)),
        out_shape=jax.ShapeDtypeStruct((bsz, seq, n), BF16),
        scratch_shapes=[pltpu.VMEM((nseq, L + SUBLANES, 3 * n), F32),
                        pltpu.VMEM((nseq, L + SUBLANES, 256), F32),
                        pltpu.VMEM((nseq, RWKV_HEADS // 2, LANES, LANES), F32)],
        compiler_params=_cparams("parallel", "arbitrary"),
    )(pn3, pc3, r1(mu[:3 * n]), r1(mu[3 * n:]), r1(w0), w2p, r1(a0), a2p, g2.astype(F32),
      r1(k_k), r1(k_a), r1(r_k), r1(ln_w), r1(ln_b))
    return out.reshape(bsz * seq, n)


def _merge_kernel(x_ref, ys_ref, yg_ref, yr_ref, gate_ref, ps_ref, pg_ref, pr_ref, wo_ref, o_ref):
    d = D_MODEL
    m = _sigmoid(gate_ref[:, 0:d]) * jnp.dot(ys_ref[...], ps_ref[...], preferred_element_type=F32)
    m = m + _sigmoid(gate_ref[:, d:2 * d]) * jnp.dot(yg_ref[...], pg_ref[...], preferred_element_type=F32)
    m = m + _sigmoid(gate_ref[:, 2 * d:3 * d]) * jnp.dot(yr_ref[...], pr_ref[...], preferred_element_type=F32)
    o_ref[...] = x_ref[...] + jnp.dot(m.astype(BF16), wo_ref[...], preferred_element_type=F32)


def _merge(x, y_ssm, y_gdn, y_rwkv, pn, p_ssm, p_gdn, p_rwkv, w_out, tm=512):
    t, d = x.shape
    tok = lambda: pl.BlockSpec((tm, d), lambda i: (i, 0))
    wgt = lambda: pl.BlockSpec((d, d), lambda i: (0, 0))
    return pl.pallas_call(
        _merge_kernel,
        grid=(t // tm,),
        in_specs=[tok(), tok(), tok(), tok(),
                  pl.BlockSpec((tm, 3 * d), lambda i: (i, PN_GATE // (3 * d))),
                  wgt(), wgt(), wgt(), wgt()],
        out_specs=tok(),
        out_shape=jax.ShapeDtypeStruct((t, d), F32),
        compiler_params=_cparams("parallel"),
    )(x, y_ssm, y_gdn, y_rwkv, pn, p_ssm, p_gdn, p_rwkv, w_out)


def _ffn_kernel(x_ref, nw_ref, wg_ref, wu_ref, wd_ref, *rest, emit_next):
    if emit_next:
        nnw_ref, o_ref, hn_ref, h_ref, acc_ref = rest
    else:
        o_ref, h_ref, acc_ref = rest
    f = pl.program_id(1)

    @pl.when(f == 0)
    def _():
        x = x_ref[...]
        ms = jnp.mean(x * x, axis=-1, keepdims=True)
        h_ref[...] = (x * lax.rsqrt(ms + NORM_EPS) * nw_ref[...]).astype(BF16)
        acc_ref[...] = x

    h = h_ref[...]
    act = _silu(jnp.dot(h, wg_ref[...], preferred_element_type=F32)) * jnp.dot(
        h, wu_ref[...], preferred_element_type=F32)
    acc_ref[...] += jnp.dot(act.astype(BF16), wd_ref[...], preferred_element_type=F32)

    @pl.when(f == pl.num_programs(1) - 1)
    def _():
        y = acc_ref[...]
        o_ref[...] = y
        if emit_next:
            ms = jnp.mean(y * y, axis=-1, keepdims=True)
            hn_ref[...] = (y * lax.rsqrt(ms + NORM_EPS) * nnw_ref[...]).astype(BF16)


def _ffn_dense(x, norm_w, w_gate, w_up, w_down, next_norm_w=None, tm=512, tf=2816):
    t, d = x.shape
    nf = w_gate.shape[1] // tf
    emit_next = next_norm_w is not None
    tok = pl.BlockSpec((tm, d), lambda i, f: (i, 0))
    vec = pl.BlockSpec((1, d), lambda i, f: (0, 0))
    in_specs = [tok, vec,
                pl.BlockSpec((d, tf), lambda i, f: (0, f)),
                pl.BlockSpec((d, tf), lambda i, f: (0, f)),
                pl.BlockSpec((tf, d), lambda i, f: (f, 0))]
    args = [x, norm_w.reshape(1, d).astype(F32), w_gate, w_up, w_down]
    out_specs, out_shape = tok, jax.ShapeDtypeStruct((t, d), F32)
    if emit_next:
        in_specs.append(vec)
        args.append(next_norm_w.reshape(1, d).astype(F32))
        out_specs, out_shape = [tok, tok], [out_shape, jax.ShapeDtypeStruct((t, d), BF16)]
    return pl.pallas_call(
        functools.partial(_ffn_kernel, emit_next=emit_next),
        grid=(t // tm, nf),
        in_specs=in_specs,
        out_specs=out_specs,
        out_shape=out_shape,
        scratch_shapes=[pltpu.VMEM((tm, d), BF16), pltpu.VMEM((tm, d), F32)],
        compiler_params=_cparams("parallel", "arbitrary"),
    )(*args)


MOE_TM = 1024
MOE_BLK = 128
MOE_FIRST_ROWS = 288


def _route_kernel(x_ref, nw_ref, rt_ref, h_ref, cmb_ref, cnt_ref):
    x = x_ref[...]
    lane = lax.broadcasted_iota(jnp.int32, cmb_ref.shape, 1)
    ms = jnp.mean(x * x, axis=-1, keepdims=True)
    hf = x * lax.rsqrt(ms + NORM_EPS) * nw_ref[...]
    h_ref[...] = hf.astype(BF16)
    logits = jnp.where(lane < N_EXPERTS, _dot_hl(hf, rt_ref[...]), -jnp.inf)
    m1 = jnp.max(logits, axis=-1, keepdims=True)
    i1 = jnp.min(jnp.where(logits == m1, lane, LANES), axis=-1, keepdims=True)
    rest = jnp.where(lane == i1, -jnp.inf, logits)
    m2 = jnp.max(rest, axis=-1, keepdims=True)
    i2 = jnp.min(jnp.where(rest == m2, lane, LANES), axis=-1, keepdims=True)
    e2 = jnp.exp(m2 - m1)
    p1 = 1.0 / (1.0 + e2)
    cmb = jnp.where(lane == i1, p1, 0.0) + jnp.where(lane == i2, e2 * p1, 0.0)
    cmb_ref[...] = cmb
    cnt_ref[0] = jnp.sum(jnp.where(cmb > 0.0, 1.0, 0.0), axis=0, keepdims=True).astype(jnp.int32)


def _moe_route(x, norm_w, router, tm):
    t, d = x.shape
    ne = router.shape[1]
    rt = jnp.concatenate([router.astype(F32), jnp.zeros((d, LANES - ne), F32)], axis=1)
    return pl.pallas_call(
        _route_kernel,
        grid=(t // tm,),
        in_specs=[pl.BlockSpec((tm, d), lambda i: (i, 0)),
                  pl.BlockSpec((1, d), lambda i: (0, 0)),
                  pl.BlockSpec((d, LANES), lambda i: (0, 0))],
        out_specs=[pl.BlockSpec((tm, d), lambda i: (i, 0)),
                   pl.BlockSpec((tm, LANES), lambda i: (i, 0)),
                   pl.BlockSpec((1, 1, LANES), lambda i: (i, 0, 0))],
        out_shape=[jax.ShapeDtypeStruct((t, d), BF16),
                   jax.ShapeDtypeStruct((t, LANES), F32),
                   jax.ShapeDtypeStruct((t // tm, 1, LANES), jnp.int32)],
        compiler_params=_cparams("parallel"),
    )(x, norm_w.reshape(1, d).astype(F32), rt)


def _moe_kernel(cnt_ref, x_ref, h_ref, cmb_ref, wg_ref, wu_ref, wd_ref, fw_ref, o_ref,
                rank_ref, xc_ref, yc_ref, oh_ref, *, tm, final_norm):
    i = pl.program_id(0)
    e = pl.program_id(1)
    f = pl.program_id(2)
    last_f = f == pl.num_programs(2) - 1
    lane = lax.broadcasted_iota(jnp.int32, (tm, LANES), 1)

    @pl.when((e == 0) & (f == 0))
    def _():
        o_ref[...] = jnp.zeros(o_ref.shape, F32)
        r = lax.broadcasted_iota(jnp.int32, (tm, tm), 0)
        c = lax.broadcasted_iota(jnp.int32, (tm, tm), 1)
        before = jnp.where(c < r, 1.0, 0.0).astype(BF16)
        sel = jnp.where(cmb_ref[...] > 0.0, 1.0, 0.0).astype(BF16)
        rank_ref[...] = jnp.dot(before, sel, preferred_element_type=F32)

    cnt = cnt_ref[i, e]

    def weight():
        return jnp.sum(jnp.where(lane == e, cmb_ref[...], 0.0), axis=-1, keepdims=True)

    def one_hot(off, rows):
        r_e = jnp.sum(jnp.where(lane == e, rank_ref[...], 0.0), axis=-1, keepdims=True)
        key = jnp.where(weight() > 0.0, r_e, -1.0) - jnp.asarray(off, F32)
        slot = lax.broadcasted_iota(jnp.int32, (tm, rows), 1).astype(F32)
        return jnp.where(key == slot, 1.0, 0.0).astype(BF16)

    def process(off, rows, cache):
        rs = pl.ds(off, rows)

        @pl.when(f == 0)
        def _():
            onehot = one_hot(off, rows)
            if cache:
                oh_ref[...] = onehot
            xc_ref[rs, :] = _dot_tn(onehot, h_ref[...]).astype(BF16)
            yc_ref[rs, :] = jnp.zeros((rows, yc_ref.shape[1]), F32)

        xc = xc_ref[rs, :]
        act = _silu(jnp.dot(xc, wg_ref[0], preferred_element_type=F32)) * jnp.dot(
            xc, wu_ref[0], preferred_element_type=F32)
        yc_ref[rs, :] += jnp.dot(act.astype(BF16), wd_ref[0], preferred_element_type=F32)

        @pl.when(last_f)
        def _():
            onehot = oh_ref[...] if cache else one_hot(off, rows)
            o_ref[...] += weight() * jnp.dot(onehot, yc_ref[rs, :].astype(BF16), preferred_element_type=F32)

    @pl.when(cnt > 0)
    def _():
        process(0, MOE_FIRST_ROWS, True)

    def extra(b, carry):
        process(pl.multiple_of(MOE_FIRST_ROWS + b * MOE_BLK, 32), MOE_BLK, False)
        return carry

    lax.fori_loop(0, (jnp.maximum(cnt - MOE_FIRST_ROWS, 0) + MOE_BLK - 1) // MOE_BLK, extra, 0)

    @pl.when((e == pl.num_programs(1) - 1) & last_f)
    def _():
        y = x_ref[...] + o_ref[...]
        if final_norm:
            y = y * lax.rsqrt(jnp.mean(y * y, axis=-1, keepdims=True) + NORM_EPS) * fw_ref[...]
        o_ref[...] = y


def _ffn_moe(x, norm_w, router, w_gate, w_up, w_down, final_w, tm=MOE_TM, tf=1408):
    t, d = x.shape
    ne, _, ff = w_gate.shape
    nf = ff // tf
    h, cmb, cnt = _moe_route(x, norm_w, router, tm)
    final_norm = final_w is not None
    fw = (final_w if final_norm else jnp.ones((d,), F32)).reshape(1, d).astype(F32)
    cap = MOE_FIRST_ROWS + -(-max(tm - MOE_FIRST_ROWS, 0) // MOE_BLK) * MOE_BLK
    grid_spec = pltpu.PrefetchScalarGridSpec(
        num_scalar_prefetch=1,
        grid=(t // tm, ne, nf),
        in_specs=[pl.BlockSpec((tm, d), lambda i, e, f, c: (i, 0)),
                  pl.BlockSpec((tm, d), lambda i, e, f, c: (i, 0)),
                  pl.BlockSpec((tm, LANES), lambda i, e, f, c: (i, 0)),
                  pl.BlockSpec((1, d, tf), lambda i, e, f, c: (e, 0, f)),
                  pl.BlockSpec((1, d, tf), lambda i, e, f, c: (e, 0, f)),
                  pl.BlockSpec((1, tf, d), lambda i, e, f, c: (e, f, 0)),
                  pl.BlockSpec((1, d), lambda i, e, f, c: (0, 0))],
        out_specs=pl.BlockSpec((tm, d), lambda i, e, f, c: (i, 0)),
        scratch_shapes=[pltpu.VMEM((tm, LANES), F32), pltpu.VMEM((cap, d), BF16), pltpu.VMEM((cap, d), F32),
                        pltpu.VMEM((tm, MOE_FIRST_ROWS), BF16)])
    return pl.pallas_call(
        functools.partial(_moe_kernel, tm=tm, final_norm=final_norm),
        grid_spec=grid_spec,
        out_shape=jax.ShapeDtypeStruct((t, d), F32),
        compiler_params=_cparams("parallel", "arbitrary", "arbitrary"),
    )(cnt.reshape(t // tm, LANES), x, h, cmb, w_gate, w_up, w_down, fw)


def _reorder_kernel(w_ref, wc_ref, wn_ref):
    w = w_ref[0]
    wc = [w[:, 0:1024], w[:, 1536:4608], w[:, 1024:1536], w[:, 9760:10016], w[:, 5632:5648],
          w[:, 6672:6688], jnp.zeros((w.shape[0], N_PC - PC_SMALL - 32), w.dtype)]
    wn = [w[:, 6688:9760], w[:, 10016:13088], w[:, 4608:5632], w[:, 5648:6672]]
    wc_ref[...] = jnp.concatenate(wc, axis=1).astype(BF16)
    wn_ref[...] = jnp.concatenate(wn, axis=1).astype(BF16)


def _reorder_w_in(w, layer, tk=128):
    _, d, n = w.shape
    return pl.pallas_call(
        _reorder_kernel,
        grid=(d // tk,),
        in_specs=[pl.BlockSpec((1, tk, n), lambda i: (layer, i, 0))],
        out_specs=[pl.BlockSpec((tk, N_PC), lambda i: (i, 0)), pl.BlockSpec((tk, N_PN), lambda i: (i, 0))],
        out_shape=[jax.ShapeDtypeStruct((d, N_PC), BF16), jax.ShapeDtypeStruct((d, N_PN), BF16)],
        compiler_params=_cparams("parallel"),
    )(w)


def _token_mixer_layer(x, h, i, bsz, seq, p):
    if h is None:
        h = _rmsnorm(x, p["attn_norm_w"][i], BF16)
    wc, wn = _reorder_w_in(p["w_in"], i)
    pc, pn = _inproj(h, wc, wn, p["conv_w"][i], p["conv_b"][i], seq, tm=1024)
    y_ssm = _ssd_branch(pc, pn, p["ssm_a_log"][i], p["ssm_dt_bias"][i], p["ssm_d"][i],
                        p["ssm_norm_w"][i], bsz, seq)
    y_gdn = _gdn_branch(pc, pn, p["gdn_a_log"][i], p["gdn_dt_bias"][i], p["gdn_norm_w"][i],
                        bsz, seq)
    y_rwkv = _rwkv_branch(pc, pn, p["rwkv_mu"][i], p["rwkv_w0"][i], p["rwkv_w2"][i], p["rwkv_a0"][i],
                          p["rwkv_a2"][i], p["rwkv_g2"][i], p["rwkv_k_k"][i], p["rwkv_k_a"][i],
                          p["rwkv_r_k"][i], p["rwkv_ln_w"][i], p["rwkv_ln_b"][i], bsz, seq)
    return _merge(x, y_ssm, y_gdn, y_rwkv, pn, p["proj_ssm"][i], p["proj_gdn"][i], p["proj_rwkv"][i],
                  p["w_out"][i])


def kernel(x, attn_norm_w, w_in, conv_w, conv_b, ssm_a_log, ssm_dt_bias, ssm_d, ssm_norm_w, gdn_a_log, gdn_dt_bias, gdn_norm_w, rwkv_mu, rwkv_w0, rwkv_w2, rwkv_a0, rwkv_a2, rwkv_g2, rwkv_k_k, rwkv_k_a, rwkv_r_k, rwkv_ln_w, rwkv_ln_b, proj_ssm, proj_gdn, proj_rwkv, w_out, ffn_norm_w, dense_w_gate, dense_w_up, dense_w_down, moe_router, moe_w_gate, moe_w_up, moe_w_down, final_norm_w):
    p = dict(attn_norm_w=attn_norm_w, w_in=w_in, conv_w=conv_w, conv_b=conv_b, ssm_a_log=ssm_a_log,
             ssm_dt_bias=ssm_dt_bias, ssm_d=ssm_d, ssm_norm_w=ssm_norm_w, gdn_a_log=gdn_a_log,
             gdn_dt_bias=gdn_dt_bias, gdn_norm_w=gdn_norm_w, rwkv_mu=rwkv_mu, rwkv_w0=rwkv_w0,
             rwkv_w2=rwkv_w2, rwkv_a0=rwkv_a0, rwkv_a2=rwkv_a2, rwkv_g2=rwkv_g2, rwkv_k_k=rwkv_k_k,
             rwkv_k_a=rwkv_k_a, rwkv_r_k=rwkv_r_k, rwkv_ln_w=rwkv_ln_w, rwkv_ln_b=rwkv_ln_b,
             proj_ssm=proj_ssm, proj_gdn=proj_gdn, proj_rwkv=proj_rwkv, w_out=w_out)
    for name in ("w_in", "proj_ssm", "proj_gdn", "proj_rwkv", "w_out"):
        p[name] = p[name].astype(BF16)
    dense_w_gate, dense_w_up, dense_w_down = (t.astype(BF16) for t in (dense_w_gate, dense_w_up, dense_w_down))
    moe_w_gate, moe_w_up, moe_w_down = (t.astype(BF16) for t in (moe_w_gate, moe_w_up, moe_w_down))
    bsz, seq, d = x.shape
    depth = attn_norm_w.shape[0]
    xt = x.reshape(bsz * seq, d)
    h_next = None
    for i in range(depth):
        xt = _token_mixer_layer(xt, h_next, i, bsz, seq, p)
        h_next = None
        j = i // 2
        last = i == depth - 1
        if i % 2 == 0:
            if last:
                xt = _ffn_dense(xt, ffn_norm_w[i], dense_w_gate[j], dense_w_up[j], dense_w_down[j])
                xt = _rmsnorm(xt, final_norm_w, F32)
            else:
                xt, h_next = _ffn_dense(xt, ffn_norm_w[i], dense_w_gate[j], dense_w_up[j], dense_w_down[j],
                                        next_norm_w=attn_norm_w[i + 1])
        else:
            xt = _ffn_moe(xt, ffn_norm_w[i], moe_router[j], moe_w_gate[j], moe_w_up[j], moe_w_down[j],
                          final_norm_w if last else None)
    return xt.reshape(bsz, seq, d)
```

```python
import functools
import math

import numpy as np
import jax
import jax.numpy as jnp
from jax import lax
from jax.experimental import pallas as pl
from jax.experimental.pallas import tpu as pltpu

F32 = jnp.float32
BF16 = jnp.bfloat16

LANES = 128
SUBLANES = 8
VMEM_LIMIT_BYTES = 56 * 1024 * 1024

D_MODEL = 1024
SSM_HEADS, SSM_HEAD_DIM, SSM_GROUPS, SSM_STATE = 16, 64, 2, 128
SSM_INNER = SSM_HEADS * SSM_HEAD_DIM
GDN_HEADS, GDN_HEAD_DIM = 8, 128
GDN_INNER = GDN_HEADS * GDN_HEAD_DIM
RWKV_HEADS, RWKV_HEAD_DIM = 16, 64
RWKV_INNER = RWKV_HEADS * RWKV_HEAD_DIM
RWKV_DECAY_LORA, RWKV_ICLR_LORA, RWKV_GATE_LORA = 64, 64, 128
RWKV_GN_EPS = 64e-5
CONV_K = 4
FFN_DIM = 2816
N_EXPERTS = 8
NORM_EPS = 1e-6
L2_EPS = 1e-6

PC_XS, PC_Q, PC_K, PC_V = 0, 1024, 2048, 3072
PC_BS, PC_CS = 4096, 4352
PC_CONV_END = 4608
PC_LORA = 4608
PC_SMALL = 4864
N_PC = 5120
PN_RKV = 0
PN_GATE = 3072
PN_ZSSM, PN_ZGDN = 6144, 7168
N_PN = 8192
SMALL_DT, SMALL_A, SMALL_B = 0, 16, 24

SSD_CHUNK = 128
DPLR_CHUNK = 64
SEQ_PER_STEP = 4
INPROJ_ROW_CHUNK = 512


def _cparams(*sem):
    return pltpu.CompilerParams(dimension_semantics=sem, vmem_limit_bytes=VMEM_LIMIT_BYTES)


def _dot(a, b):
    return jnp.dot(a.astype(BF16), b.astype(BF16), preferred_element_type=F32)


def _dot_nt(a, b):
    return lax.dot_general(a.astype(BF16), b.astype(BF16), (((1,), (1,)), ((), ())),
                           preferred_element_type=F32)


def _dot_tn(a, b):
    return lax.dot_general(a.astype(BF16), b.astype(BF16), (((0,), (0,)), ((), ())),
                           preferred_element_type=F32)


def _split3(x):
    hi = x.astype(BF16)
    r1 = x - hi.astype(F32)
    mid = r1.astype(BF16)
    lo = (r1 - mid.astype(F32)).astype(BF16)
    return hi, mid, lo


def _dot_x3(x, w):
    hi, mid, lo = _split3(x)
    return _dot(hi, w) + _dot(mid, w) + _dot(lo, w)


def _dot_w3(w, x):
    hi, mid, lo = _split3(x)
    return _dot(w, hi) + _dot(w, mid) + _dot(w, lo)


def _dot_w2(w, x):
    xh = x.astype(BF16)
    xl = (x - xh.astype(F32)).astype(BF16)
    return _dot(w, xh) + _dot(w, xl)


def _dot_hl(a, b):
    ah = a.astype(BF16)
    al = (a - ah.astype(F32)).astype(BF16)
    bh = b.astype(BF16)
    bl = (b - bh.astype(F32)).astype(BF16)
    return _dot(ah, bh) + _dot(ah, bl) + _dot(al, bh)


def _dot_a2(a, b):
    ah = a.astype(BF16)
    al = (a - ah.astype(F32)).astype(BF16)
    return _dot(ah, b) + _dot(al, b)


def _sigmoid(x):
    return 0.5 * jnp.tanh(0.5 * x) + 0.5


def _silu(x):
    return x * _sigmoid(x)


def _softplus(x):
    return jnp.maximum(x, 0.0) + jnp.log(1.0 + jnp.exp(-jnp.abs(x)))


def _tri_incl(n):
    r = lax.broadcasted_iota(jnp.int32, (n, n), 0)
    c = lax.broadcasted_iota(jnp.int32, (n, n), 1)
    return jnp.where(r >= c, 1.0, 0.0).astype(BF16)


def _half_ones():
    r = lax.broadcasted_iota(jnp.int32, (LANES, LANES), 0)
    c = lax.broadcasted_iota(jnp.int32, (LANES, LANES), 1)
    return jnp.where((r // 64) == (c // 64), 1.0, 0.0).astype(BF16)


def _rmsnorm_kernel(x_ref, w_ref, o_ref):
    x = x_ref[...]
    ms = jnp.mean(x * x, axis=-1, keepdims=True)
    o_ref[...] = (x * lax.rsqrt(ms + NORM_EPS) * w_ref[...]).astype(o_ref.dtype)


def _rmsnorm(x, w, out_dtype, tm=512):
    t, d = x.shape
    return pl.pallas_call(
        _rmsnorm_kernel,
        grid=(t // tm,),
        in_specs=[pl.BlockSpec((tm, d), lambda i: (i, 0)),
                  pl.BlockSpec((1, d), lambda i: (0, 0))],
        out_specs=pl.BlockSpec((tm, d), lambda i: (i, 0)),
        out_shape=jax.ShapeDtypeStruct((t, d), out_dtype),
        compiler_params=_cparams("parallel"),
    )(x, w.reshape(1, d))


def _inproj_kernel(a_ref, wc_ref, wn_ref, cw_ref, cb_ref, cm_ref, oc_ref, on_ref, sh_ref, *,
                   tm, tiles_per_seq):
    i = pl.program_id(1)
    tc = oc_ref.shape[1]

    @pl.when(i % tiles_per_seq == 0)
    def _():
        sh_ref[0:SUBLANES, :] = jnp.zeros((SUBLANES, tc), F32)

    rc = min(tm, INPROJ_ROW_CHUNK)
    starts = list(range(0, tm, rc))
    acc = jnp.dot(a_ref[0:rc, :], wc_ref[...], preferred_element_type=F32)
    for n, r0 in enumerate(starts):
        sh_ref[SUBLANES + r0:SUBLANES + r0 + rc, :] = acc
        on_ref[r0:r0 + rc, :] = jnp.dot(a_ref[r0:r0 + rc, :], wn_ref[...], preferred_element_type=F32)
        if n + 1 < len(starts):
            acc_next = jnp.dot(a_ref[r0 + rc:r0 + 2 * rc, :], wc_ref[...], preferred_element_type=F32)
        c = acc * cw_ref[CONV_K - 1:CONV_K, :] + cb_ref[...]
        for k in range(1, CONV_K):
            c = c + sh_ref[pl.ds(SUBLANES + r0 - k, rc), :] * cw_ref[CONV_K - 1 - k:CONV_K - k, :]
        oc_ref[r0:r0 + rc, :] = jnp.where(cm_ref[...] > 0.0, _silu(c), acc)
        if n + 1 < len(starts):
            acc = acc_next
    sh_ref[0:SUBLANES, :] = sh_ref[tm:tm + SUBLANES, :]


def _conv_cols(c):
    r = c.shape[0]
    return jnp.concatenate([c[:, 0:1024], c[:, 1536:4608], c[:, 1024:1536],
                            jnp.zeros((r, N_PC - PC_CONV_END), F32)], axis=1).astype(F32)


def _inproj(a, wc, wn, conv_w, conv_b, seq, tm, nsteps=4):
    m, k = a.shape
    tc, tn = N_PC // nsteps, N_PN // nsteps
    cw = _conv_cols(conv_w)
    cb = _conv_cols(conv_b.reshape(1, -1))
    cm = _conv_cols(jnp.ones((1, conv_w.shape[1]), F32))
    return pl.pallas_call(
        functools.partial(_inproj_kernel, tm=tm, tiles_per_seq=seq // tm),
        grid=(nsteps, m // tm),
        in_specs=[pl.BlockSpec((tm, k), lambda j, i: (i, 0)),
                  pl.BlockSpec((k, tc), lambda j, i: (0, j)),
                  pl.BlockSpec((k, tn), lambda j, i: (0, j)),
                  pl.BlockSpec((CONV_K, tc), lambda j, i: (0, j)),
                  pl.BlockSpec((1, tc), lambda j, i: (0, j)),
                  pl.BlockSpec((1, tc), lambda j, i: (0, j))],
        out_specs=[pl.BlockSpec((tm, tc), lambda j, i: (i, j)),
                   pl.BlockSpec((tm, tn), lambda j, i: (i, j))],
        out_shape=[jax.ShapeDtypeStruct((m, N_PC), F32), jax.ShapeDtypeStruct((m, N_PN), F32)],
        scratch_shapes=[pltpu.VMEM((tm + SUBLANES, tc), F32)],
        compiler_params=_cparams("parallel", "arbitrary"),
    )(a, wc, wn, cw, cb, cm)


def _ssd_kernel(xs_ref, bs_ref, cs_ref, z_ref, sm_ref, alog_ref, dtb_ref, dsk_ref, nw_ref, ex_ref,
                o_ref, st_ref, *, L, nseq):
    @pl.when(pl.program_id(1) == 0)
    def _():
        st_ref[...] = jnp.zeros(st_ref.shape, F32)

    hd = SSM_HEAD_DIM
    gw = SSM_INNER // SSM_GROUPS
    heads_per_group = SSM_HEADS // SSM_GROUPS
    ex = ex_ref[...]
    row = lax.broadcasted_iota(jnp.int32, (L, L), 0)
    col = lax.broadcasted_iota(jnp.int32, (L, L), 1)
    causal = row >= col
    lane = lax.broadcasted_iota(jnp.int32, (L, LANES), 1)
    first_half = lane < hd
    for s in range(nseq):
        dt = _softplus(sm_ref[s] + dtb_ref[...])
        la = -jnp.exp(alog_ref[...]) * dt
        g_cum = _dot_w3(_tri_incl(L), la)
        g_cum_t = g_cum.T
        dt_e = _dot_a2(dt, ex)
        g_e = _dot_a2(g_cum, ex)
        g_last = g_e[L - 1:L, :]
        e_g = jnp.exp(g_e)
        e_dec = jnp.exp(g_last - g_e)
        e_last = jnp.exp(g_last)

        xs = xs_ref[s]
        xdt = xs * dt_e
        xdec = xdt * e_dec
        y_parts = []
        for g in range(SSM_GROUPS):
            gsl = slice(g * gw, (g + 1) * gw)
            b_g = bs_ref[s, :, g * SSM_STATE:(g + 1) * SSM_STATE]
            c_g = cs_ref[s, :, g * SSM_STATE:(g + 1) * SSM_STATE]
            cb = _dot_nt(c_g, b_g)
            st_g = st_ref[s, :, gsl]
            y_off = _dot(c_g, st_g) * e_g[:, gsl]
            st_ref[s, :, gsl] = st_g * e_last[:, gsl] + _dot_tn(b_g, xdec[:, gsl])
            for p in range(heads_per_group // 2):
                h0 = g * heads_per_group + 2 * p
                base = h0 * hd
                sc = []
                for h in (h0, h0 + 1):
                    diff = g_cum[:, h:h + 1] - g_cum_t[h:h + 1, :]
                    sc.append(cb * jnp.exp(jnp.where(causal, diff, -jnp.inf)))
                yd = _dot(jnp.concatenate(sc, axis=0), xdt[:, base:base + LANES])
                y_parts.append(jnp.where(first_half, yd[:L], yd[L:])
                               + y_off[:, base - g * gw:base - g * gw + LANES])
        y = jnp.concatenate(y_parts, axis=1) + dsk_ref[...] * xs
        y = y * _silu(z_ref[s])
        outs = []
        for g in range(SSM_GROUPS):
            seg = y[:, g * gw:(g + 1) * gw]
            outs.append(seg * lax.rsqrt(jnp.mean(seg * seg, axis=-1, keepdims=True) + NORM_EPS))
        o_ref[s] = (jnp.concatenate(outs, axis=1) * nw_ref[...]).astype(o_ref.dtype)


def _pad_lanes(v, offset=0, width=LANES):
    out = jnp.zeros((1, width), F32)
    return lax.dynamic_update_slice(out, v.reshape(1, -1).astype(F32), (0, offset))


def _ssd_branch(pc, pn, a_log, dt_bias, d_skip, norm_w, bsz, seq):
    L = SSD_CHUNK
    nseq = SEQ_PER_STEP if bsz % SEQ_PER_STEP == 0 else 1
    pc3 = pc.reshape(bsz, seq, N_PC)
    pn3 = pn.reshape(bsz, seq, N_PN)
    ex = np.zeros((LANES, SSM_INNER), np.float32)
    for h in range(SSM_HEADS):
        ex[h, h * SSM_HEAD_DIM:(h + 1) * SSM_HEAD_DIM] = 1.0
    vec = lambda w: pl.BlockSpec((1, w), lambda b, c: (0, 0))
    tok = lambda width, col: pl.BlockSpec((nseq, L, width), lambda b, c: (b, c, col // width))
    out = pl.pallas_call(
        functools.partial(_ssd_kernel, L=L, nseq=nseq),
        grid=(bsz // nseq, seq // L),
        in_specs=[tok(SSM_INNER, PC_XS), tok(256, PC_BS), tok(256, PC_CS), tok(SSM_INNER, PN_ZSSM),
                  tok(LANES, PC_SMALL),
                  vec(LANES), vec(LANES), vec(SSM_INNER), vec(SSM_INNER),
                  pl.BlockSpec((LANES, SSM_INNER), lambda b, c: (0, 0))],
        out_specs=pl.BlockSpec((nseq, L, SSM_INNER), lambda b, c: (b, c, 0)),
        out_shape=jax.ShapeDtypeStruct((bsz, seq, SSM_INNER), BF16),
        scratch_shapes=[pltpu.VMEM((nseq, SSM_STATE, SSM_INNER), F32)],
        compiler_params=_cparams("parallel", "arbitrary"),
    )(pc3, pc3, pc3, pn3, pc3,
      _pad_lanes(a_log, SMALL_DT), _pad_lanes(dt_bias, SMALL_DT),
      jnp.repeat(d_skip.astype(F32), SSM_HEAD_DIM).reshape(1, SSM_INNER),
      norm_w.reshape(1, SSM_INNER).astype(F32), jnp.asarray(ex, BF16))
    return out.reshape(bsz * seq, SSM_INNER)


def _head_stack(x):
    half = x.shape[1] // 2
    lane = lax.broadcasted_iota(jnp.int32, x.shape, 1)
    return jnp.concatenate([jnp.where(lane < half, x, 0.0), jnp.where(lane >= half, x, 0.0)], axis=0)


def _dplr_pairs(units, *, L):
    nu = len(units)
    P = 2 * L
    vc = units[0]["v"].shape[1]
    row = lax.broadcasted_iota(jnp.int32, (L, P), 0)
    tcol = lax.broadcasted_iota(jnp.int32, (L, P), 1)
    first = tcol < L
    tcol = jnp.where(first, tcol, tcol - L)
    strict, lower = tcol < row, tcol <= row

    sc = [_dot_nt(jnp.concatenate([u["a_sc"], u["r_sc"]], axis=0),
                  jnp.concatenate([_head_stack(u["b_sc"]), _head_stack(u["k_sc"])], axis=0))
          for u in units]
    v_st = [_head_stack(u["v"]) for u in units]

    def read(u):
        rows = jnp.concatenate([u["a_sr"], u["r_sr"]], axis=0)
        st = u["s_prev"]
        if len(st) == 1:
            return _dot_nt(rows, st[0])
        kh = rows.shape[1] // 2
        return jnp.concatenate([_dot_nt(rows[:, :kh], st[0]), _dot_nt(rows[:, kh:], st[1])], axis=1)

    reads = [read(u) for u in units]
    n_mat, a_rb, aks = [], [], []
    for i, u in enumerate(units):
        sb, sk = sc[i][:, :P], sc[i][:, P:]
        if u.get("dm_a") is None:
            n_mat.append(jnp.where(strict, sb[:L], 0.0))
            a_rb.append(jnp.where(lower, sb[L:], 0.0))
            aks.append(jnp.concatenate([jnp.where(strict, sk[:L], 0.0), jnp.where(lower, sk[L:], 0.0)], axis=0))
        else:
            n_mat.append(sb[:L] * u["dm_a"])
            a_rb.append(sb[L:] * u["dm_r"])
            aks.append(jnp.concatenate([sk[:L] * u["dm_a"], sk[L:] * u["dm_r"]], axis=0))
    av = [_dot(aks[i], v_st[i]) for i in range(nu)]
    x = [reads[i][:L] + av[i][:L] for i in range(nu)]
    y_part = [reads[i][L:] + av[i][L:] for i in range(nu)]

    steps = int(math.log2(L))
    for s in range(steps):
        x_st = [_head_stack(x[i]) for i in range(nu)]
        if s + 1 < steps:
            t = [_dot(n_mat[i], jnp.concatenate(
                    [x_st[i], jnp.concatenate([jnp.where(first, n_mat[i], 0.0),
                                               jnp.where(first, 0.0, n_mat[i])], axis=0)], axis=1))
                 for i in range(nu)]
            x = [x[i] + t[i][:, :vc] for i in range(nu)]
            n_mat = [t[i][:, vc:] for i in range(nu)]
        else:
            x = [x[i] + _dot(n_mat[i], x_st[i]) for i in range(nu)]
    ys = [y_part[i] + _dot(a_rb[i], _head_stack(x[i])) for i in range(nu)]

    s_new = []
    for i, u in enumerate(units):
        xv = jnp.concatenate([x[i], u["v"]], axis=0)
        bk = jnp.concatenate([u["b_dec"], u["k_dec"]], axis=0)
        st, pl_ = u["s_prev"], u["p_last"]
        if len(st) == 1:
            upd = _dot_tn(xv, bk)
            ri = lax.broadcasted_iota(jnp.int32, upd.shape, 0) // (upd.shape[0] // 2)
            ci = lax.broadcasted_iota(jnp.int32, upd.shape, 1) // (upd.shape[1] // 2)
            s_new.append([st[0] * pl_[0] + jnp.where(ri == ci, upd, 0.0)])
        else:
            vh, kh = xv.shape[1] // 2, bk.shape[1] // 2
            s_new.append([st[0] * pl_[0] + _dot_tn(xv[:, :vh], bk[:, :kh]),
                          st[1] * pl_[1] + _dot_tn(xv[:, vh:], bk[:, kh:])])
    return ys, s_new


def _gdn_kernel(q_ref, k_ref, v_ref, z_ref, sm_ref, alog_ref, dtb_ref, nw_ref, o_ref, st_ref, *, L, nseq):
    @pl.when(pl.program_id(1) == 0)
    def _():
        st_ref[...] = jnp.zeros(st_ref.shape, F32)

    hd = GDN_HEAD_DIM
    row = lax.broadcasted_iota(jnp.int32, (L, 2 * L), 0)
    tcol = lax.broadcasted_iota(jnp.int32, (L, 2 * L), 1)
    first_t = tcol < L
    tcol = jnp.where(first_t, tcol, tcol - L)
    first_c = lax.broadcasted_iota(jnp.int32, (L, 2 * hd), 1) < hd
    nw = nw_ref[...]
    scale = hd ** -0.5
    units = []
    for s in range(nseq):
        sm = sm_ref[s]
        g = -jnp.exp(alog_ref[...]) * _softplus(sm + dtb_ref[...])
        beta = _sigmoid(sm)
        g_cum = _dot_w3(_tri_incl(L), g)
        g_cum_t = jnp.concatenate([g_cum] * (LANES // L), axis=0).T
        for p in range(GDN_HEADS // 2):
            sl = slice(2 * p * hd, (2 * p + 2) * hd)
            q = q_ref[s, :, sl]
            k = k_ref[s, :, sl]

            def per_head(t):
                return jnp.where(first_c, jnp.sum(t[:, :hd], axis=-1, keepdims=True),
                                 jnp.sum(t[:, hd:], axis=-1, keepdims=True))

            def col(arr, base):
                h0 = base + 2 * p
                return arr[:, h0:h0 + 1], arr[:, h0 + 1:h0 + 2]

            qn = q * (lax.rsqrt(per_head(q * q) + L2_EPS) * scale)
            kn = k * lax.rsqrt(per_head(k * k) + L2_EPS)
            gc0, gc1 = col(g_cum, SMALL_A)
            gs0, gs1 = col(g, SMALL_A)
            bt0, bt1 = col(beta, SMALL_B)
            ch = lambda a0, a1: jnp.where(first_c, a0, a1)
            tm_ = lambda a0, a1: jnp.where(first_t, a0, a1)
            gc, gs, bt = ch(gc0, gc1), ch(gs0, gs1), ch(bt0, bt1)
            g_prev = gc - gs
            h0 = SMALL_A + 2 * p
            gr = tm_(g_cum_t[h0:h0 + 1, :], g_cum_t[h0 + 1:h0 + 2, :])
            ge0, ge1 = g_cum[L - 1:L, h0:h0 + 1], g_cum[L - 1:L, h0 + 1:h0 + 2]
            e_end = jnp.exp(ch(ge0, ge1) - gc)
            b_vec = -(bt * jnp.exp(gs)) * kn
            k_vec = bt * kn
            units.append(dict(
                a_sc=kn, r_sc=qn, b_sc=b_vec, k_sc=k_vec,
                a_sr=kn * jnp.exp(g_prev), r_sr=qn * jnp.exp(gc),
                b_dec=b_vec * e_end, k_dec=k_vec * e_end,
                v=v_ref[s, :, sl], s_prev=[st_ref[s, 2 * p], st_ref[s, 2 * p + 1]],
                p_last=[jnp.exp(ge0), jnp.exp(ge1)],
                dm_a=jnp.exp(jnp.where(tcol < row, tm_(gc0 - gs0, gc1 - gs1) - gr, -jnp.inf)),
                dm_r=jnp.exp(jnp.where(tcol <= row, tm_(gc0, gc1) - gr, -jnp.inf))))
    ys, s_new = _dplr_pairs(units, L=L)
    npair = GDN_HEADS // 2
    for s in range(nseq):
        for p in range(npair):
            sl = slice(2 * p * hd, (2 * p + 2) * hd)
            st_ref[s, 2 * p] = s_new[s * npair + p][0]
            st_ref[s, 2 * p + 1] = s_new[s * npair + p][1]
            y = ys[s * npair + p]
            ms = jnp.where(first_c, jnp.mean(y[:, :hd] * y[:, :hd], axis=-1, keepdims=True),
                           jnp.mean(y[:, hd:] * y[:, hd:], axis=-1, keepdims=True))
            yn = y * lax.rsqrt(ms + NORM_EPS) * nw
            o_ref[s, :, sl] = (yn * _silu(z_ref[s, :, sl])).astype(o_ref.dtype)


def _gdn_branch(pc, pn, a_log, dt_bias, norm_w, bsz, seq):
    L = DPLR_CHUNK
    nseq = SEQ_PER_STEP if bsz % SEQ_PER_STEP == 0 else 1
    pc3 = pc.reshape(bsz, seq, N_PC)
    pn3 = pn.reshape(bsz, seq, N_PN)
    vec = lambda w: pl.BlockSpec((1, w), lambda b, c: (0, 0))
    tok = lambda width, col: pl.BlockSpec((nseq, L, width), lambda b, c: (b, c, col // width))
    out = pl.pallas_call(
        functools.partial(_gdn_kernel, L=L, nseq=nseq),
        grid=(bsz // nseq, seq // L),
        in_specs=[tok(GDN_INNER, PC_Q), tok(GDN_INNER, PC_K), tok(GDN_INNER, PC_V),
                  tok(GDN_INNER, PN_ZGDN), tok(LANES, PC_SMALL),
                  vec(LANES), vec(LANES), vec(2 * GDN_HEAD_DIM)],
        out_specs=pl.BlockSpec((nseq, L, GDN_INNER), lambda b, c: (b, c, 0)),
        out_shape=jax.ShapeDtypeStruct((bsz, seq, GDN_INNER), BF16),
        scratch_shapes=[pltpu.VMEM((nseq, GDN_HEADS, LANES, LANES), F32)],
        compiler_params=_cparams("parallel", "arbitrary"),
    )(pc3, pc3, pc3, pn3, pc3,
      _pad_lanes(a_log, SMALL_A), _pad_lanes(dt_bias, SMALL_A),
      jnp.tile(norm_w.reshape(1, GDN_HEAD_DIM).astype(F32), (1, 2)))
    return out.reshape(bsz * seq, GDN_INNER)


def _rwkv_kernel(rkv_ref, lora_ref, mu_ref, mul_ref, w0_ref, w2_ref, a0_ref, a2_ref, g2_ref,
                 kk_ref, ka_ref, rk_ref, lnw_ref, lnb_ref, o_ref, sh_ref, shl_ref, st_ref, *, L, nseq):
    @pl.when(pl.program_id(1) == 0)
    def _():
        st_ref[...] = jnp.zeros(st_ref.shape, F32)
        for s in range(nseq):
            sh_ref[s, 0:SUBLANES, :] = jnp.zeros((SUBLANES, sh_ref.shape[2]), F32)
            shl_ref[s, 0:SUBLANES, :] = jnp.zeros((SUBLANES, shl_ref.shape[2]), F32)

    n = RWKV_INNER
    npair = RWKV_HEADS // 2
    ones_bd = _half_ones()
    inv_hd = 1.0 / RWKV_HEAD_DIM

    def head_sums(t):
        st = jnp.concatenate([t[:, p * LANES:(p + 1) * LANES] for p in range(npair)], axis=0)
        sm = _dot(st, ones_bd)
        return jnp.concatenate([sm[p * L:(p + 1) * L] for p in range(npair)], axis=1)

    units, tails = [], []
    for s in range(nseq):
        u = rkv_ref[s]
        ul = lora_ref[s]
        sh_ref[s, SUBLANES:SUBLANES + L, :] = u
        shl_ref[s, SUBLANES:SUBLANES + L, :] = ul
        u = u + (sh_ref[s, pl.ds(SUBLANES - 1, L), :] - u) * mu_ref[...]
        ul = ul + (shl_ref[s, pl.ds(SUBLANES - 1, L), :] - ul) * mul_ref[...]
        sh_ref[s, 0:SUBLANES, :] = sh_ref[s, L:L + SUBLANES, :]
        shl_ref[s, 0:SUBLANES, :] = shl_ref[s, L:L + SUBLANES, :]

        r, k, v = u[:, 0:n], u[:, n:2 * n], u[:, 2 * n:3 * n]
        lo = ul[:, 0:LANES]
        dg = ul[:, LANES:2 * LANES]
        w_log = -_softplus(-(w0_ref[...] + _dot_a2(jnp.tanh(lo), w2_ref[...]))) - 0.5
        lw = -jnp.exp(w_log)
        a_ic = _sigmoid(a0_ref[...] + _dot_a2(lo, a2_ref[...]))
        gate = _dot(_sigmoid(dg), g2_ref[...])
        g_cum = _dot_w2(_tri_incl(L), lw)
        kk_raw = k * kk_ref[...]
        k_mod = k * (1.0 + (a_ic - 1.0) * ka_ref[...])
        kk_all = kk_raw * lax.rsqrt(head_sums(kk_raw * kk_raw) + L2_EPS)
        tails.append((head_sums(r * k_mod * rk_ref[...]) * v, gate))
        for p in range(npair):
            sl = slice(p * LANES, (p + 1) * LANES)
            kk = kk_all[:, sl]
            gc = g_cum[:, sl]
            g_end = gc[L - 1:L, :]
            e_neg = jnp.exp(-gc)
            e_end = jnp.exp(g_end - gc)
            b_vec = kk * a_ic[:, sl]
            k_vec = k_mod[:, sl]
            a_t = -kk * jnp.exp(gc - lw[:, sl])
            r_t = r[:, sl] * jnp.exp(gc)
            units.append(dict(
                a_sc=a_t, r_sc=r_t, b_sc=b_vec * e_neg, k_sc=k_vec * e_neg,
                a_sr=a_t, r_sr=r_t, b_dec=b_vec * e_end, k_dec=k_vec * e_end,
                v=v[:, sl], s_prev=[st_ref[s, p]], p_last=[jnp.exp(g_end)]))
    ys, s_new = _dplr_pairs(units, L=L)
    for s in range(nseq):
        for p in range(npair):
            st_ref[s, p] = s_new[s * npair + p][0]
        y = jnp.concatenate(ys[s * npair:(s + 1) * npair], axis=1)
        yc = y - head_sums(y) * inv_hd
        var = head_sums(yc * yc) * inv_hd
        yn = yc * lax.rsqrt(var + RWKV_GN_EPS) * lnw_ref[...] + lnb_ref[...]
        bonus, gate = tails[s]
        o_ref[s] = ((yn + bonus) * gate).astype(o_ref.dtype)


def _rwkv_branch(pc, pn, mu, w0, w2, a0, a2, g2, k_k, k_a, r_k, ln_w, ln_b, bsz, seq):
    L = DPLR_CHUNK
    n = RWKV_INNER
    nseq = SEQ_PER_STEP if bsz % SEQ_PER_STEP == 0 else 1
    pc3 = pc.reshape(bsz, seq, N_PC)
    pn3 = pn.reshape(bsz, seq, N_PN)
    vec = lambda w: pl.BlockSpec((1, w), lambda b, c: (0, 0))
    mat = lambda: pl.BlockSpec((LANES, n), lambda b, c: (0, 0))
    tok = lambda width, col: pl.BlockSpec((nseq, L, width), lambda b, c: (b, c, col // width))
    w2p = jnp.concatenate([w2, jnp.zeros((LANES - RWKV_DECAY_LORA, n), F32)], axis=0)
    a2p = jnp.concatenate([jnp.zeros((RWKV_DECAY_LORA, n), F32), a2], axis=0)
    r1 = lambda t: t.reshape(1, -1).astype(F32)
    out = pl.pallas_call(
        functools.partial(_rwkv_kernel, L=L, nseq=nseq),
        grid=(bsz // nseq, seq // L),
        in_specs=[tok(3 * n, PN_RKV), tok(256, PC_LORA),
                  vec(3 * n), vec(256), vec(n), mat(), vec(n), mat(), mat(),
                  vec(n), vec(n), vec(n), vec(n), vec(n)],
        out_specs=pl.BlockSpec((nseq, L, n), lambda b, c: (b, c, 0)),
        out_shape=jax.ShapeDtypeStruct((bsz, seq, n), BF16),
        scratch_shapes=[pltpu.VMEM((nseq, L + SUBLANES, 3 * n), F32),
                        pltpu.VMEM((nseq, L + SUBLANES, 256), F32),
                        pltpu.VMEM((nseq, RWKV_HEADS // 2, LANES, LANES), F32)],
        compiler_params=_cparams("parallel", "arbitrary"),
    )(pn3, pc3, r1(mu[:3 * n]), r1(mu[3 * n:]), r1(w0), w2p, r1(a0), a2p, g2.astype(F32),
      r1(k_k), r1(k_a), r1(r_k), r1(ln_w), r1(ln_b))
    return out.reshape(bsz * seq, n)


def _merge_kernel(x_ref, ys_ref, yg_ref, yr_ref, gate_ref, ps_ref, pg_ref, pr_ref, wo_ref, o_ref):
    d = D_MODEL
    m = _sigmoid(gate_ref[:, 0:d]) * jnp.dot(ys_ref[...], ps_ref[...], preferred_element_type=F32)
    m = m + _sigmoid(gate_ref[:, d:2 * d]) * jnp.dot(yg_ref[...], pg_ref[...], preferred_element_type=F32)
    m = m + _sigmoid(gate_ref[:, 2 * d:3 * d]) * jnp.dot(yr_ref[...], pr_ref[...], preferred_element_type=F32)
    o_ref[...] = x_ref[...] + jnp.dot(m.astype(BF16), wo_ref[...], preferred_element_type=F32)


def _merge(x, y_ssm, y_gdn, y_rwkv, pn, p_ssm, p_gdn, p_rwkv, w_out, tm=512):
    t, d = x.shape
    tok = lambda: pl.BlockSpec((tm, d), lambda i: (i, 0))
    wgt = lambda: pl.BlockSpec((d, d), lambda i: (0, 0))
    return pl.pallas_call(
        _merge_kernel,
        grid=(t // tm,),
        in_specs=[tok(), tok(), tok(), tok(),
                  pl.BlockSpec((tm, 3 * d), lambda i: (i, PN_GATE // (3 * d))),
                  wgt(), wgt(), wgt(), wgt()],
        out_specs=tok(),
        out_shape=jax.ShapeDtypeStruct((t, d), F32),
        compiler_params=_cparams("parallel"),
    )(x, y_ssm, y_gdn, y_rwkv, pn, p_ssm, p_gdn, p_rwkv, w_out)


def _ffn_kernel(x_ref, nw_ref, wg_ref, wu_ref, wd_ref, *rest, emit_next):
    if emit_next:
        nnw_ref, o_ref, hn_ref, h_ref, acc_ref = rest
    else:
        o_ref, h_ref, acc_ref = rest
    f = pl.program_id(1)

    @pl.when(f == 0)
    def _():
        x = x_ref[...]
        ms = jnp.mean(x * x, axis=-1, keepdims=True)
        h_ref[...] = (x * lax.rsqrt(ms + NORM_EPS) * nw_ref[...]).astype(BF16)
        acc_ref[...] = x

    h = h_ref[...]
    act = _silu(jnp.dot(h, wg_ref[...], preferred_element_type=F32)) * jnp.dot(
        h, wu_ref[...], preferred_element_type=F32)
    acc_ref[...] += jnp.dot(act.astype(BF16), wd_ref[...], preferred_element_type=F32)

    @pl.when(f == pl.num_programs(1) - 1)
    def _():
        y = acc_ref[...]
        o_ref[...] = y
        if emit_next:
            ms = jnp.mean(y * y, axis=-1, keepdims=True)
            hn_ref[...] = (y * lax.rsqrt(ms + NORM_EPS) * nnw_ref[...]).astype(BF16)


def _ffn_dense(x, norm_w, w_gate, w_up, w_down, next_norm_w=None, tm=512, tf=2816):
    t, d = x.shape
    nf = w_gate.shape[1] // tf
    emit_next = next_norm_w is not None
    tok = pl.BlockSpec((tm, d), lambda i, f: (i, 0))
    vec = pl.BlockSpec((1, d), lambda i, f: (0, 0))
    in_specs = [tok, vec,
                pl.BlockSpec((d, tf), lambda i, f: (0, f)),
                pl.BlockSpec((d, tf), lambda i, f: (0, f)),
                pl.BlockSpec((tf, d), lambda i, f: (f, 0))]
    args = [x, norm_w.reshape(1, d).astype(F32), w_gate, w_up, w_down]
    out_specs, out_shape = tok, jax.ShapeDtypeStruct((t, d), F32)
    if emit_next:
        in_specs.append(vec)
        args.append(next_norm_w.reshape(1, d).astype(F32))
        out_specs, out_shape = [tok, tok], [out_shape, jax.ShapeDtypeStruct((t, d), BF16)]
    return pl.pallas_call(
        functools.partial(_ffn_kernel, emit_next=emit_next),
        grid=(t // tm, nf),
        in_specs=in_specs,
        out_specs=out_specs,
        out_shape=out_shape,
        scratch_shapes=[pltpu.VMEM((tm, d), BF16), pltpu.VMEM((tm, d), F32)],
        compiler_params=_cparams("parallel", "arbitrary"),
    )(*args)


MOE_TM = 1024
MOE_BLK = 128
MOE_FIRST_ROWS = 288


def _route_kernel(x_ref, nw_ref, rt_ref, h_ref, cmb_ref, cnt_ref):
    x = x_ref[...]
    lane = lax.broadcasted_iota(jnp.int32, cmb_ref.shape, 1)
    ms = jnp.mean(x * x, axis=-1, keepdims=True)
    hf = x * lax.rsqrt(ms + NORM_EPS) * nw_ref[...]
    h_ref[...] = hf.astype(BF16)
    logits = jnp.where(lane < N_EXPERTS, _dot_hl(hf, rt_ref[...]), -jnp.inf)
    m1 = jnp.max(logits, axis=-1, keepdims=True)
    i1 = jnp.min(jnp.where(logits == m1, lane, LANES), axis=-1, keepdims=True)
    rest = jnp.where(lane == i1, -jnp.inf, logits)
    m2 = jnp.max(rest, axis=-1, keepdims=True)
    i2 = jnp.min(jnp.where(rest == m2, lane, LANES), axis=-1, keepdims=True)
    e2 = jnp.exp(m2 - m1)
    p1 = 1.0 / (1.0 + e2)
    cmb = jnp.where(lane == i1, p1, 0.0) + jnp.where(lane == i2, e2 * p1, 0.0)
    cmb_ref[...] = cmb
    cnt_ref[0] = jnp.sum(jnp.where(cmb > 0.0, 1.0, 0.0), axis=0, keepdims=True).astype(jnp.int32)


def _moe_route(x, norm_w, router, tm):
    t, d = x.shape
    ne = router.shape[1]
    rt = jnp.concatenate([router.astype(F32), jnp.zeros((d, LANES - ne), F32)], axis=1)
    return pl.pallas_call(
        _route_kernel,
        grid=(t // tm,),
        in_specs=[pl.BlockSpec((tm, d), lambda i: (i, 0)),
                  pl.BlockSpec((1, d), lambda i: (0, 0)),
                  pl.BlockSpec((d, LANES), lambda i: (0, 0))],
        out_specs=[pl.BlockSpec((tm, d), lambda i: (i, 0)),
                   pl.BlockSpec((tm, LANES), lambda i: (i, 0)),
                   pl.BlockSpec((1, 1, LANES), lambda i: (i, 0, 0))],
        out_shape=[jax.ShapeDtypeStruct((t, d), BF16),
                   jax.ShapeDtypeStruct((t, LANES), F32),
                   jax.ShapeDtypeStruct((t // tm, 1, LANES), jnp.int32)],
        compiler_params=_cparams("parallel"),
    )(x, norm_w.reshape(1, d).astype(F32), rt)


def _moe_kernel(cnt_ref, x_ref, h_ref, cmb_ref, wg_ref, wu_ref, wd_ref, fw_ref, o_ref,
                rank_ref, xc_ref, yc_ref, oh_ref, *, tm, final_norm):
    i = pl.program_id(0)
    e = pl.program_id(1)
    f = pl.program_id(2)
    last_f = f == pl.num_programs(2) - 1
    lane = lax.broadcasted_iota(jnp.int32, (tm, LANES), 1)

    @pl.when((e == 0) & (f == 0))
    def _():
        o_ref[...] = jnp.zeros(o_ref.shape, F32)
        r = lax.broadcasted_iota(jnp.int32, (tm, tm), 0)
        c = lax.broadcasted_iota(jnp.int32, (tm, tm), 1)
        before = jnp.where(c < r, 1.0, 0.0).astype(BF16)
        sel = jnp.where(cmb_ref[...] > 0.0, 1.0, 0.0).astype(BF16)
        rank_ref[...] = jnp.dot(before, sel, preferred_element_type=F32)

    cnt = cnt_ref[i, e]

    def weight():
        return jnp.sum(jnp.where(lane == e, cmb_ref[...], 0.0), axis=-1, keepdims=True)

    def one_hot(off, rows):
        r_e = jnp.sum(jnp.where(lane == e, rank_ref[...], 0.0), axis=-1, keepdims=True)
        key = jnp.where(weight() > 0.0, r_e, -1.0) - jnp.asarray(off, F32)
        slot = lax.broadcasted_iota(jnp.int32, (tm, rows), 1).astype(F32)
        return jnp.where(key == slot, 1.0, 0.0).astype(BF16)

    def process(off, rows, cache):
        rs = pl.ds(off, rows)

        @pl.when(f == 0)
        def _():
            onehot = one_hot(off, rows)
            if cache:
                oh_ref[...] = onehot
            xc_ref[rs, :] = _dot_tn(onehot, h_ref[...]).astype(BF16)
            yc_ref[rs, :] = jnp.zeros((rows, yc_ref.shape[1]), F32)

        xc = xc_ref[rs, :]
        act = _silu(jnp.dot(xc, wg_ref[0], preferred_element_type=F32)) * jnp.dot(
            xc, wu_ref[0], preferred_element_type=F32)
        yc_ref[rs, :] += jnp.dot(act.astype(BF16), wd_ref[0], preferred_element_type=F32)

        @pl.when(last_f)
        def _():
            onehot = oh_ref[...] if cache else one_hot(off, rows)
            o_ref[...] += weight() * jnp.dot(onehot, yc_ref[rs, :].astype(BF16), preferred_element_type=F32)

    @pl.when(cnt > 0)
    def _():
        process(0, MOE_FIRST_ROWS, True)

    def extra(b, carry):
        process(pl.multiple_of(MOE_FIRST_ROWS + b * MOE_BLK, 32), MOE_BLK, False)
        return carry

    lax.fori_loop(0, (jnp.maximum(cnt - MOE_FIRST_ROWS, 0) + MOE_BLK - 1) // MOE_BLK, extra, 0)

    @pl.when((e == pl.num_programs(1) - 1) & last_f)
    def _():
        y = x_ref[...] + o_ref[...]
        if final_norm:
            y = y * lax.rsqrt(jnp.mean(y * y, axis=-1, keepdims=True) + NORM_EPS) * fw_ref[...]
        o_ref[...] = y


def _ffn_moe(x, norm_w, router, w_gate, w_up, w_down, final_w, tm=MOE_TM, tf=1408):
    t, d = x.shape
    ne, _, ff = w_gate.shape
    nf = ff // tf
    h, cmb, cnt = _moe_route(x, norm_w, router, tm)
    final_norm = final_w is not None
    fw = (final_w if final_norm else jnp.ones((d,), F32)).reshape(1, d).astype(F32)
    cap = MOE_FIRST_ROWS + -(-max(tm - MOE_FIRST_ROWS, 0) // MOE_BLK) * MOE_BLK
    grid_spec = pltpu.PrefetchScalarGridSpec(
        num_scalar_prefetch=1,
        grid=(t // tm, ne, nf),
        in_specs=[pl.BlockSpec((tm, d), lambda i, e, f, c: (i, 0)),
                  pl.BlockSpec((tm, d), lambda i, e, f, c: (i, 0)),
                  pl.BlockSpec((tm, LANES), lambda i, e, f, c: (i, 0)),
                  pl.BlockSpec((1, d, tf), lambda i, e, f, c: (e, 0, f)),
                  pl.BlockSpec((1, d, tf), lambda i, e, f, c: (e, 0, f)),
                  pl.BlockSpec((1, tf, d), lambda i, e, f, c: (e, f, 0)),
                  pl.BlockSpec((1, d), lambda i, e, f, c: (0, 0))],
        out_specs=pl.BlockSpec((tm, d), lambda i, e, f, c: (i, 0)),
        scratch_shapes=[pltpu.VMEM((tm, LANES), F32), pltpu.VMEM((cap, d), BF16), pltpu.VMEM((cap, d), F32),
                        pltpu.VMEM((tm, MOE_FIRST_ROWS), BF16)])
    return pl.pallas_call(
        functools.partial(_moe_kernel, tm=tm, final_norm=final_norm),
        grid_spec=grid_spec,
        out_shape=jax.ShapeDtypeStruct((t, d), F32),
        compiler_params=_cparams("parallel", "arbitrary", "arbitrary"),
    )(cnt.reshape(t // tm, LANES), x, h, cmb, w_gate, w_up, w_down, fw)


def _reorder_kernel(w_ref, wc_ref, wn_ref):
    w = w_ref[0]
    wc = [w[:, 0:1024], w[:, 1536:4608], w[:, 1024:1536], w[:, 9760:10016], w[:, 5632:5648],
          w[:, 6672:6688], jnp.zeros((w.shape[0], N_PC - PC_SMALL - 32), w.dtype)]
    wn = [w[:, 6688:9760], w[:, 10016:13088], w[:, 4608:5632], w[:, 5648:6672]]
    wc_ref[...] = jnp.concatenate(wc, axis=1).astype(BF16)
    wn_ref[...] = jnp.concatenate(wn, axis=1).astype(BF16)


def _reorder_w_in(w, layer, tk=128):
    _, d, n = w.shape
    return pl.pallas_call(
        _reorder_kernel,
        grid=(d // tk,),
        in_specs=[pl.BlockSpec((1, tk, n), lambda i: (layer, i, 0))],
        out_specs=[pl.BlockSpec((tk, N_PC), lambda i: (i, 0)), pl.BlockSpec((tk, N_PN), lambda i: (i, 0))],
        out_shape=[jax.ShapeDtypeStruct((d, N_PC), BF16), jax.ShapeDtypeStruct((d, N_PN), BF16)],
        compiler_params=_cparams("parallel"),
    )(w)


def _token_mixer_layer(x, h, i, bsz, seq, p):
    if h is None:
        h = _rmsnorm(x, p["attn_norm_w"][i], BF16)
    wc, wn = _reorder_w_in(p["w_in"], i)
    pc, pn = _inproj(h, wc, wn, p["conv_w"][i], p["conv_b"][i], seq, tm=1024)
    y_ssm = _ssd_branch(pc, pn, p["ssm_a_log"][i], p["ssm_dt_bias"][i], p["ssm_d"][i],
                        p["ssm_norm_w"][i], bsz, seq)
    y_gdn = _gdn_branch(pc, pn, p["gdn_a_log"][i], p["gdn_dt_bias"][i], p["gdn_norm_w"][i],
                        bsz, seq)
    y_rwkv = _rwkv_branch(pc, pn, p["rwkv_mu"][i], p["rwkv_w0"][i], p["rwkv_w2"][i], p["rwkv_a0"][i],
                          p["rwkv_a2"][i], p["rwkv_g2"][i], p["rwkv_k_k"][i], p["rwkv_k_a"][i],
                          p["rwkv_r_k"][i], p["rwkv_ln_w"][i], p["rwkv_ln_b"][i], bsz, seq)
    return _merge(x, y_ssm, y_gdn, y_rwkv, pn, p["proj_ssm"][i], p["proj_gdn"][i], p["proj_rwkv"][i],
                  p["w_out"][i])


def kernel(x, attn_norm_w, w_in, conv_w, conv_b, ssm_a_log, ssm_dt_bias, ssm_d, ssm_norm_w, gdn_a_log, gdn_dt_bias, gdn_norm_w, rwkv_mu, rwkv_w0, rwkv_w2, rwkv_a0, rwkv_a2, rwkv_g2, rwkv_k_k, rwkv_k_a, rwkv_r_k, rwkv_ln_w, rwkv_ln_b, proj_ssm, proj_gdn, proj_rwkv, w_out, ffn_norm_w, dense_w_gate, dense_w_up, dense_w_down, moe_router, moe_w_gate, moe_w_up, moe_w_down, final_norm_w):
    p = dict(attn_norm_w=attn_norm_w, w_in=w_in, conv_w=conv_w, conv_b=conv_b, ssm_a_log=ssm_a_log,
             ssm_dt_bias=ssm_dt_bias, ssm_d=ssm_d, ssm_norm_w=ssm_norm_w, gdn_a_log=gdn_a_log,
             gdn_dt_bias=gdn_dt_bias, gdn_norm_w=gdn_norm_w, rwkv_mu=rwkv_mu, rwkv_w0=rwkv_w0,
             rwkv_w2=rwkv_w2, rwkv_a0=rwkv_a0, rwkv_a2=rwkv_a2, rwkv_g2=rwkv_g2, rwkv_k_k=rwkv_k_k,
             rwkv_k_a=rwkv_k_a, rwkv_r_k=rwkv_r_k, rwkv_ln_w=rwkv_ln_w, rwkv_ln_b=rwkv_ln_b,
             proj_ssm=proj_ssm, proj_gdn=proj_gdn, proj_rwkv=proj_rwkv, w_out=w_out)
    for name in ("w_in", "proj_ssm", "proj_gdn", "proj_rwkv", "w_out"):
        p[name] = p[name].astype(BF16)
    dense_w_gate, dense_w_up, dense_w_down = (t.astype(BF16) for t in (dense_w_gate, dense_w_up, dense_w_down))
    moe_w_gate, moe_w_up, moe_w_down = (t.astype(BF16) for t in (moe_w_gate, moe_w_up, moe_w_down))
    bsz, seq, d = x.shape
    depth = attn_norm_w.shape[0]
    xt = x.reshape(bsz * seq, d)
    h_next = None
    for i in range(depth):
        xt = _token_mixer_layer(xt, h_next, i, bsz, seq, p)
        h_next = None
        j = i // 2
        last = i == depth - 1
        if i % 2 == 0:
            if last:
                xt = _ffn_dense(xt, ffn_norm_w[i], dense_w_gate[j], dense_w_up[j], dense_w_down[j])
                xt = _rmsnorm(xt, final_norm_w, F32)
            else:
                xt, h_next = _ffn_dense(xt, ffn_norm_w[i], dense_w_gate[j], dense_w_up[j], dense_w_down[j],
                                        next_norm_w=attn_norm_w[i + 1])
        else:
            xt = _ffn_moe(xt, ffn_norm_w[i], moe_router[j], moe_w_gate[j], moe_w_up[j], moe_w_down[j],
                          final_norm_w if last else None)
    return xt.reshape(bsz, seq, d)
```
